```python
import jax, jax.numpy as jnp
from jax import lax
import numpy as np

D_MODEL = 1024
BATCH = 16
SEQ = 2048
DEPTH = 1

D_MIX = D_MODEL
HEAD_DIM = 64
RW_HEADS = 8
RW_WIDTH = RW_HEADS * HEAD_DIM
RW_LORA_W = 64
RW_LORA_A = 64
RW_LORA_G = 128
RW_GN_EPS = 64e-5
AT_WIDTH = D_MIX - RW_WIDTH
AT_HEADS = AT_WIDTH // HEAD_DIM
IDX_HEADS = 4
IDX_DIM = 64
IDX_TOPK_MAX = 256
QBLOCK = 128
ROPE_THETA = 10000.0
NORM_EPS = 1e-6
LN_EPS = 1e-6
N_EXPERTS = 256
TOP_K = 8
N_GROUPS = 8
TOPK_GROUPS = 4
EXPERT_DIM = 256
SHARED_DIM = 256
ROUTED_SCALE = 2.5
EXPERT_BLOCK = 64
RW_SIZES = [RW_WIDTH, RW_WIDTH, RW_WIDTH, RW_LORA_W, RW_LORA_A, RW_LORA_G]
AT_SIZES = [AT_WIDTH, AT_WIDTH, AT_WIDTH, IDX_HEADS * IDX_DIM, IDX_DIM, IDX_HEADS]
RW_COLS = sum(RW_SIZES)
AT_COLS = sum(AT_SIZES)
IN_COLS = RW_COLS + AT_COLS

kernel_name = "hybrid_rwkv7_dsa_moe_block"


def _split(x, sizes):
    offs = np.cumsum(sizes)[:-1].tolist()
    return jnp.split(x, offs, axis=-1)


def _rmsnorm(x, g, eps=NORM_EPS):
    xf = x.astype(jnp.float32)
    y = xf * lax.rsqrt(jnp.mean(xf * xf, axis=-1, keepdims=True) + eps)
    return (y * g.astype(jnp.float32)).astype(x.dtype)


def _layernorm(x, g, b, eps=LN_EPS):
    xf = x.astype(jnp.float32)
    mu = jnp.mean(xf, axis=-1, keepdims=True)
    xc = xf - mu
    y = xc * lax.rsqrt(jnp.mean(xc * xc, axis=-1, keepdims=True) + eps)
    return (y * g.astype(jnp.float32) + b.astype(jnp.float32)).astype(x.dtype)


def _rope(x, positions):
    d = x.shape[-1]
    inv = ROPE_THETA ** (-jnp.arange(0, d, 2, dtype=jnp.float32) / d)
    ang = positions.astype(jnp.float32)[..., None] * inv
    ang = ang.reshape(ang.shape[:2] + (1,) * (x.ndim - 3) + (d // 2,))
    cos, sin = jnp.cos(ang), jnp.sin(ang)
    x1 = x[..., : d // 2].astype(jnp.float32)
    x2 = x[..., d // 2:].astype(jnp.float32)
    return jnp.concatenate([x1 * cos - x2 * sin, x2 * cos + x1 * sin], axis=-1).astype(x.dtype)


def _swiglu(x, w1, w3, w2):
    return (jax.nn.silu(x @ w1) * (x @ w3)) @ w2


def _rwkv7_scan(r, decay, k, v, kk, a):
    B, S, H, N = r.shape

    def step(state, inp):
        r_t, w_t, k_t, v_t, kk_t, a_t = inp
        sa = jnp.einsum('bhvk,bhk->bhv', state, kk_t)
        state = (state * w_t[:, :, None, :]
                 - sa[..., None] * (kk_t * a_t)[:, :, None, :]
                 + v_t[..., None] * k_t[:, :, None, :])
        return state, jnp.einsum('bhvk,bhk->bhv', state, r_t)

    xs = tuple(jnp.moveaxis(t, 1, 0) for t in (r, decay, k, v, kk, a))
    s0 = jnp.zeros((B, H, N, N), jnp.float32)
    _, ys = lax.scan(step, s0, xs)
    return jnp.moveaxis(ys, 0, 1)


def _rwkv7_mixer(p, mu, w0, w2, a0, a2, g2, k_k, k_a, r_k, ln_w, ln_b):
    B, S, _ = p.shape
    p_prev = jnp.pad(p, ((0, 0), (1, 0), (0, 0)))[:, :S]
    p = p + (p_prev - p) * mu
    r, k, v, wl, al, gl = _split(p, RW_SIZES)
    w = -jax.nn.softplus(-(w0 + jnp.tanh(wl) @ w2)) - 0.5
    decay = jnp.exp(-jnp.exp(w.astype(jnp.float32)))
    a = jax.nn.sigmoid(a0 + al @ a2)
    g = jax.nn.sigmoid(gl) @ g2

    def heads(t):
        return t.reshape(B, S, RW_HEADS, HEAD_DIM).astype(jnp.float32)

    r, k, v, a, decay = heads(r), heads(k), heads(v), heads(a), heads(decay)
    kk = k * k_k.reshape(RW_HEADS, HEAD_DIM).astype(jnp.float32)
    kk = kk / jnp.maximum(jnp.sqrt(jnp.sum(kk * kk, axis=-1, keepdims=True)), 1e-12)
    k = k * (1.0 + (a - 1.0) * k_a.reshape(RW_HEADS, HEAD_DIM).astype(jnp.float32))
    y = _rwkv7_scan(r, decay, k, v, kk, a)
    mu_y = jnp.mean(y, axis=-1, keepdims=True)
    yc = y - mu_y
    y = yc * lax.rsqrt(jnp.mean(yc * yc, axis=-1, keepdims=True) + RW_GN_EPS)
    y = y * ln_w.reshape(RW_HEADS, HEAD_DIM).astype(jnp.float32) + ln_b.reshape(RW_HEADS, HEAD_DIM).astype(jnp.float32)
    y = y + jnp.sum(r * k * r_k.astype(jnp.float32), axis=-1, keepdims=True) * v
    return (y.reshape(B, S, RW_WIDTH) * g.astype(jnp.float32)).astype(p.dtype)


def _gather_rows(t, idx):
    return jax.vmap(lambda tb, ib: tb[ib])(t, idx)


def _dsa_mixer(p, positions, q_norm_g, k_norm_g, idx_ln_w, idx_ln_b, out_g):
    B, S, _ = p.shape
    q, k, v, qi, ki, wi = _split(p, AT_SIZES)
    q = _rope(_rmsnorm(q.reshape(B, S, AT_HEADS, HEAD_DIM), q_norm_g), positions)
    k = _rope(_rmsnorm(k.reshape(B, S, AT_HEADS, HEAD_DIM), k_norm_g), positions)
    qi = _rope(qi.reshape(B, S, IDX_HEADS, IDX_DIM), positions)
    ki = _rope(_layernorm(ki, idx_ln_w, idx_ln_b), positions)
    wi = wi * (IDX_HEADS ** -0.5 * IDX_DIM ** -0.5)
    kf = k.reshape(B, S, AT_WIDTH)
    vf = v.reshape(B, S, AT_WIDTH)
    topk = min(IDX_TOPK_MAX, S // 4)
    key_pos = jnp.arange(S, dtype=jnp.int32)
    scale = HEAD_DIM ** -0.5

    def block(t0):
        qb = lax.dynamic_slice_in_dim(q, t0, QBLOCK, axis=1)
        qib = lax.dynamic_slice_in_dim(qi, t0, QBLOCK, axis=1)
        wib = lax.dynamic_slice_in_dim(wi, t0, QBLOCK, axis=1)
        qpos = t0 + jnp.arange(QBLOCK, dtype=jnp.int32)
        logits = jnp.einsum('bqhd,bsd->bqhs', qib, ki)
        score = jnp.einsum('bqh,bqhs->bqs', wib, jax.nn.relu(logits)).astype(jnp.float32)
        causal = key_pos[None, :] <= qpos[:, None]
        score = jnp.where(causal[None], score, -jnp.inf)
        _, sel = lax.top_k(score, topk)
        valid = sel <= qpos[None, :, None]
        ks = _gather_rows(kf, sel).reshape(B, QBLOCK, topk, AT_HEADS, HEAD_DIM)
        vs = _gather_rows(vf, sel).reshape(B, QBLOCK, topk, AT_HEADS, HEAD_DIM)
        s = jnp.einsum('bqhd,bqkhd->bqhk', qb, ks).astype(jnp.float32) * scale
        s = jnp.where(valid[:, :, None, :], s, -jnp.inf)
        pr = jax.nn.softmax(s, axis=-1)
        return jnp.einsum('bqhk,bqkhd->bqhd', pr.astype(vs.dtype), vs)

    starts = jnp.arange(S // QBLOCK, dtype=jnp.int32) * QBLOCK
    o = lax.map(block, starts)
    o = jnp.moveaxis(o, 0, 1).reshape(B, S, AT_HEADS, HEAD_DIM)
    o = _rmsnorm(o, out_g)
    return o.reshape(B, S, AT_WIDTH)


def _moe(h, router_w, router_bias, w1, w3, w2, sw1, sw3, sw2):
    T, D = h.shape
    scores = jax.nn.sigmoid(h.astype(jnp.float32) @ router_w.astype(jnp.float32))
    biased = scores + router_bias.astype(jnp.float32)
    per_group = N_EXPERTS // N_GROUPS
    grp_score = lax.top_k(biased.reshape(T, N_GROUPS, per_group), 2)[0].sum(-1)
    _, top_g = lax.top_k(grp_score, TOPK_GROUPS)
    gmask = jnp.zeros((T, N_GROUPS), bool).at[jnp.arange(T)[:, None], top_g].set(True)
    masked = jnp.where(jnp.repeat(gmask, per_group, axis=1), biased, -jnp.inf)
    _, eidx = lax.top_k(masked, TOP_K)
    gw = jnp.take_along_axis(scores, eidx, axis=1)
    gw = gw / jnp.sum(gw, axis=-1, keepdims=True) * ROUTED_SCALE

    A = T * TOP_K
    flat_e = eidx.reshape(A)
    order = jnp.argsort(flat_e)
    e_sorted = flat_e[order]
    counts = jnp.bincount(flat_e, length=N_EXPERTS)
    padded = (counts + EXPERT_BLOCK - 1) // EXPERT_BLOCK * EXPERT_BLOCK
    start = jnp.cumsum(counts) - counts
    pend = jnp.cumsum(padded)
    pstart = pend - padded
    dest = pstart[e_sorted] + jnp.arange(A, dtype=jnp.int32) - start[e_sorted]
    n_blocks = -(-(A + N_EXPERTS * (EXPERT_BLOCK - 1)) // EXPERT_BLOCK)
    P = n_blocks * EXPERT_BLOCK
    slot_tok = jnp.full((P,), T, jnp.int32).at[dest].set((order // TOP_K).astype(jnp.int32))
    slot_w = jnp.zeros((P,), jnp.float32).at[dest].set(gw.reshape(A)[order])
    block_e = jnp.searchsorted(pend, jnp.arange(n_blocks, dtype=jnp.int32) * EXPERT_BLOCK, side='right')
    block_e = jnp.minimum(block_e, N_EXPERTS - 1)
    h_pad = jnp.concatenate([h, jnp.zeros((1, D), h.dtype)], axis=0)

    def expert_block(args):
        tok, e = args
        return _swiglu(h_pad[tok], w1[e], w3[e], w2[e]).astype(jnp.float32)

    out = lax.map(expert_block, (slot_tok.reshape(n_blocks, EXPERT_BLOCK), block_e))
    routed = jnp.zeros((T + 1, D), jnp.float32).at[slot_tok].add(out.reshape(P, D) * slot_w[:, None])[:T]
    shared = _swiglu(h, sw1, sw3, sw2).astype(jnp.float32)
    return (routed + shared).astype(h.dtype)


def setup_inputs(seed: int = 0) -> dict:
    key = jax.random.key(seed)
    ks = iter(jax.random.split(key, 48))
    f32 = jnp.float32
    L = DEPTH

    def nrm(shape, scale):
        return jax.random.normal(next(ks), shape, f32) * scale

    x = nrm((BATCH, SEQ, D_MODEL), 1.0)
    c = nrm((BATCH, D_MODEL), 1.0)
    offset = jax.random.randint(next(ks), (BATCH, 1), 0, 4096, jnp.int32)
    positions = offset + jnp.arange(SEQ, dtype=jnp.int32)[None, :]
    return {
        "x": x,
        "c": c,
        "positions": positions,
        "w_ada": nrm((L, D_MODEL, 6 * D_MODEL), 0.5 * D_MODEL ** -0.5),
        "b_ada": nrm((L, 6 * D_MODEL), 0.02),
        "norm1_g": 1.0 + nrm((L, D_MODEL), 0.02),
        "norm2_g": 1.0 + nrm((L, D_MODEL), 0.02),
        "w_in": nrm((L, D_MODEL, IN_COLS), D_MODEL ** -0.5),
        "rw_mu": jax.random.uniform(next(ks), (L, RW_COLS), f32, 0.1, 0.9),
        "rw_w0": jax.random.uniform(next(ks), (L, RW_WIDTH), f32, -6.0, 0.0),
        "rw_w2": nrm((L, RW_LORA_W, RW_WIDTH), RW_LORA_W ** -0.5),
        "rw_a0": nrm((L, RW_WIDTH), 0.5),
        "rw_a2": nrm((L, RW_LORA_A, RW_WIDTH), RW_LORA_A ** -0.5),
        "rw_g2": nrm((L, RW_LORA_G, RW_WIDTH), RW_LORA_G ** -0.5),
        "rw_k_k": 0.85 + nrm((L, RW_WIDTH), 0.05),
        "rw_k_a": 1.0 + nrm((L, RW_WIDTH), 0.05),
        "rw_r_k": nrm((L, RW_HEADS, HEAD_DIM), 0.1),
        "rw_ln_w": 1.0 + nrm((L, RW_WIDTH), 0.02),
        "rw_ln_b": nrm((L, RW_WIDTH), 0.02),
        "q_norm_g": 1.0 + nrm((L, HEAD_DIM), 0.02),
        "k_norm_g": 1.0 + nrm((L, HEAD_DIM), 0.02),
        "idx_ln_w": 1.0 + nrm((L, IDX_DIM), 0.02),
        "idx_ln_b": nrm((L, IDX_DIM), 0.02),
        "at_out_g": 1.0 + nrm((L, AT_HEADS, HEAD_DIM), 0.02),
        "w_out": nrm((L, D_MIX, D_MODEL), D_MIX ** -0.5),
        "router_w": nrm((L, D_MODEL, N_EXPERTS), D_MODEL ** -0.5),
        "router_bias": nrm((L, N_EXPERTS), 0.01),
        "exp_w1": nrm((L, N_EXPERTS, D_MODEL, EXPERT_DIM), D_MODEL ** -0.5),
        "exp_w3": nrm((L, N_EXPERTS, D_MODEL, EXPERT_DIM), D_MODEL ** -0.5),
        "exp_w2": nrm((L, N_EXPERTS, EXPERT_DIM, D_MODEL), EXPERT_DIM ** -0.5),
        "shared_w1": nrm((L, D_MODEL, SHARED_DIM), D_MODEL ** -0.5),
        "shared_w3": nrm((L, D_MODEL, SHARED_DIM), D_MODEL ** -0.5),
        "shared_w2": nrm((L, SHARED_DIM, D_MODEL), SHARED_DIM ** -0.5),
    }


def reference(x, c, positions, w_ada, b_ada, norm1_g, norm2_g, w_in, rw_mu, rw_w0, rw_w2,
              rw_a0, rw_a2, rw_g2, rw_k_k, rw_k_a, rw_r_k, rw_ln_w, rw_ln_b, q_norm_g,
              k_norm_g, idx_ln_w, idx_ln_b, at_out_g, w_out, router_w, router_bias,
              exp_w1, exp_w3, exp_w2, shared_w1, shared_w3, shared_w2):
    B, S, D = x.shape
    c_act = jax.nn.silu(c)
    for l in range(DEPTH):
        mod = c_act @ w_ada[l] + b_ada[l]
        shift1, scale1, gate1, shift2, scale2, gate2 = [m[:, None, :] for m in jnp.split(mod, 6, axis=-1)]

        h = _rmsnorm(x, norm1_g[l]) * (1.0 + scale1) + shift1
        p = h @ w_in[l]
        p_rw, p_at = p[..., :RW_COLS], p[..., RW_COLS:]
        y_rw = _rwkv7_mixer(p_rw, rw_mu[l], rw_w0[l], rw_w2[l], rw_a0[l], rw_a2[l], rw_g2[l],
                            rw_k_k[l], rw_k_a[l], rw_r_k[l], rw_ln_w[l], rw_ln_b[l])
        y_at = _dsa_mixer(p_at, positions, q_norm_g[l], k_norm_g[l], idx_ln_w[l], idx_ln_b[l], at_out_g[l])
        mix = jnp.concatenate([y_rw.astype(x.dtype), y_at.astype(x.dtype)], axis=-1) @ w_out[l]
        x = x + gate1 * mix

        h2 = _rmsnorm(x, norm2_g[l]) * (1.0 + scale2) + shift2
        y = _moe(h2.reshape(B * S, D), router_w[l], router_bias[l], exp_w1[l], exp_w3[l], exp_w2[l],
                 shared_w1[l], shared_w3[l], shared_w2[l]).reshape(B, S, D)
        x = x + gate2 * y
    return x
```

```python
import functools

import jax
import jax.numpy as jnp
import numpy as np
from jax import lax
from jax.experimental import pallas as pl
from jax.experimental.pallas import tpu as pltpu

F32 = jnp.float32
BF16 = jnp.bfloat16
I32 = jnp.int32
HIGHEST = lax.Precision.HIGHEST

LANES = 128
HEAD_DIM = 64
HALF = HEAD_DIM // 2
RW_HEADS = 8
RW_WIDTH = RW_HEADS * HEAD_DIM
AT_HEADS = 8
AT_WIDTH = AT_HEADS * HEAD_DIM
IDX_HEADS = 4
RW_LORA_W, RW_LORA_A, RW_LORA_G = 64, 64, 128
RW_COLS = 3 * RW_WIDTH + RW_LORA_W + RW_LORA_A + RW_LORA_G
KI_OFF = 3 * AT_WIDTH + IDX_HEADS * HEAD_DIM
WI_OFF = KI_OFF + HEAD_DIM
PT_ROWS = 2 * AT_WIDTH + IDX_HEADS * HEAD_DIM + 8
PK_COLS = AT_WIDTH + LANES
ROPE_THETA = 10000.0
NORM_EPS = 1e-6
LN_EPS = 1e-6
RW_GN_EPS = 64e-5
IDX_TOPK_MAX = 256
N_EXPERTS = 256
TOP_K = 8
N_GROUPS = 8
TOPK_GROUPS = 4
ROUTED_SCALE = 2.5
INT_MIN = -2 ** 31
NEG_BIG = -1e30

RW_CHUNK = 128
QBLK = 128
EXP_BLK = 256
VMEM_LIMIT = 48 * 1024 * 1024


def _cparams(sem):
    return pltpu.CompilerParams(dimension_semantics=sem, vmem_limit_bytes=VMEM_LIMIT)


def _sigmoid(x):
    return 1.0 / (1.0 + jnp.exp(-x))


def _silu(x):
    return x * _sigmoid(x)


def _dot(a, b):
    return jnp.dot(a, b, preferred_element_type=F32)


def _dot_split(a, b):
    hi = a.astype(BF16)
    lo = (a - hi.astype(F32)).astype(BF16)
    return _dot(hi, b) + _dot(lo, b)


def _dot_nt(a, b):
    return lax.dot_general(a, b, (((1,), (1,)), ((), ())), preferred_element_type=F32)


def _mod_kernel(c_ref, w_ref, b_ref, o_ref):
    c = c_ref[...]
    o_ref[...] = jnp.dot(_silu(c), w_ref[...], precision=HIGHEST,
                         preferred_element_type=F32) + b_ref[...]


def _mod(c, w_ada, b_ada, interpret):
    B, D = c.shape
    n = w_ada.shape[1] // D
    return pl.pallas_call(
        _mod_kernel,
        grid=(n,),
        in_specs=[pl.BlockSpec((B, D), lambda i: (0, 0)),
                  pl.BlockSpec((D, D), lambda i: (0, i)),
                  pl.BlockSpec((1, D), lambda i: (0, i))],
        out_specs=pl.BlockSpec((B, D), lambda i: (0, i)),
        out_shape=jax.ShapeDtypeStruct((B, n * D), F32),
        compiler_params=_cparams(("arbitrary",)),
        name="mod", interpret=interpret,
    )(c, w_ada, b_ada.reshape(1, -1))


def _rope_tab_kernel(pc_ref, pr_ref, cr_ref, sr_ref, ct_ref, st_ref):
    log_theta = float(np.log(ROPE_THETA))
    lane = lax.broadcasted_iota(I32, (1, LANES), 1)
    inv_r = jnp.exp((lane % HALF).astype(F32) * (-log_theta / HALF))
    ang = pc_ref[0].astype(F32) * inv_r
    cr_ref[0] = jnp.cos(ang)
    sr_ref[0] = jnp.where((lane % HEAD_DIM) < HALF, -jnp.sin(ang), jnp.sin(ang))
    sub = lax.broadcasted_iota(I32, (HALF, 1), 0)
    inv_c = jnp.exp(sub.astype(F32) * (-log_theta / HALF))
    ang_t = inv_c * pr_ref[0].astype(F32)
    ct_ref[0] = jnp.cos(ang_t)
    st_ref[0] = jnp.sin(ang_t)


def _rope_tables(positions, interpret):
    B, S = positions.shape
    ts = min(S, 512)
    return pl.pallas_call(
        _rope_tab_kernel,
        grid=(B, S // ts),
        in_specs=[pl.BlockSpec((1, ts, 1), lambda b, i: (b, i, 0)),
                  pl.BlockSpec((1, 1, ts), lambda b, i: (b, 0, i))],
        out_specs=[pl.BlockSpec((1, ts, LANES), lambda b, i: (b, i, 0)),
                   pl.BlockSpec((1, ts, LANES), lambda b, i: (b, i, 0)),
                   pl.BlockSpec((1, HALF, ts), lambda b, i: (b, 0, i)),
                   pl.BlockSpec((1, HALF, ts), lambda b, i: (b, 0, i))],
        out_shape=[jax.ShapeDtypeStruct((B, S, LANES), F32),
                   jax.ShapeDtypeStruct((B, S, LANES), F32),
                   jax.ShapeDtypeStruct((B, HALF, S), F32),
                   jax.ShapeDtypeStruct((B, HALF, S), F32)],
        compiler_params=_cparams(("arbitrary", "arbitrary")),
        name="rope_tab", interpret=interpret,
    )(positions.reshape(B, S, 1), positions.reshape(B, 1, S))


def _rope_rows(y, cos, sin_signed):
    lane = lax.broadcasted_iota(I32, (1, LANES), 1)
    partner = jnp.where((lane % HEAD_DIM) < HALF,
                        pltpu.roll(y, LANES - HALF, 1), pltpu.roll(y, HALF, 1))
    return y * cos + partner * sin_signed


def _rope_cols(y, cos_t, sin_t):
    x1, x2 = y[:, :HALF], y[:, HALF:]
    return jnp.concatenate([x1 * cos_t - x2 * sin_t, x2 * cos_t + x1 * sin_t], axis=1)


def _inproj_kernel(x_ref, sc_ref, sh_ref, g_ref, wrw_ref, wk_ref, wt_ref, kg_ref, iw_ref, ib_ref,
                   qg_ref, gsum_ref, cr_ref, sr_ref, ct_ref, st_ref,
                   prw_ref, k_ref, ki_ref, qt_ref, vt_ref, qit_ref, wit_ref):
    tm = x_ref.shape[1]
    x = x_ref[0]
    ms = jnp.mean(x * x, axis=-1, keepdims=True)
    h = x * lax.rsqrt(ms + NORM_EPS) * g_ref[...] * (1.0 + sc_ref[0]) + sh_ref[0]
    hb = h.astype(BF16)
    prw_ref[0] = _dot(hb, wrw_ref[...])
    pk = _dot(hb, wk_ref[...])
    pt = _dot_nt(wt_ref[...], hb)

    cos_r, sin_r = cr_ref[0], sr_ref[0]
    gsum = gsum_ref[...]
    inv_hd = 1.0 / HEAD_DIM
    for p in range(AT_WIDTH // LANES):
        xk = pk[:, p * LANES:(p + 1) * LANES]
        ss = jnp.dot(xk * xk, gsum, precision=HIGHEST, preferred_element_type=F32)
        y = xk * lax.rsqrt(ss * inv_hd + NORM_EPS) * kg_ref[...]
        k_ref[0, :, p * LANES:(p + 1) * LANES] = _rope_rows(y, cos_r, sin_r).astype(BF16)
    xi = pk[:, AT_WIDTH:AT_WIDTH + LANES]
    mu = jnp.dot(xi, gsum, precision=HIGHEST, preferred_element_type=F32) * inv_hd
    xc = xi - mu
    var = jnp.dot(xc * xc, gsum, precision=HIGHEST, preferred_element_type=F32) * inv_hd
    yi = xc * lax.rsqrt(var + LN_EPS) * iw_ref[...] + ib_ref[...]
    ki_ref[0] = _rope_rows(yi, cos_r, sin_r).astype(BF16)

    cos_t, sin_t = ct_ref[0][None], st_ref[0][None]
    xq = pt[0:AT_WIDTH].reshape(AT_HEADS, HEAD_DIM, tm)
    msq = jnp.mean(xq * xq, axis=1, keepdims=True)
    yq = xq * lax.rsqrt(msq + NORM_EPS) * qg_ref[...][None]
    yq = _rope_cols(yq, cos_t, sin_t) * (HEAD_DIM ** -0.5)
    zq = jnp.zeros((HEAD_DIM, tm), BF16)
    for hh in range(AT_HEADS):
        parts = [yq[hh].astype(BF16), zq] if hh % 2 == 0 else [zq, yq[hh].astype(BF16)]
        qt_ref[0, hh] = jnp.concatenate(parts, axis=0)
    vt = pt[AT_WIDTH:2 * AT_WIDTH].astype(BF16)
    for cblk in range(tm // QBLK):
        vt_ref[0, cblk] = vt[:, cblk * QBLK:(cblk + 1) * QBLK]
    xqi = pt[2 * AT_WIDTH:2 * AT_WIDTH + IDX_HEADS * HEAD_DIM].reshape(IDX_HEADS, HEAD_DIM, tm)
    yqi = _rope_cols(xqi, cos_t, sin_t)
    for hh in range(IDX_HEADS):
        qit_ref[0, hh] = jnp.concatenate([yqi[hh].astype(BF16), zq], axis=0)
    wit_ref[0] = pt[PT_ROWS - 8:PT_ROWS] * (IDX_HEADS ** -0.5 * HEAD_DIM ** -0.5)


def _inproj(x, scale1, shift1, norm1_g, w_in, k_norm_g, idx_ln_w, idx_ln_b, q_norm_g,
            tabs, interpret):
    B, S, D = x.shape
    tm = min(S, 256)
    cos_r, sin_r, cos_t, sin_t = tabs
    w_at = w_in[:, RW_COLS:]
    w_rw = w_in[:, :RW_COLS].astype(BF16)
    w_k = jnp.concatenate([w_at[:, AT_WIDTH:2 * AT_WIDTH], w_at[:, KI_OFF:KI_OFF + HEAD_DIM],
                           jnp.zeros((D, HEAD_DIM), F32)], axis=1).astype(BF16)
    w_t = jnp.concatenate([w_at[:, 0:AT_WIDTH], w_at[:, 2 * AT_WIDTH:3 * AT_WIDTH],
                           w_at[:, 3 * AT_WIDTH:KI_OFF], w_at[:, WI_OFF:WI_OFF + IDX_HEADS],
                           jnp.zeros((D, 8 - IDX_HEADS), F32)], axis=1).T.astype(BF16)
    kg = jnp.tile(k_norm_g, 2).reshape(1, LANES)
    zpad = jnp.zeros((HEAD_DIM,), F32)
    iw = jnp.concatenate([idx_ln_w, zpad]).reshape(1, LANES)
    ib = jnp.concatenate([idx_ln_b, zpad]).reshape(1, LANES)
    qg = q_norm_g.reshape(HEAD_DIM, 1)
    li = np.arange(LANES)
    gsum = jnp.asarray((li[:, None] // HEAD_DIM == li[None, :] // HEAD_DIM).astype(np.float32))

    full = lambda shape: pl.BlockSpec(shape, lambda b, i: (0,) * len(shape))
    return pl.pallas_call(
        _inproj_kernel,
        grid=(B, S // tm),
        in_specs=[pl.BlockSpec((1, tm, D), lambda b, i: (b, i, 0)),
                  pl.BlockSpec((1, 1, D), lambda b, i: (b, 0, 0)),
                  pl.BlockSpec((1, 1, D), lambda b, i: (b, 0, 0)),
                  full((1, D)), full((D, RW_COLS)), full((D, PK_COLS)), full((PT_ROWS, D)),
                  full((1, LANES)), full((1, LANES)), full((1, LANES)), full((HEAD_DIM, 1)),
                  full((LANES, LANES)),
                  pl.BlockSpec((1, tm, LANES), lambda b, i: (b, i, 0)),
                  pl.BlockSpec((1, tm, LANES), lambda b, i: (b, i, 0)),
                  pl.BlockSpec((1, HALF, tm), lambda b, i: (b, 0, i)),
                  pl.BlockSpec((1, HALF, tm), lambda b, i: (b, 0, i))],
        out_specs=[pl.BlockSpec((1, tm, RW_COLS), lambda b, i: (b, i, 0)),
                   pl.BlockSpec((1, tm, AT_WIDTH), lambda b, i: (b, i, 0)),
                   pl.BlockSpec((1, tm, LANES), lambda b, i: (b, i, 0)),
                   pl.BlockSpec((1, AT_HEADS, LANES, tm), lambda b, i: (b, 0, 0, i)),
                   pl.BlockSpec((1, tm // QBLK, AT_WIDTH, QBLK), lambda b, i: (b, i, 0, 0)),
                   pl.BlockSpec((1, IDX_HEADS, LANES, tm), lambda b, i: (b, 0, 0, i)),
                   pl.BlockSpec((1, 8, tm), lambda b, i: (b, 0, i))],
        out_shape=[jax.ShapeDtypeStruct((B, S, RW_COLS), F32),
                   jax.ShapeDtypeStruct((B, S, AT_WIDTH), BF16),
                   jax.ShapeDtypeStruct((B, S, LANES), BF16),
                   jax.ShapeDtypeStruct((B, AT_HEADS, LANES, S), BF16),
                   jax.ShapeDtypeStruct((B, S // QBLK, AT_WIDTH, QBLK), BF16),
                   jax.ShapeDtypeStruct((B, IDX_HEADS, LANES, S), BF16),
                   jax.ShapeDtypeStruct((B, 8, S), F32)],
        compiler_params=_cparams(("arbitrary", "arbitrary")),
        name="inproj", interpret=interpret,
    )(x, scale1, shift1, norm1_g.reshape(1, D), w_rw, w_k, w_t, kg, iw, ib, qg, gsum,
      cos_r, sin_r, cos_t, sin_t)


def _rwkv_kernel(p_ref, mu_ref, w0_ref, w2_ref, a0_ref, a2_ref, g2_ref, kk_ref, ka_ref, rk_ref,
                 lnw_ref, lnb_ref, gsum_ref, y_ref, s_ref, prev_ref, yt_ref):
    C = RW_CHUNK
    W = RW_WIDTH

    @pl.when(pl.program_id(1) == 0)
    def _():
        s_ref[...] = jnp.zeros_like(s_ref)
        prev_ref[...] = jnp.zeros_like(prev_ref)

    p = p_ref[0]
    row = lax.broadcasted_iota(I32, (C, 1), 0)
    pprev = jnp.where(row == 0, prev_ref[...], pltpu.roll(p, 1, 0))
    prev_ref[...] = p[C - 1:C]
    ps = p + (pprev - p) * mu_ref[...]
    r, k, v = ps[:, 0:W], ps[:, W:2 * W], ps[:, 2 * W:3 * W]
    o = 3 * W
    wl = ps[:, o:o + RW_LORA_W]
    al = ps[:, o + RW_LORA_W:o + RW_LORA_W + RW_LORA_A]
    gl = ps[:, o + RW_LORA_W + RW_LORA_A:]

    z = w0_ref[...] + _dot(jnp.tanh(wl).astype(BF16), w2_ref[...])
    nz = -z
    softplus = jnp.maximum(nz, 0.0) + jnp.log(1.0 + jnp.exp(-jnp.abs(nz)))
    logw = -jnp.exp(-softplus - 0.5)
    a = _sigmoid(a0_ref[...] + _dot(al.astype(BF16), a2_ref[...]))
    g = _dot(_sigmoid(gl).astype(BF16), g2_ref[...])
    gsum = gsum_ref[...]
    kk = k * kk_ref[...]
    ss = _dot((kk * kk).astype(BF16), gsum)
    kk = kk * (1.0 / jnp.maximum(jnp.sqrt(ss), 1e-12))
    k2 = k * (1.0 + (a - 1.0) * ka_ref[...])
    bb = kk * a

    cw = logw
    sh = 1
    while sh < C:
        cw = cw + jnp.where(row >= sh, pltpu.roll(cw, sh, 0), 0.0)
        sh *= 2
    cw_last = cw[C - 1:C]
    e_neg = jnp.exp(-cw)
    e_end = jnp.exp(cw_last - cw)
    rw = r * jnp.exp(cw)
    kkp = kk * jnp.exp(cw - logw)
    bw, kw = bb * e_neg, k2 * e_neg
    bend, kend = bb * e_end, k2 * e_end
    wc = jnp.exp(cw_last)
    vt_all = v.T.astype(BF16)

    ri = lax.broadcasted_iota(I32, (C, C), 0)
    ci = lax.broadcasted_iota(I32, (C, C), 1)
    strict = ri < ci
    incl = ri <= ci
    incl2 = jnp.concatenate([incl, incl], axis=0)
    lane_half = lax.broadcasted_iota(I32, (1, LANES), 1) // HEAD_DIM

    for h in range(RW_HEADS):
        pr = slice((h // 2) * LANES, (h // 2 + 1) * LANES)
        own = lane_half == (h % 2)
        lh = jnp.concatenate([kkp[:, pr], rw[:, pr]], axis=0).astype(BF16)
        rh = jnp.where(own, jnp.concatenate([bw[:, pr], kw[:, pr]], axis=0), 0.0).astype(BF16)
        aat = _dot_nt(rh, lh)
        s_h = s_ref[h]
        sl = _dot_nt(s_h.astype(BF16), lh)
        vt = vt_all[h * HEAD_DIM:(h + 1) * HEAD_DIM]
        akt = jnp.where(strict, aat[C:, :C], 0.0).astype(BF16)
        nt = jnp.where(strict, aat[:C, :C], 0.0).astype(BF16)
        xs = -(sl[:, :C] + _dot(vt, akt))
        xs = xs - _dot_split(xs, nt)
        m = nt
        lvl = 2
        while lvl < C:
            m = _dot(m, m).astype(BF16)
            xs = xs + _dot_split(xs, m)
            lvl *= 2
        zt = jnp.concatenate([xs.astype(BF16), vt], axis=1)
        ymat = jnp.where(incl2, aat[:, C:], 0.0).astype(BF16)
        yt_ref[h * HEAD_DIM:(h + 1) * HEAD_DIM, :] = sl[:, C:] + _dot(zt, ymat)
        endz = jnp.where(own, jnp.concatenate([bend[:, pr], kend[:, pr]], axis=0), 0.0).astype(BF16)
        s_ref[h] = s_h * wc[:, pr] + _dot(zt, endz)

    yt = yt_ref[...].reshape(RW_HEADS, HEAD_DIM, C)
    mean = jnp.mean(yt, axis=1, keepdims=True)
    yc = yt - mean
    var = jnp.mean(yc * yc, axis=1, keepdims=True)
    lnw = lnw_ref[...].reshape(RW_HEADS, HEAD_DIM, 1)
    lnb = lnb_ref[...].reshape(RW_HEADS, HEAD_DIM, 1)
    yn = yc * lax.rsqrt(var + RW_GN_EPS) * lnw + lnb
    y = yn.reshape(W, C).T
    bonus = _dot((r * k2 * rk_ref[...]).astype(BF16), gsum) * v
    y_ref[0] = ((y + bonus) * g).astype(BF16)


def _rwkv(p_rw, rw_mu, rw_w0, rw_w2, rw_a0, rw_a2, rw_g2, rw_k_k, rw_k_a, rw_r_k, rw_ln_w, rw_ln_b,
          interpret):
    B, S, _ = p_rw.shape
    C, W = RW_CHUNK, RW_WIDTH
    li = np.arange(W)
    gsum = jnp.asarray((li[:, None] // HEAD_DIM == li[None, :] // HEAD_DIM).astype(np.float32)).astype(BF16)
    row = lambda a: a.reshape(1, -1)
    full = lambda shape: pl.BlockSpec(shape, lambda b, i: (0,) * len(shape))
    return pl.pallas_call(
        _rwkv_kernel,
        grid=(B, S // C),
        in_specs=[pl.BlockSpec((1, C, RW_COLS), lambda b, i: (b, i, 0)),
                  full((1, RW_COLS)), full((1, W)), full((RW_LORA_W, W)), full((1, W)),
                  full((RW_LORA_A, W)), full((RW_LORA_G, W)), full((1, W)), full((1, W)),
                  full((1, W)), full((W, 1)), full((W, 1)), full((W, W))],
        out_specs=pl.BlockSpec((1, C, W), lambda b, i: (b, i, 0)),
        out_shape=jax.ShapeDtypeStruct((B, S, W), BF16),
        scratch_shapes=[pltpu.VMEM((RW_HEADS, HEAD_DIM, LANES), F32),
                        pltpu.VMEM((1, RW_COLS), F32),
                        pltpu.VMEM((W, C), F32)],
        compiler_params=_cparams(("arbitrary", "arbitrary")),
        name="rwkv", interpret=interpret,
    )(p_rw, row(rw_mu), row(rw_w0), rw_w2.astype(BF16), row(rw_a0), rw_a2.astype(BF16),
      rw_g2.astype(BF16), row(rw_k_k), row(rw_k_a), row(rw_r_k), rw_ln_w.reshape(W, 1),
      rw_ln_b.reshape(W, 1), gsum)


def _dsa_kernel(topk, nbits, k_ref, vt_ref, ki_ref, qt_ref, qit_ref, wit_ref, og_ref, o_ref,
                key_s, acc_s, m_s, l_s, thr_s):
    j = pl.program_id(1)
    nkb = j + 1
    lane = lax.broadcasted_iota(I32, (QBLK, QBLK), 1)
    sub = lax.broadcasted_iota(I32, (QBLK, QBLK), 0)
    qpos = j * QBLK + lane
    wit = wit_ref[0]

    def score_block(kb, carry):
        kib = ki_ref[0, pl.ds(pl.multiple_of(kb * QBLK, QBLK), QBLK), :]
        s = jnp.zeros((QBLK, QBLK), F32)
        for hh in range(IDX_HEADS):
            s = s + wit[hh:hh + 1, :] * jnp.maximum(_dot(kib, qit_ref[0, hh]), 0.0)
        s = jnp.where(s == 0.0, 0.0, s)
        bits = pltpu.bitcast(s, I32)
        skey = jnp.where(bits < 0, bits ^ 0x7FFFFFFF, bits)
        key_s[kb] = jnp.where(kb * QBLK + sub <= qpos, skey, INT_MIN)
        return carry

    lax.fori_loop(0, nkb, score_block, 0)

    def count(pred):
        def body(kb, acc):
            hit = pred(key_s[kb], kb * QBLK + sub).astype(I32)
            return acc + jnp.sum(hit.reshape(QBLK // 8, 8, QBLK), axis=0)
        acc = lax.fori_loop(0, nkb, body, jnp.zeros((8, QBLK), I32))
        return jnp.sum(acc, axis=0, keepdims=True)

    @pl.when(nkb * QBLK <= topk)
    def _():
        thr_s[0:1, :] = jnp.full((1, QBLK), INT_MIN, I32)
        thr_s[1:2, :] = jnp.zeros((1, QBLK), I32)

    @pl.when(nkb * QBLK > topk)
    def _():
        c0 = count(lambda ky, ix: ky >= 0)
        t0 = jnp.where(c0 >= topk, 0, INT_MIN).astype(I32)

        def bit_step(i, t):
            cand = t | jnp.left_shift(jnp.int32(1), 30 - i)
            c = count(lambda ky, ix: ky >= cand)
            return jnp.where(c >= topk, cand, t)

        thr = lax.fori_loop(0, 31, bit_step, t0)
        need = topk - count(lambda ky, ix: ky > thr)

        def idx_step(i, mm):
            cand = mm | jnp.left_shift(jnp.int32(1), nbits - 1 - i)
            c = count(lambda ky, ix: (ky == thr) & (ix < cand))
            return jnp.where(c < need, cand, mm)

        mm = lax.fori_loop(0, nbits, idx_step, jnp.zeros((1, QBLK), I32))
        thr_s[0:1, :] = thr
        thr_s[1:2, :] = mm

    thr = thr_s[0:1, :]
    mm = thr_s[1:2, :]
    m_s[...] = jnp.full_like(m_s, NEG_BIG)
    l_s[...] = jnp.zeros_like(l_s)
    acc_s[...] = jnp.zeros_like(acc_s)

    def attn_block(kb, carry):
        skey = key_s[kb]
        kidx = kb * QBLK + sub
        sel = (kidx <= qpos) & ((skey > thr) | ((skey == thr) & (kidx <= mm)))
        kblk = k_ref[0, pl.ds(pl.multiple_of(kb * QBLK, QBLK), QBLK), :]
        vtb = vt_ref[0, kb]
        for hh in range(AT_HEADS):
            pr = slice((hh // 2) * LANES, (hh // 2 + 1) * LANES)
            hs = slice(hh * HEAD_DIM, (hh + 1) * HEAD_DIM)
            s = jnp.where(sel, _dot(kblk[:, pr], qt_ref[0, hh]), NEG_BIG)
            m_old = m_s[hh:hh + 1, :]
            m_new = jnp.maximum(m_old, jnp.max(s, axis=0, keepdims=True))
            pexp = jnp.where(sel, jnp.exp(s - m_new), 0.0)
            alpha = jnp.exp(m_old - m_new)
            l_s[hh:hh + 1, :] = alpha * l_s[hh:hh + 1, :] + jnp.sum(pexp, axis=0, keepdims=True)
            acc_s[hs, :] = alpha * acc_s[hs, :] + _dot(vtb[hs, :], pexp.astype(BF16))
            m_s[hh:hh + 1, :] = m_new
        return carry

    lax.fori_loop(0, nkb, attn_block, 0)

    for hh in range(AT_HEADS):
        hs = slice(hh * HEAD_DIM, (hh + 1) * HEAD_DIM)
        oh = acc_s[hs, :] * (1.0 / l_s[hh:hh + 1, :])
        ms = jnp.mean(oh * oh, axis=0, keepdims=True)
        acc_s[hs, :] = oh * lax.rsqrt(ms + NORM_EPS) * og_ref[hs, :]
    o_ref[0] = acc_s[...].T.astype(BF16)


def _dsa(k, vt, ki, qt, qit, wit, at_out_g, interpret):
    B, S, _ = k.shape
    nq = S // QBLK
    topk = min(IDX_TOPK_MAX, S // 4)
    nbits = int(np.log2(S))
    assert 2 ** nbits == S
    return pl.pallas_call(
        functools.partial(_dsa_kernel, topk, nbits),
        grid=(B, nq),
        in_specs=[pl.BlockSpec((1, S, AT_WIDTH), lambda b, j: (b, 0, 0)),
                  pl.BlockSpec((1, nq, AT_WIDTH, QBLK), lambda b, j: (b, 0, 0, 0)),
                  pl.BlockSpec((1, S, LANES), lambda b, j: (b, 0, 0)),
                  pl.BlockSpec((1, AT_HEADS, LANES, QBLK), lambda b, j: (b, 0, 0, j)),
                  pl.BlockSpec((1, IDX_HEADS, LANES, QBLK), lambda b, j: (b, 0, 0, j)),
                  pl.BlockSpec((1, 8, QBLK), lambda b, j: (b, 0, j)),
                  pl.BlockSpec((AT_WIDTH, 1), lambda b, j: (0, 0))],
        out_specs=pl.BlockSpec((1, QBLK, AT_WIDTH), lambda b, j: (b, j, 0)),
        out_shape=jax.ShapeDtypeStruct((B, S, AT_WIDTH), BF16),
        scratch_shapes=[pltpu.VMEM((nq, QBLK, QBLK), I32),
                        pltpu.VMEM((AT_WIDTH, QBLK), F32),
                        pltpu.VMEM((AT_HEADS, QBLK), F32),
                        pltpu.VMEM((AT_HEADS, QBLK), F32),
                        pltpu.VMEM((8, QBLK), I32)],
        compiler_params=_cparams(("arbitrary", "arbitrary")),
        name="dsa", interpret=interpret,
    )(k, vt, ki, qt, qit, wit, at_out_g.reshape(AT_WIDTH, 1))


def _post_kernel(x_ref, yrw_ref, yat_ref, g1_ref, sc_ref, sh_ref, g2_ref, ng_ref, wo_ref, rw_ref,
                 s1_ref, s3_ref, s2_ref, base_ref, h2_ref, logit_ref):
    W = RW_WIDTH
    mix = _dot(yrw_ref[0], wo_ref[0:W, :]) + _dot(yat_ref[0], wo_ref[W:, :])
    x1 = x_ref[0] + g1_ref[0] * mix
    ms = jnp.mean(x1 * x1, axis=-1, keepdims=True)
    h2 = x1 * lax.rsqrt(ms + NORM_EPS) * ng_ref[...] * (1.0 + sc_ref[0]) + sh_ref[0]
    hb = h2.astype(BF16)
    h2_ref[0] = hb
    logit_ref[0] = jnp.dot(h2, rw_ref[...], precision=HIGHEST, preferred_element_type=F32)
    act = (_silu(_dot(hb, s1_ref[...])) * _dot(hb, s3_ref[...])).astype(BF16)
    base_ref[0] = x1 + g2_ref[0] * _dot(act, s2_ref[...])


def _post(x, y_rw, y_at, gate1, scale2, shift2, gate2, norm2_g, w_out, router_w, sw1, sw3, sw2,
          interpret):
    B, S, D = x.shape
    tm = min(S, 256)
    E = router_w.shape[1]
    sd = sw1.shape[1]
    full = lambda shape: pl.BlockSpec(shape, lambda b, i: (0,) * len(shape))
    tok = lambda w: pl.BlockSpec((1, tm, w), lambda b, i: (b, i, 0))
    per_b = pl.BlockSpec((1, 1, D), lambda b, i: (b, 0, 0))
    return pl.pallas_call(
        _post_kernel,
        grid=(B, S // tm),
        in_specs=[tok(D), tok(RW_WIDTH), tok(AT_WIDTH), per_b, per_b, per_b, per_b, full((1, D)),
                  full((D, D)), full((D, E)), full((D, sd)), full((D, sd)), full((sd, D))],
        out_specs=[tok(D), tok(D), tok(E)],
        out_shape=[jax.ShapeDtypeStruct((B, S, D), F32),
                   jax.ShapeDtypeStruct((B, S, D), BF16),
                   jax.ShapeDtypeStruct((B, S, E), F32)],
        compiler_params=_cparams(("arbitrary", "arbitrary")),
        name="post", interpret=interpret,
    )(x, y_rw, y_at, gate1, scale2, shift2, gate2, norm2_g.reshape(1, D), w_out.astype(BF16),
      router_w, sw1.astype(BF16), sw3.astype(BF16), sw2.astype(BF16))


def _expert_kernel(be_ref, xs_ref, w1_ref, w3_ref, w2_ref, sw_ref, o_ref):
    del be_ref
    x = xs_ref[...]
    h1 = _dot(x, w1_ref[0].astype(BF16))
    h3 = _dot(x, w3_ref[0].astype(BF16))
    act = (_silu(h1) * h3).astype(BF16)
    o_ref[...] = _dot(act, w2_ref[0].astype(BF16)) * sw_ref[...]


def _experts(xs, block_e, slot_w, w1, w3, w2, interpret):
    P, D = xs.shape
    E, _, F = w1.shape
    nb = P // EXP_BLK
    grid_spec = pltpu.PrefetchScalarGridSpec(
        num_scalar_prefetch=1,
        grid=(nb,),
        in_specs=[pl.BlockSpec((EXP_BLK, D), lambda i, be: (i, 0)),
                  pl.BlockSpec((1, D, F), lambda i, be: (be[i], 0, 0)),
                  pl.BlockSpec((1, D, F), lambda i, be: (be[i], 0, 0)),
                  pl.BlockSpec((1, F, D), lambda i, be: (be[i], 0, 0)),
                  pl.BlockSpec((EXP_BLK, 1), lambda i, be: (i, 0))],
        out_specs=pl.BlockSpec((EXP_BLK, D), lambda i, be: (i, 0)),
    )
    return pl.pallas_call(
        _expert_kernel,
        grid_spec=grid_spec,
        out_shape=jax.ShapeDtypeStruct((P, D), F32),
        compiler_params=_cparams(("arbitrary",)),
        name="experts", interpret=interpret,
    )(block_e, xs, w1, w3, w2, slot_w.reshape(P, 1))


def _route(logits, router_bias):
    T, E = logits.shape
    scores = jax.nn.sigmoid(logits)
    biased = scores + router_bias.astype(F32)
    per_group = E // N_GROUPS
    grp_score = lax.top_k(biased.reshape(T, N_GROUPS, per_group), 2)[0].sum(-1)
    _, top_g = lax.top_k(grp_score, TOPK_GROUPS)
    gmask = jnp.zeros((T, N_GROUPS), bool).at[jnp.arange(T)[:, None], top_g].set(True)
    masked = jnp.where(jnp.repeat(gmask, per_group, axis=1), biased, -jnp.inf)
    _, eidx = lax.top_k(masked, TOP_K)
    gw = jnp.take_along_axis(scores, eidx, axis=1)
    gw = gw / jnp.sum(gw, axis=-1, keepdims=True) * ROUTED_SCALE
    return eidx.astype(I32), gw


def _dispatch_plan(eidx, gw):
    T = eidx.shape[0]
    A = T * TOP_K
    flat_e = eidx.reshape(A)
    order = jnp.argsort(flat_e)
    e_sorted = flat_e[order]
    counts = jnp.bincount(flat_e, length=N_EXPERTS)
    padded = (counts + EXP_BLK - 1) // EXP_BLK * EXP_BLK
    start = jnp.cumsum(counts) - counts
    pend = jnp.cumsum(padded)
    pstart = pend - padded
    dest_sorted = (pstart[e_sorted] + jnp.arange(A, dtype=I32) - start[e_sorted]).astype(I32)
    nb = -(-(A + N_EXPERTS * (EXP_BLK - 1)) // EXP_BLK)
    P = nb * EXP_BLK
    slot_tok = jnp.full((P,), T, I32).at[dest_sorted].set((order // TOP_K).astype(I32))
    slot_w = jnp.zeros((P,), F32).at[dest_sorted].set(gw.reshape(A)[order])
    block_e = jnp.searchsorted(pend, jnp.arange(nb, dtype=I32) * EXP_BLK, side='right')
    block_e = jnp.minimum(block_e, N_EXPERTS - 1).astype(I32)
    dest = jnp.zeros((A,), I32).at[order].set(dest_sorted).reshape(T, TOP_K)
    return slot_tok, slot_w, block_e, dest


def _forward(x, c, positions, w_ada, b_ada, norm1_g, norm2_g, w_in, rw_mu, rw_w0, rw_w2,
             rw_a0, rw_a2, rw_g2, rw_k_k, rw_k_a, rw_r_k, rw_ln_w, rw_ln_b, q_norm_g,
             k_norm_g, idx_ln_w, idx_ln_b, at_out_g, w_out, router_w, router_bias,
             exp_w1, exp_w3, exp_w2, shared_w1, shared_w3, shared_w2, interpret=False):
    B, S, D = x.shape
    depth = w_ada.shape[0]
    for l in range(depth):
        mod = _mod(c, w_ada[l], b_ada[l], interpret)
        shift1, scale1, gate1, shift2, scale2, gate2 = [
            m.reshape(B, 1, D) for m in jnp.split(mod, 6, axis=-1)]
        tabs = _rope_tables(positions, interpret)
        p_rw, k, ki, qt, vt, qit, wit = _inproj(
            x, scale1, shift1, norm1_g[l], w_in[l], k_norm_g[l], idx_ln_w[l], idx_ln_b[l],
            q_norm_g[l], tabs, interpret)
        y_rw = _rwkv(p_rw, rw_mu[l], rw_w0[l], rw_w2[l], rw_a0[l], rw_a2[l], rw_g2[l], rw_k_k[l],
                     rw_k_a[l], rw_r_k[l], rw_ln_w[l], rw_ln_b[l], interpret)
        y_at = _dsa(k, vt, ki, qt, qit, wit, at_out_g[l], interpret)
        base, h2, logits = _post(x, y_rw, y_at, gate1, scale2, shift2, gate2, norm2_g[l], w_out[l],
                                 router_w[l], shared_w1[l], shared_w3[l], shared_w2[l], interpret)
        T = B * S
        eidx, gw = _route(logits.reshape(T, -1), router_bias[l])
        slot_tok, slot_w, block_e, dest = _dispatch_plan(eidx, gw)
        h2_pad = jnp.concatenate([h2.reshape(T, D), jnp.zeros((1, D), BF16)], axis=0)
        xs = h2_pad[slot_tok]
        ys = _experts(xs, block_e, slot_w, exp_w1[l], exp_w3[l], exp_w2[l], interpret)
        routed = jnp.sum(ys[dest], axis=1).reshape(B, S, D)
        x = base + gate2 * routed
    return x


def kernel(x, c, positions, w_ada, b_ada, norm1_g, norm2_g, w_in, rw_mu, rw_w0, rw_w2, rw_a0, rw_a2, rw_g2, rw_k_k, rw_k_a, rw_r_k, rw_ln_w, rw_ln_b, q_norm_g, k_norm_g, idx_ln_w, idx_ln_b, at_out_g, w_out, router_w, router_bias, exp_w1, exp_w3, exp_w2, shared_w1, shared_w3, shared_w2):
    return _forward(x, c, positions, w_ada, b_ada, norm1_g, norm2_g, w_in, rw_mu, rw_w0, rw_w2,
                    rw_a0, rw_a2, rw_g2, rw_k_k, rw_k_a, rw_r_k, rw_ln_w, rw_ln_b, q_norm_g,
                    k_norm_g, idx_ln_w, idx_ln_b, at_out_g, w_out, router_w, router_bias,
                    exp_w1, exp_w3, exp_w2, shared_w1, shared_w3, shared_w2)
```

```python
import functools

import jax
import jax.numpy as jnp
import numpy as np
from jax import lax
from jax.experimental import pallas as pl
from jax.experimental.pallas import tpu as pltpu

F32 = jnp.float32
BF16 = jnp.bfloat16
I32 = jnp.int32
HIGHEST = lax.Precision.HIGHEST

LANES = 128
HEAD_DIM = 64
HALF = HEAD_DIM // 2
RW_HEADS = 8
RW_WIDTH = RW_HEADS * HEAD_DIM
AT_HEADS = 8
AT_WIDTH = AT_HEADS * HEAD_DIM
IDX_HEADS = 4
RW_LORA_W, RW_LORA_A, RW_LORA_G = 64, 64, 128
RW_COLS = 3 * RW_WIDTH + RW_LORA_W + RW_LORA_A + RW_LORA_G
KI_OFF = 3 * AT_WIDTH + IDX_HEADS * HEAD_DIM
WI_OFF = KI_OFF + HEAD_DIM
PT_ROWS = 2 * AT_WIDTH + IDX_HEADS * HEAD_DIM + 8
PK_COLS = AT_WIDTH + LANES
ROPE_THETA = 10000.0
NORM_EPS = 1e-6
LN_EPS = 1e-6
RW_GN_EPS = 64e-5
IDX_TOPK_MAX = 256
N_EXPERTS = 256
TOP_K = 8
N_GROUPS = 8
TOPK_GROUPS = 4
ROUTED_SCALE = 2.5
INT_MIN = -2 ** 31
NEG_BIG = -1e30

RW_CHUNK = 128
QBLK = 128
EXP_BLK = 256
ROW_TILE = 256
VMEM_LIMIT = 48 * 1024 * 1024


def _cparams(sem):
    return pltpu.CompilerParams(dimension_semantics=sem, vmem_limit_bytes=VMEM_LIMIT)


def _sigmoid(x):
    return 1.0 / (1.0 + jnp.exp(-x))


def _silu(x):
    return x * _sigmoid(x)


def _dot(a, b):
    return jnp.dot(a, b, preferred_element_type=F32)


def _dot_split(a, b):
    hi = a.astype(BF16)
    lo = (a - hi.astype(F32)).astype(BF16)
    return _dot(hi, b) + _dot(lo, b)


def _dot_nt(a, b):
    return lax.dot_general(a, b, (((1,), (1,)), ((), ())), preferred_element_type=F32)


def _mod_kernel(c_ref, w_ref, b_ref, o_ref):
    c = c_ref[...]
    o_ref[...] = jnp.dot(_silu(c), w_ref[...], precision=HIGHEST,
                         preferred_element_type=F32) + b_ref[...]


def _mod(c, w_ada, b_ada, interpret):
    B, D = c.shape
    n = w_ada.shape[1] // D
    return pl.pallas_call(
        _mod_kernel,
        grid=(n,),
        in_specs=[pl.BlockSpec((B, D), lambda i: (0, 0)),
                  pl.BlockSpec((D, D), lambda i: (0, i)),
                  pl.BlockSpec((1, D), lambda i: (0, i))],
        out_specs=pl.BlockSpec((B, D), lambda i: (0, i)),
        out_shape=jax.ShapeDtypeStruct((B, n * D), F32),
        compiler_params=_cparams(("arbitrary",)),
        name="mod", interpret=interpret,
    )(c, w_ada, b_ada.reshape(1, -1))


def _rope_tab_kernel(pc_ref, pr_ref, cr_ref, sr_ref, ct_ref, st_ref):
    log_theta = float(np.log(ROPE_THETA))
    lane = lax.broadcasted_iota(I32, (1, LANES), 1)
    inv_r = jnp.exp((lane % HALF).astype(F32) * (-log_theta / HALF))
    ang = pc_ref[0].astype(F32) * inv_r
    cr_ref[0] = jnp.cos(ang)
    sr_ref[0] = jnp.where((lane % HEAD_DIM) < HALF, -jnp.sin(ang), jnp.sin(ang))
    sub = lax.broadcasted_iota(I32, (HALF, 1), 0)
    inv_c = jnp.exp(sub.astype(F32) * (-log_theta / HALF))
    ang_t = inv_c * pr_ref[0].astype(F32)
    ct_ref[0] = jnp.cos(ang_t)
    st_ref[0] = jnp.sin(ang_t)


def _rope_tables(positions, interpret):
    B, S = positions.shape
    ts = min(S, 512)
    return pl.pallas_call(
        _rope_tab_kernel,
        grid=(B, S // ts),
        in_specs=[pl.BlockSpec((1, ts, 1), lambda b, i: (b, i, 0)),
                  pl.BlockSpec((1, 1, ts), lambda b, i: (b, 0, i))],
        out_specs=[pl.BlockSpec((1, ts, LANES), lambda b, i: (b, i, 0)),
                   pl.BlockSpec((1, ts, LANES), lambda b, i: (b, i, 0)),
                   pl.BlockSpec((1, HALF, ts), lambda b, i: (b, 0, i)),
                   pl.BlockSpec((1, HALF, ts), lambda b, i: (b, 0, i))],
        out_shape=[jax.ShapeDtypeStruct((B, S, LANES), F32),
                   jax.ShapeDtypeStruct((B, S, LANES), F32),
                   jax.ShapeDtypeStruct((B, HALF, S), F32),
                   jax.ShapeDtypeStruct((B, HALF, S), F32)],
        compiler_params=_cparams(("arbitrary", "arbitrary")),
        name="rope_tab", interpret=interpret,
    )(positions.reshape(B, S, 1), positions.reshape(B, 1, S))


def _rope_rows(y, cos, sin_signed):
    lane = lax.broadcasted_iota(I32, (1, LANES), 1)
    partner = jnp.where((lane % HEAD_DIM) < HALF,
                        pltpu.roll(y, LANES - HALF, 1), pltpu.roll(y, HALF, 1))
    return y * cos + partner * sin_signed


def _rope_cols(y, cos_t, sin_t):
    x1, x2 = y[:, :HALF], y[:, HALF:]
    return jnp.concatenate([x1 * cos_t - x2 * sin_t, x2 * cos_t + x1 * sin_t], axis=1)


def _inproj_kernel(x_ref, sc_ref, sh_ref, g_ref, wrw_ref, wk_ref, wt_ref, kg_ref, iw_ref, ib_ref,
                   qg_ref, gsum_ref, cr_ref, sr_ref, ct_ref, st_ref,
                   prw_ref, k_ref, ki_ref, qt_ref, vt_ref, qit_ref, wit_ref):
    tm = x_ref.shape[1]
    x = x_ref[0]
    ms = jnp.mean(x * x, axis=-1, keepdims=True)
    h = x * lax.rsqrt(ms + NORM_EPS) * g_ref[...] * (1.0 + sc_ref[0]) + sh_ref[0]
    hb = h.astype(BF16)
    prw_ref[0] = _dot(hb, wrw_ref[...])
    pk = _dot(hb, wk_ref[...])
    pt = _dot_nt(wt_ref[...], hb)

    cos_r, sin_r = cr_ref[0], sr_ref[0]
    gsum = gsum_ref[...]
    inv_hd = 1.0 / HEAD_DIM
    for p in range(AT_WIDTH // LANES):
        xk = pk[:, p * LANES:(p + 1) * LANES]
        ss = jnp.dot(xk * xk, gsum, precision=HIGHEST, preferred_element_type=F32)
        y = xk * lax.rsqrt(ss * inv_hd + NORM_EPS) * kg_ref[...]
        k_ref[0, :, p * LANES:(p + 1) * LANES] = _rope_rows(y, cos_r, sin_r).astype(BF16)
    xi = pk[:, AT_WIDTH:AT_WIDTH + LANES]
    mu = jnp.dot(xi, gsum, precision=HIGHEST, preferred_element_type=F32) * inv_hd
    xc = xi - mu
    var = jnp.dot(xc * xc, gsum, precision=HIGHEST, preferred_element_type=F32) * inv_hd
    yi = xc * lax.rsqrt(var + LN_EPS) * iw_ref[...] + ib_ref[...]
    ki_ref[0] = _rope_rows(yi, cos_r, sin_r).astype(BF16)

    cos_t, sin_t = ct_ref[0][None], st_ref[0][None]
    xq = pt[0:AT_WIDTH].reshape(AT_HEADS, HEAD_DIM, tm)
    msq = jnp.mean(xq * xq, axis=1, keepdims=True)
    yq = xq * lax.rsqrt(msq + NORM_EPS) * qg_ref[...][None]
    yq = _rope_cols(yq, cos_t, sin_t) * (HEAD_DIM ** -0.5)
    zq = jnp.zeros((HEAD_DIM, tm), BF16)
    for hh in range(AT_HEADS):
        parts = [yq[hh].astype(BF16), zq] if hh % 2 == 0 else [zq, yq[hh].astype(BF16)]
        qt_ref[0, hh] = jnp.concatenate(parts, axis=0)
    vt = pt[AT_WIDTH:2 * AT_WIDTH].astype(BF16)
    for cblk in range(tm // QBLK):
        vt_ref[0, cblk] = vt[:, cblk * QBLK:(cblk + 1) * QBLK]
    xqi = pt[2 * AT_WIDTH:2 * AT_WIDTH + IDX_HEADS * HEAD_DIM].reshape(IDX_HEADS, HEAD_DIM, tm)
    yqi = _rope_cols(xqi, cos_t, sin_t)
    for hh in range(IDX_HEADS):
        qit_ref[0, hh] = jnp.concatenate([yqi[hh].astype(BF16), zq], axis=0)
    wit_ref[0] = pt[PT_ROWS - 8:PT_ROWS] * (IDX_HEADS ** -0.5 * HEAD_DIM ** -0.5)


def _inproj(x, scale1, shift1, norm1_g, w_in, k_norm_g, idx_ln_w, idx_ln_b, q_norm_g,
            tabs, interpret):
    B, S, D = x.shape
    tm = min(S, 256)
    cos_r, sin_r, cos_t, sin_t = tabs
    w_at = w_in[:, RW_COLS:]
    w_rw = w_in[:, :RW_COLS].astype(BF16)
    w_k = jnp.concatenate([w_at[:, AT_WIDTH:2 * AT_WIDTH], w_at[:, KI_OFF:KI_OFF + HEAD_DIM],
                           jnp.zeros((D, HEAD_DIM), F32)], axis=1).astype(BF16)
    w_t = jnp.concatenate([w_at[:, 0:AT_WIDTH], w_at[:, 2 * AT_WIDTH:3 * AT_WIDTH],
                           w_at[:, 3 * AT_WIDTH:KI_OFF], w_at[:, WI_OFF:WI_OFF + IDX_HEADS],
                           jnp.zeros((D, 8 - IDX_HEADS), F32)], axis=1).T.astype(BF16)
    kg = jnp.tile(k_norm_g, 2).reshape(1, LANES)
    zpad = jnp.zeros((HEAD_DIM,), F32)
    iw = jnp.concatenate([idx_ln_w, zpad]).reshape(1, LANES)
    ib = jnp.concatenate([idx_ln_b, zpad]).reshape(1, LANES)
    qg = q_norm_g.reshape(HEAD_DIM, 1)
    li = np.arange(LANES)
    gsum = jnp.asarray((li[:, None] // HEAD_DIM == li[None, :] // HEAD_DIM).astype(np.float32))

    full = lambda shape: pl.BlockSpec(shape, lambda b, i: (0,) * len(shape))
    return pl.pallas_call(
        _inproj_kernel,
        grid=(B, S // tm),
        in_specs=[pl.BlockSpec((1, tm, D), lambda b, i: (b, i, 0)),
                  pl.BlockSpec((1, 1, D), lambda b, i: (b, 0, 0)),
                  pl.BlockSpec((1, 1, D), lambda b, i: (b, 0, 0)),
                  full((1, D)), full((D, RW_COLS)), full((D, PK_COLS)), full((PT_ROWS, D)),
                  full((1, LANES)), full((1, LANES)), full((1, LANES)), full((HEAD_DIM, 1)),
                  full((LANES, LANES)),
                  pl.BlockSpec((1, tm, LANES), lambda b, i: (b, i, 0)),
                  pl.BlockSpec((1, tm, LANES), lambda b, i: (b, i, 0)),
                  pl.BlockSpec((1, HALF, tm), lambda b, i: (b, 0, i)),
                  pl.BlockSpec((1, HALF, tm), lambda b, i: (b, 0, i))],
        out_specs=[pl.BlockSpec((1, tm, RW_COLS), lambda b, i: (b, i, 0)),
                   pl.BlockSpec((1, tm, AT_WIDTH), lambda b, i: (b, i, 0)),
                   pl.BlockSpec((1, tm, LANES), lambda b, i: (b, i, 0)),
                   pl.BlockSpec((1, AT_HEADS, LANES, tm), lambda b, i: (b, 0, 0, i)),
                   pl.BlockSpec((1, tm // QBLK, AT_WIDTH, QBLK), lambda b, i: (b, i, 0, 0)),
                   pl.BlockSpec((1, IDX_HEADS, LANES, tm), lambda b, i: (b, 0, 0, i)),
                   pl.BlockSpec((1, 8, tm), lambda b, i: (b, 0, i))],
        out_shape=[jax.ShapeDtypeStruct((B, S, RW_COLS), F32),
                   jax.ShapeDtypeStruct((B, S, AT_WIDTH), BF16),
                   jax.ShapeDtypeStruct((B, S, LANES), BF16),
                   jax.ShapeDtypeStruct((B, AT_HEADS, LANES, S), BF16),
                   jax.ShapeDtypeStruct((B, S // QBLK, AT_WIDTH, QBLK), BF16),
                   jax.ShapeDtypeStruct((B, IDX_HEADS, LANES, S), BF16),
                   jax.ShapeDtypeStruct((B, 8, S), F32)],
        compiler_params=_cparams(("arbitrary", "arbitrary")),
        name="inproj", interpret=interpret,
    )(x, scale1, shift1, norm1_g.reshape(1, D), w_rw, w_k, w_t, kg, iw, ib, qg, gsum,
      cos_r, sin_r, cos_t, sin_t)


def _rwkv_kernel(p_ref, mu_ref, w0_ref, w2_ref, a0_ref, a2_ref, g2_ref, kk_ref, ka_ref, rk_ref,
                 lnw_ref, lnb_ref, gsum_ref, y_ref, s_ref, prev_ref, yt_ref):
    C = RW_CHUNK
    W = RW_WIDTH

    @pl.when(pl.program_id(1) == 0)
    def _():
        s_ref[...] = jnp.zeros_like(s_ref)
        prev_ref[...] = jnp.zeros_like(prev_ref)

    p = p_ref[0]
    row = lax.broadcasted_iota(I32, (C, 1), 0)
    pprev = jnp.where(row == 0, prev_ref[...], pltpu.roll(p, 1, 0))
    prev_ref[...] = p[C - 1:C]
    ps = p + (pprev - p) * mu_ref[...]
    r, k, v = ps[:, 0:W], ps[:, W:2 * W], ps[:, 2 * W:3 * W]
    o = 3 * W
    wl = ps[:, o:o + RW_LORA_W]
    al = ps[:, o + RW_LORA_W:o + RW_LORA_W + RW_LORA_A]
    gl = ps[:, o + RW_LORA_W + RW_LORA_A:]

    z = w0_ref[...] + _dot(jnp.tanh(wl).astype(BF16), w2_ref[...])
    nz = -z
    softplus = jnp.maximum(nz, 0.0) + jnp.log(1.0 + jnp.exp(-jnp.abs(nz)))
    logw = -jnp.exp(-softplus - 0.5)
    a = _sigmoid(a0_ref[...] + _dot(al.astype(BF16), a2_ref[...]))
    g = _dot(_sigmoid(gl).astype(BF16), g2_ref[...])
    gsum = gsum_ref[...]
    kk = k * kk_ref[...]
    ss = _dot((kk * kk).astype(BF16), gsum)
    kk = kk * (1.0 / jnp.maximum(jnp.sqrt(ss), 1e-12))
    k2 = k * (1.0 + (a - 1.0) * ka_ref[...])
    bb = kk * a

    cw = logw
    sh = 1
    while sh < C:
        cw = cw + jnp.where(row >= sh, pltpu.roll(cw, sh, 0), 0.0)
        sh *= 2
    cw_last = cw[C - 1:C]
    e_neg = jnp.exp(-cw)
    e_end = jnp.exp(cw_last - cw)
    rw = r * jnp.exp(cw)
    kkp = kk * jnp.exp(cw - logw)
    bw, kw = bb * e_neg, k2 * e_neg
    bend, kend = bb * e_end, k2 * e_end
    wc = jnp.exp(cw_last)
    vt_all = v.T.astype(BF16)

    ri = lax.broadcasted_iota(I32, (C, C), 0)
    ci = lax.broadcasted_iota(I32, (C, C), 1)
    strict = ri < ci
    incl = ri <= ci
    incl2 = jnp.concatenate([incl, incl], axis=0)
    lane_half = lax.broadcasted_iota(I32, (1, LANES), 1) // HEAD_DIM

    for h in range(RW_HEADS):
        pr = slice((h // 2) * LANES, (h // 2 + 1) * LANES)
        own = lane_half == (h % 2)
        lh = jnp.concatenate([kkp[:, pr], rw[:, pr]], axis=0).astype(BF16)
        rh = jnp.where(own, jnp.concatenate([bw[:, pr], kw[:, pr]], axis=0), 0.0).astype(BF16)
        aat = _dot_nt(rh, lh)
        s_h = s_ref[h]
        sl = _dot_nt(s_h.astype(BF16), lh)
        vt = vt_all[h * HEAD_DIM:(h + 1) * HEAD_DIM]
        akt = jnp.where(strict, aat[C:, :C], 0.0).astype(BF16)
        nt = jnp.where(strict, aat[:C, :C], 0.0).astype(BF16)
        xs = -(sl[:, :C] + _dot(vt, akt))
        xs = xs - _dot_split(xs, nt)
        m = nt
        lvl = 2
        while lvl < C:
            m = _dot(m, m).astype(BF16)
            xs = xs + _dot_split(xs, m)
            lvl *= 2
        zt = jnp.concatenate([xs.astype(BF16), vt], axis=1)
        ymat = jnp.where(incl2, aat[:, C:], 0.0).astype(BF16)
        yt_ref[h * HEAD_DIM:(h + 1) * HEAD_DIM, :] = sl[:, C:] + _dot(zt, ymat)
        endz = jnp.where(own, jnp.concatenate([bend[:, pr], kend[:, pr]], axis=0), 0.0).astype(BF16)
        s_ref[h] = s_h * wc[:, pr] + _dot(zt, endz)

    yt = yt_ref[...].reshape(RW_HEADS, HEAD_DIM, C)
    mean = jnp.mean(yt, axis=1, keepdims=True)
    yc = yt - mean
    var = jnp.mean(yc * yc, axis=1, keepdims=True)
    lnw = lnw_ref[...].reshape(RW_HEADS, HEAD_DIM, 1)
    lnb = lnb_ref[...].reshape(RW_HEADS, HEAD_DIM, 1)
    yn = yc * lax.rsqrt(var + RW_GN_EPS) * lnw + lnb
    y = yn.reshape(W, C).T
    bonus = _dot((r * k2 * rk_ref[...]).astype(BF16), gsum) * v
    y_ref[0] = ((y + bonus) * g).astype(BF16)


def _rwkv(p_rw, rw_mu, rw_w0, rw_w2, rw_a0, rw_a2, rw_g2, rw_k_k, rw_k_a, rw_r_k, rw_ln_w, rw_ln_b,
          interpret):
    B, S, _ = p_rw.shape
    C, W = RW_CHUNK, RW_WIDTH
    li = np.arange(W)
    gsum = jnp.asarray((li[:, None] // HEAD_DIM == li[None, :] // HEAD_DIM).astype(np.float32)).astype(BF16)
    row = lambda a: a.reshape(1, -1)
    full = lambda shape: pl.BlockSpec(shape, lambda b, i: (0,) * len(shape))
    return pl.pallas_call(
        _rwkv_kernel,
        grid=(B, S // C),
        in_specs=[pl.BlockSpec((1, C, RW_COLS), lambda b, i: (b, i, 0)),
                  full((1, RW_COLS)), full((1, W)), full((RW_LORA_W, W)), full((1, W)),
                  full((RW_LORA_A, W)), full((RW_LORA_G, W)), full((1, W)), full((1, W)),
                  full((1, W)), full((W, 1)), full((W, 1)), full((W, W))],
        out_specs=pl.BlockSpec((1, C, W), lambda b, i: (b, i, 0)),
        out_shape=jax.ShapeDtypeStruct((B, S, W), BF16),
        scratch_shapes=[pltpu.VMEM((RW_HEADS, HEAD_DIM, LANES), F32),
                        pltpu.VMEM((1, RW_COLS), F32),
                        pltpu.VMEM((W, C), F32)],
        compiler_params=_cparams(("arbitrary", "arbitrary")),
        name="rwkv", interpret=interpret,
    )(p_rw, row(rw_mu), row(rw_w0), rw_w2.astype(BF16), row(rw_a0), rw_a2.astype(BF16),
      rw_g2.astype(BF16), row(rw_k_k), row(rw_k_a), row(rw_r_k), rw_ln_w.reshape(W, 1),
      rw_ln_b.reshape(W, 1), gsum)


def _dsa_kernel(topk, nbits, k_ref, vt_ref, ki_ref, qt_ref, qit_ref, wit_ref, og_ref, o_ref,
                key_s, acc_s, m_s, l_s, thr_s):
    j = pl.program_id(1)
    nkb = j + 1
    lane = lax.broadcasted_iota(I32, (QBLK, QBLK), 1)
    sub = lax.broadcasted_iota(I32, (QBLK, QBLK), 0)
    qpos = j * QBLK + lane
    wit = wit_ref[0]

    def score_block(kb, carry):
        kib = ki_ref[0, pl.ds(pl.multiple_of(kb * QBLK, QBLK), QBLK), :]
        s = jnp.zeros((QBLK, QBLK), F32)
        for hh in range(IDX_HEADS):
            s = s + wit[hh:hh + 1, :] * jnp.maximum(_dot(kib, qit_ref[0, hh]), 0.0)
        s = jnp.where(s == 0.0, 0.0, s)
        bits = pltpu.bitcast(s, I32)
        skey = jnp.where(bits < 0, bits ^ 0x7FFFFFFF, bits)
        key_s[kb] = jnp.where(kb * QBLK + sub <= qpos, skey, INT_MIN)
        return carry

    lax.fori_loop(0, nkb, score_block, 0)

    def count(pred):
        def body(kb, acc):
            hit = pred(key_s[kb], kb * QBLK + sub).astype(I32)
            return acc + jnp.sum(hit.reshape(QBLK // 8, 8, QBLK), axis=0)
        acc = lax.fori_loop(0, nkb, body, jnp.zeros((8, QBLK), I32))
        return jnp.sum(acc, axis=0, keepdims=True)

    @pl.when(nkb * QBLK <= topk)
    def _():
        thr_s[0:1, :] = jnp.full((1, QBLK), INT_MIN, I32)
        thr_s[1:2, :] = jnp.zeros((1, QBLK), I32)

    @pl.when(nkb * QBLK > topk)
    def _():
        c0 = count(lambda ky, ix: ky >= 0)
        t0 = jnp.where(c0 >= topk, 0, INT_MIN).astype(I32)

        def bit_step(i, t):
            cand = t | jnp.left_shift(jnp.int32(1), 30 - i)
            c = count(lambda ky, ix: ky >= cand)
            return jnp.where(c >= topk, cand, t)

        thr = lax.fori_loop(0, 31, bit_step, t0)
        need = topk - count(lambda ky, ix: ky > thr)

        def idx_step(i, mm):
            cand = mm | jnp.left_shift(jnp.int32(1), nbits - 1 - i)
            c = count(lambda ky, ix: (ky == thr) & (ix < cand))
            return jnp.where(c < need, cand, mm)

        mm = lax.fori_loop(0, nbits, idx_step, jnp.zeros((1, QBLK), I32))
        thr_s[0:1, :] = thr
        thr_s[1:2, :] = mm

    thr = thr_s[0:1, :]
    mm = thr_s[1:2, :]
    m_s[...] = jnp.full_like(m_s, NEG_BIG)
    l_s[...] = jnp.zeros_like(l_s)
    acc_s[...] = jnp.zeros_like(acc_s)

    def attn_block(kb, carry):
        skey = key_s[kb]
        kidx = kb * QBLK + sub
        sel = (kidx <= qpos) & ((skey > thr) | ((skey == thr) & (kidx <= mm)))
        kblk = k_ref[0, pl.ds(pl.multiple_of(kb * QBLK, QBLK), QBLK), :]
        vtb = vt_ref[0, kb]
        for hh in range(AT_HEADS):
            pr = slice((hh // 2) * LANES, (hh // 2 + 1) * LANES)
            hs = slice(hh * HEAD_DIM, (hh + 1) * HEAD_DIM)
            s = jnp.where(sel, _dot(kblk[:, pr], qt_ref[0, hh]), NEG_BIG)
            m_old = m_s[hh:hh + 1, :]
            m_new = jnp.maximum(m_old, jnp.max(s, axis=0, keepdims=True))
            pexp = jnp.where(sel, jnp.exp(s - m_new), 0.0)
            alpha = jnp.exp(m_old - m_new)
            l_s[hh:hh + 1, :] = alpha * l_s[hh:hh + 1, :] + jnp.sum(pexp, axis=0, keepdims=True)
            acc_s[hs, :] = alpha * acc_s[hs, :] + _dot(vtb[hs, :], pexp.astype(BF16))
            m_s[hh:hh + 1, :] = m_new
        return carry

    lax.fori_loop(0, nkb, attn_block, 0)

    for hh in range(AT_HEADS):
        hs = slice(hh * HEAD_DIM, (hh + 1) * HEAD_DIM)
        oh = acc_s[hs, :] * (1.0 / l_s[hh:hh + 1, :])
        ms = jnp.mean(oh * oh, axis=0, keepdims=True)
        acc_s[hs, :] = oh * lax.rsqrt(ms + NORM_EPS) * og_ref[hs, :]
    o_ref[0] = acc_s[...].T.astype(BF16)


def _dsa(k, vt, ki, qt, qit, wit, at_out_g, interpret):
    B, S, _ = k.shape
    nq = S // QBLK
    topk = min(IDX_TOPK_MAX, S // 4)
    nbits = int(np.log2(S))
    assert 2 ** nbits == S
    return pl.pallas_call(
        functools.partial(_dsa_kernel, topk, nbits),
        grid=(B, nq),
        in_specs=[pl.BlockSpec((1, S, AT_WIDTH), lambda b, j: (b, 0, 0)),
                  pl.BlockSpec((1, nq, AT_WIDTH, QBLK), lambda b, j: (b, 0, 0, 0)),
                  pl.BlockSpec((1, S, LANES), lambda b, j: (b, 0, 0)),
                  pl.BlockSpec((1, AT_HEADS, LANES, QBLK), lambda b, j: (b, 0, 0, j)),
                  pl.BlockSpec((1, IDX_HEADS, LANES, QBLK), lambda b, j: (b, 0, 0, j)),
                  pl.BlockSpec((1, 8, QBLK), lambda b, j: (b, 0, j)),
                  pl.BlockSpec((AT_WIDTH, 1), lambda b, j: (0, 0))],
        out_specs=pl.BlockSpec((1, QBLK, AT_WIDTH), lambda b, j: (b, j, 0)),
        out_shape=jax.ShapeDtypeStruct((B, S, AT_WIDTH), BF16),
        scratch_shapes=[pltpu.VMEM((nq, QBLK, QBLK), I32),
                        pltpu.VMEM((AT_WIDTH, QBLK), F32),
                        pltpu.VMEM((AT_HEADS, QBLK), F32),
                        pltpu.VMEM((AT_HEADS, QBLK), F32),
                        pltpu.VMEM((8, QBLK), I32)],
        compiler_params=_cparams(("arbitrary", "arbitrary")),
        name="dsa", interpret=interpret,
    )(k, vt, ki, qt, qit, wit, at_out_g.reshape(AT_WIDTH, 1))


def _first_max(vals, idx, axis, sentinel):
    m = jnp.max(vals, axis=axis, keepdims=True)
    return m, jnp.min(jnp.where(vals == m, idx, sentinel), axis=axis, keepdims=True)


def _route_cols(logits_t, bias_col):
    E, tm = logits_t.shape
    pg = E // N_GROUPS
    scores = _sigmoid(logits_t)
    biased = scores + bias_col
    b3 = biased.reshape(N_GROUPS, pg, tm)
    r3 = lax.broadcasted_iota(I32, (N_GROUPS, pg, tm), 1)
    m1, first = _first_max(b3, r3, 1, pg)
    m2 = jnp.max(jnp.where(r3 == first, -jnp.inf, b3), axis=1, keepdims=True)
    cur = (m1 + m2).reshape(N_GROUPS, tm)
    grow = lax.broadcasted_iota(I32, (N_GROUPS, tm), 0)
    gsel = jnp.zeros((N_GROUPS, tm), F32)
    for _ in range(TOPK_GROUPS):
        _, gi = _first_max(cur, grow, 0, N_GROUPS)
        hit = grow == gi
        gsel = jnp.where(hit, 1.0, gsel)
        cur = jnp.where(hit, -jnp.inf, cur)
    gmask = jnp.broadcast_to(gsel.reshape(N_GROUPS, 1, tm), (N_GROUPS, pg, tm)).reshape(E, tm)
    cur = jnp.where(gmask > 0.0, biased, -jnp.inf)
    row = lax.broadcasted_iota(I32, (E, tm), 0)
    onehot = jnp.zeros((E, tm), F32)
    eids, gws = [], []
    for _ in range(TOP_K):
        _, ei = _first_max(cur, row, 0, E)
        hit = row == ei
        eids.append(ei)
        gws.append(jnp.sum(jnp.where(hit, scores, 0.0), axis=0, keepdims=True))
        onehot = jnp.where(hit, 1.0, onehot)
        cur = jnp.where(hit, -jnp.inf, cur)
    eid = jnp.concatenate(eids, axis=0)
    gw = jnp.concatenate(gws, axis=0)
    gw = gw * (ROUTED_SCALE / jnp.sum(gw, axis=0, keepdims=True))
    return eid, gw, onehot


def _post_kernel(x_ref, yrw_ref, yat_ref, g1_ref, sc_ref, sh_ref, g2_ref, ng_ref, wo_ref, rwt_ref,
                 rb_ref, s1_ref, s3_ref, s2_ref, base_ref, h2_ref, eid_ref, gw_ref, rank_ref, cnt_ref):
    W = RW_WIDTH
    tm = x_ref.shape[1]
    E = rwt_ref.shape[0]

    @pl.when((pl.program_id(0) == 0) & (pl.program_id(1) == 0))
    def _():
        cnt_ref[...] = jnp.zeros_like(cnt_ref)

    mix = _dot(yrw_ref[0], wo_ref[0:W, :]) + _dot(yat_ref[0], wo_ref[W:, :])
    x1 = x_ref[0] + g1_ref[0] * mix
    ms = jnp.mean(x1 * x1, axis=-1, keepdims=True)
    h2 = x1 * lax.rsqrt(ms + NORM_EPS) * ng_ref[...] * (1.0 + sc_ref[0]) + sh_ref[0]
    hb = h2.astype(BF16)
    h2_ref[0] = h2
    act = (_silu(_dot(hb, s1_ref[...])) * _dot(hb, s3_ref[...])).astype(BF16)
    base_ref[0] = x1 + g2_ref[0] * _dot(act, s2_ref[...])

    logits_t = lax.dot_general(rwt_ref[...], h2, (((1,), (1,)), ((), ())), precision=HIGHEST,
                               preferred_element_type=F32)
    eid, gw, onehot = _route_cols(logits_t, rb_ref[...])
    eid_ref[0] = eid
    gw_ref[0] = gw
    ti = lax.broadcasted_iota(I32, (tm, tm), 0)
    tj = lax.broadcasted_iota(I32, (tm, tm), 1)
    before = _dot(onehot.astype(BF16), (ti < tj).astype(BF16)) + cnt_ref[:, 0:1]
    row = lax.broadcasted_iota(I32, (E, tm), 0)
    ranks = [jnp.sum(jnp.where(row == eid[kk:kk + 1, :], before, 0.0), axis=0, keepdims=True)
             for kk in range(TOP_K)]
    rank_ref[0] = jnp.concatenate(ranks, axis=0).astype(I32)
    cnt_ref[...] = cnt_ref[...] + jnp.sum(onehot, axis=1, keepdims=True)


def _post(x, y_rw, y_at, gate1, scale2, shift2, gate2, norm2_g, w_out, router_w, router_bias,
          sw1, sw3, sw2, interpret):
    B, S, D = x.shape
    tm = min(S, 256)
    E = router_w.shape[1]
    sd = sw1.shape[1]
    full = lambda shape: pl.BlockSpec(shape, lambda b, i: (0,) * len(shape))
    tok = lambda w: pl.BlockSpec((1, tm, w), lambda b, i: (b, i, 0))
    per_b = pl.BlockSpec((1, 1, D), lambda b, i: (b, 0, 0))
    col8 = pl.BlockSpec((1, TOP_K, tm), lambda b, i: (b, 0, i))
    return pl.pallas_call(
        _post_kernel,
        grid=(B, S // tm),
        in_specs=[tok(D), tok(RW_WIDTH), tok(AT_WIDTH), per_b, per_b, per_b, per_b, full((1, D)),
                  full((D, D)), full((E, D)), full((E, 1)), full((D, sd)), full((D, sd)),
                  full((sd, D))],
        out_specs=[tok(D), tok(D), col8, col8, col8, full((E, LANES))],
        out_shape=[jax.ShapeDtypeStruct((B, S, D), F32),
                   jax.ShapeDtypeStruct((B, S, D), F32),
                   jax.ShapeDtypeStruct((B, TOP_K, S), I32),
                   jax.ShapeDtypeStruct((B, TOP_K, S), F32),
                   jax.ShapeDtypeStruct((B, TOP_K, S), I32),
                   jax.ShapeDtypeStruct((E, LANES), F32)],
        compiler_params=_cparams(("arbitrary", "arbitrary")),
        name="post", interpret=interpret,
    )(x, y_rw, y_at, gate1, scale2, shift2, gate2, norm2_g.reshape(1, D), w_out.astype(BF16),
      router_w.T, router_bias.reshape(E, 1), sw1.astype(BF16), sw3.astype(BF16), sw2.astype(BF16))


def _expert_kernel(be_ref, xs_ref, w1_ref, w3_ref, w2_ref, o_ref):
    del be_ref
    x = xs_ref[...].astype(BF16)
    h1 = _dot(x, w1_ref[0].astype(BF16))
    h3 = _dot(x, w3_ref[0].astype(BF16))
    act = (_silu(h1) * h3).astype(BF16)
    o_ref[...] = _dot(act, w2_ref[0].astype(BF16))


def _experts(xs, block_e, w1, w3, w2, interpret):
    P, D = xs.shape
    E, _, F = w1.shape
    nb = P // EXP_BLK
    grid_spec = pltpu.PrefetchScalarGridSpec(
        num_scalar_prefetch=1,
        grid=(nb,),
        in_specs=[pl.BlockSpec((EXP_BLK, D), lambda i, be: (i, 0)),
                  pl.BlockSpec((1, D, F), lambda i, be: (be[i], 0, 0)),
                  pl.BlockSpec((1, D, F), lambda i, be: (be[i], 0, 0)),
                  pl.BlockSpec((1, F, D), lambda i, be: (be[i], 0, 0))],
        out_specs=pl.BlockSpec((EXP_BLK, D), lambda i, be: (i, 0)),
    )
    return pl.pallas_call(
        _expert_kernel,
        grid_spec=grid_spec,
        out_shape=jax.ShapeDtypeStruct((P, D), F32),
        compiler_params=_cparams(("arbitrary",)),
        name="experts", interpret=interpret,
    )(block_e, xs, w1, w3, w2)


def _row_copy(src, s_row, dst, d_row, sem):
    return pltpu.make_async_copy(src.at[pl.ds(s_row, 1)], dst.at[pl.ds(d_row, 1)], sem)


def _dispatch_kernel(dest_ref, h2_ref, xs_in_ref, xs_ref, sem):
    del xs_in_ref
    td = h2_ref.shape[0]

    def issue(t, carry):
        for kk in range(TOP_K):
            _row_copy(h2_ref, t, xs_ref, dest_ref[0, 0, kk * td + t], sem).start()
        return carry

    def drain(t, carry):
        for kk in range(TOP_K):
            _row_copy(h2_ref, t, xs_ref, dest_ref[0, 0, kk * td + t], sem).wait()
        return carry

    lax.fori_loop(0, td, issue, 0)
    lax.fori_loop(0, td, drain, 0)


def _dispatch(h2, dest_tiles, n_slots, interpret):
    T, D = h2.shape
    nt, _, n = dest_tiles.shape
    td = n // TOP_K
    xs0 = jnp.zeros((n_slots, D), F32)
    return pl.pallas_call(
        _dispatch_kernel,
        grid=(nt,),
        in_specs=[pl.BlockSpec((1, 1, n), lambda i: (i, 0, 0), memory_space=pltpu.SMEM),
                  pl.BlockSpec((td, D), lambda i: (i, 0)),
                  pl.BlockSpec(memory_space=pl.ANY)],
        out_specs=pl.BlockSpec(memory_space=pl.ANY),
        out_shape=jax.ShapeDtypeStruct((n_slots, D), F32),
        scratch_shapes=[pltpu.SemaphoreType.DMA(())],
        input_output_aliases={2: 0},
        compiler_params=_cparams(("arbitrary",)),
        name="dispatch", interpret=interpret,
    )(dest_tiles, h2, xs0)


def _combine_kernel(dest_ref, base_ref, g2_ref, gw_ref, ys_ref, o_ref, buf, sem):
    td = base_ref.shape[0]

    def issue(t, carry):
        for kk in range(TOP_K):
            _row_copy(ys_ref, dest_ref[0, 0, kk * td + t], buf.at[kk], t, sem).start()
        return carry

    def drain(t, carry):
        for kk in range(TOP_K):
            _row_copy(ys_ref, dest_ref[0, 0, kk * td + t], buf.at[kk], t, sem).wait()
        return carry

    lax.fori_loop(0, td, issue, 0)
    lax.fori_loop(0, td, drain, 0)
    gw = gw_ref[...]
    acc = gw[:, 0:1] * buf[0]
    for kk in range(1, TOP_K):
        acc = acc + gw[:, kk:kk + 1] * buf[kk]
    o_ref[...] = base_ref[...] + g2_ref[0] * acc


def _combine(base, gate2, gw_tok, ys, dest_tiles, tiles_per_batch, interpret):
    T, D = base.shape
    nt, _, n = dest_tiles.shape
    td = n // TOP_K
    return pl.pallas_call(
        _combine_kernel,
        grid=(nt,),
        in_specs=[pl.BlockSpec((1, 1, n), lambda i: (i, 0, 0), memory_space=pltpu.SMEM),
                  pl.BlockSpec((td, D), lambda i: (i, 0)),
                  pl.BlockSpec((1, 1, D), lambda i: (i // tiles_per_batch, 0, 0)),
                  pl.BlockSpec((td, TOP_K), lambda i: (i, 0)),
                  pl.BlockSpec(memory_space=pl.ANY)],
        out_specs=pl.BlockSpec((td, D), lambda i: (i, 0)),
        out_shape=jax.ShapeDtypeStruct((T, D), F32),
        scratch_shapes=[pltpu.VMEM((TOP_K, td, D), F32), pltpu.SemaphoreType.DMA(())],
        compiler_params=_cparams(("arbitrary",)),
        name="combine", interpret=interpret,
    )(dest_tiles, base, gate2, gw_tok, ys)


def _slot_plan(counts, eid_t, rank_t, td):
    B, _, S = eid_t.shape
    padded = (counts + EXP_BLK - 1) // EXP_BLK * EXP_BLK
    pend = jnp.cumsum(padded)
    pstart = pend - padded
    nb = -(-(B * S * TOP_K + N_EXPERTS * (EXP_BLK - 1)) // EXP_BLK)
    block_e = jnp.searchsorted(pend, jnp.arange(nb, dtype=I32) * EXP_BLK, side='right')
    block_e = jnp.minimum(block_e, N_EXPERTS - 1).astype(I32)
    dest = (pstart[eid_t] + rank_t).astype(I32)
    dest_tiles = dest.reshape(B, TOP_K, S // td, td).transpose(0, 2, 1, 3).reshape(-1, 1, TOP_K * td)
    return block_e, dest_tiles, nb * EXP_BLK


def _forward(x, c, positions, w_ada, b_ada, norm1_g, norm2_g, w_in, rw_mu, rw_w0, rw_w2,
             rw_a0, rw_a2, rw_g2, rw_k_k, rw_k_a, rw_r_k, rw_ln_w, rw_ln_b, q_norm_g,
             k_norm_g, idx_ln_w, idx_ln_b, at_out_g, w_out, router_w, router_bias,
             exp_w1, exp_w3, exp_w2, shared_w1, shared_w3, shared_w2, interpret=False):
    B, S, D = x.shape
    depth = w_ada.shape[0]
    for l in range(depth):
        mod = _mod(c, w_ada[l], b_ada[l], interpret)
        shift1, scale1, gate1, shift2, scale2, gate2 = [
            m.reshape(B, 1, D) for m in jnp.split(mod, 6, axis=-1)]
        tabs = _rope_tables(positions, interpret)
        p_rw, k, ki, qt, vt, qit, wit = _inproj(
            x, scale1, shift1, norm1_g[l], w_in[l], k_norm_g[l], idx_ln_w[l], idx_ln_b[l],
            q_norm_g[l], tabs, interpret)
        y_rw = _rwkv(p_rw, rw_mu[l], rw_w0[l], rw_w2[l], rw_a0[l], rw_a2[l], rw_g2[l], rw_k_k[l],
                     rw_k_a[l], rw_r_k[l], rw_ln_w[l], rw_ln_b[l], interpret)
        y_at = _dsa(k, vt, ki, qt, qit, wit, at_out_g[l], interpret)
        base, h2, eid_t, gw_t, rank_t, cnt = _post(
            x, y_rw, y_at, gate1, scale2, shift2, gate2, norm2_g[l], w_out[l], router_w[l],
            router_bias[l], shared_w1[l], shared_w3[l], shared_w2[l], interpret)
        T = B * S
        td = min(S, ROW_TILE)
        block_e, dest_tiles, n_slots = _slot_plan(cnt[:, 0].astype(I32), eid_t, rank_t, td)
        xs = _dispatch(h2.reshape(T, D), dest_tiles, n_slots, interpret)
        ys = _experts(xs, block_e, exp_w1[l], exp_w3[l], exp_w2[l], interpret)
        gw_tok = gw_t.transpose(0, 2, 1).reshape(T, TOP_K)
        x = _combine(base.reshape(T, D), gate2, gw_tok, ys, dest_tiles, S // td,
                     interpret).reshape(B, S, D)
    return x


def kernel(x, c, positions, w_ada, b_ada, norm1_g, norm2_g, w_in, rw_mu, rw_w0, rw_w2, rw_a0, rw_a2, rw_g2, rw_k_k, rw_k_a, rw_r_k, rw_ln_w, rw_ln_b, q_norm_g, k_norm_g, idx_ln_w, idx_ln_b, at_out_g, w_out, router_w, router_bias, exp_w1, exp_w3, exp_w2, shared_w1, shared_w3, shared_w2):
    return _forward(x, c, positions, w_ada, b_ada, norm1_g, norm2_g, w_in, rw_mu, rw_w0, rw_w2,
                    rw_a0, rw_a2, rw_g2, rw_k_k, rw_k_a, rw_r_k, rw_ln_w, rw_ln_b, q_norm_g,
                    k_norm_g, idx_ln_w, idx_ln_b, at_out_g, w_out, router_w, router_bias,
                    exp_w1, exp_w3, exp_w2, shared_w1, shared_w3, shared_w2)
```

```python
import functools

import jax
import jax.numpy as jnp
import numpy as np
from jax import lax
from jax.experimental import pallas as pl
from jax.experimental.pallas import tpu as pltpu

F32 = jnp.float32
BF16 = jnp.bfloat16
I32 = jnp.int32
HIGHEST = lax.Precision.HIGHEST

LANES = 128
HEAD_DIM = 64
HALF = HEAD_DIM // 2
RW_HEADS = 8
RW_WIDTH = RW_HEADS * HEAD_DIM
AT_HEADS = 8
AT_WIDTH = AT_HEADS * HEAD_DIM
IDX_HEADS = 4
RW_LORA_W, RW_LORA_A, RW_LORA_G = 64, 64, 128
RW_COLS = 3 * RW_WIDTH + RW_LORA_W + RW_LORA_A + RW_LORA_G
KI_OFF = 3 * AT_WIDTH + IDX_HEADS * HEAD_DIM
WI_OFF = KI_OFF + HEAD_DIM
PT_ROWS = 2 * AT_WIDTH + IDX_HEADS * HEAD_DIM + 8
PK_COLS = AT_WIDTH + LANES
ROPE_THETA = 10000.0
NORM_EPS = 1e-6
LN_EPS = 1e-6
RW_GN_EPS = 64e-5
IDX_TOPK_MAX = 256
N_EXPERTS = 256
TOP_K = 8
N_GROUPS = 8
TOPK_GROUPS = 4
ROUTED_SCALE = 2.5
INT_MIN = -2 ** 31
NEG_BIG = -1e30

RW_CHUNK = 128
QBLK = 128
SCORE_UNROLL = 2
ATTN_UNROLL = 2
COUNT_UNROLL = 4
EXP_BLK = 256
ROW_TILE = 256
VMEM_LIMIT = 48 * 1024 * 1024


def _cparams(sem):
    return pltpu.CompilerParams(dimension_semantics=sem, vmem_limit_bytes=VMEM_LIMIT)


def _sigmoid(x):
    return 1.0 / (1.0 + jnp.exp(-x))


def _silu(x):
    return x * _sigmoid(x)


def _dot(a, b):
    return jnp.dot(a, b, preferred_element_type=F32)


def _dot_split(a, b):
    hi = a.astype(BF16)
    lo = (a - hi.astype(F32)).astype(BF16)
    return _dot(hi, b) + _dot(lo, b)


def _dot_nt(a, b):
    return lax.dot_general(a, b, (((1,), (1,)), ((), ())), preferred_element_type=F32)


def _mod_kernel(c_ref, w_ref, b_ref, o_ref):
    c = c_ref[...]
    o_ref[...] = jnp.dot(_silu(c), w_ref[...], precision=HIGHEST,
                         preferred_element_type=F32) + b_ref[...]


def _mod(c, w_ada, b_ada, interpret):
    B, D = c.shape
    n = w_ada.shape[1] // D
    return pl.pallas_call(
        _mod_kernel,
        grid=(n,),
        in_specs=[pl.BlockSpec((B, D), lambda i: (0, 0)),
                  pl.BlockSpec((D, D), lambda i: (0, i)),
                  pl.BlockSpec((1, D), lambda i: (0, i))],
        out_specs=pl.BlockSpec((B, D), lambda i: (0, i)),
        out_shape=jax.ShapeDtypeStruct((B, n * D), F32),
        compiler_params=_cparams(("arbitrary",)),
        name="mod", interpret=interpret,
    )(c, w_ada, b_ada.reshape(1, -1))


def _rope_tab_kernel(pc_ref, pr_ref, cr_ref, sr_ref, ct_ref, st_ref):
    log_theta = float(np.log(ROPE_THETA))
    lane = lax.broadcasted_iota(I32, (1, LANES), 1)
    inv_r = jnp.exp((lane % HALF).astype(F32) * (-log_theta / HALF))
    ang = pc_ref[0].astype(F32) * inv_r
    cr_ref[0] = jnp.cos(ang)
    sr_ref[0] = jnp.where((lane % HEAD_DIM) < HALF, -jnp.sin(ang), jnp.sin(ang))
    sub = lax.broadcasted_iota(I32, (HALF, 1), 0)
    inv_c = jnp.exp(sub.astype(F32) * (-log_theta / HALF))
    ang_t = inv_c * pr_ref[0].astype(F32)
    ct_ref[0] = jnp.cos(ang_t)
    st_ref[0] = jnp.sin(ang_t)


def _rope_tables(positions, interpret):
    B, S = positions.shape
    ts = min(S, 512)
    return pl.pallas_call(
        _rope_tab_kernel,
        grid=(B, S // ts),
        in_specs=[pl.BlockSpec((1, ts, 1), lambda b, i: (b, i, 0)),
                  pl.BlockSpec((1, 1, ts), lambda b, i: (b, 0, i))],
        out_specs=[pl.BlockSpec((1, ts, LANES), lambda b, i: (b, i, 0)),
                   pl.BlockSpec((1, ts, LANES), lambda b, i: (b, i, 0)),
                   pl.BlockSpec((1, HALF, ts), lambda b, i: (b, 0, i)),
                   pl.BlockSpec((1, HALF, ts), lambda b, i: (b, 0, i))],
        out_shape=[jax.ShapeDtypeStruct((B, S, LANES), F32),
                   jax.ShapeDtypeStruct((B, S, LANES), F32),
                   jax.ShapeDtypeStruct((B, HALF, S), F32),
                   jax.ShapeDtypeStruct((B, HALF, S), F32)],
        compiler_params=_cparams(("arbitrary", "arbitrary")),
        name="rope_tab", interpret=interpret,
    )(positions.reshape(B, S, 1), positions.reshape(B, 1, S))


def _rope_rows(y, cos, sin_signed):
    lane = lax.broadcasted_iota(I32, (1, LANES), 1)
    partner = jnp.where((lane % HEAD_DIM) < HALF,
                        pltpu.roll(y, LANES - HALF, 1), pltpu.roll(y, HALF, 1))
    return y * cos + partner * sin_signed


def _rope_cols(y, cos_t, sin_t):
    x1, x2 = y[:, :HALF], y[:, HALF:]
    return jnp.concatenate([x1 * cos_t - x2 * sin_t, x2 * cos_t + x1 * sin_t], axis=1)


def _inproj_kernel(x_ref, sc_ref, sh_ref, g_ref, wrw_ref, wk_ref, wt_ref, kg_ref, iw_ref, ib_ref,
                   qg_ref, gsum_ref, cr_ref, sr_ref, ct_ref, st_ref,
                   prw_ref, k_ref, ki_ref, qt_ref, vt_ref, qit_ref, wit_ref):
    tm = x_ref.shape[1]
    x = x_ref[0]
    ms = jnp.mean(x * x, axis=-1, keepdims=True)
    h = x * lax.rsqrt(ms + NORM_EPS) * g_ref[...] * (1.0 + sc_ref[0]) + sh_ref[0]
    hb = h.astype(BF16)
    prw_ref[0] = _dot(hb, wrw_ref[...])
    pk = _dot(hb, wk_ref[...])
    pt = _dot_nt(wt_ref[...], hb)

    cos_r, sin_r = cr_ref[0], sr_ref[0]
    gsum = gsum_ref[...]
    inv_hd = 1.0 / HEAD_DIM
    for p in range(AT_WIDTH // LANES):
        xk = pk[:, p * LANES:(p + 1) * LANES]
        ss = jnp.dot(xk * xk, gsum, precision=HIGHEST, preferred_element_type=F32)
        y = xk * lax.rsqrt(ss * inv_hd + NORM_EPS) * kg_ref[...]
        k_ref[0, :, p * LANES:(p + 1) * LANES] = _rope_rows(y, cos_r, sin_r).astype(BF16)
    xi = pk[:, AT_WIDTH:AT_WIDTH + LANES]
    mu = jnp.dot(xi, gsum, precision=HIGHEST, preferred_element_type=F32) * inv_hd
    xc = xi - mu
    var = jnp.dot(xc * xc, gsum, precision=HIGHEST, preferred_element_type=F32) * inv_hd
    yi = xc * lax.rsqrt(var + LN_EPS) * iw_ref[...] + ib_ref[...]
    ki_ref[0] = _rope_rows(yi, cos_r, sin_r).astype(BF16)

    cos_t, sin_t = ct_ref[0][None], st_ref[0][None]
    xq = pt[0:AT_WIDTH].reshape(AT_HEADS, HEAD_DIM, tm)
    msq = jnp.mean(xq * xq, axis=1, keepdims=True)
    yq = xq * lax.rsqrt(msq + NORM_EPS) * qg_ref[...][None]
    yq = _rope_cols(yq, cos_t, sin_t) * (HEAD_DIM ** -0.5)
    zq = jnp.zeros((HEAD_DIM, tm), BF16)
    for hh in range(AT_HEADS):
        parts = [yq[hh].astype(BF16), zq] if hh % 2 == 0 else [zq, yq[hh].astype(BF16)]
        qt_ref[0, hh] = jnp.concatenate(parts, axis=0)
    vt = pt[AT_WIDTH:2 * AT_WIDTH].astype(BF16)
    for cblk in range(tm // QBLK):
        vt_ref[0, cblk] = vt[:, cblk * QBLK:(cblk + 1) * QBLK]
    xqi = pt[2 * AT_WIDTH:2 * AT_WIDTH + IDX_HEADS * HEAD_DIM].reshape(IDX_HEADS, HEAD_DIM, tm)
    yqi = _rope_cols(xqi, cos_t, sin_t)
    for hh in range(IDX_HEADS):
        qit_ref[0, hh] = jnp.concatenate([yqi[hh].astype(BF16), zq], axis=0)
    wit_ref[0] = pt[PT_ROWS - 8:PT_ROWS] * (IDX_HEADS ** -0.5 * HEAD_DIM ** -0.5)


def _inproj(x, scale1, shift1, norm1_g, w_in, k_norm_g, idx_ln_w, idx_ln_b, q_norm_g,
            tabs, interpret):
    B, S, D = x.shape
    tm = min(S, 256)
    cos_r, sin_r, cos_t, sin_t = tabs
    w_at = w_in[:, RW_COLS:]
    w_rw = w_in[:, :RW_COLS].astype(BF16)
    w_k = jnp.concatenate([w_at[:, AT_WIDTH:2 * AT_WIDTH], w_at[:, KI_OFF:KI_OFF + HEAD_DIM],
                           jnp.zeros((D, HEAD_DIM), F32)], axis=1).astype(BF16)
    w_t = jnp.concatenate([w_at[:, 0:AT_WIDTH], w_at[:, 2 * AT_WIDTH:3 * AT_WIDTH],
                           w_at[:, 3 * AT_WIDTH:KI_OFF], w_at[:, WI_OFF:WI_OFF + IDX_HEADS],
                           jnp.zeros((D, 8 - IDX_HEADS), F32)], axis=1).T.astype(BF16)
    kg = jnp.tile(k_norm_g, 2).reshape(1, LANES)
    zpad = jnp.zeros((HEAD_DIM,), F32)
    iw = jnp.concatenate([idx_ln_w, zpad]).reshape(1, LANES)
    ib = jnp.concatenate([idx_ln_b, zpad]).reshape(1, LANES)
    qg = q_norm_g.reshape(HEAD_DIM, 1)
    li = np.arange(LANES)
    gsum = jnp.asarray((li[:, None] // HEAD_DIM == li[None, :] // HEAD_DIM).astype(np.float32))

    full = lambda shape: pl.BlockSpec(shape, lambda b, i: (0,) * len(shape))
    return pl.pallas_call(
        _inproj_kernel,
        grid=(B, S // tm),
        in_specs=[pl.BlockSpec((1, tm, D), lambda b, i: (b, i, 0)),
                  pl.BlockSpec((1, 1, D), lambda b, i: (b, 0, 0)),
                  pl.BlockSpec((1, 1, D), lambda b, i: (b, 0, 0)),
                  full((1, D)), full((D, RW_COLS)), full((D, PK_COLS)), full((PT_ROWS, D)),
                  full((1, LANES)), full((1, LANES)), full((1, LANES)), full((HEAD_DIM, 1)),
                  full((LANES, LANES)),
                  pl.BlockSpec((1, tm, LANES), lambda b, i: (b, i, 0)),
                  pl.BlockSpec((1, tm, LANES), lambda b, i: (b, i, 0)),
                  pl.BlockSpec((1, HALF, tm), lambda b, i: (b, 0, i)),
                  pl.BlockSpec((1, HALF, tm), lambda b, i: (b, 0, i))],
        out_specs=[pl.BlockSpec((1, tm, RW_COLS), lambda b, i: (b, i, 0)),
                   pl.BlockSpec((1, tm, AT_WIDTH), lambda b, i: (b, i, 0)),
                   pl.BlockSpec((1, tm, LANES), lambda b, i: (b, i, 0)),
                   pl.BlockSpec((1, AT_HEADS, LANES, tm), lambda b, i: (b, 0, 0, i)),
                   pl.BlockSpec((1, tm // QBLK, AT_WIDTH, QBLK), lambda b, i: (b, i, 0, 0)),
                   pl.BlockSpec((1, IDX_HEADS, LANES, tm), lambda b, i: (b, 0, 0, i)),
                   pl.BlockSpec((1, 8, tm), lambda b, i: (b, 0, i))],
        out_shape=[jax.ShapeDtypeStruct((B, S, RW_COLS), F32),
                   jax.ShapeDtypeStruct((B, S, AT_WIDTH), BF16),
                   jax.ShapeDtypeStruct((B, S, LANES), BF16),
                   jax.ShapeDtypeStruct((B, AT_HEADS, LANES, S), BF16),
                   jax.ShapeDtypeStruct((B, S // QBLK, AT_WIDTH, QBLK), BF16),
                   jax.ShapeDtypeStruct((B, IDX_HEADS, LANES, S), BF16),
                   jax.ShapeDtypeStruct((B, 8, S), F32)],
        compiler_params=_cparams(("arbitrary", "arbitrary")),
        name="inproj", interpret=interpret,
    )(x, scale1, shift1, norm1_g.reshape(1, D), w_rw, w_k, w_t, kg, iw, ib, qg, gsum,
      cos_r, sin_r, cos_t, sin_t)


def _rwkv_kernel(p_ref, mu_ref, w0_ref, w2_ref, a0_ref, a2_ref, g2_ref, kk_ref, ka_ref, rk_ref,
                 lnw_ref, lnb_ref, gsum_ref, y_ref, s_ref, prev_ref, yt_ref):
    C = RW_CHUNK
    W = RW_WIDTH

    @pl.when(pl.program_id(1) == 0)
    def _():
        s_ref[...] = jnp.zeros_like(s_ref)
        prev_ref[...] = jnp.zeros_like(prev_ref)

    p = p_ref[0]
    row = lax.broadcasted_iota(I32, (C, 1), 0)
    pprev = jnp.where(row == 0, prev_ref[...], pltpu.roll(p, 1, 0))
    prev_ref[...] = p[C - 1:C]
    ps = p + (pprev - p) * mu_ref[...]
    r, k, v = ps[:, 0:W], ps[:, W:2 * W], ps[:, 2 * W:3 * W]
    o = 3 * W
    wl = ps[:, o:o + RW_LORA_W]
    al = ps[:, o + RW_LORA_W:o + RW_LORA_W + RW_LORA_A]
    gl = ps[:, o + RW_LORA_W + RW_LORA_A:]

    z = w0_ref[...] + _dot(jnp.tanh(wl).astype(BF16), w2_ref[...])
    nz = -z
    softplus = jnp.maximum(nz, 0.0) + jnp.log(1.0 + jnp.exp(-jnp.abs(nz)))
    logw = -jnp.exp(-softplus - 0.5)
    a = _sigmoid(a0_ref[...] + _dot(al.astype(BF16), a2_ref[...]))
    g = _dot(_sigmoid(gl).astype(BF16), g2_ref[...])
    gsum = gsum_ref[...]
    kk = k * kk_ref[...]
    ss = _dot((kk * kk).astype(BF16), gsum)
    kk = kk * (1.0 / jnp.maximum(jnp.sqrt(ss), 1e-12))
    k2 = k * (1.0 + (a - 1.0) * ka_ref[...])
    bb = kk * a

    cw = logw
    sh = 1
    while sh < C:
        cw = cw + jnp.where(row >= sh, pltpu.roll(cw, sh, 0), 0.0)
        sh *= 2
    cw_last = cw[C - 1:C]
    e_neg = jnp.exp(-cw)
    e_end = jnp.exp(cw_last - cw)
    rw = r * jnp.exp(cw)
    kkp = kk * jnp.exp(cw - logw)
    bw, kw = bb * e_neg, k2 * e_neg
    bend, kend = bb * e_end, k2 * e_end
    wc = jnp.exp(cw_last)
    vt_all = v.T.astype(BF16)

    ri = lax.broadcasted_iota(I32, (C, C), 0)
    ci = lax.broadcasted_iota(I32, (C, C), 1)
    strict = ri < ci
    incl = ri <= ci
    incl2 = jnp.concatenate([incl, incl], axis=0)
    lane_half = lax.broadcasted_iota(I32, (1, LANES), 1) // HEAD_DIM

    heads = range(RW_HEADS)
    pair = lambda h: slice((h // 2) * LANES, (h // 2 + 1) * LANES)
    own = [lane_half == (h % 2) for h in heads]
    lh = [jnp.concatenate([kkp[:, pair(2 * q)], rw[:, pair(2 * q)]], axis=0).astype(BF16)
          for q in range(RW_HEADS // 2)]
    rh = [jnp.where(own[h], jnp.concatenate([bw[:, pair(h)], kw[:, pair(h)]], axis=0), 0.0).astype(BF16)
          for h in heads]
    aat = [_dot_nt(rh[h], lh[h // 2]) for h in heads]
    s_old = [s_ref[h] for h in heads]
    sl = [_dot_nt(s_old[h].astype(BF16), lh[h // 2]) for h in heads]
    vt = [vt_all[h * HEAD_DIM:(h + 1) * HEAD_DIM] for h in heads]
    akt = [jnp.where(strict, aat[h][C:, :C], 0.0).astype(BF16) for h in heads]
    m = [jnp.where(strict, aat[h][:C, :C], 0.0).astype(BF16) for h in heads]
    xs = [-(sl[h][:, :C] + _dot(vt[h], akt[h])) for h in heads]
    xs = [xs[h] - _dot_split(xs[h], m[h]) for h in heads]
    lvl = 2
    while lvl < C:
        m = [_dot(m[h], m[h]).astype(BF16) for h in heads]
        xs = [xs[h] + _dot_split(xs[h], m[h]) for h in heads]
        lvl *= 2
    zt = [jnp.concatenate([xs[h].astype(BF16), vt[h]], axis=1) for h in heads]
    for h in heads:
        ymat = jnp.where(incl2, aat[h][:, C:], 0.0).astype(BF16)
        yt_ref[h * HEAD_DIM:(h + 1) * HEAD_DIM, :] = sl[h][:, C:] + _dot(zt[h], ymat)
    for h in heads:
        endz = jnp.where(own[h], jnp.concatenate([bend[:, pair(h)], kend[:, pair(h)]], axis=0),
                         0.0).astype(BF16)
        s_ref[h] = s_old[h] * wc[:, pair(h)] + _dot(zt[h], endz)

    yt = yt_ref[...].reshape(RW_HEADS, HEAD_DIM, C)
    mean = jnp.mean(yt, axis=1, keepdims=True)
    yc = yt - mean
    var = jnp.mean(yc * yc, axis=1, keepdims=True)
    lnw = lnw_ref[...].reshape(RW_HEADS, HEAD_DIM, 1)
    lnb = lnb_ref[...].reshape(RW_HEADS, HEAD_DIM, 1)
    yn = yc * lax.rsqrt(var + RW_GN_EPS) * lnw + lnb
    y = yn.reshape(W, C).T
    bonus = _dot((r * k2 * rk_ref[...]).astype(BF16), gsum) * v
    y_ref[0] = ((y + bonus) * g).astype(BF16)


def _rwkv(p_rw, rw_mu, rw_w0, rw_w2, rw_a0, rw_a2, rw_g2, rw_k_k, rw_k_a, rw_r_k, rw_ln_w, rw_ln_b,
          interpret):
    B, S, _ = p_rw.shape
    C, W = RW_CHUNK, RW_WIDTH
    li = np.arange(W)
    gsum = jnp.asarray((li[:, None] // HEAD_DIM == li[None, :] // HEAD_DIM).astype(np.float32)).astype(BF16)
    row = lambda a: a.reshape(1, -1)
    full = lambda shape: pl.BlockSpec(shape, lambda b, i: (0,) * len(shape))
    return pl.pallas_call(
        _rwkv_kernel,
        grid=(B, S // C),
        in_specs=[pl.BlockSpec((1, C, RW_COLS), lambda b, i: (b, i, 0)),
                  full((1, RW_COLS)), full((1, W)), full((RW_LORA_W, W)), full((1, W)),
                  full((RW_LORA_A, W)), full((RW_LORA_G, W)), full((1, W)), full((1, W)),
                  full((1, W)), full((W, 1)), full((W, 1)), full((W, W))],
        out_specs=pl.BlockSpec((1, C, W), lambda b, i: (b, i, 0)),
        out_shape=jax.ShapeDtypeStruct((B, S, W), BF16),
        scratch_shapes=[pltpu.VMEM((RW_HEADS, HEAD_DIM, LANES), F32),
                        pltpu.VMEM((1, RW_COLS), F32),
                        pltpu.VMEM((W, C), F32)],
        compiler_params=_cparams(("arbitrary", "arbitrary")),
        name="rwkv", interpret=interpret,
    )(p_rw, row(rw_mu), row(rw_w0), rw_w2.astype(BF16), row(rw_a0), rw_a2.astype(BF16),
      rw_g2.astype(BF16), row(rw_k_k), row(rw_k_a), row(rw_r_k), rw_ln_w.reshape(W, 1),
      rw_ln_b.reshape(W, 1), gsum)


def _dsa_kernel(topk, nbits, k_ref, vt_ref, ki_ref, qt_ref, qit_ref, wit_ref, og_ref, o_ref,
                key_s, acc_s, m_s, l_s, thr_s):
    j = pl.program_id(1)
    nkb = j + 1
    lane = lax.broadcasted_iota(I32, (QBLK, QBLK), 1)
    sub = lax.broadcasted_iota(I32, (QBLK, QBLK), 0)
    qpos = j * QBLK + lane
    wit = wit_ref[0]

    def score_blocks(i, carry):
        kbs = [i * SCORE_UNROLL + u for u in range(SCORE_UNROLL)]
        kib = [ki_ref[0, pl.ds(pl.multiple_of(kb * QBLK, QBLK), QBLK), :] for kb in kbs]
        lg = [[_dot(kib[u], qit_ref[0, hh]) for hh in range(IDX_HEADS)] for u in range(SCORE_UNROLL)]
        for u, kb in enumerate(kbs):
            s = wit[0:1, :] * jnp.maximum(lg[u][0], 0.0)
            for hh in range(1, IDX_HEADS):
                s = s + wit[hh:hh + 1, :] * jnp.maximum(lg[u][hh], 0.0)
            s = jnp.where(s == 0.0, 0.0, s)
            bits = pltpu.bitcast(s, I32)
            skey = jnp.where(bits < 0, bits ^ 0x7FFFFFFF, bits)
            key_s[kb] = jnp.where(kb * QBLK + sub <= qpos, skey, INT_MIN)
        return carry

    lax.fori_loop(0, pl.cdiv(nkb, SCORE_UNROLL), score_blocks, 0)

    @pl.when(nkb * QBLK <= topk)
    def _():
        thr_s[0:1, :] = jnp.full((1, QBLK), INT_MIN, I32)
        thr_s[1:2, :] = jnp.zeros((1, QBLK), I32)

    @pl.when(nkb * QBLK > topk)
    def _():
        n_done = pl.cdiv(nkb, SCORE_UNROLL) * SCORE_UNROLL
        n_cnt = pl.cdiv(nkb, COUNT_UNROLL)

        def fill(kb, carry):
            key_s[kb] = jnp.full((QBLK, QBLK), INT_MIN, I32)
            return carry

        lax.fori_loop(n_done, n_cnt * COUNT_UNROLL, fill, 0)

        def count(preds):
            def body(i, accs):
                accs = list(accs)
                for u in range(COUNT_UNROLL):
                    kb = i * COUNT_UNROLL + u
                    ky = key_s[kb]
                    for n, pred in enumerate(preds):
                        hit = pred(ky, kb * QBLK + sub).astype(I32)
                        accs[n] = accs[n] + jnp.sum(hit.reshape(QBLK // 8, 8, QBLK), axis=0)
                return tuple(accs)
            accs = lax.fori_loop(0, n_cnt, body, tuple(jnp.zeros((8, QBLK), I32) for _ in preds))
            return [jnp.sum(a, axis=0, keepdims=True) for a in accs]

        c0, = count([lambda ky, ix: ky >= 0])
        t0 = jnp.where(c0 >= topk, 0, INT_MIN).astype(I32)

        def bit_step(i, t):
            cand = t | jnp.left_shift(jnp.int32(1), 30 - i)
            c, = count([lambda ky, ix: ky >= cand])
            return jnp.where(c >= topk, cand, t)

        thr = lax.fori_loop(0, 31, bit_step, t0)
        n_gt, n_eq = count([lambda ky, ix: ky > thr, lambda ky, ix: ky == thr])
        need = topk - n_gt
        thr_s[0:1, :] = thr
        thr_s[1:2, :] = jnp.full((1, QBLK), 2 ** nbits, I32)

        @pl.when(jnp.max(jnp.abs(n_eq - need)) > 0)
        def _():
            def idx_step(i, mm):
                cand = mm | jnp.left_shift(jnp.int32(1), nbits - 1 - i)
                c, = count([lambda ky, ix: (ky == thr) & (ix < cand)])
                return jnp.where(c < need, cand, mm)

            thr_s[1:2, :] = lax.fori_loop(0, nbits, idx_step, jnp.zeros((1, QBLK), I32))

    thr = thr_s[0:1, :]
    mm = thr_s[1:2, :]
    m_s[...] = jnp.full_like(m_s, NEG_BIG)
    l_s[...] = jnp.zeros_like(l_s)
    acc_s[...] = jnp.zeros_like(acc_s)

    qt2 = [jnp.concatenate([qt_ref[0, 2 * q], qt_ref[0, 2 * q + 1]], axis=1)
           for q in range(AT_HEADS // 2)]

    def attn_blocks(i, carry):
        kbs = [i * ATTN_UNROLL + u for u in range(ATTN_UNROLL)]
        sel = []
        for kb in kbs:
            skey = key_s[kb]
            kidx = kb * QBLK + sub
            sel.append((kidx <= qpos) & ((skey > thr) | ((skey == thr) & (kidx <= mm))))
        sel = jnp.concatenate(sel, axis=0)
        kblk = [k_ref[0, pl.ds(pl.multiple_of(kb * QBLK, QBLK), QBLK), :] for kb in kbs]
        vtb = jnp.concatenate([vt_ref[0, kb] for kb in kbs], axis=1)
        s2 = [[_dot(kblk[u][:, q * LANES:(q + 1) * LANES], qt2[q]) for u in range(ATTN_UNROLL)]
              for q in range(AT_HEADS // 2)]
        pexp, alpha = [], []
        for hh in range(AT_HEADS):
            half = slice((hh % 2) * QBLK, (hh % 2 + 1) * QBLK)
            s = jnp.concatenate([s2[hh // 2][u][:, half] for u in range(ATTN_UNROLL)], axis=0)
            s = jnp.where(sel, s, NEG_BIG)
            m_old = m_s[hh:hh + 1, :]
            m_new = jnp.maximum(m_old, jnp.max(s, axis=0, keepdims=True))
            pe = jnp.exp(s - m_new)
            al = jnp.exp(m_old - m_new)
            l_s[hh:hh + 1, :] = al * l_s[hh:hh + 1, :] + jnp.sum(pe, axis=0, keepdims=True)
            m_s[hh:hh + 1, :] = m_new
            pexp.append(pe.astype(BF16))
            alpha.append(al)
        for hh in range(AT_HEADS):
            hs = slice(hh * HEAD_DIM, (hh + 1) * HEAD_DIM)
            acc_s[hs, :] = alpha[hh] * acc_s[hs, :] + _dot(vtb[hs, :], pexp[hh])
        return carry

    lax.fori_loop(0, pl.cdiv(nkb, ATTN_UNROLL), attn_blocks, 0)

    for hh in range(AT_HEADS):
        hs = slice(hh * HEAD_DIM, (hh + 1) * HEAD_DIM)
        oh = acc_s[hs, :] * (1.0 / l_s[hh:hh + 1, :])
        ms = jnp.mean(oh * oh, axis=0, keepdims=True)
        acc_s[hs, :] = oh * lax.rsqrt(ms + NORM_EPS) * og_ref[hs, :]
    o_ref[0] = acc_s[...].T.astype(BF16)


def _dsa(k, vt, ki, qt, qit, wit, at_out_g, interpret):
    B, S, _ = k.shape
    nq = S // QBLK
    topk = min(IDX_TOPK_MAX, S // 4)
    nbits = int(np.log2(S))
    assert 2 ** nbits == S and nq % COUNT_UNROLL == 0 and SCORE_UNROLL == ATTN_UNROLL
    assert COUNT_UNROLL % SCORE_UNROLL == 0
    return pl.pallas_call(
        functools.partial(_dsa_kernel, topk, nbits),
        grid=(B, nq),
        in_specs=[pl.BlockSpec((1, S, AT_WIDTH), lambda b, j: (b, 0, 0)),
                  pl.BlockSpec((1, nq, AT_WIDTH, QBLK), lambda b, j: (b, 0, 0, 0)),
                  pl.BlockSpec((1, S, LANES), lambda b, j: (b, 0, 0)),
                  pl.BlockSpec((1, AT_HEADS, LANES, QBLK), lambda b, j: (b, 0, 0, j)),
                  pl.BlockSpec((1, IDX_HEADS, LANES, QBLK), lambda b, j: (b, 0, 0, j)),
                  pl.BlockSpec((1, 8, QBLK), lambda b, j: (b, 0, j)),
                  pl.BlockSpec((AT_WIDTH, 1), lambda b, j: (0, 0))],
        out_specs=pl.BlockSpec((1, QBLK, AT_WIDTH), lambda b, j: (b, j, 0)),
        out_shape=jax.ShapeDtypeStruct((B, S, AT_WIDTH), BF16),
        scratch_shapes=[pltpu.VMEM((nq, QBLK, QBLK), I32),
                        pltpu.VMEM((AT_WIDTH, QBLK), F32),
                        pltpu.VMEM((AT_HEADS, QBLK), F32),
                        pltpu.VMEM((AT_HEADS, QBLK), F32),
                        pltpu.VMEM((8, QBLK), I32)],
        compiler_params=_cparams(("arbitrary", "arbitrary")),
        name="dsa", interpret=interpret,
    )(k, vt, ki, qt, qit, wit, at_out_g.reshape(AT_WIDTH, 1))


def _first_max(vals, idx, axis, sentinel):
    m = jnp.max(vals, axis=axis, keepdims=True)
    return m, jnp.min(jnp.where(vals == m, idx, sentinel), axis=axis, keepdims=True)


def _route_cols(logits_t, bias_col):
    E, tm = logits_t.shape
    pg = E // N_GROUPS
    scores = _sigmoid(logits_t)
    biased = scores + bias_col
    b3 = biased.reshape(N_GROUPS, pg, tm)
    r3 = lax.broadcasted_iota(I32, (N_GROUPS, pg, tm), 1)
    m1, first = _first_max(b3, r3, 1, pg)
    m2 = jnp.max(jnp.where(r3 == first, -jnp.inf, b3), axis=1, keepdims=True)
    cur = (m1 + m2).reshape(N_GROUPS, tm)
    grow = lax.broadcasted_iota(I32, (N_GROUPS, tm), 0)
    gsel = jnp.zeros((N_GROUPS, tm), F32)
    for _ in range(TOPK_GROUPS):
        _, gi = _first_max(cur, grow, 0, N_GROUPS)
        hit = grow == gi
        gsel = jnp.where(hit, 1.0, gsel)
        cur = jnp.where(hit, -jnp.inf, cur)
    gmask = jnp.broadcast_to(gsel.reshape(N_GROUPS, 1, tm), (N_GROUPS, pg, tm)).reshape(E, tm)
    cur = jnp.where(gmask > 0.0, biased, -jnp.inf)
    row = lax.broadcasted_iota(I32, (E, tm), 0)
    onehot = jnp.zeros((E, tm), F32)
    eids, gws = [], []
    for _ in range(TOP_K):
        _, ei = _first_max(cur, row, 0, E)
        hit = row == ei
        eids.append(ei)
        gws.append(jnp.sum(jnp.where(hit, scores, 0.0), axis=0, keepdims=True))
        onehot = jnp.where(hit, 1.0, onehot)
        cur = jnp.where(hit, -jnp.inf, cur)
    eid = jnp.concatenate(eids, axis=0)
    gw = jnp.concatenate(gws, axis=0)
    gw = gw * (ROUTED_SCALE / jnp.sum(gw, axis=0, keepdims=True))
    return eid, gw, onehot


def _post_kernel(x_ref, yrw_ref, yat_ref, g1_ref, sc_ref, sh_ref, g2_ref, ng_ref, wo_ref, rwt_ref,
                 rb_ref, s1_ref, s3_ref, s2_ref, base_ref, h2_ref, eid_ref, gw_ref, rank_ref, cnt_ref):
    W = RW_WIDTH
    tm = x_ref.shape[1]
    E = rwt_ref.shape[0]

    @pl.when((pl.program_id(0) == 0) & (pl.program_id(1) == 0))
    def _():
        cnt_ref[...] = jnp.zeros_like(cnt_ref)

    mix = _dot(yrw_ref[0], wo_ref[0:W, :]) + _dot(yat_ref[0], wo_ref[W:, :])
    x1 = x_ref[0] + g1_ref[0] * mix
    ms = jnp.mean(x1 * x1, axis=-1, keepdims=True)
    h2 = x1 * lax.rsqrt(ms + NORM_EPS) * ng_ref[...] * (1.0 + sc_ref[0]) + sh_ref[0]
    hb = h2.astype(BF16)
    h2_ref[0] = h2
    act = (_silu(_dot(hb, s1_ref[...])) * _dot(hb, s3_ref[...])).astype(BF16)
    base_ref[0] = x1 + g2_ref[0] * _dot(act, s2_ref[...])

    logits_t = lax.dot_general(rwt_ref[...], h2, (((1,), (1,)), ((), ())), precision=HIGHEST,
                               preferred_element_type=F32)
    eid, gw, onehot = _route_cols(logits_t, rb_ref[...])
    eid_ref[0] = eid
    gw_ref[0] = gw
    ti = lax.broadcasted_iota(I32, (tm, tm), 0)
    tj = lax.broadcasted_iota(I32, (tm, tm), 1)
    before = _dot(onehot.astype(BF16), (ti < tj).astype(BF16)) + cnt_ref[:, 0:1]
    row = lax.broadcasted_iota(I32, (E, tm), 0)
    ranks = [jnp.sum(jnp.where(row == eid[kk:kk + 1, :], before, 0.0), axis=0, keepdims=True)
             for kk in range(TOP_K)]
    rank_ref[0] = jnp.concatenate(ranks, axis=0).astype(I32)
    cnt_ref[...] = cnt_ref[...] + jnp.sum(onehot, axis=1, keepdims=True)


def _post(x, y_rw, y_at, gate1, scale2, shift2, gate2, norm2_g, w_out, router_w, router_bias,
          sw1, sw3, sw2, interpret):
    B, S, D = x.shape
    tm = min(S, 256)
    E = router_w.shape[1]
    sd = sw1.shape[1]
    full = lambda shape: pl.BlockSpec(shape, lambda b, i: (0,) * len(shape))
    tok = lambda w: pl.BlockSpec((1, tm, w), lambda b, i: (b, i, 0))
    per_b = pl.BlockSpec((1, 1, D), lambda b, i: (b, 0, 0))
    col8 = pl.BlockSpec((1, TOP_K, tm), lambda b, i: (b, 0, i))
    return pl.pallas_call(
        _post_kernel,
        grid=(B, S // tm),
        in_specs=[tok(D), tok(RW_WIDTH), tok(AT_WIDTH), per_b, per_b, per_b, per_b, full((1, D)),
                  full((D, D)), full((E, D)), full((E, 1)), full((D, sd)), full((D, sd)),
                  full((sd, D))],
        out_specs=[tok(D), tok(D), col8, col8, col8, full((E, LANES))],
        out_shape=[jax.ShapeDtypeStruct((B, S, D), F32),
                   jax.ShapeDtypeStruct((B, S, D), F32),
                   jax.ShapeDtypeStruct((B, TOP_K, S), I32),
                   jax.ShapeDtypeStruct((B, TOP_K, S), F32),
                   jax.ShapeDtypeStruct((B, TOP_K, S), I32),
                   jax.ShapeDtypeStruct((E, LANES), F32)],
        compiler_params=_cparams(("arbitrary", "arbitrary")),
        name="post", interpret=interpret,
    )(x, y_rw, y_at, gate1, scale2, shift2, gate2, norm2_g.reshape(1, D), w_out.astype(BF16),
      router_w.T, router_bias.reshape(E, 1), sw1.astype(BF16), sw3.astype(BF16), sw2.astype(BF16))


def _expert_kernel(be_ref, xs_ref, w1_ref, w3_ref, w2_ref, o_ref):
    del be_ref
    x = xs_ref[...].astype(BF16)
    h1 = _dot(x, w1_ref[0].astype(BF16))
    h3 = _dot(x, w3_ref[0].astype(BF16))
    act = (_silu(h1) * h3).astype(BF16)
    o_ref[...] = _dot(act, w2_ref[0].astype(BF16))


def _experts(xs, block_e, w1, w3, w2, interpret):
    P, D = xs.shape
    E, _, F = w1.shape
    nb = P // EXP_BLK
    grid_spec = pltpu.PrefetchScalarGridSpec(
        num_scalar_prefetch=1,
        grid=(nb,),
        in_specs=[pl.BlockSpec((EXP_BLK, D), lambda i, be: (i, 0)),
                  pl.BlockSpec((1, D, F), lambda i, be: (be[i], 0, 0)),
                  pl.BlockSpec((1, D, F), lambda i, be: (be[i], 0, 0)),
                  pl.BlockSpec((1, F, D), lambda i, be: (be[i], 0, 0))],
        out_specs=pl.BlockSpec((EXP_BLK, D), lambda i, be: (i, 0)),
    )
    return pl.pallas_call(
        _expert_kernel,
        grid_spec=grid_spec,
        out_shape=jax.ShapeDtypeStruct((P, D), F32),
        compiler_params=_cparams(("arbitrary",)),
        name="experts", interpret=interpret,
    )(block_e, xs, w1, w3, w2)


def _row_copy(src, s_row, dst, d_row, sem):
    return pltpu.make_async_copy(src.at[pl.ds(s_row, 1)], dst.at[pl.ds(d_row, 1)], sem)


def _dispatch_kernel(dest_ref, h2_ref, xs_in_ref, xs_ref, sem):
    del xs_in_ref
    td = h2_ref.shape[0]

    def issue(t, carry):
        for kk in range(TOP_K):
            _row_copy(h2_ref, t, xs_ref, dest_ref[0, 0, kk * td + t], sem).start()
        return carry

    def drain(t, carry):
        for kk in range(TOP_K):
            _row_copy(h2_ref, t, xs_ref, dest_ref[0, 0, kk * td + t], sem).wait()
        return carry

    lax.fori_loop(0, td, issue, 0)
    lax.fori_loop(0, td, drain, 0)


def _dispatch(h2, dest_tiles, n_slots, interpret):
    T, D = h2.shape
    nt, _, n = dest_tiles.shape
    td = n // TOP_K
    xs0 = jnp.zeros((n_slots, D), F32)
    return pl.pallas_call(
        _dispatch_kernel,
        grid=(nt,),
        in_specs=[pl.BlockSpec((1, 1, n), lambda i: (i, 0, 0), memory_space=pltpu.SMEM),
                  pl.BlockSpec((td, D), lambda i: (i, 0)),
                  pl.BlockSpec(memory_space=pl.ANY)],
        out_specs=pl.BlockSpec(memory_space=pl.ANY),
        out_shape=jax.ShapeDtypeStruct((n_slots, D), F32),
        scratch_shapes=[pltpu.SemaphoreType.DMA(())],
        input_output_aliases={2: 0},
        compiler_params=_cparams(("arbitrary",)),
        name="dispatch", interpret=interpret,
    )(dest_tiles, h2, xs0)


def _combine_kernel(dest_ref, base_ref, g2_ref, gw_ref, ys_ref, o_ref, buf, sem):
    td = base_ref.shape[0]

    def issue(t, carry):
        for kk in range(TOP_K):
            _row_copy(ys_ref, dest_ref[0, 0, kk * td + t], buf.at[kk], t, sem).start()
        return carry

    def drain(t, carry):
        for kk in range(TOP_K):
            _row_copy(ys_ref, dest_ref[0, 0, kk * td + t], buf.at[kk], t, sem).wait()
        return carry

    lax.fori_loop(0, td, issue, 0)
    lax.fori_loop(0, td, drain, 0)
    gw = gw_ref[...]
    acc = gw[:, 0:1] * buf[0]
    for kk in range(1, TOP_K):
        acc = acc + gw[:, kk:kk + 1] * buf[kk]
    o_ref[...] = base_ref[...] + g2_ref[0] * acc


def _combine(base, gate2, gw_tok, ys, dest_tiles, tiles_per_batch, interpret):
    T, D = base.shape
    nt, _, n = dest_tiles.shape
    td = n // TOP_K
    return pl.pallas_call(
        _combine_kernel,
        grid=(nt,),
        in_specs=[pl.BlockSpec((1, 1, n), lambda i: (i, 0, 0), memory_space=pltpu.SMEM),
                  pl.BlockSpec((td, D), lambda i: (i, 0)),
                  pl.BlockSpec((1, 1, D), lambda i: (i // tiles_per_batch, 0, 0)),
                  pl.BlockSpec((td, TOP_K), lambda i: (i, 0)),
                  pl.BlockSpec(memory_space=pl.ANY)],
        out_specs=pl.BlockSpec((td, D), lambda i: (i, 0)),
        out_shape=jax.ShapeDtypeStruct((T, D), F32),
        scratch_shapes=[pltpu.VMEM((TOP_K, td, D), F32), pltpu.SemaphoreType.DMA(())],
        compiler_params=_cparams(("arbitrary",)),
        name="combine", interpret=interpret,
    )(dest_tiles, base, gate2, gw_tok, ys)


def _slots_kernel(eid_ref, rank_ref, pstart_ref, dest_ref):
    td = eid_ref.shape[2]
    E = pstart_ref.shape[0]
    row = lax.broadcasted_iota(I32, (E, td), 0)
    pstart = pstart_ref[...]
    eid = eid_ref[0]
    for kk in range(TOP_K):
        base = jnp.sum(jnp.where(row == eid[kk:kk + 1, :], pstart, 0), axis=0, keepdims=True)
        dest_ref[0, :, kk * td:(kk + 1) * td] = base + rank_ref[0, kk:kk + 1, :]


def _slot_plan(counts, eid_t, rank_t, td, interpret):
    B, _, S = eid_t.shape
    E = counts.shape[0]
    padded = (counts + EXP_BLK - 1) // EXP_BLK * EXP_BLK
    pend = jnp.cumsum(padded)
    pstart = (pend - padded).astype(I32)
    nb = -(-(B * S * TOP_K + E * (EXP_BLK - 1)) // EXP_BLK)
    block_e = jnp.searchsorted(pend, jnp.arange(nb, dtype=I32) * EXP_BLK, side='right')
    block_e = jnp.minimum(block_e, E - 1).astype(I32)
    nt = S // td
    dest_tiles = pl.pallas_call(
        _slots_kernel,
        grid=(B, nt),
        in_specs=[pl.BlockSpec((1, TOP_K, td), lambda b, i: (b, 0, i)),
                  pl.BlockSpec((1, TOP_K, td), lambda b, i: (b, 0, i)),
                  pl.BlockSpec((E, 1), lambda b, i: (0, 0))],
        out_specs=pl.BlockSpec((1, 1, TOP_K * td), lambda b, i: (b * nt + i, 0, 0)),
        out_shape=jax.ShapeDtypeStruct((B * nt, 1, TOP_K * td), I32),
        compiler_params=_cparams(("arbitrary", "arbitrary")),
        name="slots", interpret=interpret,
    )(eid_t, rank_t, pstart.reshape(E, 1))
    return block_e, dest_tiles, nb * EXP_BLK


def _forward(x, c, positions, w_ada, b_ada, norm1_g, norm2_g, w_in, rw_mu, rw_w0, rw_w2,
             rw_a0, rw_a2, rw_g2, rw_k_k, rw_k_a, rw_r_k, rw_ln_w, rw_ln_b, q_norm_g,
             k_norm_g, idx_ln_w, idx_ln_b, at_out_g, w_out, router_w, router_bias,
             exp_w1, exp_w3, exp_w2, shared_w1, shared_w3, shared_w2, interpret=False):
    B, S, D = x.shape
    depth = w_ada.shape[0]
    for l in range(depth):
        mod = _mod(c, w_ada[l], b_ada[l], interpret)
        shift1, scale1, gate1, shift2, scale2, gate2 = [
            m.reshape(B, 1, D) for m in jnp.split(mod, 6, axis=-1)]
        tabs = _rope_tables(positions, interpret)
        p_rw, k, ki, qt, vt, qit, wit = _inproj(
            x, scale1, shift1, norm1_g[l], w_in[l], k_norm_g[l], idx_ln_w[l], idx_ln_b[l],
            q_norm_g[l], tabs, interpret)
        y_rw = _rwkv(p_rw, rw_mu[l], rw_w0[l], rw_w2[l], rw_a0[l], rw_a2[l], rw_g2[l], rw_k_k[l],
                     rw_k_a[l], rw_r_k[l], rw_ln_w[l], rw_ln_b[l], interpret)
        y_at = _dsa(k, vt, ki, qt, qit, wit, at_out_g[l], interpret)
        base, h2, eid_t, gw_t, rank_t, cnt = _post(
            x, y_rw, y_at, gate1, scale2, shift2, gate2, norm2_g[l], w_out[l], router_w[l],
            router_bias[l], shared_w1[l], shared_w3[l], shared_w2[l], interpret)
        T = B * S
        td = min(S, ROW_TILE)
        block_e, dest_tiles, n_slots = _slot_plan(cnt[:, 0].astype(I32), eid_t, rank_t, td,
                                                    interpret)
        xs = _dispatch(h2.reshape(T, D), dest_tiles, n_slots, interpret)
        ys = _experts(xs, block_e, exp_w1[l], exp_w3[l], exp_w2[l], interpret)
        gw_tok = gw_t.transpose(0, 2, 1).reshape(T, TOP_K)
        x = _combine(base.reshape(T, D), gate2, gw_tok, ys, dest_tiles, S // td,
                     interpret).reshape(B, S, D)
    return x


def kernel(x, c, positions, w_ada, b_ada, norm1_g, norm2_g, w_in, rw_mu, rw_w0, rw_w2, rw_a0, rw_a2, rw_g2, rw_k_k, rw_k_a, rw_r_k, rw_ln_w, rw_ln_b, q_norm_g, k_norm_g, idx_ln_w, idx_ln_b, at_out_g, w_out, router_w, router_bias, exp_w1, exp_w3, exp_w2, shared_w1, shared_w3, shared_w2):
    return _forward(x, c, positions, w_ada, b_ada, norm1_g, norm2_g, w_in, rw_mu, rw_w0, rw_w2,
                    rw_a0, rw_a2, rw_g2, rw_k_k, rw_k_a, rw_r_k, rw_ln_w, rw_ln_b, q_norm_g,
                    k_norm_g, idx_ln_w, idx_ln_b, at_out_g, w_out, router_w, router_bias,
                    exp_w1, exp_w3, exp_w2, shared_w1, shared_w3, shared_w2)
```

```python
import functools

import jax
import jax.numpy as jnp
import numpy as np
from jax import lax
from jax.experimental import pallas as pl
from jax.experimental.pallas import tpu as pltpu

F32 = jnp.float32
BF16 = jnp.bfloat16
I32 = jnp.int32
HIGHEST = lax.Precision.HIGHEST

LANES = 128
HEAD_DIM = 64
HALF = HEAD_DIM // 2
RW_HEADS = 8
RW_WIDTH = RW_HEADS * HEAD_DIM
AT_HEADS = 8
AT_WIDTH = AT_HEADS * HEAD_DIM
IDX_HEADS = 4
RW_LORA_W, RW_LORA_A, RW_LORA_G = 64, 64, 128
RW_COLS = 3 * RW_WIDTH + RW_LORA_W + RW_LORA_A + RW_LORA_G
KI_OFF = 3 * AT_WIDTH + IDX_HEADS * HEAD_DIM
WI_OFF = KI_OFF + HEAD_DIM
PT_ROWS = 2 * AT_WIDTH + IDX_HEADS * HEAD_DIM + 8
PK_COLS = AT_WIDTH + LANES
ROPE_THETA = 10000.0
NORM_EPS = 1e-6
LN_EPS = 1e-6
RW_GN_EPS = 64e-5
IDX_TOPK_MAX = 256
N_EXPERTS = 256
TOP_K = 8
N_GROUPS = 8
TOPK_GROUPS = 4
ROUTED_SCALE = 2.5
INT_MIN = -2 ** 31
PAIR_HI_MASK = -65536
NEG_BIG = -1e30

RW_CHUNK = 128
QBLK = 128
SCORE_UNROLL = 2
ATTN_UNROLL = 2
COUNT_UNROLL = 4
EXP_BLK = 256
ROW_TILE = 256
VMEM_LIMIT = 48 * 1024 * 1024


def _cparams(sem):
    return pltpu.CompilerParams(dimension_semantics=sem, vmem_limit_bytes=VMEM_LIMIT)


def _sigmoid(x):
    return 1.0 / (1.0 + jnp.exp(-x))


def _silu(x):
    return x * _sigmoid(x)


def _dot(a, b):
    return jnp.dot(a, b, preferred_element_type=F32)


def _dot_split(a, b):
    hi = a.astype(BF16)
    lo = (a - hi.astype(F32)).astype(BF16)
    return _dot(hi, b) + _dot(lo, b)


def _pack_halves(x):
    w = x.shape[1] // 2
    lo = pltpu.bitcast(x[:, :w].astype(BF16).astype(F32), I32)
    hi = pltpu.bitcast(x[:, w:].astype(BF16).astype(F32), I32)
    return (hi & PAIR_HI_MASK) | lax.shift_right_logical(lo, 16)


def _unpack_halves(p):
    return pltpu.bitcast(p << 16, F32), pltpu.bitcast(p & PAIR_HI_MASK, F32)


def _dot_nt(a, b):
    return lax.dot_general(a, b, (((1,), (1,)), ((), ())), preferred_element_type=F32)


def _mod_kernel(c_ref, w_ref, b_ref, o_ref):
    c = c_ref[...]
    o_ref[...] = jnp.dot(_silu(c), w_ref[...], precision=HIGHEST,
                         preferred_element_type=F32) + b_ref[...]


def _mod(c, w_ada, b_ada, interpret):
    B, D = c.shape
    n = w_ada.shape[1] // D
    return pl.pallas_call(
        _mod_kernel,
        grid=(n,),
        in_specs=[pl.BlockSpec((B, D), lambda i: (0, 0)),
                  pl.BlockSpec((D, D), lambda i: (0, i)),
                  pl.BlockSpec((1, D), lambda i: (0, i))],
        out_specs=pl.BlockSpec((B, D), lambda i: (0, i)),
        out_shape=jax.ShapeDtypeStruct((B, n * D), F32),
        compiler_params=_cparams(("arbitrary",)),
        name="mod", interpret=interpret,
    )(c, w_ada, b_ada.reshape(1, -1))


def _rope_tab_kernel(pc_ref, pr_ref, cr_ref, sr_ref, ct_ref, st_ref):
    log_theta = float(np.log(ROPE_THETA))
    lane = lax.broadcasted_iota(I32, (1, LANES), 1)
    inv_r = jnp.exp((lane % HALF).astype(F32) * (-log_theta / HALF))
    ang = pc_ref[0].astype(F32) * inv_r
    cr_ref[0] = jnp.cos(ang)
    sr_ref[0] = jnp.where((lane % HEAD_DIM) < HALF, -jnp.sin(ang), jnp.sin(ang))
    sub = lax.broadcasted_iota(I32, (HALF, 1), 0)
    inv_c = jnp.exp(sub.astype(F32) * (-log_theta / HALF))
    ang_t = inv_c * pr_ref[0].astype(F32)
    ct_ref[0] = jnp.cos(ang_t)
    st_ref[0] = jnp.sin(ang_t)


def _rope_tables(positions, interpret):
    B, S = positions.shape
    ts = min(S, 512)
    return pl.pallas_call(
        _rope_tab_kernel,
        grid=(B, S // ts),
        in_specs=[pl.BlockSpec((1, ts, 1), lambda b, i: (b, i, 0)),
                  pl.BlockSpec((1, 1, ts), lambda b, i: (b, 0, i))],
        out_specs=[pl.BlockSpec((1, ts, LANES), lambda b, i: (b, i, 0)),
                   pl.BlockSpec((1, ts, LANES), lambda b, i: (b, i, 0)),
                   pl.BlockSpec((1, HALF, ts), lambda b, i: (b, 0, i)),
                   pl.BlockSpec((1, HALF, ts), lambda b, i: (b, 0, i))],
        out_shape=[jax.ShapeDtypeStruct((B, S, LANES), F32),
                   jax.ShapeDtypeStruct((B, S, LANES), F32),
                   jax.ShapeDtypeStruct((B, HALF, S), F32),
                   jax.ShapeDtypeStruct((B, HALF, S), F32)],
        compiler_params=_cparams(("arbitrary", "arbitrary")),
        name="rope_tab", interpret=interpret,
    )(positions.reshape(B, S, 1), positions.reshape(B, 1, S))


def _rope_rows(y, cos, sin_signed):
    lane = lax.broadcasted_iota(I32, (1, LANES), 1)
    partner = jnp.where((lane % HEAD_DIM) < HALF,
                        pltpu.roll(y, LANES - HALF, 1), pltpu.roll(y, HALF, 1))
    return y * cos + partner * sin_signed


def _rope_cols(y, cos_t, sin_t):
    x1, x2 = y[:, :HALF], y[:, HALF:]
    return jnp.concatenate([x1 * cos_t - x2 * sin_t, x2 * cos_t + x1 * sin_t], axis=1)


def _inproj_kernel(x_ref, sc_ref, sh_ref, g_ref, wrw_ref, wk_ref, wt_ref, kg_ref, iw_ref, ib_ref,
                   qg_ref, gsum_ref, cr_ref, sr_ref, ct_ref, st_ref,
                   prw_ref, k_ref, ki_ref, qt_ref, vt_ref, qit_ref, wit_ref):
    tm = x_ref.shape[1]
    x = x_ref[0]
    ms = jnp.mean(x * x, axis=-1, keepdims=True)
    h = x * lax.rsqrt(ms + NORM_EPS) * g_ref[...] * (1.0 + sc_ref[0]) + sh_ref[0]
    hb = h.astype(BF16)
    prw_ref[0] = _dot(hb, wrw_ref[...])
    pk = _dot(hb, wk_ref[...])
    pt = _dot_nt(wt_ref[...], hb)

    cos_r, sin_r = cr_ref[0], sr_ref[0]
    gsum = gsum_ref[...]
    inv_hd = 1.0 / HEAD_DIM
    for p in range(AT_WIDTH // LANES):
        xk = pk[:, p * LANES:(p + 1) * LANES]
        ss = jnp.dot(xk * xk, gsum, precision=HIGHEST, preferred_element_type=F32)
        y = xk * lax.rsqrt(ss * inv_hd + NORM_EPS) * kg_ref[...]
        k_ref[0, :, p * LANES:(p + 1) * LANES] = _rope_rows(y, cos_r, sin_r).astype(BF16)
    xi = pk[:, AT_WIDTH:AT_WIDTH + LANES]
    mu = jnp.dot(xi, gsum, precision=HIGHEST, preferred_element_type=F32) * inv_hd
    xc = xi - mu
    var = jnp.dot(xc * xc, gsum, precision=HIGHEST, preferred_element_type=F32) * inv_hd
    yi = xc * lax.rsqrt(var + LN_EPS) * iw_ref[...] + ib_ref[...]
    ki_ref[0] = _rope_rows(yi, cos_r, sin_r).astype(BF16)

    cos_t, sin_t = ct_ref[0][None], st_ref[0][None]
    xq = pt[0:AT_WIDTH].reshape(AT_HEADS, HEAD_DIM, tm)
    msq = jnp.mean(xq * xq, axis=1, keepdims=True)
    yq = xq * lax.rsqrt(msq + NORM_EPS) * qg_ref[...][None]
    yq = _rope_cols(yq, cos_t, sin_t) * (HEAD_DIM ** -0.5)
    zq = jnp.zeros((HEAD_DIM, tm), BF16)
    for hh in range(AT_HEADS):
        parts = [yq[hh].astype(BF16), zq] if hh % 2 == 0 else [zq, yq[hh].astype(BF16)]
        qt_ref[0, hh] = jnp.concatenate(parts, axis=0)
    vt = pt[AT_WIDTH:2 * AT_WIDTH].astype(BF16)
    for cblk in range(tm // QBLK):
        vt_ref[0, cblk] = vt[:, cblk * QBLK:(cblk + 1) * QBLK]
    xqi = pt[2 * AT_WIDTH:2 * AT_WIDTH + IDX_HEADS * HEAD_DIM].reshape(IDX_HEADS, HEAD_DIM, tm)
    yqi = _rope_cols(xqi, cos_t, sin_t)
    for hh in range(IDX_HEADS):
        qit_ref[0, hh] = jnp.concatenate([yqi[hh].astype(BF16), zq], axis=0)
    wit_ref[0] = pt[PT_ROWS - 8:PT_ROWS] * (IDX_HEADS ** -0.5 * HEAD_DIM ** -0.5)


def _inproj(x, scale1, shift1, norm1_g, w_in, k_norm_g, idx_ln_w, idx_ln_b, q_norm_g,
            tabs, interpret):
    B, S, D = x.shape
    tm = min(S, 256)
    cos_r, sin_r, cos_t, sin_t = tabs
    w_at = w_in[:, RW_COLS:]
    w_rw = w_in[:, :RW_COLS].astype(BF16)
    w_k = jnp.concatenate([w_at[:, AT_WIDTH:2 * AT_WIDTH], w_at[:, KI_OFF:KI_OFF + HEAD_DIM],
                           jnp.zeros((D, HEAD_DIM), F32)], axis=1).astype(BF16)
    w_t = jnp.concatenate([w_at[:, 0:AT_WIDTH], w_at[:, 2 * AT_WIDTH:3 * AT_WIDTH],
                           w_at[:, 3 * AT_WIDTH:KI_OFF], w_at[:, WI_OFF:WI_OFF + IDX_HEADS],
                           jnp.zeros((D, 8 - IDX_HEADS), F32)], axis=1).T.astype(BF16)
    kg = jnp.tile(k_norm_g, 2).reshape(1, LANES)
    zpad = jnp.zeros((HEAD_DIM,), F32)
    iw = jnp.concatenate([idx_ln_w, zpad]).reshape(1, LANES)
    ib = jnp.concatenate([idx_ln_b, zpad]).reshape(1, LANES)
    qg = q_norm_g.reshape(HEAD_DIM, 1)
    li = np.arange(LANES)
    gsum = jnp.asarray((li[:, None] // HEAD_DIM == li[None, :] // HEAD_DIM).astype(np.float32))

    full = lambda shape: pl.BlockSpec(shape, lambda b, i: (0,) * len(shape))
    return pl.pallas_call(
        _inproj_kernel,
        grid=(B, S // tm),
        in_specs=[pl.BlockSpec((1, tm, D), lambda b, i: (b, i, 0)),
                  pl.BlockSpec((1, 1, D), lambda b, i: (b, 0, 0)),
                  pl.BlockSpec((1, 1, D), lambda b, i: (b, 0, 0)),
                  full((1, D)), full((D, RW_COLS)), full((D, PK_COLS)), full((PT_ROWS, D)),
                  full((1, LANES)), full((1, LANES)), full((1, LANES)), full((HEAD_DIM, 1)),
                  full((LANES, LANES)),
                  pl.BlockSpec((1, tm, LANES), lambda b, i: (b, i, 0)),
                  pl.BlockSpec((1, tm, LANES), lambda b, i: (b, i, 0)),
                  pl.BlockSpec((1, HALF, tm), lambda b, i: (b, 0, i)),
                  pl.BlockSpec((1, HALF, tm), lambda b, i: (b, 0, i))],
        out_specs=[pl.BlockSpec((1, tm, RW_COLS), lambda b, i: (b, i, 0)),
                   pl.BlockSpec((1, tm, AT_WIDTH), lambda b, i: (b, i, 0)),
                   pl.BlockSpec((1, tm, LANES), lambda b, i: (b, i, 0)),
                   pl.BlockSpec((1, AT_HEADS, LANES, tm), lambda b, i: (b, 0, 0, i)),
                   pl.BlockSpec((1, tm // QBLK, AT_WIDTH, QBLK), lambda b, i: (b, i, 0, 0)),
                   pl.BlockSpec((1, IDX_HEADS, LANES, tm), lambda b, i: (b, 0, 0, i)),
                   pl.BlockSpec((1, 8, tm), lambda b, i: (b, 0, i))],
        out_shape=[jax.ShapeDtypeStruct((B, S, RW_COLS), F32),
                   jax.ShapeDtypeStruct((B, S, AT_WIDTH), BF16),
                   jax.ShapeDtypeStruct((B, S, LANES), BF16),
                   jax.ShapeDtypeStruct((B, AT_HEADS, LANES, S), BF16),
                   jax.ShapeDtypeStruct((B, S // QBLK, AT_WIDTH, QBLK), BF16),
                   jax.ShapeDtypeStruct((B, IDX_HEADS, LANES, S), BF16),
                   jax.ShapeDtypeStruct((B, 8, S), F32)],
        compiler_params=_cparams(("arbitrary", "arbitrary")),
        name="inproj", interpret=interpret,
    )(x, scale1, shift1, norm1_g.reshape(1, D), w_rw, w_k, w_t, kg, iw, ib, qg, gsum,
      cos_r, sin_r, cos_t, sin_t)


def _rwkv_kernel(p_ref, mu_ref, w0_ref, w2_ref, a0_ref, a2_ref, g2_ref, kk_ref, ka_ref, rk_ref,
                 lnw_ref, lnb_ref, gsum_ref, y_ref, s_ref, prev_ref, yt_ref):
    C = RW_CHUNK
    W = RW_WIDTH

    @pl.when(pl.program_id(1) == 0)
    def _():
        s_ref[...] = jnp.zeros_like(s_ref)
        prev_ref[...] = jnp.zeros_like(prev_ref)

    p = p_ref[0]
    row = lax.broadcasted_iota(I32, (C, 1), 0)
    pprev = jnp.where(row == 0, prev_ref[...], pltpu.roll(p, 1, 0))
    prev_ref[...] = p[C - 1:C]
    ps = p + (pprev - p) * mu_ref[...]
    r, k, v = ps[:, 0:W], ps[:, W:2 * W], ps[:, 2 * W:3 * W]
    o = 3 * W
    wl = ps[:, o:o + RW_LORA_W]
    al = ps[:, o + RW_LORA_W:o + RW_LORA_W + RW_LORA_A]
    gl = ps[:, o + RW_LORA_W + RW_LORA_A:]

    z = w0_ref[...] + _dot(jnp.tanh(wl).astype(BF16), w2_ref[...])
    nz = -z
    softplus = jnp.maximum(nz, 0.0) + jnp.log(1.0 + jnp.exp(-jnp.abs(nz)))
    logw = -jnp.exp(-softplus - 0.5)
    a = _sigmoid(a0_ref[...] + _dot(al.astype(BF16), a2_ref[...]))
    g = _dot(_sigmoid(gl).astype(BF16), g2_ref[...])
    gsum = gsum_ref[...]
    kk = k * kk_ref[...]
    ss = _dot((kk * kk).astype(BF16), gsum)
    kk = kk * (1.0 / jnp.maximum(jnp.sqrt(ss), 1e-12))
    k2 = k * (1.0 + (a - 1.0) * ka_ref[...])
    bb = kk * a

    cw = logw
    sh = 1
    while sh < C:
        cw = cw + jnp.where(row >= sh, pltpu.roll(cw, sh, 0), 0.0)
        sh *= 2
    cw_last = cw[C - 1:C]
    e_neg = jnp.exp(-cw)
    e_end = jnp.exp(cw_last - cw)
    rw = r * jnp.exp(cw)
    kkp = kk * jnp.exp(cw - logw)
    bw, kw = bb * e_neg, k2 * e_neg
    bend, kend = bb * e_end, k2 * e_end
    wc = jnp.exp(cw_last)
    vt_all = v.T.astype(BF16)

    ri = lax.broadcasted_iota(I32, (C, C), 0)
    ci = lax.broadcasted_iota(I32, (C, C), 1)
    strict = ri < ci
    incl = ri <= ci
    incl2 = jnp.concatenate([incl, incl], axis=0)
    lane_half = lax.broadcasted_iota(I32, (1, LANES), 1) // HEAD_DIM

    heads = range(RW_HEADS)
    pair = lambda h: slice((h // 2) * LANES, (h // 2 + 1) * LANES)
    own = [lane_half == (h % 2) for h in heads]
    lh = [jnp.concatenate([kkp[:, pair(2 * q)], rw[:, pair(2 * q)]], axis=0).astype(BF16)
          for q in range(RW_HEADS // 2)]
    rh = [jnp.where(own[h], jnp.concatenate([bw[:, pair(h)], kw[:, pair(h)]], axis=0), 0.0).astype(BF16)
          for h in heads]
    aat = [_dot_nt(rh[h], lh[h // 2]) for h in heads]
    s_old = [s_ref[h] for h in heads]
    sl = [_dot_nt(s_old[h].astype(BF16), lh[h // 2]) for h in heads]
    vt = [vt_all[h * HEAD_DIM:(h + 1) * HEAD_DIM] for h in heads]
    akt = [jnp.where(strict, aat[h][C:, :C], 0.0).astype(BF16) for h in heads]
    m = [jnp.where(strict, aat[h][:C, :C], 0.0).astype(BF16) for h in heads]
    xs = [-(sl[h][:, :C] + _dot(vt[h], akt[h])) for h in heads]
    xs = [xs[h] - _dot_split(xs[h], m[h]) for h in heads]
    lvl = 2
    while lvl < C:
        m = [_dot(m[h], m[h]).astype(BF16) for h in heads]
        xs = [xs[h] + _dot_split(xs[h], m[h]) for h in heads]
        lvl *= 2
    zt = [jnp.concatenate([xs[h].astype(BF16), vt[h]], axis=1) for h in heads]
    for h in heads:
        ymat = jnp.where(incl2, aat[h][:, C:], 0.0).astype(BF16)
        yt_ref[h * HEAD_DIM:(h + 1) * HEAD_DIM, :] = sl[h][:, C:] + _dot(zt[h], ymat)
    for h in heads:
        endz = jnp.where(own[h], jnp.concatenate([bend[:, pair(h)], kend[:, pair(h)]], axis=0),
                         0.0).astype(BF16)
        s_ref[h] = s_old[h] * wc[:, pair(h)] + _dot(zt[h], endz)

    yt = yt_ref[...].reshape(RW_HEADS, HEAD_DIM, C)
    mean = jnp.mean(yt, axis=1, keepdims=True)
    yc = yt - mean
    var = jnp.mean(yc * yc, axis=1, keepdims=True)
    lnw = lnw_ref[...].reshape(RW_HEADS, HEAD_DIM, 1)
    lnb = lnb_ref[...].reshape(RW_HEADS, HEAD_DIM, 1)
    yn = yc * lax.rsqrt(var + RW_GN_EPS) * lnw + lnb
    y = yn.reshape(W, C).T
    bonus = _dot((r * k2 * rk_ref[...]).astype(BF16), gsum) * v
    y_ref[0] = ((y + bonus) * g).astype(BF16)


def _rwkv(p_rw, rw_mu, rw_w0, rw_w2, rw_a0, rw_a2, rw_g2, rw_k_k, rw_k_a, rw_r_k, rw_ln_w, rw_ln_b,
          interpret):
    B, S, _ = p_rw.shape
    C, W = RW_CHUNK, RW_WIDTH
    li = np.arange(W)
    gsum = jnp.asarray((li[:, None] // HEAD_DIM == li[None, :] // HEAD_DIM).astype(np.float32)).astype(BF16)
    row = lambda a: a.reshape(1, -1)
    full = lambda shape: pl.BlockSpec(shape, lambda b, i: (0,) * len(shape))
    return pl.pallas_call(
        _rwkv_kernel,
        grid=(B, S // C),
        in_specs=[pl.BlockSpec((1, C, RW_COLS), lambda b, i: (b, i, 0)),
                  full((1, RW_COLS)), full((1, W)), full((RW_LORA_W, W)), full((1, W)),
                  full((RW_LORA_A, W)), full((RW_LORA_G, W)), full((1, W)), full((1, W)),
                  full((1, W)), full((W, 1)), full((W, 1)), full((W, W))],
        out_specs=pl.BlockSpec((1, C, W), lambda b, i: (b, i, 0)),
        out_shape=jax.ShapeDtypeStruct((B, S, W), BF16),
        scratch_shapes=[pltpu.VMEM((RW_HEADS, HEAD_DIM, LANES), F32),
                        pltpu.VMEM((1, RW_COLS), F32),
                        pltpu.VMEM((W, C), F32)],
        compiler_params=_cparams(("arbitrary", "arbitrary")),
        name="rwkv", interpret=interpret,
    )(p_rw, row(rw_mu), row(rw_w0), rw_w2.astype(BF16), row(rw_a0), rw_a2.astype(BF16),
      rw_g2.astype(BF16), row(rw_k_k), row(rw_k_a), row(rw_r_k), rw_ln_w.reshape(W, 1),
      rw_ln_b.reshape(W, 1), gsum)


def _dsa_kernel(topk, nbits, k_ref, vt_ref, ki_ref, qt_ref, qit_ref, wit_ref, og_ref, o_ref,
                key_s, acc_s, m_s, l_s, thr_s):
    j = pl.program_id(1)
    nkb = j + 1
    lane = lax.broadcasted_iota(I32, (QBLK, QBLK), 1)
    sub = lax.broadcasted_iota(I32, (QBLK, QBLK), 0)
    qpos = j * QBLK + lane
    wit = wit_ref[0]

    def score_blocks(i, carry):
        kbs = [i * SCORE_UNROLL + u for u in range(SCORE_UNROLL)]
        kib = [ki_ref[0, pl.ds(pl.multiple_of(kb * QBLK, QBLK), QBLK), :] for kb in kbs]
        lg = [[_dot(kib[u], qit_ref[0, hh]) for hh in range(IDX_HEADS)] for u in range(SCORE_UNROLL)]
        for u, kb in enumerate(kbs):
            s = wit[0:1, :] * jnp.maximum(lg[u][0], 0.0)
            for hh in range(1, IDX_HEADS):
                s = s + wit[hh:hh + 1, :] * jnp.maximum(lg[u][hh], 0.0)
            s = jnp.where(s == 0.0, 0.0, s)
            bits = pltpu.bitcast(s, I32)
            skey = jnp.where(bits < 0, bits ^ 0x7FFFFFFF, bits)
            key_s[kb] = jnp.where(kb * QBLK + sub <= qpos, skey, INT_MIN)
        return carry

    lax.fori_loop(0, pl.cdiv(nkb, SCORE_UNROLL), score_blocks, 0)

    @pl.when(nkb * QBLK <= topk)
    def _():
        thr_s[0:1, :] = jnp.full((1, QBLK), INT_MIN, I32)
        thr_s[1:2, :] = jnp.zeros((1, QBLK), I32)

    @pl.when(nkb * QBLK > topk)
    def _():
        n_done = pl.cdiv(nkb, SCORE_UNROLL) * SCORE_UNROLL
        n_cnt = pl.cdiv(nkb, COUNT_UNROLL)

        def fill(kb, carry):
            key_s[kb] = jnp.full((QBLK, QBLK), INT_MIN, I32)
            return carry

        lax.fori_loop(n_done, n_cnt * COUNT_UNROLL, fill, 0)

        def count(preds):
            def body(i, accs):
                accs = list(accs)
                for u in range(COUNT_UNROLL):
                    kb = i * COUNT_UNROLL + u
                    ky = key_s[kb]
                    for n, pred in enumerate(preds):
                        hit = pred(ky, kb * QBLK + sub).astype(I32)
                        accs[n] = accs[n] + jnp.sum(hit.reshape(QBLK // 8, 8, QBLK), axis=0)
                return tuple(accs)
            accs = lax.fori_loop(0, n_cnt, body, tuple(jnp.zeros((8, QBLK), I32) for _ in preds))
            return [jnp.sum(a, axis=0, keepdims=True) for a in accs]

        c0, = count([lambda ky, ix: ky >= 0])
        t0 = jnp.where(c0 >= topk, 0, INT_MIN).astype(I32)

        def bit_step(i, t):
            cand = t | jnp.left_shift(jnp.int32(1), 30 - i)
            c, = count([lambda ky, ix: ky >= cand])
            return jnp.where(c >= topk, cand, t)

        thr = lax.fori_loop(0, 31, bit_step, t0)
        n_gt, n_eq = count([lambda ky, ix: ky > thr, lambda ky, ix: ky == thr])
        need = topk - n_gt
        thr_s[0:1, :] = thr
        thr_s[1:2, :] = jnp.full((1, QBLK), 2 ** nbits, I32)

        @pl.when(jnp.max(jnp.abs(n_eq - need)) > 0)
        def _():
            def idx_step(i, mm):
                cand = mm | jnp.left_shift(jnp.int32(1), nbits - 1 - i)
                c, = count([lambda ky, ix: (ky == thr) & (ix < cand)])
                return jnp.where(c < need, cand, mm)

            thr_s[1:2, :] = lax.fori_loop(0, nbits, idx_step, jnp.zeros((1, QBLK), I32))

    thr = thr_s[0:1, :]
    mm = thr_s[1:2, :]
    m_s[...] = jnp.full_like(m_s, NEG_BIG)
    l_s[...] = jnp.zeros_like(l_s)
    acc_s[...] = jnp.zeros_like(acc_s)

    qt2 = [jnp.concatenate([qt_ref[0, 2 * q], qt_ref[0, 2 * q + 1]], axis=1)
           for q in range(AT_HEADS // 2)]

    def attn_blocks(i, carry):
        kbs = [i * ATTN_UNROLL + u for u in range(ATTN_UNROLL)]
        sel = []
        for kb in kbs:
            skey = key_s[kb]
            kidx = kb * QBLK + sub
            sel.append((kidx <= qpos) & ((skey > thr) | ((skey == thr) & (kidx <= mm))))
        sel = jnp.concatenate(sel, axis=0)
        kblk = [k_ref[0, pl.ds(pl.multiple_of(kb * QBLK, QBLK), QBLK), :] for kb in kbs]
        vtb = jnp.concatenate([vt_ref[0, kb] for kb in kbs], axis=1)
        s2 = [[_dot(kblk[u][:, q * LANES:(q + 1) * LANES], qt2[q]) for u in range(ATTN_UNROLL)]
              for q in range(AT_HEADS // 2)]
        pexp, alpha = [], []
        for hh in range(AT_HEADS):
            half = slice((hh % 2) * QBLK, (hh % 2 + 1) * QBLK)
            s = jnp.concatenate([s2[hh // 2][u][:, half] for u in range(ATTN_UNROLL)], axis=0)
            s = jnp.where(sel, s, NEG_BIG)
            m_old = m_s[hh:hh + 1, :]
            m_new = jnp.maximum(m_old, jnp.max(s, axis=0, keepdims=True))
            pe = jnp.exp(s - m_new)
            al = jnp.exp(m_old - m_new)
            l_s[hh:hh + 1, :] = al * l_s[hh:hh + 1, :] + jnp.sum(pe, axis=0, keepdims=True)
            m_s[hh:hh + 1, :] = m_new
            pexp.append(pe.astype(BF16))
            alpha.append(al)
        for hh in range(AT_HEADS):
            hs = slice(hh * HEAD_DIM, (hh + 1) * HEAD_DIM)
            acc_s[hs, :] = alpha[hh] * acc_s[hs, :] + _dot(vtb[hs, :], pexp[hh])
        return carry

    lax.fori_loop(0, pl.cdiv(nkb, ATTN_UNROLL), attn_blocks, 0)

    for hh in range(AT_HEADS):
        hs = slice(hh * HEAD_DIM, (hh + 1) * HEAD_DIM)
        oh = acc_s[hs, :] * (1.0 / l_s[hh:hh + 1, :])
        ms = jnp.mean(oh * oh, axis=0, keepdims=True)
        acc_s[hs, :] = oh * lax.rsqrt(ms + NORM_EPS) * og_ref[hs, :]
    o_ref[0] = acc_s[...].T.astype(BF16)


def _dsa(k, vt, ki, qt, qit, wit, at_out_g, interpret):
    B, S, _ = k.shape
    nq = S // QBLK
    topk = min(IDX_TOPK_MAX, S // 4)
    nbits = int(np.log2(S))
    assert 2 ** nbits == S and nq % COUNT_UNROLL == 0 and SCORE_UNROLL == ATTN_UNROLL
    assert COUNT_UNROLL % SCORE_UNROLL == 0
    return pl.pallas_call(
        functools.partial(_dsa_kernel, topk, nbits),
        grid=(B, nq),
        in_specs=[pl.BlockSpec((1, S, AT_WIDTH), lambda b, j: (b, 0, 0)),
                  pl.BlockSpec((1, nq, AT_WIDTH, QBLK), lambda b, j: (b, 0, 0, 0)),
                  pl.BlockSpec((1, S, LANES), lambda b, j: (b, 0, 0)),
                  pl.BlockSpec((1, AT_HEADS, LANES, QBLK), lambda b, j: (b, 0, 0, j)),
                  pl.BlockSpec((1, IDX_HEADS, LANES, QBLK), lambda b, j: (b, 0, 0, j)),
                  pl.BlockSpec((1, 8, QBLK), lambda b, j: (b, 0, j)),
                  pl.BlockSpec((AT_WIDTH, 1), lambda b, j: (0, 0))],
        out_specs=pl.BlockSpec((1, QBLK, AT_WIDTH), lambda b, j: (b, j, 0)),
        out_shape=jax.ShapeDtypeStruct((B, S, AT_WIDTH), BF16),
        scratch_shapes=[pltpu.VMEM((nq, QBLK, QBLK), I32),
                        pltpu.VMEM((AT_WIDTH, QBLK), F32),
                        pltpu.VMEM((AT_HEADS, QBLK), F32),
                        pltpu.VMEM((AT_HEADS, QBLK), F32),
                        pltpu.VMEM((8, QBLK), I32)],
        compiler_params=_cparams(("arbitrary", "arbitrary")),
        name="dsa", interpret=interpret,
    )(k, vt, ki, qt, qit, wit, at_out_g.reshape(AT_WIDTH, 1))


def _first_max(vals, idx, axis, sentinel):
    m = jnp.max(vals, axis=axis, keepdims=True)
    return m, jnp.min(jnp.where(vals == m, idx, sentinel), axis=axis, keepdims=True)


def _route_cols(logits_t, bias_col):
    E, tm = logits_t.shape
    pg = E // N_GROUPS
    scores = _sigmoid(logits_t)
    biased = scores + bias_col
    b3 = biased.reshape(N_GROUPS, pg, tm)
    r3 = lax.broadcasted_iota(I32, (N_GROUPS, pg, tm), 1)
    m1, first = _first_max(b3, r3, 1, pg)
    m2 = jnp.max(jnp.where(r3 == first, -jnp.inf, b3), axis=1, keepdims=True)
    cur = (m1 + m2).reshape(N_GROUPS, tm)
    grow = lax.broadcasted_iota(I32, (N_GROUPS, tm), 0)
    gsel = jnp.zeros((N_GROUPS, tm), F32)
    for _ in range(TOPK_GROUPS):
        _, gi = _first_max(cur, grow, 0, N_GROUPS)
        hit = grow == gi
        gsel = jnp.where(hit, 1.0, gsel)
        cur = jnp.where(hit, -jnp.inf, cur)
    gmask = jnp.broadcast_to(gsel.reshape(N_GROUPS, 1, tm), (N_GROUPS, pg, tm)).reshape(E, tm)
    cur = jnp.where(gmask > 0.0, biased, -jnp.inf)
    row = lax.broadcasted_iota(I32, (E, tm), 0)
    onehot = jnp.zeros((E, tm), F32)
    eids, gws = [], []
    for _ in range(TOP_K):
        _, ei = _first_max(cur, row, 0, E)
        hit = row == ei
        eids.append(ei)
        gws.append(jnp.sum(jnp.where(hit, scores, 0.0), axis=0, keepdims=True))
        onehot = jnp.where(hit, 1.0, onehot)
        cur = jnp.where(hit, -jnp.inf, cur)
    eid = jnp.concatenate(eids, axis=0)
    gw = jnp.concatenate(gws, axis=0)
    gw = gw * (ROUTED_SCALE / jnp.sum(gw, axis=0, keepdims=True))
    return eid, gw, onehot


def _post_kernel(x_ref, yrw_ref, yat_ref, g1_ref, sc_ref, sh_ref, g2_ref, ng_ref, wo_ref, rwt_ref,
                 rb_ref, s1_ref, s3_ref, s2_ref, base_ref, h2_ref, eid_ref, gw_ref, rank_ref, cnt_ref):
    W = RW_WIDTH
    tm = x_ref.shape[1]
    E = rwt_ref.shape[0]

    @pl.when((pl.program_id(0) == 0) & (pl.program_id(1) == 0))
    def _():
        cnt_ref[...] = jnp.zeros_like(cnt_ref)

    mix = _dot(yrw_ref[0], wo_ref[0:W, :]) + _dot(yat_ref[0], wo_ref[W:, :])
    x1 = x_ref[0] + g1_ref[0] * mix
    ms = jnp.mean(x1 * x1, axis=-1, keepdims=True)
    h2 = x1 * lax.rsqrt(ms + NORM_EPS) * ng_ref[...] * (1.0 + sc_ref[0]) + sh_ref[0]
    hb = h2.astype(BF16)
    h2_ref[0] = _pack_halves(h2)
    act =(_silu(_dot(hb, s1_ref[...])) * _dot(hb, s3_ref[...])).astype(BF16)
    base_ref[0] = x1 + g2_ref[0] * _dot(act, s2_ref[...])

    logits_t = lax.dot_general(rwt_ref[...], h2, (((1,), (1,)), ((), ())), precision=HIGHEST,
                               preferred_element_type=F32)
    eid, gw, onehot = _route_cols(logits_t, rb_ref[...])
    eid_ref[0] = eid
    gw_ref[0] = gw
    ti = lax.broadcasted_iota(I32, (tm, tm), 0)
    tj = lax.broadcasted_iota(I32, (tm, tm), 1)
    before = _dot(onehot.astype(BF16), (ti < tj).astype(BF16)) + cnt_ref[:, 0:1]
    row = lax.broadcasted_iota(I32, (E, tm), 0)
    ranks = [jnp.sum(jnp.where(row == eid[kk:kk + 1, :], before, 0.0), axis=0, keepdims=True)
             for kk in range(TOP_K)]
    rank_ref[0] = jnp.concatenate(ranks, axis=0).astype(I32)
    cnt_ref[...] = cnt_ref[...] + jnp.sum(onehot, axis=1, keepdims=True)


def _post(x, y_rw, y_at, gate1, scale2, shift2, gate2, norm2_g, w_out, router_w, router_bias,
          sw1, sw3, sw2, interpret):
    B, S, D = x.shape
    tm = min(S, 256)
    E = router_w.shape[1]
    sd = sw1.shape[1]
    full = lambda shape: pl.BlockSpec(shape, lambda b, i: (0,) * len(shape))
    tok = lambda w: pl.BlockSpec((1, tm, w), lambda b, i: (b, i, 0))
    per_b = pl.BlockSpec((1, 1, D), lambda b, i: (b, 0, 0))
    col8 = pl.BlockSpec((1, TOP_K, tm), lambda b, i: (b, 0, i))
    return pl.pallas_call(
        _post_kernel,
        grid=(B, S // tm),
        in_specs=[tok(D), tok(RW_WIDTH), tok(AT_WIDTH), per_b, per_b, per_b, per_b, full((1, D)),
                  full((D, D)), full((E, D)), full((E, 1)), full((D, sd)), full((D, sd)),
                  full((sd, D))],
        out_specs=[tok(D), tok(D // 2), col8, col8, col8, full((E, LANES))],
        out_shape=[jax.ShapeDtypeStruct((B, S, D), F32),
                   jax.ShapeDtypeStruct((B, S, D // 2), I32),
                   jax.ShapeDtypeStruct((B, TOP_K, S), I32),
                   jax.ShapeDtypeStruct((B, TOP_K, S), F32),
                   jax.ShapeDtypeStruct((B, TOP_K, S), I32),
                   jax.ShapeDtypeStruct((E, LANES), F32)],
        compiler_params=_cparams(("arbitrary", "arbitrary")),
        name="post", interpret=interpret,
    )(x, y_rw, y_at, gate1, scale2, shift2, gate2, norm2_g.reshape(1, D), w_out.astype(BF16),
      router_w.T, router_bias.reshape(E, 1), sw1.astype(BF16), sw3.astype(BF16), sw2.astype(BF16))


def _expert_kernel(be_ref, nu_ref, xs_ref, w1_ref, w3_ref, w2_ref, o_ref):
    del be_ref
    hw = xs_ref.shape[1]

    @pl.when(pl.program_id(0) < nu_ref[0])
    def _():
        x_lo, x_hi = _unpack_halves(xs_ref[...])
        x_lo, x_hi = x_lo.astype(BF16), x_hi.astype(BF16)
        w1 = w1_ref[0].astype(BF16)
        w3 = w3_ref[0].astype(BF16)
        h1 = _dot(x_lo, w1[:hw]) + _dot(x_hi, w1[hw:])
        h3 = _dot(x_lo, w3[:hw]) + _dot(x_hi, w3[hw:])
        act = (_silu(h1) * h3).astype(BF16)
        o_ref[...] = _pack_halves(_dot(act, w2_ref[0].astype(BF16)))


def _experts(xs, block_e, n_used, w1, w3, w2, interpret):
    P, hw = xs.shape
    E, D, F = w1.shape
    nb = P // EXP_BLK
    blk = lambda i, be, nu: jnp.minimum(i, nu[0] - 1)
    grid_spec = pltpu.PrefetchScalarGridSpec(
        num_scalar_prefetch=2,
        grid=(nb,),
        in_specs=[pl.BlockSpec((EXP_BLK, hw), lambda i, be, nu: (blk(i, be, nu), 0)),
                  pl.BlockSpec((1, D, F), lambda i, be, nu: (be[blk(i, be, nu)], 0, 0)),
                  pl.BlockSpec((1, D, F), lambda i, be, nu: (be[blk(i, be, nu)], 0, 0)),
                  pl.BlockSpec((1, F, D), lambda i, be, nu: (be[blk(i, be, nu)], 0, 0))],
        out_specs=pl.BlockSpec((EXP_BLK, hw), lambda i, be, nu: (blk(i, be, nu), 0)),
    )
    return pl.pallas_call(
        _expert_kernel,
        grid_spec=grid_spec,
        out_shape=jax.ShapeDtypeStruct((P, hw), I32),
        compiler_params=_cparams(("arbitrary",)),
        name="experts", interpret=interpret,
    )(block_e, n_used, xs, w1, w3, w2)


def _row_copy(src, s_row, dst, d_row, sem):
    return pltpu.make_async_copy(src.at[pl.ds(s_row, 1)], dst.at[pl.ds(d_row, 1)], sem)


def _dispatch_kernel(dest_ref, h2_ref, xs_in_ref, xs_ref, sem):
    del xs_in_ref
    td = h2_ref.shape[0]

    def issue(t, carry):
        for kk in range(TOP_K):
            _row_copy(h2_ref, t, xs_ref, dest_ref[0, 0, kk * td + t], sem).start()
        return carry

    def drain(t, carry):
        for kk in range(TOP_K):
            _row_copy(h2_ref, t, xs_ref, dest_ref[0, 0, kk * td + t], sem).wait()
        return carry

    lax.fori_loop(0, td, issue, 0)
    lax.fori_loop(0, td, drain, 0)


def _dispatch(h2, dest_tiles, n_slots, interpret):
    T, D = h2.shape
    nt, _, n = dest_tiles.shape
    td = n // TOP_K
    xs0 = jnp.zeros((n_slots, D), h2.dtype)
    return pl.pallas_call(
        _dispatch_kernel,
        grid=(nt,),
        in_specs=[pl.BlockSpec((1, 1, n), lambda i: (i, 0, 0), memory_space=pltpu.SMEM),
                  pl.BlockSpec((td, D), lambda i: (i, 0)),
                  pl.BlockSpec(memory_space=pl.ANY)],
        out_specs=pl.BlockSpec(memory_space=pl.ANY),
        out_shape=jax.ShapeDtypeStruct((n_slots, D), h2.dtype),
        scratch_shapes=[pltpu.SemaphoreType.DMA(())],
        input_output_aliases={2: 0},
        compiler_params=_cparams(("arbitrary",)),
        name="dispatch", interpret=interpret,
    )(dest_tiles, h2, xs0)


def _combine_kernel(dest_ref, base_ref, g2_ref, gw_ref, ys_ref, o_ref, buf, sem):
    td = base_ref.shape[0]

    def issue(t, carry):
        for kk in range(TOP_K):
            _row_copy(ys_ref, dest_ref[0, 0, kk * td + t], buf.at[kk], t, sem).start()
        return carry

    def drain(t, carry):
        for kk in range(TOP_K):
            _row_copy(ys_ref, dest_ref[0, 0, kk * td + t], buf.at[kk], t, sem).wait()
        return carry

    lax.fori_loop(0, td, issue, 0)
    lax.fori_loop(0, td, drain, 0)
    gw = gw_ref[...]
    acc_lo, acc_hi = _unpack_halves(buf[0])
    acc_lo, acc_hi = gw[:, 0:1] * acc_lo, gw[:, 0:1] * acc_hi
    for kk in range(1, TOP_K):
        y_lo, y_hi = _unpack_halves(buf[kk])
        acc_lo = acc_lo + gw[:, kk:kk + 1] * y_lo
        acc_hi = acc_hi + gw[:, kk:kk + 1] * y_hi
    o_ref[...] = base_ref[...] + g2_ref[0] * jnp.concatenate([acc_lo, acc_hi], axis=1)


def _combine(base, gate2, gw_tok, ys, dest_tiles, tiles_per_batch, interpret):
    T, D = base.shape
    nt, _, n = dest_tiles.shape
    td = n // TOP_K
    return pl.pallas_call(
        _combine_kernel,
        grid=(nt,),
        in_specs=[pl.BlockSpec((1, 1, n), lambda i: (i, 0, 0), memory_space=pltpu.SMEM),
                  pl.BlockSpec((td, D), lambda i: (i, 0)),
                  pl.BlockSpec((1, 1, D), lambda i: (i // tiles_per_batch, 0, 0)),
                  pl.BlockSpec((td, TOP_K), lambda i: (i, 0)),
                  pl.BlockSpec(memory_space=pl.ANY)],
        out_specs=pl.BlockSpec((td, D), lambda i: (i, 0)),
        out_shape=jax.ShapeDtypeStruct((T, D), F32),
        scratch_shapes=[pltpu.VMEM((TOP_K, td, D // 2), I32), pltpu.SemaphoreType.DMA(())],
        compiler_params=_cparams(("arbitrary",)),
        name="combine", interpret=interpret,
    )(dest_tiles, base, gate2, gw_tok, ys)


def _slots_kernel(eid_ref, rank_ref, pstart_ref, dest_ref):
    td = eid_ref.shape[2]
    E = pstart_ref.shape[0]
    row = lax.broadcasted_iota(I32, (E, td), 0)
    pstart = pstart_ref[...]
    eid = eid_ref[0]
    for kk in range(TOP_K):
        base = jnp.sum(jnp.where(row == eid[kk:kk + 1, :], pstart, 0), axis=0, keepdims=True)
        dest_ref[0, :, kk * td:(kk + 1) * td] = base + rank_ref[0, kk:kk + 1, :]


def _slot_plan(counts, eid_t, rank_t, td, interpret):
    B, _, S = eid_t.shape
    E = counts.shape[0]
    padded = (counts + EXP_BLK - 1) // EXP_BLK * EXP_BLK
    pend = jnp.cumsum(padded)
    pstart = (pend - padded).astype(I32)
    nb = -(-(B * S * TOP_K + E * (EXP_BLK - 1)) // EXP_BLK)
    block_e = jnp.searchsorted(pend, jnp.arange(nb, dtype=I32) * EXP_BLK, side='right')
    block_e = jnp.minimum(block_e, E - 1).astype(I32)
    nt = S // td
    dest_tiles = pl.pallas_call(
        _slots_kernel,
        grid=(B, nt),
        in_specs=[pl.BlockSpec((1, TOP_K, td), lambda b, i: (b, 0, i)),
                  pl.BlockSpec((1, TOP_K, td), lambda b, i: (b, 0, i)),
                  pl.BlockSpec((E, 1), lambda b, i: (0, 0))],
        out_specs=pl.BlockSpec((1, 1, TOP_K * td), lambda b, i: (b * nt + i, 0, 0)),
        out_shape=jax.ShapeDtypeStruct((B * nt, 1, TOP_K * td), I32),
        compiler_params=_cparams(("arbitrary", "arbitrary")),
        name="slots", interpret=interpret,
    )(eid_t, rank_t, pstart.reshape(E, 1))
    n_used = (pend[-1:] // EXP_BLK).astype(I32)
    return block_e, n_used, dest_tiles, nb * EXP_BLK


def _forward(x, c, positions, w_ada, b_ada, norm1_g, norm2_g, w_in, rw_mu, rw_w0, rw_w2,
             rw_a0, rw_a2, rw_g2, rw_k_k, rw_k_a, rw_r_k, rw_ln_w, rw_ln_b, q_norm_g,
             k_norm_g, idx_ln_w, idx_ln_b, at_out_g, w_out, router_w, router_bias,
             exp_w1, exp_w3, exp_w2, shared_w1, shared_w3, shared_w2, interpret=False):
    B, S, D = x.shape
    depth = w_ada.shape[0]
    for l in range(depth):
        mod = _mod(c, w_ada[l], b_ada[l], interpret)
        shift1, scale1, gate1, shift2, scale2, gate2 = [
            m.reshape(B, 1, D) for m in jnp.split(mod, 6, axis=-1)]
        tabs = _rope_tables(positions, interpret)
        p_rw, k, ki, qt, vt, qit, wit = _inproj(
            x, scale1, shift1, norm1_g[l], w_in[l], k_norm_g[l], idx_ln_w[l], idx_ln_b[l],
            q_norm_g[l], tabs, interpret)
        y_rw = _rwkv(p_rw, rw_mu[l], rw_w0[l], rw_w2[l], rw_a0[l], rw_a2[l], rw_g2[l], rw_k_k[l],
                     rw_k_a[l], rw_r_k[l], rw_ln_w[l], rw_ln_b[l], interpret)
        y_at = _dsa(k, vt, ki, qt, qit, wit, at_out_g[l], interpret)
        base, h2, eid_t, gw_t, rank_t, cnt = _post(
            x, y_rw, y_at, gate1, scale2, shift2, gate2, norm2_g[l], w_out[l], router_w[l],
            router_bias[l], shared_w1[l], shared_w3[l], shared_w2[l], interpret)
        T = B * S
        td = min(S, ROW_TILE)
        block_e, n_used, dest_tiles, n_slots = _slot_plan(cnt[:, 0].astype(I32), eid_t, rank_t, td,
                                                            interpret)
        xs = _dispatch(h2.reshape(T, D // 2), dest_tiles, n_slots, interpret)
        ys = _experts(xs, block_e, n_used, exp_w1[l], exp_w3[l], exp_w2[l], interpret)
        gw_tok = gw_t.transpose(0, 2, 1).reshape(T, TOP_K)
        x = _combine(base.reshape(T, D), gate2, gw_tok, ys, dest_tiles, S // td,
                     interpret).reshape(B, S, D)
    return x


def kernel(x, c, positions, w_ada, b_ada, norm1_g, norm2_g, w_in, rw_mu, rw_w0, rw_w2, rw_a0, rw_a2, rw_g2, rw_k_k, rw_k_a, rw_r_k, rw_ln_w, rw_ln_b, q_norm_g, k_norm_g, idx_ln_w, idx_ln_b, at_out_g, w_out, router_w, router_bias, exp_w1, exp_w3, exp_w2, shared_w1, shared_w3, shared_w2):
    return _forward(x, c, positions, w_ada, b_ada, norm1_g, norm2_g, w_in, rw_mu, rw_w0, rw_w2,
                    rw_a0, rw_a2, rw_g2, rw_k_k, rw_k_a, rw_r_k, rw_ln_w, rw_ln_b, q_norm_g,
                    k_norm_g, idx_ln_w, idx_ln_b, at_out_g, w_out, router_w, router_bias,
                    exp_w1, exp_w3, exp_w2, shared_w1, shared_w3, shared_w2)
```

```python
import functools

import jax
import jax.numpy as jnp
import numpy as np
from jax import lax
from jax.experimental import pallas as pl
from jax.experimental.pallas import tpu as pltpu

F32 = jnp.float32
BF16 = jnp.bfloat16
I32 = jnp.int32
HIGHEST = lax.Precision.HIGHEST

LANES = 128
HEAD_DIM = 64
HALF = HEAD_DIM // 2
RW_HEADS = 8
RW_WIDTH = RW_HEADS * HEAD_DIM
AT_HEADS = 8
AT_WIDTH = AT_HEADS * HEAD_DIM
IDX_HEADS = 4
RW_LORA_W, RW_LORA_A, RW_LORA_G = 64, 64, 128
RW_COLS = 3 * RW_WIDTH + RW_LORA_W + RW_LORA_A + RW_LORA_G
KI_OFF = 3 * AT_WIDTH + IDX_HEADS * HEAD_DIM
WI_OFF = KI_OFF + HEAD_DIM
PT_ROWS = 2 * AT_WIDTH + IDX_HEADS * HEAD_DIM + 8
PK_COLS = AT_WIDTH + LANES
ROPE_THETA = 10000.0
NORM_EPS = 1e-6
LN_EPS = 1e-6
RW_GN_EPS = 64e-5
IDX_TOPK_MAX = 256
N_EXPERTS = 256
TOP_K = 8
N_GROUPS = 8
TOPK_GROUPS = 4
ROUTED_SCALE = 2.5
INT_MIN = -2 ** 31
PAIR_HI_MASK = -65536
NEG_BIG = -1e30

RW_CHUNK = 128
QBLK = 128
SCORE_UNROLL = 2
ATTN_UNROLL = 2
COUNT_UNROLL = 4
EXP_BLK = 256
ROW_TILE = 256
VMEM_LIMIT = 48 * 1024 * 1024


def _cparams(sem):
    return pltpu.CompilerParams(dimension_semantics=sem, vmem_limit_bytes=VMEM_LIMIT)


def _sigmoid(x):
    return 1.0 / (1.0 + jnp.exp(-x))


def _silu(x):
    return x * _sigmoid(x)


def _dot(a, b):
    return jnp.dot(a, b, preferred_element_type=F32)


def _dot_split(a, b):
    hi = a.astype(BF16)
    lo = (a - hi.astype(F32)).astype(BF16)
    return _dot(hi, b) + _dot(lo, b)


def _pack_halves(x):
    w = x.shape[1] // 2
    lo = pltpu.bitcast(x[:, :w].astype(BF16).astype(F32), I32)
    hi = pltpu.bitcast(x[:, w:].astype(BF16).astype(F32), I32)
    return (hi & PAIR_HI_MASK) | lax.shift_right_logical(lo, 16)


def _unpack_halves(p):
    return pltpu.bitcast(p << 16, F32), pltpu.bitcast(p & PAIR_HI_MASK, F32)


def _dot_nt(a, b):
    return lax.dot_general(a, b, (((1,), (1,)), ((), ())), preferred_element_type=F32)


def _mod_kernel(c_ref, w_ref, b_ref, o_ref):
    c = c_ref[...]
    o_ref[...] = jnp.dot(_silu(c), w_ref[...], precision=HIGHEST,
                         preferred_element_type=F32) + b_ref[...]


def _mod(c, w_ada, b_ada, interpret):
    B, D = c.shape
    n = w_ada.shape[1] // D
    return pl.pallas_call(
        _mod_kernel,
        grid=(n,),
        in_specs=[pl.BlockSpec((B, D), lambda i: (0, 0)),
                  pl.BlockSpec((D, D), lambda i: (0, i)),
                  pl.BlockSpec((1, D), lambda i: (0, i))],
        out_specs=pl.BlockSpec((B, D), lambda i: (0, i)),
        out_shape=jax.ShapeDtypeStruct((B, n * D), F32),
        compiler_params=_cparams(("arbitrary",)),
        name="mod", interpret=interpret,
    )(c, w_ada, b_ada.reshape(1, -1))


def _rope_tab_kernel(pc_ref, pr_ref, cr_ref, sr_ref, ct_ref, st_ref):
    log_theta = float(np.log(ROPE_THETA))
    lane = lax.broadcasted_iota(I32, (1, LANES), 1)
    inv_r = jnp.exp((lane % HALF).astype(F32) * (-log_theta / HALF))
    ang = pc_ref[0].astype(F32) * inv_r
    cr_ref[0] = jnp.cos(ang)
    sr_ref[0] = jnp.where((lane % HEAD_DIM) < HALF, -jnp.sin(ang), jnp.sin(ang))
    sub = lax.broadcasted_iota(I32, (HALF, 1), 0)
    inv_c = jnp.exp(sub.astype(F32) * (-log_theta / HALF))
    ang_t = inv_c * pr_ref[0].astype(F32)
    ct_ref[0] = jnp.cos(ang_t)
    st_ref[0] = jnp.sin(ang_t)


def _rope_tables(positions, interpret):
    B, S = positions.shape
    ts = min(S, 512)
    return pl.pallas_call(
        _rope_tab_kernel,
        grid=(B, S // ts),
        in_specs=[pl.BlockSpec((1, ts, 1), lambda b, i: (b, i, 0)),
                  pl.BlockSpec((1, 1, ts), lambda b, i: (b, 0, i))],
        out_specs=[pl.BlockSpec((1, ts, LANES), lambda b, i: (b, i, 0)),
                   pl.BlockSpec((1, ts, LANES), lambda b, i: (b, i, 0)),
                   pl.BlockSpec((1, HALF, ts), lambda b, i: (b, 0, i)),
                   pl.BlockSpec((1, HALF, ts), lambda b, i: (b, 0, i))],
        out_shape=[jax.ShapeDtypeStruct((B, S, LANES), F32),
                   jax.ShapeDtypeStruct((B, S, LANES), F32),
                   jax.ShapeDtypeStruct((B, HALF, S), F32),
                   jax.ShapeDtypeStruct((B, HALF, S), F32)],
        compiler_params=_cparams(("arbitrary", "arbitrary")),
        name="rope_tab", interpret=interpret,
    )(positions.reshape(B, S, 1), positions.reshape(B, 1, S))


def _rope_rows(y, cos, sin_signed):
    lane = lax.broadcasted_iota(I32, (1, LANES), 1)
    partner = jnp.where((lane % HEAD_DIM) < HALF,
                        pltpu.roll(y, LANES - HALF, 1), pltpu.roll(y, HALF, 1))
    return y * cos + partner * sin_signed


def _rope_cols(y, cos_t, sin_t):
    x1, x2 = y[:, :HALF], y[:, HALF:]
    return jnp.concatenate([x1 * cos_t - x2 * sin_t, x2 * cos_t + x1 * sin_t], axis=1)


def _inproj_kernel(x_ref, sc_ref, sh_ref, g_ref, wrw_ref, wk_ref, wt_ref, kg_ref, iw_ref, ib_ref,
                   qg_ref, gsum_ref, cr_ref, sr_ref, ct_ref, st_ref,
                   prw_ref, k_ref, ki_ref, qt_ref, vt_ref, qit_ref, wit_ref):
    tm = x_ref.shape[1]
    x = x_ref[0]
    ms = jnp.mean(x * x, axis=-1, keepdims=True)
    h = x * lax.rsqrt(ms + NORM_EPS) * g_ref[...] * (1.0 + sc_ref[0]) + sh_ref[0]
    hb = h.astype(BF16)
    prw_ref[0] = _dot(hb, wrw_ref[...])
    pk = _dot(hb, wk_ref[...])
    pt = _dot_nt(wt_ref[...], hb)

    cos_r, sin_r = cr_ref[0], sr_ref[0]
    gsum = gsum_ref[...]
    inv_hd = 1.0 / HEAD_DIM
    for p in range(AT_WIDTH // LANES):
        xk = pk[:, p * LANES:(p + 1) * LANES]
        ss = jnp.dot(xk * xk, gsum, precision=HIGHEST, preferred_element_type=F32)
        y = xk * lax.rsqrt(ss * inv_hd + NORM_EPS) * kg_ref[...]
        k_ref[0, :, p * LANES:(p + 1) * LANES] = _rope_rows(y, cos_r, sin_r).astype(BF16)
    xi = pk[:, AT_WIDTH:AT_WIDTH + LANES]
    mu = jnp.dot(xi, gsum, precision=HIGHEST, preferred_element_type=F32) * inv_hd
    xc = xi - mu
    var = jnp.dot(xc * xc, gsum, precision=HIGHEST, preferred_element_type=F32) * inv_hd
    yi = xc * lax.rsqrt(var + LN_EPS) * iw_ref[...] + ib_ref[...]
    ki_ref[0] = _rope_rows(yi, cos_r, sin_r).astype(BF16)

    cos_t, sin_t = ct_ref[0][None], st_ref[0][None]
    xq = pt[0:AT_WIDTH].reshape(AT_HEADS, HEAD_DIM, tm)
    msq = jnp.mean(xq * xq, axis=1, keepdims=True)
    yq = xq * lax.rsqrt(msq + NORM_EPS) * qg_ref[...][None]
    yq = _rope_cols(yq, cos_t, sin_t) * (HEAD_DIM ** -0.5)
    zq = jnp.zeros((HEAD_DIM, tm), BF16)
    for hh in range(AT_HEADS):
        parts = [yq[hh].astype(BF16), zq] if hh % 2 == 0 else [zq, yq[hh].astype(BF16)]
        qt_ref[0, hh] = jnp.concatenate(parts, axis=0)
    vt = pt[AT_WIDTH:2 * AT_WIDTH].astype(BF16)
    for cblk in range(tm // QBLK):
        vt_ref[0, cblk] = vt[:, cblk * QBLK:(cblk + 1) * QBLK]
    xqi = pt[2 * AT_WIDTH:2 * AT_WIDTH + IDX_HEADS * HEAD_DIM].reshape(IDX_HEADS, HEAD_DIM, tm)
    yqi = _rope_cols(xqi, cos_t, sin_t)
    for hh in range(IDX_HEADS):
        qit_ref[0, hh] = jnp.concatenate([yqi[hh].astype(BF16), zq], axis=0)
    wit_ref[0] = pt[PT_ROWS - 8:PT_ROWS] * (IDX_HEADS ** -0.5 * HEAD_DIM ** -0.5)


def _inproj(x, scale1, shift1, norm1_g, w_in, k_norm_g, idx_ln_w, idx_ln_b, q_norm_g,
            tabs, interpret):
    B, S, D = x.shape
    tm = min(S, 256)
    cos_r, sin_r, cos_t, sin_t = tabs
    w_at = w_in[:, RW_COLS:]
    w_rw = w_in[:, :RW_COLS].astype(BF16)
    w_k = jnp.concatenate([w_at[:, AT_WIDTH:2 * AT_WIDTH], w_at[:, KI_OFF:KI_OFF + HEAD_DIM],
                           jnp.zeros((D, HEAD_DIM), F32)], axis=1).astype(BF16)
    w_t = jnp.concatenate([w_at[:, 0:AT_WIDTH], w_at[:, 2 * AT_WIDTH:3 * AT_WIDTH],
                           w_at[:, 3 * AT_WIDTH:KI_OFF], w_at[:, WI_OFF:WI_OFF + IDX_HEADS],
                           jnp.zeros((D, 8 - IDX_HEADS), F32)], axis=1).T.astype(BF16)
    kg = jnp.tile(k_norm_g, 2).reshape(1, LANES)
    zpad = jnp.zeros((HEAD_DIM,), F32)
    iw = jnp.concatenate([idx_ln_w, zpad]).reshape(1, LANES)
    ib = jnp.concatenate([idx_ln_b, zpad]).reshape(1, LANES)
    qg = q_norm_g.reshape(HEAD_DIM, 1)
    li = np.arange(LANES)
    gsum = jnp.asarray((li[:, None] // HEAD_DIM == li[None, :] // HEAD_DIM).astype(np.float32))

    full = lambda shape: pl.BlockSpec(shape, lambda b, i: (0,) * len(shape))
    return pl.pallas_call(
        _inproj_kernel,
        grid=(B, S // tm),
        in_specs=[pl.BlockSpec((1, tm, D), lambda b, i: (b, i, 0)),
                  pl.BlockSpec((1, 1, D), lambda b, i: (b, 0, 0)),
                  pl.BlockSpec((1, 1, D), lambda b, i: (b, 0, 0)),
                  full((1, D)), full((D, RW_COLS)), full((D, PK_COLS)), full((PT_ROWS, D)),
                  full((1, LANES)), full((1, LANES)), full((1, LANES)), full((HEAD_DIM, 1)),
                  full((LANES, LANES)),
                  pl.BlockSpec((1, tm, LANES), lambda b, i: (b, i, 0)),
                  pl.BlockSpec((1, tm, LANES), lambda b, i: (b, i, 0)),
                  pl.BlockSpec((1, HALF, tm), lambda b, i: (b, 0, i)),
                  pl.BlockSpec((1, HALF, tm), lambda b, i: (b, 0, i))],
        out_specs=[pl.BlockSpec((1, tm, RW_COLS), lambda b, i: (b, i, 0)),
                   pl.BlockSpec((1, tm, AT_WIDTH), lambda b, i: (b, i, 0)),
                   pl.BlockSpec((1, tm, LANES), lambda b, i: (b, i, 0)),
                   pl.BlockSpec((1, AT_HEADS, LANES, tm), lambda b, i: (b, 0, 0, i)),
                   pl.BlockSpec((1, tm // QBLK, AT_WIDTH, QBLK), lambda b, i: (b, i, 0, 0)),
                   pl.BlockSpec((1, IDX_HEADS, LANES, tm), lambda b, i: (b, 0, 0, i)),
                   pl.BlockSpec((1, 8, tm), lambda b, i: (b, 0, i))],
        out_shape=[jax.ShapeDtypeStruct((B, S, RW_COLS), F32),
                   jax.ShapeDtypeStruct((B, S, AT_WIDTH), BF16),
                   jax.ShapeDtypeStruct((B, S, LANES), BF16),
                   jax.ShapeDtypeStruct((B, AT_HEADS, LANES, S), BF16),
                   jax.ShapeDtypeStruct((B, S // QBLK, AT_WIDTH, QBLK), BF16),
                   jax.ShapeDtypeStruct((B, IDX_HEADS, LANES, S), BF16),
                   jax.ShapeDtypeStruct((B, 8, S), F32)],
        compiler_params=_cparams(("arbitrary", "arbitrary")),
        name="inproj", interpret=interpret,
    )(x, scale1, shift1, norm1_g.reshape(1, D), w_rw, w_k, w_t, kg, iw, ib, qg, gsum,
      cos_r, sin_r, cos_t, sin_t)


def _rwkv_kernel(p_ref, mu_ref, w0_ref, w2_ref, a0_ref, a2_ref, g2_ref, kk_ref, ka_ref, rk_ref,
                 lnw_ref, lnb_ref, gsum_ref, y_ref, s_ref, prev_ref, yt_ref):
    C = RW_CHUNK
    W = RW_WIDTH

    @pl.when(pl.program_id(1) == 0)
    def _():
        s_ref[...] = jnp.zeros_like(s_ref)
        prev_ref[...] = jnp.zeros_like(prev_ref)

    p = p_ref[0]
    row = lax.broadcasted_iota(I32, (C, 1), 0)
    pprev = jnp.where(row == 0, prev_ref[...], pltpu.roll(p, 1, 0))
    prev_ref[...] = p[C - 1:C]
    ps = p + (pprev - p) * mu_ref[...]
    r, k, v = ps[:, 0:W], ps[:, W:2 * W], ps[:, 2 * W:3 * W]
    o = 3 * W
    wl = ps[:, o:o + RW_LORA_W]
    al = ps[:, o + RW_LORA_W:o + RW_LORA_W + RW_LORA_A]
    gl = ps[:, o + RW_LORA_W + RW_LORA_A:]

    z = w0_ref[...] + _dot(jnp.tanh(wl).astype(BF16), w2_ref[...])
    nz = -z
    softplus = jnp.maximum(nz, 0.0) + jnp.log(1.0 + jnp.exp(-jnp.abs(nz)))
    logw = -jnp.exp(-softplus - 0.5)
    a = _sigmoid(a0_ref[...] + _dot(al.astype(BF16), a2_ref[...]))
    g = _dot(_sigmoid(gl).astype(BF16), g2_ref[...])
    gsum = gsum_ref[...]
    kk = k * kk_ref[...]
    ss = _dot((kk * kk).astype(BF16), gsum)
    kk = kk * (1.0 / jnp.maximum(jnp.sqrt(ss), 1e-12))
    k2 = k * (1.0 + (a - 1.0) * ka_ref[...])
    bb = kk * a

    cw = logw
    sh = 1
    while sh < C:
        cw = cw + jnp.where(row >= sh, pltpu.roll(cw, sh, 0), 0.0)
        sh *= 2
    cw_last = cw[C - 1:C]
    e_neg = jnp.exp(-cw)
    e_end = jnp.exp(cw_last - cw)
    rw = r * jnp.exp(cw)
    kkp = kk * jnp.exp(cw - logw)
    bw, kw = bb * e_neg, k2 * e_neg
    bend, kend = bb * e_end, k2 * e_end
    wc = jnp.exp(cw_last)
    vt_all = v.T.astype(BF16)

    ri = lax.broadcasted_iota(I32, (C, C), 0)
    ci = lax.broadcasted_iota(I32, (C, C), 1)
    strict = ri < ci
    incl = ri <= ci
    incl2 = jnp.concatenate([incl, incl], axis=0)
    lane_half = lax.broadcasted_iota(I32, (1, LANES), 1) // HEAD_DIM

    heads = range(RW_HEADS)
    pair = lambda h: slice((h // 2) * LANES, (h // 2 + 1) * LANES)
    own = [lane_half == (h % 2) for h in heads]
    lh = [jnp.concatenate([kkp[:, pair(2 * q)], rw[:, pair(2 * q)]], axis=0).astype(BF16)
          for q in range(RW_HEADS // 2)]
    rh = [jnp.where(own[h], jnp.concatenate([bw[:, pair(h)], kw[:, pair(h)]], axis=0), 0.0).astype(BF16)
          for h in heads]
    aat = [_dot_nt(rh[h], lh[h // 2]) for h in heads]
    s_old = [s_ref[h] for h in heads]
    sl = [_dot_nt(s_old[h].astype(BF16), lh[h // 2]) for h in heads]
    vt = [vt_all[h * HEAD_DIM:(h + 1) * HEAD_DIM] for h in heads]
    akt = [jnp.where(strict, aat[h][C:, :C], 0.0).astype(BF16) for h in heads]
    m = [jnp.where(strict, aat[h][:C, :C], 0.0).astype(BF16) for h in heads]
    xs = [-(sl[h][:, :C] + _dot(vt[h], akt[h])) for h in heads]
    xs = [xs[h] - _dot_split(xs[h], m[h]) for h in heads]
    lvl = 2
    while lvl < C:
        m = [_dot(m[h], m[h]).astype(BF16) for h in heads]
        xs = [xs[h] + _dot_split(xs[h], m[h]) for h in heads]
        lvl *= 2
    zt = [jnp.concatenate([xs[h].astype(BF16), vt[h]], axis=1) for h in heads]
    for h in heads:
        ymat = jnp.where(incl2, aat[h][:, C:], 0.0).astype(BF16)
        yt_ref[h * HEAD_DIM:(h + 1) * HEAD_DIM, :] = sl[h][:, C:] + _dot(zt[h], ymat)
    for h in heads:
        endz = jnp.where(own[h], jnp.concatenate([bend[:, pair(h)], kend[:, pair(h)]], axis=0),
                         0.0).astype(BF16)
        s_ref[h] = s_old[h] * wc[:, pair(h)] + _dot(zt[h], endz)

    yt = yt_ref[...].reshape(RW_HEADS, HEAD_DIM, C)
    mean = jnp.mean(yt, axis=1, keepdims=True)
    yc = yt - mean
    var = jnp.mean(yc * yc, axis=1, keepdims=True)
    lnw = lnw_ref[...].reshape(RW_HEADS, HEAD_DIM, 1)
    lnb = lnb_ref[...].reshape(RW_HEADS, HEAD_DIM, 1)
    yn = yc * lax.rsqrt(var + RW_GN_EPS) * lnw + lnb
    y = yn.reshape(W, C).T
    bonus = _dot((r * k2 * rk_ref[...]).astype(BF16), gsum) * v
    y_ref[0] = ((y + bonus) * g).astype(BF16)


def _rwkv(p_rw, rw_mu, rw_w0, rw_w2, rw_a0, rw_a2, rw_g2, rw_k_k, rw_k_a, rw_r_k, rw_ln_w, rw_ln_b,
          interpret):
    B, S, _ = p_rw.shape
    C, W = RW_CHUNK, RW_WIDTH
    li = np.arange(W)
    gsum = jnp.asarray((li[:, None] // HEAD_DIM == li[None, :] // HEAD_DIM).astype(np.float32)).astype(BF16)
    row = lambda a: a.reshape(1, -1)
    full = lambda shape: pl.BlockSpec(shape, lambda b, i: (0,) * len(shape))
    return pl.pallas_call(
        _rwkv_kernel,
        grid=(B, S // C),
        in_specs=[pl.BlockSpec((1, C, RW_COLS), lambda b, i: (b, i, 0)),
                  full((1, RW_COLS)), full((1, W)), full((RW_LORA_W, W)), full((1, W)),
                  full((RW_LORA_A, W)), full((RW_LORA_G, W)), full((1, W)), full((1, W)),
                  full((1, W)), full((W, 1)), full((W, 1)), full((W, W))],
        out_specs=pl.BlockSpec((1, C, W), lambda b, i: (b, i, 0)),
        out_shape=jax.ShapeDtypeStruct((B, S, W), BF16),
        scratch_shapes=[pltpu.VMEM((RW_HEADS, HEAD_DIM, LANES), F32),
                        pltpu.VMEM((1, RW_COLS), F32),
                        pltpu.VMEM((W, C), F32)],
        compiler_params=_cparams(("arbitrary", "arbitrary")),
        name="rwkv", interpret=interpret,
    )(p_rw, row(rw_mu), row(rw_w0), rw_w2.astype(BF16), row(rw_a0), rw_a2.astype(BF16),
      rw_g2.astype(BF16), row(rw_k_k), row(rw_k_a), row(rw_r_k), rw_ln_w.reshape(W, 1),
      rw_ln_b.reshape(W, 1), gsum)


def _dsa_kernel(topk, nbits, k_ref, vt_ref, ki_ref, qt_ref, qit_ref, wit_ref, og_ref, o_ref,
                key_s, acc_s, m_s, l_s, thr_s):
    j = pl.program_id(1)
    nkb = j + 1
    lane = lax.broadcasted_iota(I32, (QBLK, QBLK), 1)
    sub = lax.broadcasted_iota(I32, (QBLK, QBLK), 0)
    qpos = j * QBLK + lane
    wit = wit_ref[0]

    def score_blocks(i, carry):
        kbs = [i * SCORE_UNROLL + u for u in range(SCORE_UNROLL)]
        kib = [ki_ref[0, pl.ds(pl.multiple_of(kb * QBLK, QBLK), QBLK), :] for kb in kbs]
        lg = [[_dot(kib[u], qit_ref[0, hh]) for hh in range(IDX_HEADS)] for u in range(SCORE_UNROLL)]
        for u, kb in enumerate(kbs):
            s = wit[0:1, :] * jnp.maximum(lg[u][0], 0.0)
            for hh in range(1, IDX_HEADS):
                s = s + wit[hh:hh + 1, :] * jnp.maximum(lg[u][hh], 0.0)
            s = jnp.where(s == 0.0, 0.0, s)
            bits = pltpu.bitcast(s, I32)
            skey = jnp.where(bits < 0, bits ^ 0x7FFFFFFF, bits)
            key_s[kb] = jnp.where(kb * QBLK + sub <= qpos, skey, INT_MIN)
        return carry

    lax.fori_loop(0, pl.cdiv(nkb, SCORE_UNROLL), score_blocks, 0)

    @pl.when(nkb * QBLK <= topk)
    def _():
        thr_s[0:1, :] = jnp.full((1, QBLK), INT_MIN, I32)
        thr_s[1:2, :] = jnp.zeros((1, QBLK), I32)

    @pl.when(nkb * QBLK > topk)
    def _():
        n_done = pl.cdiv(nkb, SCORE_UNROLL) * SCORE_UNROLL
        n_cnt = pl.cdiv(nkb, COUNT_UNROLL)

        def fill(kb, carry):
            key_s[kb] = jnp.full((QBLK, QBLK), INT_MIN, I32)
            return carry

        lax.fori_loop(n_done, n_cnt * COUNT_UNROLL, fill, 0)

        def count(preds):
            def body(i, accs):
                accs = list(accs)
                for u in range(COUNT_UNROLL):
                    kb = i * COUNT_UNROLL + u
                    ky = key_s[kb]
                    for n, pred in enumerate(preds):
                        hit = pred(ky, kb * QBLK + sub).astype(I32)
                        accs[n] = accs[n] + jnp.sum(hit.reshape(QBLK // 8, 8, QBLK), axis=0)
                return tuple(accs)
            accs = lax.fori_loop(0, n_cnt, body, tuple(jnp.zeros((8, QBLK), I32) for _ in preds))
            return [jnp.sum(a, axis=0, keepdims=True) for a in accs]

        c0, = count([lambda ky, ix: ky >= 0])
        t0 = jnp.where(c0 >= topk, 0, INT_MIN).astype(I32)

        def bit_step(i, t):
            cand = t | jnp.left_shift(jnp.int32(1), 30 - i)
            c, = count([lambda ky, ix: ky >= cand])
            return jnp.where(c >= topk, cand, t)

        thr = lax.fori_loop(0, 31, bit_step, t0)
        n_gt, n_eq = count([lambda ky, ix: ky > thr, lambda ky, ix: ky == thr])
        need = topk - n_gt
        thr_s[0:1, :] = thr
        thr_s[1:2, :] = jnp.full((1, QBLK), 2 ** nbits, I32)

        @pl.when(jnp.max(jnp.abs(n_eq - need)) > 0)
        def _():
            def idx_step(i, mm):
                cand = mm | jnp.left_shift(jnp.int32(1), nbits - 1 - i)
                c, = count([lambda ky, ix: (ky == thr) & (ix < cand)])
                return jnp.where(c < need, cand, mm)

            thr_s[1:2, :] = lax.fori_loop(0, nbits, idx_step, jnp.zeros((1, QBLK), I32))

    thr = thr_s[0:1, :]
    mm = thr_s[1:2, :]
    m_s[...] = jnp.full_like(m_s, NEG_BIG)
    l_s[...] = jnp.zeros_like(l_s)
    acc_s[...] = jnp.zeros_like(acc_s)

    qt2 = [jnp.concatenate([qt_ref[0, 2 * q], qt_ref[0, 2 * q + 1]], axis=1)
           for q in range(AT_HEADS // 2)]

    def attn_blocks(i, carry):
        kbs = [i * ATTN_UNROLL + u for u in range(ATTN_UNROLL)]
        sel = []
        for kb in kbs:
            skey = key_s[kb]
            kidx = kb * QBLK + sub
            sel.append((kidx <= qpos) & ((skey > thr) | ((skey == thr) & (kidx <= mm))))
        sel = jnp.concatenate(sel, axis=0)
        kblk = [k_ref[0, pl.ds(pl.multiple_of(kb * QBLK, QBLK), QBLK), :] for kb in kbs]
        vtb = jnp.concatenate([vt_ref[0, kb] for kb in kbs], axis=1)
        s2 = [[_dot(kblk[u][:, q * LANES:(q + 1) * LANES], qt2[q]) for u in range(ATTN_UNROLL)]
              for q in range(AT_HEADS // 2)]
        pexp, alpha = [], []
        for hh in range(AT_HEADS):
            half = slice((hh % 2) * QBLK, (hh % 2 + 1) * QBLK)
            s = jnp.concatenate([s2[hh // 2][u][:, half] for u in range(ATTN_UNROLL)], axis=0)
            s = jnp.where(sel, s, NEG_BIG)
            m_old = m_s[hh:hh + 1, :]
            m_new = jnp.maximum(m_old, jnp.max(s, axis=0, keepdims=True))
            pe = jnp.exp(s - m_new)
            al = jnp.exp(m_old - m_new)
            l_s[hh:hh + 1, :] = al * l_s[hh:hh + 1, :] + jnp.sum(pe, axis=0, keepdims=True)
            m_s[hh:hh + 1, :] = m_new
            pexp.append(pe.astype(BF16))
            alpha.append(al)
        for hh in range(AT_HEADS):
            hs = slice(hh * HEAD_DIM, (hh + 1) * HEAD_DIM)
            acc_s[hs, :] = alpha[hh] * acc_s[hs, :] + _dot(vtb[hs, :], pexp[hh])
        return carry

    lax.fori_loop(0, pl.cdiv(nkb, ATTN_UNROLL), attn_blocks, 0)

    for hh in range(AT_HEADS):
        hs = slice(hh * HEAD_DIM, (hh + 1) * HEAD_DIM)
        oh = acc_s[hs, :] * (1.0 / l_s[hh:hh + 1, :])
        ms = jnp.mean(oh * oh, axis=0, keepdims=True)
        acc_s[hs, :] = oh * lax.rsqrt(ms + NORM_EPS) * og_ref[hs, :]
    o_ref[0] = acc_s[...].T.astype(BF16)


def _dsa(k, vt, ki, qt, qit, wit, at_out_g, interpret):
    B, S, _ = k.shape
    nq = S // QBLK
    topk = min(IDX_TOPK_MAX, S // 4)
    nbits = int(np.log2(S))
    assert 2 ** nbits == S and nq % COUNT_UNROLL == 0 and SCORE_UNROLL == ATTN_UNROLL
    assert COUNT_UNROLL % SCORE_UNROLL == 0
    return pl.pallas_call(
        functools.partial(_dsa_kernel, topk, nbits),
        grid=(B, nq),
        in_specs=[pl.BlockSpec((1, S, AT_WIDTH), lambda b, j: (b, 0, 0)),
                  pl.BlockSpec((1, nq, AT_WIDTH, QBLK), lambda b, j: (b, 0, 0, 0)),
                  pl.BlockSpec((1, S, LANES), lambda b, j: (b, 0, 0)),
                  pl.BlockSpec((1, AT_HEADS, LANES, QBLK), lambda b, j: (b, 0, 0, j)),
                  pl.BlockSpec((1, IDX_HEADS, LANES, QBLK), lambda b, j: (b, 0, 0, j)),
                  pl.BlockSpec((1, 8, QBLK), lambda b, j: (b, 0, j)),
                  pl.BlockSpec((AT_WIDTH, 1), lambda b, j: (0, 0))],
        out_specs=pl.BlockSpec((1, QBLK, AT_WIDTH), lambda b, j: (b, j, 0)),
        out_shape=jax.ShapeDtypeStruct((B, S, AT_WIDTH), BF16),
        scratch_shapes=[pltpu.VMEM((nq, QBLK, QBLK), I32),
                        pltpu.VMEM((AT_WIDTH, QBLK), F32),
                        pltpu.VMEM((AT_HEADS, QBLK), F32),
                        pltpu.VMEM((AT_HEADS, QBLK), F32),
                        pltpu.VMEM((8, QBLK), I32)],
        compiler_params=_cparams(("arbitrary", "arbitrary")),
        name="dsa", interpret=interpret,
    )(k, vt, ki, qt, qit, wit, at_out_g.reshape(AT_WIDTH, 1))


def _first_max(vals, idx, axis, sentinel):
    m = jnp.max(vals, axis=axis, keepdims=True)
    return m, jnp.min(jnp.where(vals == m, idx, sentinel), axis=axis, keepdims=True)


def _route_cols(logits_t, bias_col):
    E, tm = logits_t.shape
    pg = E // N_GROUPS
    scores = _sigmoid(logits_t)
    biased = scores + bias_col
    b3 = biased.reshape(N_GROUPS, pg, tm)
    r3 = lax.broadcasted_iota(I32, (N_GROUPS, pg, tm), 1)
    m1, first = _first_max(b3, r3, 1, pg)
    m2 = jnp.max(jnp.where(r3 == first, -jnp.inf, b3), axis=1, keepdims=True)
    cur = (m1 + m2).reshape(N_GROUPS, tm)
    grow = lax.broadcasted_iota(I32, (N_GROUPS, tm), 0)
    gsel = jnp.zeros((N_GROUPS, tm), F32)
    for _ in range(TOPK_GROUPS):
        _, gi = _first_max(cur, grow, 0, N_GROUPS)
        hit = grow == gi
        gsel = jnp.where(hit, 1.0, gsel)
        cur = jnp.where(hit, -jnp.inf, cur)
    gmask = jnp.broadcast_to(gsel.reshape(N_GROUPS, 1, tm), (N_GROUPS, pg, tm)).reshape(E, tm)
    cur = jnp.where(gmask > 0.0, biased, -jnp.inf)
    row = lax.broadcasted_iota(I32, (E, tm), 0)
    onehot = jnp.zeros((E, tm), F32)
    eids, gws = [], []
    for _ in range(TOP_K):
        _, ei = _first_max(cur, row, 0, E)
        hit = row == ei
        eids.append(ei)
        gws.append(jnp.sum(jnp.where(hit, scores, 0.0), axis=0, keepdims=True))
        onehot = jnp.where(hit, 1.0, onehot)
        cur = jnp.where(hit, -jnp.inf, cur)
    eid = jnp.concatenate(eids, axis=0)
    gw = jnp.concatenate(gws, axis=0)
    gw = gw * (ROUTED_SCALE / jnp.sum(gw, axis=0, keepdims=True))
    return eid, gw, onehot


def _post_kernel(x_ref, yrw_ref, yat_ref, g1_ref, sc_ref, sh_ref, g2_ref, ng_ref, wo_ref, rwt_ref,
                 rb_ref, s1_ref, s3_ref, s2_ref, base_ref, h2_ref, eid_ref, gw_ref, rank_ref, cnt_ref):
    W = RW_WIDTH
    tm = x_ref.shape[1]
    E = rwt_ref.shape[0]

    @pl.when((pl.program_id(0) == 0) & (pl.program_id(1) == 0))
    def _():
        cnt_ref[...] = jnp.zeros_like(cnt_ref)

    mix = _dot(yrw_ref[0], wo_ref[0:W, :]) + _dot(yat_ref[0], wo_ref[W:, :])
    x1 = x_ref[0] + g1_ref[0] * mix
    ms = jnp.mean(x1 * x1, axis=-1, keepdims=True)
    h2 = x1 * lax.rsqrt(ms + NORM_EPS) * ng_ref[...] * (1.0 + sc_ref[0]) + sh_ref[0]
    hb = h2.astype(BF16)
    h2_ref[0] = _pack_halves(h2)
    act =(_silu(_dot(hb, s1_ref[...])) * _dot(hb, s3_ref[...])).astype(BF16)
    base_ref[0] = x1 + g2_ref[0] * _dot(act, s2_ref[...])

    logits_t = lax.dot_general(rwt_ref[...], h2, (((1,), (1,)), ((), ())), precision=HIGHEST,
                               preferred_element_type=F32)
    eid, gw, onehot = _route_cols(logits_t, rb_ref[...])
    eid_ref[0] = eid
    gw_ref[0] = gw
    ti = lax.broadcasted_iota(I32, (tm, tm), 0)
    tj = lax.broadcasted_iota(I32, (tm, tm), 1)
    before = _dot(onehot.astype(BF16), (ti < tj).astype(BF16)) + cnt_ref[:, 0:1]
    row = lax.broadcasted_iota(I32, (E, tm), 0)
    ranks = [jnp.sum(jnp.where(row == eid[kk:kk + 1, :], before, 0.0), axis=0, keepdims=True)
             for kk in range(TOP_K)]
    rank_ref[0] = jnp.concatenate(ranks, axis=0).astype(I32)
    cnt_ref[...] = cnt_ref[...] + jnp.sum(onehot, axis=1, keepdims=True)


def _post(x, y_rw, y_at, gate1, scale2, shift2, gate2, norm2_g, w_out, router_w, router_bias,
          sw1, sw3, sw2, interpret):
    B, S, D = x.shape
    tm = min(S, 256)
    E = router_w.shape[1]
    sd = sw1.shape[1]
    full = lambda shape: pl.BlockSpec(shape, lambda b, i: (0,) * len(shape))
    tok = lambda w: pl.BlockSpec((1, tm, w), lambda b, i: (b, i, 0))
    per_b = pl.BlockSpec((1, 1, D), lambda b, i: (b, 0, 0))
    col8 = pl.BlockSpec((1, TOP_K, tm), lambda b, i: (b, 0, i))
    return pl.pallas_call(
        _post_kernel,
        grid=(B, S // tm),
        in_specs=[tok(D), tok(RW_WIDTH), tok(AT_WIDTH), per_b, per_b, per_b, per_b, full((1, D)),
                  full((D, D)), full((E, D)), full((E, 1)), full((D, sd)), full((D, sd)),
                  full((sd, D))],
        out_specs=[tok(D), tok(D // 2), col8, col8, col8, full((E, LANES))],
        out_shape=[jax.ShapeDtypeStruct((B, S, D), F32),
                   jax.ShapeDtypeStruct((B, S, D // 2), I32),
                   jax.ShapeDtypeStruct((B, TOP_K, S), I32),
                   jax.ShapeDtypeStruct((B, TOP_K, S), F32),
                   jax.ShapeDtypeStruct((B, TOP_K, S), I32),
                   jax.ShapeDtypeStruct((E, LANES), F32)],
        compiler_params=_cparams(("arbitrary", "arbitrary")),
        name="post", interpret=interpret,
    )(x, y_rw, y_at, gate1, scale2, shift2, gate2, norm2_g.reshape(1, D), w_out.astype(BF16),
      router_w.T, router_bias.reshape(E, 1), sw1.astype(BF16), sw3.astype(BF16), sw2.astype(BF16))


def _expert_kernel(be_ref, nu_ref, xs_ref, w1_ref, w3_ref, w2_ref, o_ref, w13_s, w2_s):
    i = pl.program_id(0)
    hw = xs_ref.shape[1]
    F = w1_ref.shape[2]
    used = i < nu_ref[0]

    @pl.when(used & ((i == 0) | (be_ref[i] != be_ref[jnp.maximum(i - 1, 0)])))
    def _():
        w13_s[:, :F] = w1_ref[0].astype(BF16)
        w13_s[:, F:] = w3_ref[0].astype(BF16)
        w2_s[...] = w2_ref[0].astype(BF16)

    @pl.when(used)
    def _():
        x_lo, x_hi = _unpack_halves(xs_ref[...])
        h13 = (_dot(x_lo.astype(BF16), w13_s[:hw, :]) + _dot(x_hi.astype(BF16), w13_s[hw:, :]))
        act = (_silu(h13[:, :F]) * h13[:, F:]).astype(BF16)
        o_ref[...] = _pack_halves(_dot(act, w2_s[...]))


def _experts(xs, block_e, n_used, w1, w3, w2, interpret):
    P, hw = xs.shape
    E, D, F = w1.shape
    nb = P // EXP_BLK
    blk = lambda i, be, nu: jnp.minimum(i, nu[0] - 1)
    grid_spec = pltpu.PrefetchScalarGridSpec(
        num_scalar_prefetch=2,
        grid=(nb,),
        in_specs=[pl.BlockSpec((EXP_BLK, hw), lambda i, be, nu: (blk(i, be, nu), 0)),
                  pl.BlockSpec((1, D, F), lambda i, be, nu: (be[blk(i, be, nu)], 0, 0)),
                  pl.BlockSpec((1, D, F), lambda i, be, nu: (be[blk(i, be, nu)], 0, 0)),
                  pl.BlockSpec((1, F, D), lambda i, be, nu: (be[blk(i, be, nu)], 0, 0))],
        out_specs=pl.BlockSpec((EXP_BLK, hw), lambda i, be, nu: (blk(i, be, nu), 0)),
        scratch_shapes=[pltpu.VMEM((D, 2 * F), BF16), pltpu.VMEM((F, D), BF16)],
    )
    return pl.pallas_call(
        _expert_kernel,
        grid_spec=grid_spec,
        out_shape=jax.ShapeDtypeStruct((P, hw), I32),
        compiler_params=_cparams(("arbitrary",)),
        name="experts", interpret=interpret,
    )(block_e, n_used, xs, w1, w3, w2)


def _row_copy(src, s_row, dst, d_row, sem):
    return pltpu.make_async_copy(src.at[pl.ds(s_row, 1)], dst.at[pl.ds(d_row, 1)], sem)


def _dispatch_kernel(dest_ref, h2_ref, xs_in_ref, xs_ref, sem):
    del xs_in_ref
    td = h2_ref.shape[0]

    def issue(t, carry):
        for kk in range(TOP_K):
            _row_copy(h2_ref, t, xs_ref, dest_ref[0, 0, kk * td + t], sem).start()
        return carry

    def drain(t, carry):
        for kk in range(TOP_K):
            _row_copy(h2_ref, t, xs_ref, dest_ref[0, 0, kk * td + t], sem).wait()
        return carry

    lax.fori_loop(0, td, issue, 0)
    lax.fori_loop(0, td, drain, 0)


def _dispatch(h2, dest_tiles, n_slots, interpret):
    T, D = h2.shape
    nt, _, n = dest_tiles.shape
    td = n // TOP_K
    xs0 = jnp.zeros((n_slots, D), h2.dtype)
    return pl.pallas_call(
        _dispatch_kernel,
        grid=(nt,),
        in_specs=[pl.BlockSpec((1, 1, n), lambda i: (i, 0, 0), memory_space=pltpu.SMEM),
                  pl.BlockSpec((td, D), lambda i: (i, 0)),
                  pl.BlockSpec(memory_space=pl.ANY)],
        out_specs=pl.BlockSpec(memory_space=pl.ANY),
        out_shape=jax.ShapeDtypeStruct((n_slots, D), h2.dtype),
        scratch_shapes=[pltpu.SemaphoreType.DMA(())],
        input_output_aliases={2: 0},
        compiler_params=_cparams(("arbitrary",)),
        name="dispatch", interpret=interpret,
    )(dest_tiles, h2, xs0)


def _combine_kernel(dest_ref, base_ref, g2_ref, gw_ref, ys_ref, o_ref, buf, sem):
    td = base_ref.shape[0]

    def issue(t, carry):
        for kk in range(TOP_K):
            _row_copy(ys_ref, dest_ref[0, 0, kk * td + t], buf.at[kk], t, sem).start()
        return carry

    def drain(t, carry):
        for kk in range(TOP_K):
            _row_copy(ys_ref, dest_ref[0, 0, kk * td + t], buf.at[kk], t, sem).wait()
        return carry

    lax.fori_loop(0, td, issue, 0)
    lax.fori_loop(0, td, drain, 0)
    gw = gw_ref[...]
    acc_lo, acc_hi = _unpack_halves(buf[0])
    acc_lo, acc_hi = gw[:, 0:1] * acc_lo, gw[:, 0:1] * acc_hi
    for kk in range(1, TOP_K):
        y_lo, y_hi = _unpack_halves(buf[kk])
        acc_lo = acc_lo + gw[:, kk:kk + 1] * y_lo
        acc_hi = acc_hi + gw[:, kk:kk + 1] * y_hi
    o_ref[...] = base_ref[...] + g2_ref[0] * jnp.concatenate([acc_lo, acc_hi], axis=1)


def _combine(base, gate2, gw_tok, ys, dest_tiles, tiles_per_batch, interpret):
    T, D = base.shape
    nt, _, n = dest_tiles.shape
    td = n // TOP_K
    return pl.pallas_call(
        _combine_kernel,
        grid=(nt,),
        in_specs=[pl.BlockSpec((1, 1, n), lambda i: (i, 0, 0), memory_space=pltpu.SMEM),
                  pl.BlockSpec((td, D), lambda i: (i, 0)),
                  pl.BlockSpec((1, 1, D), lambda i: (i // tiles_per_batch, 0, 0)),
                  pl.BlockSpec((td, TOP_K), lambda i: (i, 0)),
                  pl.BlockSpec(memory_space=pl.ANY)],
        out_specs=pl.BlockSpec((td, D), lambda i: (i, 0)),
        out_shape=jax.ShapeDtypeStruct((T, D), F32),
        scratch_shapes=[pltpu.VMEM((TOP_K, td, D // 2), I32), pltpu.SemaphoreType.DMA(())],
        compiler_params=_cparams(("arbitrary",)),
        name="combine", interpret=interpret,
    )(dest_tiles, base, gate2, gw_tok, ys)


def _slots_kernel(eid_ref, rank_ref, pstart_ref, dest_ref):
    td = eid_ref.shape[2]
    E = pstart_ref.shape[0]
    row = lax.broadcasted_iota(I32, (E, td), 0)
    pstart = pstart_ref[...]
    eid = eid_ref[0]
    for kk in range(TOP_K):
        base = jnp.sum(jnp.where(row == eid[kk:kk + 1, :], pstart, 0), axis=0, keepdims=True)
        dest_ref[0, :, kk * td:(kk + 1) * td] = base + rank_ref[0, kk:kk + 1, :]


def _slot_plan(counts, eid_t, rank_t, td, interpret):
    B, _, S = eid_t.shape
    E = counts.shape[0]
    padded = (counts + EXP_BLK - 1) // EXP_BLK * EXP_BLK
    pend = jnp.cumsum(padded)
    pstart = (pend - padded).astype(I32)
    nb = -(-(B * S * TOP_K + E * (EXP_BLK - 1)) // EXP_BLK)
    block_e = jnp.searchsorted(pend, jnp.arange(nb, dtype=I32) * EXP_BLK, side='right')
    block_e = jnp.minimum(block_e, E - 1).astype(I32)
    nt = S // td
    dest_tiles = pl.pallas_call(
        _slots_kernel,
        grid=(B, nt),
        in_specs=[pl.BlockSpec((1, TOP_K, td), lambda b, i: (b, 0, i)),
                  pl.BlockSpec((1, TOP_K, td), lambda b, i: (b, 0, i)),
                  pl.BlockSpec((E, 1), lambda b, i: (0, 0))],
        out_specs=pl.BlockSpec((1, 1, TOP_K * td), lambda b, i: (b * nt + i, 0, 0)),
        out_shape=jax.ShapeDtypeStruct((B * nt, 1, TOP_K * td), I32),
        compiler_params=_cparams(("arbitrary", "arbitrary")),
        name="slots", interpret=interpret,
    )(eid_t, rank_t, pstart.reshape(E, 1))
    n_used = (pend[-1:] // EXP_BLK).astype(I32)
    return block_e, n_used, dest_tiles, nb * EXP_BLK


def _forward(x, c, positions, w_ada, b_ada, norm1_g, norm2_g, w_in, rw_mu, rw_w0, rw_w2,
             rw_a0, rw_a2, rw_g2, rw_k_k, rw_k_a, rw_r_k, rw_ln_w, rw_ln_b, q_norm_g,
             k_norm_g, idx_ln_w, idx_ln_b, at_out_g, w_out, router_w, router_bias,
             exp_w1, exp_w3, exp_w2, shared_w1, shared_w3, shared_w2, interpret=False):
    B, S, D = x.shape
    depth = w_ada.shape[0]
    for l in range(depth):
        mod = _mod(c, w_ada[l], b_ada[l], interpret)
        shift1, scale1, gate1, shift2, scale2, gate2 = [
            m.reshape(B, 1, D) for m in jnp.split(mod, 6, axis=-1)]
        tabs = _rope_tables(positions, interpret)
        p_rw, k, ki, qt, vt, qit, wit = _inproj(
            x, scale1, shift1, norm1_g[l], w_in[l], k_norm_g[l], idx_ln_w[l], idx_ln_b[l],
            q_norm_g[l], tabs, interpret)
        y_rw = _rwkv(p_rw, rw_mu[l], rw_w0[l], rw_w2[l], rw_a0[l], rw_a2[l], rw_g2[l], rw_k_k[l],
                     rw_k_a[l], rw_r_k[l], rw_ln_w[l], rw_ln_b[l], interpret)
        y_at = _dsa(k, vt, ki, qt, qit, wit, at_out_g[l], interpret)
        base, h2, eid_t, gw_t, rank_t, cnt = _post(
            x, y_rw, y_at, gate1, scale2, shift2, gate2, norm2_g[l], w_out[l], router_w[l],
            router_bias[l], shared_w1[l], shared_w3[l], shared_w2[l], interpret)
        T = B * S
        td = min(S, ROW_TILE)
        block_e, n_used, dest_tiles, n_slots = _slot_plan(cnt[:, 0].astype(I32), eid_t, rank_t, td,
                                                            interpret)
        xs = _dispatch(h2.reshape(T, D // 2), dest_tiles, n_slots, interpret)
        ys = _experts(xs, block_e, n_used, exp_w1[l], exp_w3[l], exp_w2[l], interpret)
        gw_tok = gw_t.transpose(0, 2, 1).reshape(T, TOP_K)
        x = _combine(base.reshape(T, D), gate2, gw_tok, ys, dest_tiles, S // td,
                     interpret).reshape(B, S, D)
    return x


def kernel(x, c, positions, w_ada, b_ada, norm1_g, norm2_g, w_in, rw_mu, rw_w0, rw_w2, rw_a0, rw_a2, rw_g2, rw_k_k, rw_k_a, rw_r_k, rw_ln_w, rw_ln_b, q_norm_g, k_norm_g, idx_ln_w, idx_ln_b, at_out_g, w_out, router_w, router_bias, exp_w1, exp_w3, exp_w2, shared_w1, shared_w3, shared_w2):
    return _forward(x, c, positions, w_ada, b_ada, norm1_g, norm2_g, w_in, rw_mu, rw_w0, rw_w2,
                    rw_a0, rw_a2, rw_g2, rw_k_k, rw_k_a, rw_r_k, rw_ln_w, rw_ln_b, q_norm_g,
                    k_norm_g, idx_ln_w, idx_ln_b, at_out_g, w_out, router_w, router_bias,
                    exp_w1, exp_w3, exp_w2, shared_w1, shared_w3, shared_w2)
```

```python
import functools

import jax
import jax.numpy as jnp
import numpy as np
from jax import lax
from jax.experimental import pallas as pl
from jax.experimental.pallas import tpu as pltpu

F32 = jnp.float32
BF16 = jnp.bfloat16
I32 = jnp.int32
HIGHEST = lax.Precision.HIGHEST

LANES = 128
HEAD_DIM = 64
HALF = HEAD_DIM // 2
RW_HEADS = 8
RW_WIDTH = RW_HEADS * HEAD_DIM
AT_HEADS = 8
AT_WIDTH = AT_HEADS * HEAD_DIM
IDX_HEADS = 4
RW_LORA_W, RW_LORA_A, RW_LORA_G = 64, 64, 128
RW_COLS = 3 * RW_WIDTH + RW_LORA_W + RW_LORA_A + RW_LORA_G
KI_OFF = 3 * AT_WIDTH + IDX_HEADS * HEAD_DIM
WI_OFF = KI_OFF + HEAD_DIM
PT_ROWS = 2 * AT_WIDTH + IDX_HEADS * HEAD_DIM + 8
PK_COLS = AT_WIDTH + LANES
ROPE_THETA = 10000.0
NORM_EPS = 1e-6
LN_EPS = 1e-6
RW_GN_EPS = 64e-5
IDX_TOPK_MAX = 256
N_EXPERTS = 256
TOP_K = 8
N_GROUPS = 8
TOPK_GROUPS = 4
ROUTED_SCALE = 2.5
INT_MIN = -2 ** 31
PAIR_HI_MASK = -65536
NEG_BIG = -1e30

RW_CHUNK = 128
QBLK = 128
SCORE_UNROLL = 2
ATTN_UNROLL = 2
COUNT_UNROLL = 4
EXP_BLK = 512
ROW_TILE = 256
VMEM_LIMIT = 48 * 1024 * 1024


def _cparams(sem):
    return pltpu.CompilerParams(dimension_semantics=sem, vmem_limit_bytes=VMEM_LIMIT)


def _sigmoid(x):
    return 1.0 / (1.0 + jnp.exp(-x))


def _silu(x):
    return x * _sigmoid(x)


def _dot(a, b):
    return jnp.dot(a, b, preferred_element_type=F32)


def _dot_split(a, b):
    hi = a.astype(BF16)
    lo = (a - hi.astype(F32)).astype(BF16)
    return _dot(hi, b) + _dot(lo, b)


def _pack_halves(x):
    w = x.shape[1] // 2
    lo = pltpu.bitcast(x[:, :w].astype(BF16).astype(F32), I32)
    hi = pltpu.bitcast(x[:, w:].astype(BF16).astype(F32), I32)
    return (hi & PAIR_HI_MASK) | lax.shift_right_logical(lo, 16)


def _unpack_halves(p):
    return pltpu.bitcast(p << 16, F32), pltpu.bitcast(p & PAIR_HI_MASK, F32)


def _dot_nt(a, b):
    return lax.dot_general(a, b, (((1,), (1,)), ((), ())), preferred_element_type=F32)


def _mod_kernel(c_ref, w_ref, b_ref, o_ref):
    c = c_ref[...]
    o_ref[...] = jnp.dot(_silu(c), w_ref[...], precision=HIGHEST,
                         preferred_element_type=F32) + b_ref[...]


def _mod(c, w_ada, b_ada, interpret):
    B, D = c.shape
    n = w_ada.shape[1] // D
    return pl.pallas_call(
        _mod_kernel,
        grid=(n,),
        in_specs=[pl.BlockSpec((B, D), lambda i: (0, 0)),
                  pl.BlockSpec((D, D), lambda i: (0, i)),
                  pl.BlockSpec((1, D), lambda i: (0, i))],
        out_specs=pl.BlockSpec((B, D), lambda i: (0, i)),
        out_shape=jax.ShapeDtypeStruct((B, n * D), F32),
        compiler_params=_cparams(("arbitrary",)),
        name="mod", interpret=interpret,
    )(c, w_ada, b_ada.reshape(1, -1))


def _rope_tab_kernel(pc_ref, pr_ref, cr_ref, sr_ref, ct_ref, st_ref):
    log_theta = float(np.log(ROPE_THETA))
    lane = lax.broadcasted_iota(I32, (1, LANES), 1)
    inv_r = jnp.exp((lane % HALF).astype(F32) * (-log_theta / HALF))
    ang = pc_ref[0].astype(F32) * inv_r
    cr_ref[0] = jnp.cos(ang)
    sr_ref[0] = jnp.where((lane % HEAD_DIM) < HALF, -jnp.sin(ang), jnp.sin(ang))
    sub = lax.broadcasted_iota(I32, (HALF, 1), 0)
    inv_c = jnp.exp(sub.astype(F32) * (-log_theta / HALF))
    ang_t = inv_c * pr_ref[0].astype(F32)
    ct_ref[0] = jnp.cos(ang_t)
    st_ref[0] = jnp.sin(ang_t)


def _rope_tables(positions, interpret):
    B, S = positions.shape
    ts = min(S, 512)
    return pl.pallas_call(
        _rope_tab_kernel,
        grid=(B, S // ts),
        in_specs=[pl.BlockSpec((1, ts, 1), lambda b, i: (b, i, 0)),
                  pl.BlockSpec((1, 1, ts), lambda b, i: (b, 0, i))],
        out_specs=[pl.BlockSpec((1, ts, LANES), lambda b, i: (b, i, 0)),
                   pl.BlockSpec((1, ts, LANES), lambda b, i: (b, i, 0)),
                   pl.BlockSpec((1, HALF, ts), lambda b, i: (b, 0, i)),
                   pl.BlockSpec((1, HALF, ts), lambda b, i: (b, 0, i))],
        out_shape=[jax.ShapeDtypeStruct((B, S, LANES), F32),
                   jax.ShapeDtypeStruct((B, S, LANES), F32),
                   jax.ShapeDtypeStruct((B, HALF, S), F32),
                   jax.ShapeDtypeStruct((B, HALF, S), F32)],
        compiler_params=_cparams(("arbitrary", "arbitrary")),
        name="rope_tab", interpret=interpret,
    )(positions.reshape(B, S, 1), positions.reshape(B, 1, S))


def _rope_rows(y, cos, sin_signed):
    lane = lax.broadcasted_iota(I32, (1, LANES), 1)
    partner = jnp.where((lane % HEAD_DIM) < HALF,
                        pltpu.roll(y, LANES - HALF, 1), pltpu.roll(y, HALF, 1))
    return y * cos + partner * sin_signed


def _rope_cols(y, cos_t, sin_t):
    x1, x2 = y[:, :HALF], y[:, HALF:]
    return jnp.concatenate([x1 * cos_t - x2 * sin_t, x2 * cos_t + x1 * sin_t], axis=1)


def _inproj_kernel(x_ref, sc_ref, sh_ref, g_ref, wrw_ref, wk_ref, wt_ref, kg_ref, iw_ref, ib_ref,
                   qg_ref, gsum_ref, cr_ref, sr_ref, ct_ref, st_ref,
                   prw_ref, k_ref, ki_ref, qt_ref, vt_ref, qit_ref, wit_ref):
    tm = x_ref.shape[1]
    x = x_ref[0]
    ms = jnp.mean(x * x, axis=-1, keepdims=True)
    h = x * lax.rsqrt(ms + NORM_EPS) * g_ref[...] * (1.0 + sc_ref[0]) + sh_ref[0]
    hb = h.astype(BF16)
    prw_ref[0] = _dot(hb, wrw_ref[...])
    pk = _dot(hb, wk_ref[...])
    pt = _dot_nt(wt_ref[...], hb)

    cos_r, sin_r = cr_ref[0], sr_ref[0]
    gsum = gsum_ref[...]
    inv_hd = 1.0 / HEAD_DIM
    for p in range(AT_WIDTH // LANES):
        xk = pk[:, p * LANES:(p + 1) * LANES]
        ss = jnp.dot(xk * xk, gsum, precision=HIGHEST, preferred_element_type=F32)
        y = xk * lax.rsqrt(ss * inv_hd + NORM_EPS) * kg_ref[...]
        k_ref[0, :, p * LANES:(p + 1) * LANES] = _rope_rows(y, cos_r, sin_r).astype(BF16)
    xi = pk[:, AT_WIDTH:AT_WIDTH + LANES]
    mu = jnp.dot(xi, gsum, precision=HIGHEST, preferred_element_type=F32) * inv_hd
    xc = xi - mu
    var = jnp.dot(xc * xc, gsum, precision=HIGHEST, preferred_element_type=F32) * inv_hd
    yi = xc * lax.rsqrt(var + LN_EPS) * iw_ref[...] + ib_ref[...]
    ki_ref[0] = _rope_rows(yi, cos_r, sin_r).astype(BF16)

    cos_t, sin_t = ct_ref[0][None], st_ref[0][None]
    xq = pt[0:AT_WIDTH].reshape(AT_HEADS, HEAD_DIM, tm)
    msq = jnp.mean(xq * xq, axis=1, keepdims=True)
    yq = xq * lax.rsqrt(msq + NORM_EPS) * qg_ref[...][None]
    yq = _rope_cols(yq, cos_t, sin_t) * (HEAD_DIM ** -0.5)
    zq = jnp.zeros((HEAD_DIM, tm), BF16)
    for hh in range(AT_HEADS):
        parts = [yq[hh].astype(BF16), zq] if hh % 2 == 0 else [zq, yq[hh].astype(BF16)]
        qt_ref[0, hh] = jnp.concatenate(parts, axis=0)
    vt = pt[AT_WIDTH:2 * AT_WIDTH].astype(BF16)
    for cblk in range(tm // QBLK):
        vt_ref[0, cblk] = vt[:, cblk * QBLK:(cblk + 1) * QBLK]
    xqi = pt[2 * AT_WIDTH:2 * AT_WIDTH + IDX_HEADS * HEAD_DIM].reshape(IDX_HEADS, HEAD_DIM, tm)
    yqi = _rope_cols(xqi, cos_t, sin_t)
    for hh in range(IDX_HEADS):
        qit_ref[0, hh] = jnp.concatenate([yqi[hh].astype(BF16), zq], axis=0)
    wit_ref[0] = pt[PT_ROWS - 8:PT_ROWS] * (IDX_HEADS ** -0.5 * HEAD_DIM ** -0.5)


def _inproj(x, scale1, shift1, norm1_g, w_in, k_norm_g, idx_ln_w, idx_ln_b, q_norm_g,
            tabs, interpret):
    B, S, D = x.shape
    tm = min(S, 256)
    cos_r, sin_r, cos_t, sin_t = tabs
    w_at = w_in[:, RW_COLS:]
    w_rw = w_in[:, :RW_COLS].astype(BF16)
    w_k = jnp.concatenate([w_at[:, AT_WIDTH:2 * AT_WIDTH], w_at[:, KI_OFF:KI_OFF + HEAD_DIM],
                           jnp.zeros((D, HEAD_DIM), F32)], axis=1).astype(BF16)
    w_t = jnp.concatenate([w_at[:, 0:AT_WIDTH], w_at[:, 2 * AT_WIDTH:3 * AT_WIDTH],
                           w_at[:, 3 * AT_WIDTH:KI_OFF], w_at[:, WI_OFF:WI_OFF + IDX_HEADS],
                           jnp.zeros((D, 8 - IDX_HEADS), F32)], axis=1).T.astype(BF16)
    kg = jnp.tile(k_norm_g, 2).reshape(1, LANES)
    zpad = jnp.zeros((HEAD_DIM,), F32)
    iw = jnp.concatenate([idx_ln_w, zpad]).reshape(1, LANES)
    ib = jnp.concatenate([idx_ln_b, zpad]).reshape(1, LANES)
    qg = q_norm_g.reshape(HEAD_DIM, 1)
    li = np.arange(LANES)
    gsum = jnp.asarray((li[:, None] // HEAD_DIM == li[None, :] // HEAD_DIM).astype(np.float32))

    full = lambda shape: pl.BlockSpec(shape, lambda b, i: (0,) * len(shape))
    return pl.pallas_call(
        _inproj_kernel,
        grid=(B, S // tm),
        in_specs=[pl.BlockSpec((1, tm, D), lambda b, i: (b, i, 0)),
                  pl.BlockSpec((1, 1, D), lambda b, i: (b, 0, 0)),
                  pl.BlockSpec((1, 1, D), lambda b, i: (b, 0, 0)),
                  full((1, D)), full((D, RW_COLS)), full((D, PK_COLS)), full((PT_ROWS, D)),
                  full((1, LANES)), full((1, LANES)), full((1, LANES)), full((HEAD_DIM, 1)),
                  full((LANES, LANES)),
                  pl.BlockSpec((1, tm, LANES), lambda b, i: (b, i, 0)),
                  pl.BlockSpec((1, tm, LANES), lambda b, i: (b, i, 0)),
                  pl.BlockSpec((1, HALF, tm), lambda b, i: (b, 0, i)),
                  pl.BlockSpec((1, HALF, tm), lambda b, i: (b, 0, i))],
        out_specs=[pl.BlockSpec((1, tm, RW_COLS), lambda b, i: (b, i, 0)),
                   pl.BlockSpec((1, tm, AT_WIDTH), lambda b, i: (b, i, 0)),
                   pl.BlockSpec((1, tm, LANES), lambda b, i: (b, i, 0)),
                   pl.BlockSpec((1, AT_HEADS, LANES, tm), lambda b, i: (b, 0, 0, i)),
                   pl.BlockSpec((1, tm // QBLK, AT_WIDTH, QBLK), lambda b, i: (b, i, 0, 0)),
                   pl.BlockSpec((1, IDX_HEADS, LANES, tm), lambda b, i: (b, 0, 0, i)),
                   pl.BlockSpec((1, 8, tm), lambda b, i: (b, 0, i))],
        out_shape=[jax.ShapeDtypeStruct((B, S, RW_COLS), F32),
                   jax.ShapeDtypeStruct((B, S, AT_WIDTH), BF16),
                   jax.ShapeDtypeStruct((B, S, LANES), BF16),
                   jax.ShapeDtypeStruct((B, AT_HEADS, LANES, S), BF16),
                   jax.ShapeDtypeStruct((B, S // QBLK, AT_WIDTH, QBLK), BF16),
                   jax.ShapeDtypeStruct((B, IDX_HEADS, LANES, S), BF16),
                   jax.ShapeDtypeStruct((B, 8, S), F32)],
        compiler_params=_cparams(("arbitrary", "arbitrary")),
        name="inproj", interpret=interpret,
    )(x, scale1, shift1, norm1_g.reshape(1, D), w_rw, w_k, w_t, kg, iw, ib, qg, gsum,
      cos_r, sin_r, cos_t, sin_t)


def _rwkv_kernel(p_ref, mu_ref, w0_ref, w2_ref, a0_ref, a2_ref, g2_ref, kk_ref, ka_ref, rk_ref,
                 lnw_ref, lnb_ref, gsum_ref, y_ref, s_ref, prev_ref, yt_ref):
    C = RW_CHUNK
    W = RW_WIDTH

    @pl.when(pl.program_id(1) == 0)
    def _():
        s_ref[...] = jnp.zeros_like(s_ref)
        prev_ref[...] = jnp.zeros_like(prev_ref)

    p = p_ref[0]
    row = lax.broadcasted_iota(I32, (C, 1), 0)
    pprev = jnp.where(row == 0, prev_ref[...], pltpu.roll(p, 1, 0))
    prev_ref[...] = p[C - 1:C]
    ps = p + (pprev - p) * mu_ref[...]
    r, k, v = ps[:, 0:W], ps[:, W:2 * W], ps[:, 2 * W:3 * W]
    o = 3 * W
    wl = ps[:, o:o + RW_LORA_W]
    al = ps[:, o + RW_LORA_W:o + RW_LORA_W + RW_LORA_A]
    gl = ps[:, o + RW_LORA_W + RW_LORA_A:]

    z = w0_ref[...] + _dot(jnp.tanh(wl).astype(BF16), w2_ref[...])
    nz = -z
    softplus = jnp.maximum(nz, 0.0) + jnp.log(1.0 + jnp.exp(-jnp.abs(nz)))
    logw = -jnp.exp(-softplus - 0.5)
    a = _sigmoid(a0_ref[...] + _dot(al.astype(BF16), a2_ref[...]))
    g = _dot(_sigmoid(gl).astype(BF16), g2_ref[...])
    gsum = gsum_ref[...]
    kk = k * kk_ref[...]
    ss = _dot((kk * kk).astype(BF16), gsum)
    kk = kk * (1.0 / jnp.maximum(jnp.sqrt(ss), 1e-12))
    k2 = k * (1.0 + (a - 1.0) * ka_ref[...])
    bb = kk * a

    cw = logw
    sh = 1
    while sh < C:
        cw = cw + jnp.where(row >= sh, pltpu.roll(cw, sh, 0), 0.0)
        sh *= 2
    cw_last = cw[C - 1:C]
    e_neg = jnp.exp(-cw)
    e_end = jnp.exp(cw_last - cw)
    rw = r * jnp.exp(cw)
    kkp = kk * jnp.exp(cw - logw)
    bw, kw = bb * e_neg, k2 * e_neg
    bend, kend = bb * e_end, k2 * e_end
    wc = jnp.exp(cw_last)
    vt_all = v.T.astype(BF16)

    ri = lax.broadcasted_iota(I32, (C, C), 0)
    ci = lax.broadcasted_iota(I32, (C, C), 1)
    strict = ri < ci
    incl = ri <= ci
    incl2 = jnp.concatenate([incl, incl], axis=0)
    lane_half = lax.broadcasted_iota(I32, (1, LANES), 1) // HEAD_DIM

    heads = range(RW_HEADS)
    pair = lambda h: slice((h // 2) * LANES, (h // 2 + 1) * LANES)
    own = [lane_half == (h % 2) for h in heads]
    lh = [jnp.concatenate([kkp[:, pair(2 * q)], rw[:, pair(2 * q)]], axis=0).astype(BF16)
          for q in range(RW_HEADS // 2)]
    rh = [jnp.where(own[h], jnp.concatenate([bw[:, pair(h)], kw[:, pair(h)]], axis=0), 0.0).astype(BF16)
          for h in heads]
    aat = [_dot_nt(rh[h], lh[h // 2]) for h in heads]
    s_old = [s_ref[h] for h in heads]
    sl = [_dot_nt(s_old[h].astype(BF16), lh[h // 2]) for h in heads]
    vt = [vt_all[h * HEAD_DIM:(h + 1) * HEAD_DIM] for h in heads]
    akt = [jnp.where(strict, aat[h][C:, :C], 0.0).astype(BF16) for h in heads]
    m = [jnp.where(strict, aat[h][:C, :C], 0.0).astype(BF16) for h in heads]
    xs = [-(sl[h][:, :C] + _dot(vt[h], akt[h])) for h in heads]
    xs = [xs[h] - _dot_split(xs[h], m[h]) for h in heads]
    lvl = 2
    while lvl < C:
        m = [_dot(m[h], m[h]).astype(BF16) for h in heads]
        xs = [xs[h] + _dot_split(xs[h], m[h]) for h in heads]
        lvl *= 2
    zt = [jnp.concatenate([xs[h].astype(BF16), vt[h]], axis=1) for h in heads]
    for h in heads:
        ymat = jnp.where(incl2, aat[h][:, C:], 0.0).astype(BF16)
        yt_ref[h * HEAD_DIM:(h + 1) * HEAD_DIM, :] = sl[h][:, C:] + _dot(zt[h], ymat)
    for h in heads:
        endz = jnp.where(own[h], jnp.concatenate([bend[:, pair(h)], kend[:, pair(h)]], axis=0),
                         0.0).astype(BF16)
        s_ref[h] = s_old[h] * wc[:, pair(h)] + _dot(zt[h], endz)

    yt = yt_ref[...].reshape(RW_HEADS, HEAD_DIM, C)
    mean = jnp.mean(yt, axis=1, keepdims=True)
    yc = yt - mean
    var = jnp.mean(yc * yc, axis=1, keepdims=True)
    lnw = lnw_ref[...].reshape(RW_HEADS, HEAD_DIM, 1)
    lnb = lnb_ref[...].reshape(RW_HEADS, HEAD_DIM, 1)
    yn = yc * lax.rsqrt(var + RW_GN_EPS) * lnw + lnb
    y = yn.reshape(W, C).T
    bonus = _dot((r * k2 * rk_ref[...]).astype(BF16), gsum) * v
    y_ref[0] = ((y + bonus) * g).astype(BF16)


def _rwkv(p_rw, rw_mu, rw_w0, rw_w2, rw_a0, rw_a2, rw_g2, rw_k_k, rw_k_a, rw_r_k, rw_ln_w, rw_ln_b,
          interpret):
    B, S, _ = p_rw.shape
    C, W = RW_CHUNK, RW_WIDTH
    li = np.arange(W)
    gsum = jnp.asarray((li[:, None] // HEAD_DIM == li[None, :] // HEAD_DIM).astype(np.float32)).astype(BF16)
    row = lambda a: a.reshape(1, -1)
    full = lambda shape: pl.BlockSpec(shape, lambda b, i: (0,) * len(shape))
    return pl.pallas_call(
        _rwkv_kernel,
        grid=(B, S // C),
        in_specs=[pl.BlockSpec((1, C, RW_COLS), lambda b, i: (b, i, 0)),
                  full((1, RW_COLS)), full((1, W)), full((RW_LORA_W, W)), full((1, W)),
                  full((RW_LORA_A, W)), full((RW_LORA_G, W)), full((1, W)), full((1, W)),
                  full((1, W)), full((W, 1)), full((W, 1)), full((W, W))],
        out_specs=pl.BlockSpec((1, C, W), lambda b, i: (b, i, 0)),
        out_shape=jax.ShapeDtypeStruct((B, S, W), BF16),
        scratch_shapes=[pltpu.VMEM((RW_HEADS, HEAD_DIM, LANES), F32),
                        pltpu.VMEM((1, RW_COLS), F32),
                        pltpu.VMEM((W, C), F32)],
        compiler_params=_cparams(("arbitrary", "arbitrary")),
        name="rwkv", interpret=interpret,
    )(p_rw, row(rw_mu), row(rw_w0), rw_w2.astype(BF16), row(rw_a0), rw_a2.astype(BF16),
      rw_g2.astype(BF16), row(rw_k_k), row(rw_k_a), row(rw_r_k), rw_ln_w.reshape(W, 1),
      rw_ln_b.reshape(W, 1), gsum)


def _dsa_kernel(topk, nbits, k_ref, vt_ref, ki_ref, qt_ref, qit_ref, wit_ref, og_ref, o_ref,
                key_s, acc_s, m_s, l_s, thr_s):
    j = pl.program_id(1)
    nkb = j + 1
    lane = lax.broadcasted_iota(I32, (QBLK, QBLK), 1)
    sub = lax.broadcasted_iota(I32, (QBLK, QBLK), 0)
    qpos = j * QBLK + lane
    wit = wit_ref[0]

    def score_blocks(i, carry):
        kbs = [i * SCORE_UNROLL + u for u in range(SCORE_UNROLL)]
        kib = [ki_ref[0, pl.ds(pl.multiple_of(kb * QBLK, QBLK), QBLK), :] for kb in kbs]
        lg = [[_dot(kib[u], qit_ref[0, hh]) for hh in range(IDX_HEADS)] for u in range(SCORE_UNROLL)]
        for u, kb in enumerate(kbs):
            s = wit[0:1, :] * jnp.maximum(lg[u][0], 0.0)
            for hh in range(1, IDX_HEADS):
                s = s + wit[hh:hh + 1, :] * jnp.maximum(lg[u][hh], 0.0)
            s = jnp.where(s == 0.0, 0.0, s)
            bits = pltpu.bitcast(s, I32)
            skey = jnp.where(bits < 0, bits ^ 0x7FFFFFFF, bits)
            key_s[kb] = jnp.where(kb * QBLK + sub <= qpos, skey, INT_MIN)
        return carry

    lax.fori_loop(0, pl.cdiv(nkb, SCORE_UNROLL), score_blocks, 0)

    @pl.when(nkb * QBLK <= topk)
    def _():
        thr_s[0:1, :] = jnp.full((1, QBLK), INT_MIN, I32)
        thr_s[1:2, :] = jnp.zeros((1, QBLK), I32)

    @pl.when(nkb * QBLK > topk)
    def _():
        n_done = pl.cdiv(nkb, SCORE_UNROLL) * SCORE_UNROLL
        n_cnt = pl.cdiv(nkb, COUNT_UNROLL)

        def fill(kb, carry):
            key_s[kb] = jnp.full((QBLK, QBLK), INT_MIN, I32)
            return carry

        lax.fori_loop(n_done, n_cnt * COUNT_UNROLL, fill, 0)

        def count(preds):
            def body(i, accs):
                accs = list(accs)
                for u in range(COUNT_UNROLL):
                    kb = i * COUNT_UNROLL + u
                    ky = key_s[kb]
                    for n, pred in enumerate(preds):
                        hit = pred(ky, kb * QBLK + sub).astype(I32)
                        accs[n] = accs[n] + jnp.sum(hit.reshape(QBLK // 8, 8, QBLK), axis=0)
                return tuple(accs)
            accs = lax.fori_loop(0, n_cnt, body, tuple(jnp.zeros((8, QBLK), I32) for _ in preds))
            return [jnp.sum(a, axis=0, keepdims=True) for a in accs]

        c0, = count([lambda ky, ix: ky >= 0])
        t0 = jnp.where(c0 >= topk, 0, INT_MIN).astype(I32)

        def bit_step(i, t):
            cand = t | jnp.left_shift(jnp.int32(1), 30 - i)
            c, = count([lambda ky, ix: ky >= cand])
            return jnp.where(c >= topk, cand, t)

        thr = lax.fori_loop(0, 31, bit_step, t0)
        n_gt, n_eq = count([lambda ky, ix: ky > thr, lambda ky, ix: ky == thr])
        need = topk - n_gt
        thr_s[0:1, :] = thr
        thr_s[1:2, :] = jnp.full((1, QBLK), 2 ** nbits, I32)

        @pl.when(jnp.max(jnp.abs(n_eq - need)) > 0)
        def _():
            def idx_step(i, mm):
                cand = mm | jnp.left_shift(jnp.int32(1), nbits - 1 - i)
                c, = count([lambda ky, ix: (ky == thr) & (ix < cand)])
                return jnp.where(c < need, cand, mm)

            thr_s[1:2, :] = lax.fori_loop(0, nbits, idx_step, jnp.zeros((1, QBLK), I32))

    thr = thr_s[0:1, :]
    mm = thr_s[1:2, :]
    m_s[...] = jnp.full_like(m_s, NEG_BIG)
    l_s[...] = jnp.zeros_like(l_s)
    acc_s[...] = jnp.zeros_like(acc_s)

    qt2 = [jnp.concatenate([qt_ref[0, 2 * q], qt_ref[0, 2 * q + 1]], axis=1)
           for q in range(AT_HEADS // 2)]

    def attn_blocks(i, carry):
        kbs = [i * ATTN_UNROLL + u for u in range(ATTN_UNROLL)]
        sel = []
        for kb in kbs:
            skey = key_s[kb]
            kidx = kb * QBLK + sub
            sel.append((kidx <= qpos) & ((skey > thr) | ((skey == thr) & (kidx <= mm))))
        sel = jnp.concatenate(sel, axis=0)
        kblk = [k_ref[0, pl.ds(pl.multiple_of(kb * QBLK, QBLK), QBLK), :] for kb in kbs]
        vtb = jnp.concatenate([vt_ref[0, kb] for kb in kbs], axis=1)
        s2 = [[_dot(kblk[u][:, q * LANES:(q + 1) * LANES], qt2[q]) for u in range(ATTN_UNROLL)]
              for q in range(AT_HEADS // 2)]
        pexp, alpha = [], []
        for hh in range(AT_HEADS):
            half = slice((hh % 2) * QBLK, (hh % 2 + 1) * QBLK)
            s = jnp.concatenate([s2[hh // 2][u][:, half] for u in range(ATTN_UNROLL)], axis=0)
            s = jnp.where(sel, s, NEG_BIG)
            m_old = m_s[hh:hh + 1, :]
            m_new = jnp.maximum(m_old, jnp.max(s, axis=0, keepdims=True))
            pe = jnp.exp(s - m_new)
            al = jnp.exp(m_old - m_new)
            l_s[hh:hh + 1, :] = al * l_s[hh:hh + 1, :] + jnp.sum(pe, axis=0, keepdims=True)
            m_s[hh:hh + 1, :] = m_new
            pexp.append(pe.astype(BF16))
            alpha.append(al)
        for hh in range(AT_HEADS):
            hs = slice(hh * HEAD_DIM, (hh + 1) * HEAD_DIM)
            acc_s[hs, :] = alpha[hh] * acc_s[hs, :] + _dot(vtb[hs, :], pexp[hh])
        return carry

    lax.fori_loop(0, pl.cdiv(nkb, ATTN_UNROLL), attn_blocks, 0)

    for hh in range(AT_HEADS):
        hs = slice(hh * HEAD_DIM, (hh + 1) * HEAD_DIM)
        oh = acc_s[hs, :] * (1.0 / l_s[hh:hh + 1, :])
        ms = jnp.mean(oh * oh, axis=0, keepdims=True)
        acc_s[hs, :] = oh * lax.rsqrt(ms + NORM_EPS) * og_ref[hs, :]
    o_ref[0] = acc_s[...].T.astype(BF16)


def _dsa(k, vt, ki, qt, qit, wit, at_out_g, interpret):
    B, S, _ = k.shape
    nq = S // QBLK
    topk = min(IDX_TOPK_MAX, S // 4)
    nbits = int(np.log2(S))
    assert 2 ** nbits == S and nq % COUNT_UNROLL == 0 and SCORE_UNROLL == ATTN_UNROLL
    assert COUNT_UNROLL % SCORE_UNROLL == 0
    return pl.pallas_call(
        functools.partial(_dsa_kernel, topk, nbits),
        grid=(B, nq),
        in_specs=[pl.BlockSpec((1, S, AT_WIDTH), lambda b, j: (b, 0, 0)),
                  pl.BlockSpec((1, nq, AT_WIDTH, QBLK), lambda b, j: (b, 0, 0, 0)),
                  pl.BlockSpec((1, S, LANES), lambda b, j: (b, 0, 0)),
                  pl.BlockSpec((1, AT_HEADS, LANES, QBLK), lambda b, j: (b, 0, 0, j)),
                  pl.BlockSpec((1, IDX_HEADS, LANES, QBLK), lambda b, j: (b, 0, 0, j)),
                  pl.BlockSpec((1, 8, QBLK), lambda b, j: (b, 0, j)),
                  pl.BlockSpec((AT_WIDTH, 1), lambda b, j: (0, 0))],
        out_specs=pl.BlockSpec((1, QBLK, AT_WIDTH), lambda b, j: (b, j, 0)),
        out_shape=jax.ShapeDtypeStruct((B, S, AT_WIDTH), BF16),
        scratch_shapes=[pltpu.VMEM((nq, QBLK, QBLK), I32),
                        pltpu.VMEM((AT_WIDTH, QBLK), F32),
                        pltpu.VMEM((AT_HEADS, QBLK), F32),
                        pltpu.VMEM((AT_HEADS, QBLK), F32),
                        pltpu.VMEM((8, QBLK), I32)],
        compiler_params=_cparams(("arbitrary", "arbitrary")),
        name="dsa", interpret=interpret,
    )(k, vt, ki, qt, qit, wit, at_out_g.reshape(AT_WIDTH, 1))


def _first_max(vals, idx, axis, sentinel):
    m = jnp.max(vals, axis=axis, keepdims=True)
    return m, jnp.min(jnp.where(vals == m, idx, sentinel), axis=axis, keepdims=True)


def _route_cols(logits_t, bias_col):
    E, tm = logits_t.shape
    pg = E // N_GROUPS
    scores = _sigmoid(logits_t)
    biased = scores + bias_col
    b3 = biased.reshape(N_GROUPS, pg, tm)
    r3 = lax.broadcasted_iota(I32, (N_GROUPS, pg, tm), 1)
    m1, first = _first_max(b3, r3, 1, pg)
    m2 = jnp.max(jnp.where(r3 == first, -jnp.inf, b3), axis=1, keepdims=True)
    cur = (m1 + m2).reshape(N_GROUPS, tm)
    grow = lax.broadcasted_iota(I32, (N_GROUPS, tm), 0)
    gsel = jnp.zeros((N_GROUPS, tm), F32)
    for _ in range(TOPK_GROUPS):
        _, gi = _first_max(cur, grow, 0, N_GROUPS)
        hit = grow == gi
        gsel = jnp.where(hit, 1.0, gsel)
        cur = jnp.where(hit, -jnp.inf, cur)
    gmask = jnp.broadcast_to(gsel.reshape(N_GROUPS, 1, tm), (N_GROUPS, pg, tm)).reshape(E, tm)
    cur = jnp.where(gmask > 0.0, biased, -jnp.inf)
    row = lax.broadcasted_iota(I32, (E, tm), 0)
    onehot = jnp.zeros((E, tm), F32)
    eids, gws = [], []
    for _ in range(TOP_K):
        _, ei = _first_max(cur, row, 0, E)
        hit = row == ei
        eids.append(ei)
        gws.append(jnp.sum(jnp.where(hit, scores, 0.0), axis=0, keepdims=True))
        onehot = jnp.where(hit, 1.0, onehot)
        cur = jnp.where(hit, -jnp.inf, cur)
    eid = jnp.concatenate(eids, axis=0)
    gw = jnp.concatenate(gws, axis=0)
    gw = gw * (ROUTED_SCALE / jnp.sum(gw, axis=0, keepdims=True))
    return eid, gw, onehot


def _post_kernel(x_ref, yrw_ref, yat_ref, g1_ref, sc_ref, sh_ref, g2_ref, ng_ref, wo_ref, rwt_ref,
                 rb_ref, s1_ref, s3_ref, s2_ref, base_ref, h2_ref, eid_ref, gw_ref, rank_ref, cnt_ref):
    W = RW_WIDTH
    tm = x_ref.shape[1]
    E = rwt_ref.shape[0]

    @pl.when((pl.program_id(0) == 0) & (pl.program_id(1) == 0))
    def _():
        cnt_ref[...] = jnp.zeros_like(cnt_ref)

    mix = _dot(yrw_ref[0], wo_ref[0:W, :]) + _dot(yat_ref[0], wo_ref[W:, :])
    x1 = x_ref[0] + g1_ref[0] * mix
    ms = jnp.mean(x1 * x1, axis=-1, keepdims=True)
    h2 = x1 * lax.rsqrt(ms + NORM_EPS) * ng_ref[...] * (1.0 + sc_ref[0]) + sh_ref[0]
    hb = h2.astype(BF16)
    h2_ref[0] = _pack_halves(h2)
    act =(_silu(_dot(hb, s1_ref[...])) * _dot(hb, s3_ref[...])).astype(BF16)
    base_ref[0] = x1 + g2_ref[0] * _dot(act, s2_ref[...])

    logits_t = lax.dot_general(rwt_ref[...], h2, (((1,), (1,)), ((), ())), precision=HIGHEST,
                               preferred_element_type=F32)
    eid, gw, onehot = _route_cols(logits_t, rb_ref[...])
    eid_ref[0] = eid
    gw_ref[0] = gw
    ti = lax.broadcasted_iota(I32, (tm, tm), 0)
    tj = lax.broadcasted_iota(I32, (tm, tm), 1)
    before = _dot(onehot.astype(BF16), (ti < tj).astype(BF16)) + cnt_ref[:, 0:1]
    row = lax.broadcasted_iota(I32, (E, tm), 0)
    ranks = [jnp.sum(jnp.where(row == eid[kk:kk + 1, :], before, 0.0), axis=0, keepdims=True)
             for kk in range(TOP_K)]
    rank_ref[0] = jnp.concatenate(ranks, axis=0).astype(I32)
    cnt_ref[...] = cnt_ref[...] + jnp.sum(onehot, axis=1, keepdims=True)


def _post(x, y_rw, y_at, gate1, scale2, shift2, gate2, norm2_g, w_out, router_w, router_bias,
          sw1, sw3, sw2, interpret):
    B, S, D = x.shape
    tm = min(S, 256)
    E = router_w.shape[1]
    sd = sw1.shape[1]
    full = lambda shape: pl.BlockSpec(shape, lambda b, i: (0,) * len(shape))
    tok = lambda w: pl.BlockSpec((1, tm, w), lambda b, i: (b, i, 0))
    per_b = pl.BlockSpec((1, 1, D), lambda b, i: (b, 0, 0))
    col8 = pl.BlockSpec((1, TOP_K, tm), lambda b, i: (b, 0, i))
    return pl.pallas_call(
        _post_kernel,
        grid=(B, S // tm),
        in_specs=[tok(D), tok(RW_WIDTH), tok(AT_WIDTH), per_b, per_b, per_b, per_b, full((1, D)),
                  full((D, D)), full((E, D)), full((E, 1)), full((D, sd)), full((D, sd)),
                  full((sd, D))],
        out_specs=[tok(D), tok(D // 2), col8, col8, col8, full((E, LANES))],
        out_shape=[jax.ShapeDtypeStruct((B, S, D), F32),
                   jax.ShapeDtypeStruct((B, S, D // 2), I32),
                   jax.ShapeDtypeStruct((B, TOP_K, S), I32),
                   jax.ShapeDtypeStruct((B, TOP_K, S), F32),
                   jax.ShapeDtypeStruct((B, TOP_K, S), I32),
                   jax.ShapeDtypeStruct((E, LANES), F32)],
        compiler_params=_cparams(("arbitrary", "arbitrary")),
        name="post", interpret=interpret,
    )(x, y_rw, y_at, gate1, scale2, shift2, gate2, norm2_g.reshape(1, D), w_out.astype(BF16),
      router_w.T, router_bias.reshape(E, 1), sw1.astype(BF16), sw3.astype(BF16), sw2.astype(BF16))


def _expert_kernel(be_ref, nu_ref, xs_ref, w1_ref, w3_ref, w2_ref, o_ref, w13_s, w2_s):
    i = pl.program_id(0)
    hw = xs_ref.shape[1]
    F = w1_ref.shape[2]
    used = i < nu_ref[0]

    @pl.when(used & ((i == 0) | (be_ref[i] != be_ref[jnp.maximum(i - 1, 0)])))
    def _():
        w13_s[:, :F] = w1_ref[0].astype(BF16)
        w13_s[:, F:] = w3_ref[0].astype(BF16)
        w2_s[...] = w2_ref[0].astype(BF16)

    @pl.when(used)
    def _():
        x_lo, x_hi = _unpack_halves(xs_ref[...])
        h13 = (_dot(x_lo.astype(BF16), w13_s[:hw, :]) + _dot(x_hi.astype(BF16), w13_s[hw:, :]))
        act = (_silu(h13[:, :F]) * h13[:, F:]).astype(BF16)
        o_ref[...] = _pack_halves(_dot(act, w2_s[...]))


def _experts(xs, block_e, n_used, w1, w3, w2, interpret):
    P, hw = xs.shape
    E, D, F = w1.shape
    nb = P // EXP_BLK
    blk = lambda i, be, nu: jnp.minimum(i, nu[0] - 1)
    grid_spec = pltpu.PrefetchScalarGridSpec(
        num_scalar_prefetch=2,
        grid=(nb,),
        in_specs=[pl.BlockSpec((EXP_BLK, hw), lambda i, be, nu: (blk(i, be, nu), 0)),
                  pl.BlockSpec((1, D, F), lambda i, be, nu: (be[blk(i, be, nu)], 0, 0)),
                  pl.BlockSpec((1, D, F), lambda i, be, nu: (be[blk(i, be, nu)], 0, 0)),
                  pl.BlockSpec((1, F, D), lambda i, be, nu: (be[blk(i, be, nu)], 0, 0))],
        out_specs=pl.BlockSpec((EXP_BLK, hw), lambda i, be, nu: (blk(i, be, nu), 0)),
        scratch_shapes=[pltpu.VMEM((D, 2 * F), BF16), pltpu.VMEM((F, D), BF16)],
    )
    return pl.pallas_call(
        _expert_kernel,
        grid_spec=grid_spec,
        out_shape=jax.ShapeDtypeStruct((P, hw), I32),
        compiler_params=_cparams(("arbitrary",)),
        name="experts", interpret=interpret,
    )(block_e, n_used, xs, w1, w3, w2)


def _row_copy(src, s_row, dst, d_row, sem):
    return pltpu.make_async_copy(src.at[pl.ds(s_row, 1)], dst.at[pl.ds(d_row, 1)], sem)


def _dispatch_kernel(dest_ref, h2_ref, xs_in_ref, xs_ref, sem):
    del xs_in_ref
    td = h2_ref.shape[0]

    def issue(t, carry):
        for kk in range(TOP_K):
            _row_copy(h2_ref, t, xs_ref, dest_ref[0, 0, kk * td + t], sem).start()
        return carry

    def drain(t, carry):
        for kk in range(TOP_K):
            _row_copy(h2_ref, t, xs_ref, dest_ref[0, 0, kk * td + t], sem).wait()
        return carry

    lax.fori_loop(0, td, issue, 0)
    lax.fori_loop(0, td, drain, 0)


def _dispatch(h2, dest_tiles, n_slots, interpret):
    T, D = h2.shape
    nt, _, n = dest_tiles.shape
    td = n // TOP_K
    xs0 = jnp.zeros((n_slots, D), h2.dtype)
    return pl.pallas_call(
        _dispatch_kernel,
        grid=(nt,),
        in_specs=[pl.BlockSpec((1, 1, n), lambda i: (i, 0, 0), memory_space=pltpu.SMEM),
                  pl.BlockSpec((td, D), lambda i: (i, 0)),
                  pl.BlockSpec(memory_space=pl.ANY)],
        out_specs=pl.BlockSpec(memory_space=pl.ANY),
        out_shape=jax.ShapeDtypeStruct((n_slots, D), h2.dtype),
        scratch_shapes=[pltpu.SemaphoreType.DMA(())],
        input_output_aliases={2: 0},
        compiler_params=_cparams(("arbitrary",)),
        name="dispatch", interpret=interpret,
    )(dest_tiles, h2, xs0)


def _combine_kernel(dest_ref, base_ref, g2_ref, gw_ref, ys_ref, o_ref, buf, sem):
    td = base_ref.shape[0]

    def issue(t, carry):
        for kk in range(TOP_K):
            _row_copy(ys_ref, dest_ref[0, 0, kk * td + t], buf.at[kk], t, sem).start()
        return carry

    def drain(t, carry):
        for kk in range(TOP_K):
            _row_copy(ys_ref, dest_ref[0, 0, kk * td + t], buf.at[kk], t, sem).wait()
        return carry

    lax.fori_loop(0, td, issue, 0)
    lax.fori_loop(0, td, drain, 0)
    gw = gw_ref[...]
    acc_lo, acc_hi = _unpack_halves(buf[0])
    acc_lo, acc_hi = gw[:, 0:1] * acc_lo, gw[:, 0:1] * acc_hi
    for kk in range(1, TOP_K):
        y_lo, y_hi = _unpack_halves(buf[kk])
        acc_lo = acc_lo + gw[:, kk:kk + 1] * y_lo
        acc_hi = acc_hi + gw[:, kk:kk + 1] * y_hi
    o_ref[...] = base_ref[...] + g2_ref[0] * jnp.concatenate([acc_lo, acc_hi], axis=1)


def _combine(base, gate2, gw_tok, ys, dest_tiles, tiles_per_batch, interpret):
    T, D = base.shape
    nt, _, n = dest_tiles.shape
    td = n // TOP_K
    return pl.pallas_call(
        _combine_kernel,
        grid=(nt,),
        in_specs=[pl.BlockSpec((1, 1, n), lambda i: (i, 0, 0), memory_space=pltpu.SMEM),
                  pl.BlockSpec((td, D), lambda i: (i, 0)),
                  pl.BlockSpec((1, 1, D), lambda i: (i // tiles_per_batch, 0, 0)),
                  pl.BlockSpec((td, TOP_K), lambda i: (i, 0)),
                  pl.BlockSpec(memory_space=pl.ANY)],
        out_specs=pl.BlockSpec((td, D), lambda i: (i, 0)),
        out_shape=jax.ShapeDtypeStruct((T, D), F32),
        scratch_shapes=[pltpu.VMEM((TOP_K, td, D // 2), I32), pltpu.SemaphoreType.DMA(())],
        compiler_params=_cparams(("arbitrary",)),
        name="combine", interpret=interpret,
    )(dest_tiles, base, gate2, gw_tok, ys)


def _slots_kernel(eid_ref, rank_ref, pstart_ref, dest_ref):
    td = eid_ref.shape[2]
    E = pstart_ref.shape[0]
    row = lax.broadcasted_iota(I32, (E, td), 0)
    pstart = pstart_ref[...]
    eid = eid_ref[0]
    for kk in range(TOP_K):
        base = jnp.sum(jnp.where(row == eid[kk:kk + 1, :], pstart, 0), axis=0, keepdims=True)
        dest_ref[0, :, kk * td:(kk + 1) * td] = base + rank_ref[0, kk:kk + 1, :]


def _slot_plan(counts, eid_t, rank_t, td, interpret):
    B, _, S = eid_t.shape
    E = counts.shape[0]
    padded = (counts + EXP_BLK - 1) // EXP_BLK * EXP_BLK
    pend = jnp.cumsum(padded)
    pstart = (pend - padded).astype(I32)
    nb = -(-(B * S * TOP_K + E * (EXP_BLK - 1)) // EXP_BLK)
    block_e = jnp.searchsorted(pend, jnp.arange(nb, dtype=I32) * EXP_BLK, side='right')
    block_e = jnp.minimum(block_e, E - 1).astype(I32)
    nt = S // td
    dest_tiles = pl.pallas_call(
        _slots_kernel,
        grid=(B, nt),
        in_specs=[pl.BlockSpec((1, TOP_K, td), lambda b, i: (b, 0, i)),
                  pl.BlockSpec((1, TOP_K, td), lambda b, i: (b, 0, i)),
                  pl.BlockSpec((E, 1), lambda b, i: (0, 0))],
        out_specs=pl.BlockSpec((1, 1, TOP_K * td), lambda b, i: (b * nt + i, 0, 0)),
        out_shape=jax.ShapeDtypeStruct((B * nt, 1, TOP_K * td), I32),
        compiler_params=_cparams(("arbitrary", "arbitrary")),
        name="slots", interpret=interpret,
    )(eid_t, rank_t, pstart.reshape(E, 1))
    n_used = (pend[-1:] // EXP_BLK).astype(I32)
    return block_e, n_used, dest_tiles, nb * EXP_BLK


def _forward(x, c, positions, w_ada, b_ada, norm1_g, norm2_g, w_in, rw_mu, rw_w0, rw_w2,
             rw_a0, rw_a2, rw_g2, rw_k_k, rw_k_a, rw_r_k, rw_ln_w, rw_ln_b, q_norm_g,
             k_norm_g, idx_ln_w, idx_ln_b, at_out_g, w_out, router_w, router_bias,
             exp_w1, exp_w3, exp_w2, shared_w1, shared_w3, shared_w2, interpret=False):
    B, S, D = x.shape
    depth = w_ada.shape[0]
    for l in range(depth):
        mod = _mod(c, w_ada[l], b_ada[l], interpret)
        shift1, scale1, gate1, shift2, scale2, gate2 = [
            m.reshape(B, 1, D) for m in jnp.split(mod, 6, axis=-1)]
        tabs = _rope_tables(positions, interpret)
        p_rw, k, ki, qt, vt, qit, wit = _inproj(
            x, scale1, shift1, norm1_g[l], w_in[l], k_norm_g[l], idx_ln_w[l], idx_ln_b[l],
            q_norm_g[l], tabs, interpret)
        y_rw = _rwkv(p_rw, rw_mu[l], rw_w0[l], rw_w2[l], rw_a0[l], rw_a2[l], rw_g2[l], rw_k_k[l],
                     rw_k_a[l], rw_r_k[l], rw_ln_w[l], rw_ln_b[l], interpret)
        y_at = _dsa(k, vt, ki, qt, qit, wit, at_out_g[l], interpret)
        base, h2, eid_t, gw_t, rank_t, cnt = _post(
            x, y_rw, y_at, gate1, scale2, shift2, gate2, norm2_g[l], w_out[l], router_w[l],
            router_bias[l], shared_w1[l], shared_w3[l], shared_w2[l], interpret)
        T = B * S
        td = min(S, ROW_TILE)
        block_e, n_used, dest_tiles, n_slots = _slot_plan(cnt[:, 0].astype(I32), eid_t, rank_t, td,
                                                            interpret)
        xs = _dispatch(h2.reshape(T, D // 2), dest_tiles, n_slots, interpret)
        ys = _experts(xs, block_e, n_used, exp_w1[l], exp_w3[l], exp_w2[l], interpret)
        gw_tok = gw_t.transpose(0, 2, 1).reshape(T, TOP_K)
        x = _combine(base.reshape(T, D), gate2, gw_tok, ys, dest_tiles, S // td,
                     interpret).reshape(B, S, D)
    return x


def kernel(x, c, positions, w_ada, b_ada, norm1_g, norm2_g, w_in, rw_mu, rw_w0, rw_w2, rw_a0, rw_a2, rw_g2, rw_k_k, rw_k_a, rw_r_k, rw_ln_w, rw_ln_b, q_norm_g, k_norm_g, idx_ln_w, idx_ln_b, at_out_g, w_out, router_w, router_bias, exp_w1, exp_w3, exp_w2, shared_w1, shared_w3, shared_w2):
    return _forward(x, c, positions, w_ada, b_ada, norm1_g, norm2_g, w_in, rw_mu, rw_w0, rw_w2,
                    rw_a0, rw_a2, rw_g2, rw_k_k, rw_k_a, rw_r_k, rw_ln_w, rw_ln_b, q_norm_g,
                    k_norm_g, idx_ln_w, idx_ln_b, at_out_g, w_out, router_w, router_bias,
                    exp_w1, exp_w3, exp_w2, shared_w1, shared_w3, shared_w2)
```

```python
import functools

import jax
import jax.numpy as jnp
import numpy as np
from jax import lax
from jax.experimental import pallas as pl
from jax.experimental.pallas import tpu as pltpu

F32 = jnp.float32
BF16 = jnp.bfloat16
I32 = jnp.int32
HIGHEST = lax.Precision.HIGHEST

LANES = 128
HEAD_DIM = 64
HALF = HEAD_DIM // 2
RW_HEADS = 8
RW_WIDTH = RW_HEADS * HEAD_DIM
AT_HEADS = 8
AT_WIDTH = AT_HEADS * HEAD_DIM
IDX_HEADS = 4
RW_LORA_W, RW_LORA_A, RW_LORA_G = 64, 64, 128
RW_COLS = 3 * RW_WIDTH + RW_LORA_W + RW_LORA_A + RW_LORA_G
KI_OFF = 3 * AT_WIDTH + IDX_HEADS * HEAD_DIM
WI_OFF = KI_OFF + HEAD_DIM
PT_ROWS = 2 * AT_WIDTH + IDX_HEADS * HEAD_DIM + 8
PK_COLS = AT_WIDTH + LANES
ROPE_THETA = 10000.0
NORM_EPS = 1e-6
LN_EPS = 1e-6
RW_GN_EPS = 64e-5
IDX_TOPK_MAX = 256
N_EXPERTS = 256
TOP_K = 8
N_GROUPS = 8
TOPK_GROUPS = 4
ROUTED_SCALE = 2.5
INT_MIN = -2 ** 31
PAIR_HI_MASK = -65536
NEG_BIG = -1e30

RW_CHUNK = 128
QBLK = 128
SCORE_UNROLL = 4
ATTN_UNROLL = 4
COUNT_UNROLL = 4
EXP_BLK = 512
ROW_TILE = 256
VMEM_LIMIT = 48 * 1024 * 1024


def _cparams(sem):
    return pltpu.CompilerParams(dimension_semantics=sem, vmem_limit_bytes=VMEM_LIMIT)


def _sigmoid(x):
    return 1.0 / (1.0 + jnp.exp(-x))


def _silu(x):
    return x * _sigmoid(x)


def _dot(a, b):
    return jnp.dot(a, b, preferred_element_type=F32)


def _dot_split(a, b):
    hi = a.astype(BF16)
    lo = (a - hi.astype(F32)).astype(BF16)
    return _dot(hi, b) + _dot(lo, b)


def _pack_halves(x):
    w = x.shape[1] // 2
    lo = pltpu.bitcast(x[:, :w].astype(BF16).astype(F32), I32)
    hi = pltpu.bitcast(x[:, w:].astype(BF16).astype(F32), I32)
    return (hi & PAIR_HI_MASK) | lax.shift_right_logical(lo, 16)


def _unpack_halves(p):
    return pltpu.bitcast(p << 16, F32), pltpu.bitcast(p & PAIR_HI_MASK, F32)


def _dot_nt(a, b):
    return lax.dot_general(a, b, (((1,), (1,)), ((), ())), preferred_element_type=F32)


def _mod_kernel(c_ref, w_ref, b_ref, o_ref):
    c = c_ref[...]
    o_ref[...] = jnp.dot(_silu(c), w_ref[...], precision=HIGHEST,
                         preferred_element_type=F32) + b_ref[...]


def _mod(c, w_ada, b_ada, interpret):
    B, D = c.shape
    n = w_ada.shape[1] // D
    return pl.pallas_call(
        _mod_kernel,
        grid=(n,),
        in_specs=[pl.BlockSpec((B, D), lambda i: (0, 0)),
                  pl.BlockSpec((D, D), lambda i: (0, i)),
                  pl.BlockSpec((1, D), lambda i: (0, i))],
        out_specs=pl.BlockSpec((B, D), lambda i: (0, i)),
        out_shape=jax.ShapeDtypeStruct((B, n * D), F32),
        compiler_params=_cparams(("arbitrary",)),
        name="mod", interpret=interpret,
    )(c, w_ada, b_ada.reshape(1, -1))


def _rope_tab_kernel(pc_ref, pr_ref, cr_ref, sr_ref, ct_ref, st_ref):
    log_theta = float(np.log(ROPE_THETA))
    lane = lax.broadcasted_iota(I32, (1, LANES), 1)
    inv_r = jnp.exp((lane % HALF).astype(F32) * (-log_theta / HALF))
    ang = pc_ref[0].astype(F32) * inv_r
    cr_ref[0] = jnp.cos(ang)
    sr_ref[0] = jnp.where((lane % HEAD_DIM) < HALF, -jnp.sin(ang), jnp.sin(ang))
    sub = lax.broadcasted_iota(I32, (HALF, 1), 0)
    inv_c = jnp.exp(sub.astype(F32) * (-log_theta / HALF))
    ang_t = inv_c * pr_ref[0].astype(F32)
    ct_ref[0] = jnp.cos(ang_t)
    st_ref[0] = jnp.sin(ang_t)


def _rope_tables(positions, interpret):
    B, S = positions.shape
    ts = min(S, 512)
    return pl.pallas_call(
        _rope_tab_kernel,
        grid=(B, S // ts),
        in_specs=[pl.BlockSpec((1, ts, 1), lambda b, i: (b, i, 0)),
                  pl.BlockSpec((1, 1, ts), lambda b, i: (b, 0, i))],
        out_specs=[pl.BlockSpec((1, ts, LANES), lambda b, i: (b, i, 0)),
                   pl.BlockSpec((1, ts, LANES), lambda b, i: (b, i, 0)),
                   pl.BlockSpec((1, HALF, ts), lambda b, i: (b, 0, i)),
                   pl.BlockSpec((1, HALF, ts), lambda b, i: (b, 0, i))],
        out_shape=[jax.ShapeDtypeStruct((B, S, LANES), F32),
                   jax.ShapeDtypeStruct((B, S, LANES), F32),
                   jax.ShapeDtypeStruct((B, HALF, S), F32),
                   jax.ShapeDtypeStruct((B, HALF, S), F32)],
        compiler_params=_cparams(("arbitrary", "arbitrary")),
        name="rope_tab", interpret=interpret,
    )(positions.reshape(B, S, 1), positions.reshape(B, 1, S))


def _rope_rows(y, cos, sin_signed):
    lane = lax.broadcasted_iota(I32, (1, LANES), 1)
    partner = jnp.where((lane % HEAD_DIM) < HALF,
                        pltpu.roll(y, LANES - HALF, 1), pltpu.roll(y, HALF, 1))
    return y * cos + partner * sin_signed


def _rope_cols(y, cos_t, sin_t):
    x1, x2 = y[:, :HALF], y[:, HALF:]
    return jnp.concatenate([x1 * cos_t - x2 * sin_t, x2 * cos_t + x1 * sin_t], axis=1)


def _inproj_kernel(x_ref, sc_ref, sh_ref, g_ref, wrw_ref, wk_ref, wt_ref, kg_ref, iw_ref, ib_ref,
                   qg_ref, gsum_ref, cr_ref, sr_ref, ct_ref, st_ref,
                   prw_ref, k_ref, ki_ref, qt_ref, vt_ref, qit_ref, wit_ref):
    tm = x_ref.shape[1]
    x = x_ref[0]
    ms = jnp.mean(x * x, axis=-1, keepdims=True)
    h = x * lax.rsqrt(ms + NORM_EPS) * g_ref[...] * (1.0 + sc_ref[0]) + sh_ref[0]
    hb = h.astype(BF16)
    prw_ref[0] = _dot(hb, wrw_ref[...])
    pk = _dot(hb, wk_ref[...])
    pt = _dot_nt(wt_ref[...], hb)

    cos_r, sin_r = cr_ref[0], sr_ref[0]
    gsum = gsum_ref[...]
    inv_hd = 1.0 / HEAD_DIM
    for p in range(AT_WIDTH // LANES):
        xk = pk[:, p * LANES:(p + 1) * LANES]
        ss = jnp.dot(xk * xk, gsum, precision=HIGHEST, preferred_element_type=F32)
        y = xk * lax.rsqrt(ss * inv_hd + NORM_EPS) * kg_ref[...]
        k_ref[0, :, p * LANES:(p + 1) * LANES] = _rope_rows(y, cos_r, sin_r).astype(BF16)
    xi = pk[:, AT_WIDTH:AT_WIDTH + LANES]
    mu = jnp.dot(xi, gsum, precision=HIGHEST, preferred_element_type=F32) * inv_hd
    xc = xi - mu
    var = jnp.dot(xc * xc, gsum, precision=HIGHEST, preferred_element_type=F32) * inv_hd
    yi = xc * lax.rsqrt(var + LN_EPS) * iw_ref[...] + ib_ref[...]
    ki_ref[0] = _rope_rows(yi, cos_r, sin_r).astype(BF16)

    cos_t, sin_t = ct_ref[0][None], st_ref[0][None]
    xq = pt[0:AT_WIDTH].reshape(AT_HEADS, HEAD_DIM, tm)
    msq = jnp.mean(xq * xq, axis=1, keepdims=True)
    yq = xq * lax.rsqrt(msq + NORM_EPS) * qg_ref[...][None]
    yq = _rope_cols(yq, cos_t, sin_t) * (HEAD_DIM ** -0.5)
    zq = jnp.zeros((HEAD_DIM, tm), BF16)
    for hh in range(AT_HEADS):
        parts = [yq[hh].astype(BF16), zq] if hh % 2 == 0 else [zq, yq[hh].astype(BF16)]
        qt_ref[0, hh] = jnp.concatenate(parts, axis=0)
    vt = pt[AT_WIDTH:2 * AT_WIDTH].astype(BF16)
    for cblk in range(tm // QBLK):
        vt_ref[0, cblk] = vt[:, cblk * QBLK:(cblk + 1) * QBLK]
    xqi = pt[2 * AT_WIDTH:2 * AT_WIDTH + IDX_HEADS * HEAD_DIM].reshape(IDX_HEADS, HEAD_DIM, tm)
    yqi = _rope_cols(xqi, cos_t, sin_t)
    for hh in range(IDX_HEADS):
        qit_ref[0, hh] = jnp.concatenate([yqi[hh].astype(BF16), zq], axis=0)
    wit_ref[0] = pt[PT_ROWS - 8:PT_ROWS] * (IDX_HEADS ** -0.5 * HEAD_DIM ** -0.5)


def _inproj(x, scale1, shift1, norm1_g, w_in, k_norm_g, idx_ln_w, idx_ln_b, q_norm_g,
            tabs, interpret):
    B, S, D = x.shape
    tm = min(S, 256)
    cos_r, sin_r, cos_t, sin_t = tabs
    w_at = w_in[:, RW_COLS:]
    w_rw = w_in[:, :RW_COLS].astype(BF16)
    w_k = jnp.concatenate([w_at[:, AT_WIDTH:2 * AT_WIDTH], w_at[:, KI_OFF:KI_OFF + HEAD_DIM],
                           jnp.zeros((D, HEAD_DIM), F32)], axis=1).astype(BF16)
    w_t = jnp.concatenate([w_at[:, 0:AT_WIDTH], w_at[:, 2 * AT_WIDTH:3 * AT_WIDTH],
                           w_at[:, 3 * AT_WIDTH:KI_OFF], w_at[:, WI_OFF:WI_OFF + IDX_HEADS],
                           jnp.zeros((D, 8 - IDX_HEADS), F32)], axis=1).T.astype(BF16)
    kg = jnp.tile(k_norm_g, 2).reshape(1, LANES)
    zpad = jnp.zeros((HEAD_DIM,), F32)
    iw = jnp.concatenate([idx_ln_w, zpad]).reshape(1, LANES)
    ib = jnp.concatenate([idx_ln_b, zpad]).reshape(1, LANES)
    qg = q_norm_g.reshape(HEAD_DIM, 1)
    li = np.arange(LANES)
    gsum = jnp.asarray((li[:, None] // HEAD_DIM == li[None, :] // HEAD_DIM).astype(np.float32))

    full = lambda shape: pl.BlockSpec(shape, lambda b, i: (0,) * len(shape))
    return pl.pallas_call(
        _inproj_kernel,
        grid=(B, S // tm),
        in_specs=[pl.BlockSpec((1, tm, D), lambda b, i: (b, i, 0)),
                  pl.BlockSpec((1, 1, D), lambda b, i: (b, 0, 0)),
                  pl.BlockSpec((1, 1, D), lambda b, i: (b, 0, 0)),
                  full((1, D)), full((D, RW_COLS)), full((D, PK_COLS)), full((PT_ROWS, D)),
                  full((1, LANES)), full((1, LANES)), full((1, LANES)), full((HEAD_DIM, 1)),
                  full((LANES, LANES)),
                  pl.BlockSpec((1, tm, LANES), lambda b, i: (b, i, 0)),
                  pl.BlockSpec((1, tm, LANES), lambda b, i: (b, i, 0)),
                  pl.BlockSpec((1, HALF, tm), lambda b, i: (b, 0, i)),
                  pl.BlockSpec((1, HALF, tm), lambda b, i: (b, 0, i))],
        out_specs=[pl.BlockSpec((1, tm, RW_COLS), lambda b, i: (b, i, 0)),
                   pl.BlockSpec((1, tm, AT_WIDTH), lambda b, i: (b, i, 0)),
                   pl.BlockSpec((1, tm, LANES), lambda b, i: (b, i, 0)),
                   pl.BlockSpec((1, AT_HEADS, LANES, tm), lambda b, i: (b, 0, 0, i)),
                   pl.BlockSpec((1, tm // QBLK, AT_WIDTH, QBLK), lambda b, i: (b, i, 0, 0)),
                   pl.BlockSpec((1, IDX_HEADS, LANES, tm), lambda b, i: (b, 0, 0, i)),
                   pl.BlockSpec((1, 8, tm), lambda b, i: (b, 0, i))],
        out_shape=[jax.ShapeDtypeStruct((B, S, RW_COLS), F32),
                   jax.ShapeDtypeStruct((B, S, AT_WIDTH), BF16),
                   jax.ShapeDtypeStruct((B, S, LANES), BF16),
                   jax.ShapeDtypeStruct((B, AT_HEADS, LANES, S), BF16),
                   jax.ShapeDtypeStruct((B, S // QBLK, AT_WIDTH, QBLK), BF16),
                   jax.ShapeDtypeStruct((B, IDX_HEADS, LANES, S), BF16),
                   jax.ShapeDtypeStruct((B, 8, S), F32)],
        compiler_params=_cparams(("arbitrary", "arbitrary")),
        name="inproj", interpret=interpret,
    )(x, scale1, shift1, norm1_g.reshape(1, D), w_rw, w_k, w_t, kg, iw, ib, qg, gsum,
      cos_r, sin_r, cos_t, sin_t)


def _rwkv_kernel(p_ref, mu_ref, w0_ref, w2_ref, a0_ref, a2_ref, g2_ref, kk_ref, ka_ref, rk_ref,
                 lnw_ref, lnb_ref, gsum_ref, y_ref, s_ref, prev_ref, yt_ref):
    C = RW_CHUNK
    W = RW_WIDTH

    @pl.when(pl.program_id(1) == 0)
    def _():
        s_ref[...] = jnp.zeros_like(s_ref)
        prev_ref[...] = jnp.zeros_like(prev_ref)

    p = p_ref[0]
    row = lax.broadcasted_iota(I32, (C, 1), 0)
    pprev = jnp.where(row == 0, prev_ref[...], pltpu.roll(p, 1, 0))
    prev_ref[...] = p[C - 1:C]
    ps = p + (pprev - p) * mu_ref[...]
    r, k, v = ps[:, 0:W], ps[:, W:2 * W], ps[:, 2 * W:3 * W]
    o = 3 * W
    wl = ps[:, o:o + RW_LORA_W]
    al = ps[:, o + RW_LORA_W:o + RW_LORA_W + RW_LORA_A]
    gl = ps[:, o + RW_LORA_W + RW_LORA_A:]

    z = w0_ref[...] + _dot(jnp.tanh(wl).astype(BF16), w2_ref[...])
    nz = -z
    softplus = jnp.maximum(nz, 0.0) + jnp.log(1.0 + jnp.exp(-jnp.abs(nz)))
    logw = -jnp.exp(-softplus - 0.5)
    a = _sigmoid(a0_ref[...] + _dot(al.astype(BF16), a2_ref[...]))
    g = _dot(_sigmoid(gl).astype(BF16), g2_ref[...])
    gsum = gsum_ref[...]
    kk = k * kk_ref[...]
    ss = _dot((kk * kk).astype(BF16), gsum)
    kk = kk * (1.0 / jnp.maximum(jnp.sqrt(ss), 1e-12))
    k2 = k * (1.0 + (a - 1.0) * ka_ref[...])
    bb = kk * a

    cw = logw
    sh = 1
    while sh < C:
        cw = cw + jnp.where(row >= sh, pltpu.roll(cw, sh, 0), 0.0)
        sh *= 2
    cw_last = cw[C - 1:C]
    e_neg = jnp.exp(-cw)
    e_end = jnp.exp(cw_last - cw)
    rw = r * jnp.exp(cw)
    kkp = kk * jnp.exp(cw - logw)
    bw, kw = bb * e_neg, k2 * e_neg
    bend, kend = bb * e_end, k2 * e_end
    wc = jnp.exp(cw_last)
    vt_all = v.T.astype(BF16)

    ri = lax.broadcasted_iota(I32, (C, C), 0)
    ci = lax.broadcasted_iota(I32, (C, C), 1)
    strict = ri < ci
    incl = ri <= ci
    incl2 = jnp.concatenate([incl, incl], axis=0)
    lane_half = lax.broadcasted_iota(I32, (1, LANES), 1) // HEAD_DIM

    heads = range(RW_HEADS)
    pair = lambda h: slice((h // 2) * LANES, (h // 2 + 1) * LANES)
    own = [lane_half == (h % 2) for h in heads]
    lh = [jnp.concatenate([kkp[:, pair(2 * q)], rw[:, pair(2 * q)]], axis=0).astype(BF16)
          for q in range(RW_HEADS // 2)]
    rh = [jnp.where(own[h], jnp.concatenate([bw[:, pair(h)], kw[:, pair(h)]], axis=0), 0.0).astype(BF16)
          for h in heads]
    aat = [_dot_nt(rh[h], lh[h // 2]) for h in heads]
    s_old = [s_ref[h] for h in heads]
    sl = [_dot_nt(s_old[h].astype(BF16), lh[h // 2]) for h in heads]
    vt = [vt_all[h * HEAD_DIM:(h + 1) * HEAD_DIM] for h in heads]
    akt = [jnp.where(strict, aat[h][C:, :C], 0.0).astype(BF16) for h in heads]
    m = [jnp.where(strict, aat[h][:C, :C], 0.0).astype(BF16) for h in heads]
    xs = [-(sl[h][:, :C] + _dot(vt[h], akt[h])) for h in heads]
    xs = [xs[h] - _dot_split(xs[h], m[h]) for h in heads]
    lvl = 2
    while lvl < C:
        m = [_dot(m[h], m[h]).astype(BF16) for h in heads]
        xs = [xs[h] + _dot_split(xs[h], m[h]) for h in heads]
        lvl *= 2
    zt = [jnp.concatenate([xs[h].astype(BF16), vt[h]], axis=1) for h in heads]
    for h in heads:
        ymat = jnp.where(incl2, aat[h][:, C:], 0.0).astype(BF16)
        yt_ref[h * HEAD_DIM:(h + 1) * HEAD_DIM, :] = sl[h][:, C:] + _dot(zt[h], ymat)
    for h in heads:
        endz = jnp.where(own[h], jnp.concatenate([bend[:, pair(h)], kend[:, pair(h)]], axis=0),
                         0.0).astype(BF16)
        s_ref[h] = s_old[h] * wc[:, pair(h)] + _dot(zt[h], endz)

    yt = yt_ref[...].reshape(RW_HEADS, HEAD_DIM, C)
    mean = jnp.mean(yt, axis=1, keepdims=True)
    yc = yt - mean
    var = jnp.mean(yc * yc, axis=1, keepdims=True)
    lnw = lnw_ref[...].reshape(RW_HEADS, HEAD_DIM, 1)
    lnb = lnb_ref[...].reshape(RW_HEADS, HEAD_DIM, 1)
    yn = yc * lax.rsqrt(var + RW_GN_EPS) * lnw + lnb
    y = yn.reshape(W, C).T
    bonus = _dot((r * k2 * rk_ref[...]).astype(BF16), gsum) * v
    y_ref[0] = ((y + bonus) * g).astype(BF16)


def _rwkv(p_rw, rw_mu, rw_w0, rw_w2, rw_a0, rw_a2, rw_g2, rw_k_k, rw_k_a, rw_r_k, rw_ln_w, rw_ln_b,
          interpret):
    B, S, _ = p_rw.shape
    C, W = RW_CHUNK, RW_WIDTH
    li = np.arange(W)
    gsum = jnp.asarray((li[:, None] // HEAD_DIM == li[None, :] // HEAD_DIM).astype(np.float32)).astype(BF16)
    row = lambda a: a.reshape(1, -1)
    full = lambda shape: pl.BlockSpec(shape, lambda b, i: (0,) * len(shape))
    return pl.pallas_call(
        _rwkv_kernel,
        grid=(B, S // C),
        in_specs=[pl.BlockSpec((1, C, RW_COLS), lambda b, i: (b, i, 0)),
                  full((1, RW_COLS)), full((1, W)), full((RW_LORA_W, W)), full((1, W)),
                  full((RW_LORA_A, W)), full((RW_LORA_G, W)), full((1, W)), full((1, W)),
                  full((1, W)), full((W, 1)), full((W, 1)), full((W, W))],
        out_specs=pl.BlockSpec((1, C, W), lambda b, i: (b, i, 0)),
        out_shape=jax.ShapeDtypeStruct((B, S, W), BF16),
        scratch_shapes=[pltpu.VMEM((RW_HEADS, HEAD_DIM, LANES), F32),
                        pltpu.VMEM((1, RW_COLS), F32),
                        pltpu.VMEM((W, C), F32)],
        compiler_params=_cparams(("arbitrary", "arbitrary")),
        name="rwkv", interpret=interpret,
    )(p_rw, row(rw_mu), row(rw_w0), rw_w2.astype(BF16), row(rw_a0), rw_a2.astype(BF16),
      rw_g2.astype(BF16), row(rw_k_k), row(rw_k_a), row(rw_r_k), rw_ln_w.reshape(W, 1),
      rw_ln_b.reshape(W, 1), gsum)


def _dsa_kernel(topk, nbits, k_ref, vt_ref, ki_ref, qt_ref, qit_ref, wit_ref, og_ref, o_ref,
                key_s, acc_s, m_s, l_s, thr_s):
    j = pl.program_id(1)
    nkb = j + 1
    lane = lax.broadcasted_iota(I32, (QBLK, QBLK), 1)
    sub = lax.broadcasted_iota(I32, (QBLK, QBLK), 0)
    qpos = j * QBLK + lane
    wit = wit_ref[0]

    def score_blocks(i, carry):
        kbs = [i * SCORE_UNROLL + u for u in range(SCORE_UNROLL)]
        kib = [ki_ref[0, pl.ds(pl.multiple_of(kb * QBLK, QBLK), QBLK), :] for kb in kbs]
        lg = [[_dot(kib[u], qit_ref[0, hh]) for hh in range(IDX_HEADS)] for u in range(SCORE_UNROLL)]
        for u, kb in enumerate(kbs):
            s = wit[0:1, :] * jnp.maximum(lg[u][0], 0.0)
            for hh in range(1, IDX_HEADS):
                s = s + wit[hh:hh + 1, :] * jnp.maximum(lg[u][hh], 0.0)
            s = jnp.where(s == 0.0, 0.0, s)
            bits = pltpu.bitcast(s, I32)
            skey = jnp.where(bits < 0, bits ^ 0x7FFFFFFF, bits)
            key_s[kb] = jnp.where(kb * QBLK + sub <= qpos, skey, INT_MIN)
        return carry

    lax.fori_loop(0, pl.cdiv(nkb, SCORE_UNROLL), score_blocks, 0)

    @pl.when(nkb * QBLK <= topk)
    def _():
        thr_s[0:1, :] = jnp.full((1, QBLK), INT_MIN, I32)
        thr_s[1:2, :] = jnp.zeros((1, QBLK), I32)

    @pl.when(nkb * QBLK > topk)
    def _():
        n_done = pl.cdiv(nkb, SCORE_UNROLL) * SCORE_UNROLL
        n_cnt = pl.cdiv(nkb, COUNT_UNROLL)

        def fill(kb, carry):
            key_s[kb] = jnp.full((QBLK, QBLK), INT_MIN, I32)
            return carry

        lax.fori_loop(n_done, n_cnt * COUNT_UNROLL, fill, 0)

        def count(preds):
            def body(i, accs):
                accs = list(accs)
                for u in range(COUNT_UNROLL):
                    kb = i * COUNT_UNROLL + u
                    ky = key_s[kb]
                    for n, pred in enumerate(preds):
                        hit = pred(ky, kb * QBLK + sub).astype(I32)
                        accs[n] = accs[n] + jnp.sum(hit.reshape(QBLK // 8, 8, QBLK), axis=0)
                return tuple(accs)
            accs = lax.fori_loop(0, n_cnt, body, tuple(jnp.zeros((8, QBLK), I32) for _ in preds))
            return [jnp.sum(a, axis=0, keepdims=True) for a in accs]

        c0, = count([lambda ky, ix: ky >= 0])
        t0 = jnp.where(c0 >= topk, 0, INT_MIN).astype(I32)

        def bit_step(i, t):
            cand = t | jnp.left_shift(jnp.int32(1), 30 - i)
            c, = count([lambda ky, ix: ky >= cand])
            return jnp.where(c >= topk, cand, t)

        thr = lax.fori_loop(0, 31, bit_step, t0)
        n_gt, n_eq = count([lambda ky, ix: ky > thr, lambda ky, ix: ky == thr])
        need = topk - n_gt
        thr_s[0:1, :] = thr
        thr_s[1:2, :] = jnp.full((1, QBLK), 2 ** nbits, I32)

        @pl.when(jnp.max(jnp.abs(n_eq - need)) > 0)
        def _():
            def idx_step(i, mm):
                cand = mm | jnp.left_shift(jnp.int32(1), nbits - 1 - i)
                c, = count([lambda ky, ix: (ky == thr) & (ix < cand)])
                return jnp.where(c < need, cand, mm)

            thr_s[1:2, :] = lax.fori_loop(0, nbits, idx_step, jnp.zeros((1, QBLK), I32))

    thr = thr_s[0:1, :]
    mm = thr_s[1:2, :]
    m_s[...] = jnp.full_like(m_s, NEG_BIG)
    l_s[...] = jnp.zeros_like(l_s)
    acc_s[...] = jnp.zeros_like(acc_s)

    qt2 = [jnp.concatenate([qt_ref[0, 2 * q], qt_ref[0, 2 * q + 1]], axis=1)
           for q in range(AT_HEADS // 2)]

    def attn_blocks(i, carry):
        kbs = [i * ATTN_UNROLL + u for u in range(ATTN_UNROLL)]
        sel = []
        for kb in kbs:
            skey = key_s[kb]
            kidx = kb * QBLK + sub
            sel.append((kidx <= qpos) & ((skey > thr) | ((skey == thr) & (kidx <= mm))))
        sel = jnp.concatenate(sel, axis=0)
        kblk = [k_ref[0, pl.ds(pl.multiple_of(kb * QBLK, QBLK), QBLK), :] for kb in kbs]
        vtb = jnp.concatenate([vt_ref[0, kb] for kb in kbs], axis=1)
        s2 = [[_dot(kblk[u][:, q * LANES:(q + 1) * LANES], qt2[q]) for u in range(ATTN_UNROLL)]
              for q in range(AT_HEADS // 2)]
        pexp, alpha = [], []
        for hh in range(AT_HEADS):
            half = slice((hh % 2) * QBLK, (hh % 2 + 1) * QBLK)
            s = jnp.concatenate([s2[hh // 2][u][:, half] for u in range(ATTN_UNROLL)], axis=0)
            s = jnp.where(sel, s, NEG_BIG)
            m_old = m_s[hh:hh + 1, :]
            m_new = jnp.maximum(m_old, jnp.max(s, axis=0, keepdims=True))
            pe = jnp.exp(s - m_new)
            al = jnp.exp(m_old - m_new)
            l_s[hh:hh + 1, :] = al * l_s[hh:hh + 1, :] + jnp.sum(pe, axis=0, keepdims=True)
            m_s[hh:hh + 1, :] = m_new
            pexp.append(pe.astype(BF16))
            alpha.append(al)
        for hh in range(AT_HEADS):
            hs = slice(hh * HEAD_DIM, (hh + 1) * HEAD_DIM)
            acc_s[hs, :] = alpha[hh] * acc_s[hs, :] + _dot(vtb[hs, :], pexp[hh])
        return carry

    lax.fori_loop(0, pl.cdiv(nkb, ATTN_UNROLL), attn_blocks, 0)

    for hh in range(AT_HEADS):
        hs = slice(hh * HEAD_DIM, (hh + 1) * HEAD_DIM)
        oh = acc_s[hs, :] * (1.0 / l_s[hh:hh + 1, :])
        ms = jnp.mean(oh * oh, axis=0, keepdims=True)
        acc_s[hs, :] = oh * lax.rsqrt(ms + NORM_EPS) * og_ref[hs, :]
    o_ref[0] = acc_s[...].T.astype(BF16)


def _dsa(k, vt, ki, qt, qit, wit, at_out_g, interpret):
    B, S, _ = k.shape
    nq = S // QBLK
    topk = min(IDX_TOPK_MAX, S // 4)
    nbits = int(np.log2(S))
    assert 2 ** nbits == S and nq % COUNT_UNROLL == 0 and SCORE_UNROLL == ATTN_UNROLL
    assert COUNT_UNROLL % SCORE_UNROLL == 0
    return pl.pallas_call(
        functools.partial(_dsa_kernel, topk, nbits),
        grid=(B, nq),
        in_specs=[pl.BlockSpec((1, S, AT_WIDTH), lambda b, j: (b, 0, 0)),
                  pl.BlockSpec((1, nq, AT_WIDTH, QBLK), lambda b, j: (b, 0, 0, 0)),
                  pl.BlockSpec((1, S, LANES), lambda b, j: (b, 0, 0)),
                  pl.BlockSpec((1, AT_HEADS, LANES, QBLK), lambda b, j: (b, 0, 0, j)),
                  pl.BlockSpec((1, IDX_HEADS, LANES, QBLK), lambda b, j: (b, 0, 0, j)),
                  pl.BlockSpec((1, 8, QBLK), lambda b, j: (b, 0, j)),
                  pl.BlockSpec((AT_WIDTH, 1), lambda b, j: (0, 0))],
        out_specs=pl.BlockSpec((1, QBLK, AT_WIDTH), lambda b, j: (b, j, 0)),
        out_shape=jax.ShapeDtypeStruct((B, S, AT_WIDTH), BF16),
        scratch_shapes=[pltpu.VMEM((nq, QBLK, QBLK), I32),
                        pltpu.VMEM((AT_WIDTH, QBLK), F32),
                        pltpu.VMEM((AT_HEADS, QBLK), F32),
                        pltpu.VMEM((AT_HEADS, QBLK), F32),
                        pltpu.VMEM((8, QBLK), I32)],
        compiler_params=_cparams(("arbitrary", "arbitrary")),
        name="dsa", interpret=interpret,
    )(k, vt, ki, qt, qit, wit, at_out_g.reshape(AT_WIDTH, 1))


def _first_max(vals, idx, axis, sentinel):
    m = jnp.max(vals, axis=axis, keepdims=True)
    return m, jnp.min(jnp.where(vals == m, idx, sentinel), axis=axis, keepdims=True)


def _route_cols(logits_t, bias_col):
    E, tm = logits_t.shape
    pg = E // N_GROUPS
    scores = _sigmoid(logits_t)
    biased = scores + bias_col
    b3 = biased.reshape(N_GROUPS, pg, tm)
    r3 = lax.broadcasted_iota(I32, (N_GROUPS, pg, tm), 1)
    m1, first = _first_max(b3, r3, 1, pg)
    m2 = jnp.max(jnp.where(r3 == first, -jnp.inf, b3), axis=1, keepdims=True)
    cur = (m1 + m2).reshape(N_GROUPS, tm)
    grow = lax.broadcasted_iota(I32, (N_GROUPS, tm), 0)
    gsel = jnp.zeros((N_GROUPS, tm), F32)
    for _ in range(TOPK_GROUPS):
        _, gi = _first_max(cur, grow, 0, N_GROUPS)
        hit = grow == gi
        gsel = jnp.where(hit, 1.0, gsel)
        cur = jnp.where(hit, -jnp.inf, cur)
    gmask = jnp.broadcast_to(gsel.reshape(N_GROUPS, 1, tm), (N_GROUPS, pg, tm)).reshape(E, tm)
    cur = jnp.where(gmask > 0.0, biased, -jnp.inf)
    row = lax.broadcasted_iota(I32, (E, tm), 0)
    onehot = jnp.zeros((E, tm), F32)
    eids, gws = [], []
    for _ in range(TOP_K):
        _, ei = _first_max(cur, row, 0, E)
        hit = row == ei
        eids.append(ei)
        gws.append(jnp.sum(jnp.where(hit, scores, 0.0), axis=0, keepdims=True))
        onehot = jnp.where(hit, 1.0, onehot)
        cur = jnp.where(hit, -jnp.inf, cur)
    eid = jnp.concatenate(eids, axis=0)
    gw = jnp.concatenate(gws, axis=0)
    gw = gw * (ROUTED_SCALE / jnp.sum(gw, axis=0, keepdims=True))
    return eid, gw, onehot


def _post_kernel(x_ref, yrw_ref, yat_ref, g1_ref, sc_ref, sh_ref, g2_ref, ng_ref, wo_ref, rwt_ref,
                 rb_ref, s1_ref, s3_ref, s2_ref, base_ref, h2_ref, eid_ref, gw_ref, rank_ref, cnt_ref):
    W = RW_WIDTH
    tm = x_ref.shape[1]
    E = rwt_ref.shape[0]

    @pl.when((pl.program_id(0) == 0) & (pl.program_id(1) == 0))
    def _():
        cnt_ref[...] = jnp.zeros_like(cnt_ref)

    mix = _dot(yrw_ref[0], wo_ref[0:W, :]) + _dot(yat_ref[0], wo_ref[W:, :])
    x1 = x_ref[0] + g1_ref[0] * mix
    ms = jnp.mean(x1 * x1, axis=-1, keepdims=True)
    h2 = x1 * lax.rsqrt(ms + NORM_EPS) * ng_ref[...] * (1.0 + sc_ref[0]) + sh_ref[0]
    hb = h2.astype(BF16)
    h2_ref[0] = _pack_halves(h2)
    act =(_silu(_dot(hb, s1_ref[...])) * _dot(hb, s3_ref[...])).astype(BF16)
    base_ref[0] = x1 + g2_ref[0] * _dot(act, s2_ref[...])

    logits_t = lax.dot_general(rwt_ref[...], h2, (((1,), (1,)), ((), ())), precision=HIGHEST,
                               preferred_element_type=F32)
    eid, gw, onehot = _route_cols(logits_t, rb_ref[...])
    eid_ref[0] = eid
    gw_ref[0] = gw
    ti = lax.broadcasted_iota(I32, (tm, tm), 0)
    tj = lax.broadcasted_iota(I32, (tm, tm), 1)
    before = _dot(onehot.astype(BF16), (ti < tj).astype(BF16)) + cnt_ref[:, 0:1]
    row = lax.broadcasted_iota(I32, (E, tm), 0)
    ranks = [jnp.sum(jnp.where(row == eid[kk:kk + 1, :], before, 0.0), axis=0, keepdims=True)
             for kk in range(TOP_K)]
    rank_ref[0] = jnp.concatenate(ranks, axis=0).astype(I32)
    cnt_ref[...] = cnt_ref[...] + jnp.sum(onehot, axis=1, keepdims=True)


def _post(x, y_rw, y_at, gate1, scale2, shift2, gate2, norm2_g, w_out, router_w, router_bias,
          sw1, sw3, sw2, interpret):
    B, S, D = x.shape
    tm = min(S, 256)
    E = router_w.shape[1]
    sd = sw1.shape[1]
    full = lambda shape: pl.BlockSpec(shape, lambda b, i: (0,) * len(shape))
    tok = lambda w: pl.BlockSpec((1, tm, w), lambda b, i: (b, i, 0))
    per_b = pl.BlockSpec((1, 1, D), lambda b, i: (b, 0, 0))
    col8 = pl.BlockSpec((1, TOP_K, tm), lambda b, i: (b, 0, i))
    return pl.pallas_call(
        _post_kernel,
        grid=(B, S // tm),
        in_specs=[tok(D), tok(RW_WIDTH), tok(AT_WIDTH), per_b, per_b, per_b, per_b, full((1, D)),
                  full((D, D)), full((E, D)), full((E, 1)), full((D, sd)), full((D, sd)),
                  full((sd, D))],
        out_specs=[tok(D), tok(D // 2), col8, col8, col8, full((E, LANES))],
        out_shape=[jax.ShapeDtypeStruct((B, S, D), F32),
                   jax.ShapeDtypeStruct((B, S, D // 2), I32),
                   jax.ShapeDtypeStruct((B, TOP_K, S), I32),
                   jax.ShapeDtypeStruct((B, TOP_K, S), F32),
                   jax.ShapeDtypeStruct((B, TOP_K, S), I32),
                   jax.ShapeDtypeStruct((E, LANES), F32)],
        compiler_params=_cparams(("arbitrary", "arbitrary")),
        name="post", interpret=interpret,
    )(x, y_rw, y_at, gate1, scale2, shift2, gate2, norm2_g.reshape(1, D), w_out.astype(BF16),
      router_w.T, router_bias.reshape(E, 1), sw1.astype(BF16), sw3.astype(BF16), sw2.astype(BF16))


def _expert_kernel(be_ref, nv_ref, nu_ref, xs_ref, w1_ref, w3_ref, w2_ref, o_ref, w13_s, w2_s):
    i = pl.program_id(0)
    blk, hw = xs_ref.shape
    F = w1_ref.shape[2]
    used = i < nu_ref[0]

    @pl.when(used & ((i == 0) | (be_ref[i] != be_ref[jnp.maximum(i - 1, 0)])))
    def _():
        w13_s[:, :F] = w1_ref[0].astype(BF16)
        w13_s[:, F:] = w3_ref[0].astype(BF16)
        w2_s[...] = w2_ref[0].astype(BF16)

    @pl.when(used)
    def _():
        live = lax.broadcasted_iota(I32, (blk, 1), 0) < nv_ref[i]
        x_lo, x_hi = _unpack_halves(jnp.where(live, xs_ref[...], 0))
        h13 =(_dot(x_lo.astype(BF16), w13_s[:hw, :]) + _dot(x_hi.astype(BF16), w13_s[hw:, :]))
        act = (_silu(h13[:, :F]) * h13[:, F:]).astype(BF16)
        o_ref[...] = _pack_halves(_dot(act, w2_s[...]))


def _experts(xs, block_e, block_rows, n_used, w1, w3, w2, interpret):
    P, hw = xs.shape
    E, D, F = w1.shape
    nb = P // EXP_BLK
    blk = lambda i, nu: jnp.minimum(i, nu[0] - 1)
    grid_spec = pltpu.PrefetchScalarGridSpec(
        num_scalar_prefetch=3,
        grid=(nb,),
        in_specs=[pl.BlockSpec((EXP_BLK, hw), lambda i, be, nv, nu: (blk(i, nu), 0)),
                  pl.BlockSpec((1, D, F), lambda i, be, nv, nu: (be[blk(i, nu)], 0, 0)),
                  pl.BlockSpec((1, D, F), lambda i, be, nv, nu: (be[blk(i, nu)], 0, 0)),
                  pl.BlockSpec((1, F, D), lambda i, be, nv, nu: (be[blk(i, nu)], 0, 0))],
        out_specs=pl.BlockSpec((EXP_BLK, hw), lambda i, be, nv, nu: (blk(i, nu), 0)),
        scratch_shapes=[pltpu.VMEM((D, 2 * F), BF16), pltpu.VMEM((F, D), BF16)],
    )
    return pl.pallas_call(
        _expert_kernel,
        grid_spec=grid_spec,
        out_shape=jax.ShapeDtypeStruct((P, hw), I32),
        compiler_params=_cparams(("arbitrary",)),
        name="experts", interpret=interpret,
    )(block_e, block_rows, n_used, xs, w1, w3, w2)


def _row_copy(src, s_row, dst, d_row, sem):
    return pltpu.make_async_copy(src.at[pl.ds(s_row, 1)], dst.at[pl.ds(d_row, 1)], sem)


def _dispatch_kernel(dest_ref, h2_ref, xs_ref, sem):
    td = h2_ref.shape[0]

    def issue(t, carry):
        for kk in range(TOP_K):
            _row_copy(h2_ref, t, xs_ref, dest_ref[0, 0, kk * td + t], sem).start()
        return carry

    def drain(t, carry):
        for kk in range(TOP_K):
            _row_copy(h2_ref, t, xs_ref, dest_ref[0, 0, kk * td + t], sem).wait()
        return carry

    lax.fori_loop(0, td, issue, 0)
    lax.fori_loop(0, td, drain, 0)


def _dispatch(h2, dest_tiles, n_slots, interpret):
    T, D = h2.shape
    nt, _, n = dest_tiles.shape
    td = n // TOP_K
    return pl.pallas_call(
        _dispatch_kernel,
        grid=(nt,),
        in_specs=[pl.BlockSpec((1, 1, n), lambda i: (i, 0, 0), memory_space=pltpu.SMEM),
                  pl.BlockSpec((td, D), lambda i: (i, 0))],
        out_specs=pl.BlockSpec(memory_space=pl.ANY),
        out_shape=jax.ShapeDtypeStruct((n_slots, D), h2.dtype),
        scratch_shapes=[pltpu.SemaphoreType.DMA(())],
        compiler_params=_cparams(("arbitrary",)),
        name="dispatch", interpret=interpret,
    )(dest_tiles, h2)


def _combine_kernel(dest_ref, base_ref, g2_ref, gw_ref, ys_ref, o_ref, buf, sem):
    td = base_ref.shape[0]

    def issue(t, carry):
        for kk in range(TOP_K):
            _row_copy(ys_ref, dest_ref[0, 0, kk * td + t], buf.at[kk], t, sem).start()
        return carry

    def drain(t, carry):
        for kk in range(TOP_K):
            _row_copy(ys_ref, dest_ref[0, 0, kk * td + t], buf.at[kk], t, sem).wait()
        return carry

    lax.fori_loop(0, td, issue, 0)
    lax.fori_loop(0, td, drain, 0)
    gw = gw_ref[...]
    acc_lo, acc_hi = _unpack_halves(buf[0])
    acc_lo, acc_hi = gw[:, 0:1] * acc_lo, gw[:, 0:1] * acc_hi
    for kk in range(1, TOP_K):
        y_lo, y_hi = _unpack_halves(buf[kk])
        acc_lo = acc_lo + gw[:, kk:kk + 1] * y_lo
        acc_hi = acc_hi + gw[:, kk:kk + 1] * y_hi
    o_ref[...] = base_ref[...] + g2_ref[0] * jnp.concatenate([acc_lo, acc_hi], axis=1)


def _combine(base, gate2, gw_tok, ys, dest_tiles, tiles_per_batch, interpret):
    T, D = base.shape
    nt, _, n = dest_tiles.shape
    td = n // TOP_K
    return pl.pallas_call(
        _combine_kernel,
        grid=(nt,),
        in_specs=[pl.BlockSpec((1, 1, n), lambda i: (i, 0, 0), memory_space=pltpu.SMEM),
                  pl.BlockSpec((td, D), lambda i: (i, 0)),
                  pl.BlockSpec((1, 1, D), lambda i: (i // tiles_per_batch, 0, 0)),
                  pl.BlockSpec((td, TOP_K), lambda i: (i, 0)),
                  pl.BlockSpec(memory_space=pl.ANY)],
        out_specs=pl.BlockSpec((td, D), lambda i: (i, 0)),
        out_shape=jax.ShapeDtypeStruct((T, D), F32),
        scratch_shapes=[pltpu.VMEM((TOP_K, td, D // 2), I32), pltpu.SemaphoreType.DMA(())],
        compiler_params=_cparams(("arbitrary",)),
        name="combine", interpret=interpret,
    )(dest_tiles, base, gate2, gw_tok, ys)


def _slots_kernel(eid_ref, rank_ref, pstart_ref, dest_ref):
    td = eid_ref.shape[2]
    E = pstart_ref.shape[0]
    row = lax.broadcasted_iota(I32, (E, td), 0)
    pstart = pstart_ref[...]
    eid = eid_ref[0]
    for kk in range(TOP_K):
        base = jnp.sum(jnp.where(row == eid[kk:kk + 1, :], pstart, 0), axis=0, keepdims=True)
        dest_ref[0, :, kk * td:(kk + 1) * td] = base + rank_ref[0, kk:kk + 1, :]


def _slot_plan(counts, eid_t, rank_t, td, interpret):
    B, _, S = eid_t.shape
    E = counts.shape[0]
    padded = (counts + EXP_BLK - 1) // EXP_BLK * EXP_BLK
    pend = jnp.cumsum(padded)
    pstart = (pend - padded).astype(I32)
    nb = -(-(B * S * TOP_K + E * (EXP_BLK - 1)) // EXP_BLK)
    first_row = jnp.arange(nb, dtype=I32) * EXP_BLK
    block_e = jnp.sum(pend[None, :] <= first_row[:, None], axis=1)
    block_e = jnp.minimum(block_e, E - 1).astype(I32)
    block_rows = jnp.clip(pstart[block_e] + counts[block_e] - first_row, 0, EXP_BLK).astype(I32)
    nt = S // td
    dest_tiles = pl.pallas_call(
        _slots_kernel,
        grid=(B, nt),
        in_specs=[pl.BlockSpec((1, TOP_K, td), lambda b, i: (b, 0, i)),
                  pl.BlockSpec((1, TOP_K, td), lambda b, i: (b, 0, i)),
                  pl.BlockSpec((E, 1), lambda b, i: (0, 0))],
        out_specs=pl.BlockSpec((1, 1, TOP_K * td), lambda b, i: (b * nt + i, 0, 0)),
        out_shape=jax.ShapeDtypeStruct((B * nt, 1, TOP_K * td), I32),
        compiler_params=_cparams(("arbitrary", "arbitrary")),
        name="slots", interpret=interpret,
    )(eid_t, rank_t, pstart.reshape(E, 1))
    n_used = (pend[-1:] // EXP_BLK).astype(I32)
    return block_e, block_rows, n_used, dest_tiles, nb * EXP_BLK


def _forward(x, c, positions, w_ada, b_ada, norm1_g, norm2_g, w_in, rw_mu, rw_w0, rw_w2,
             rw_a0, rw_a2, rw_g2, rw_k_k, rw_k_a, rw_r_k, rw_ln_w, rw_ln_b, q_norm_g,
             k_norm_g, idx_ln_w, idx_ln_b, at_out_g, w_out, router_w, router_bias,
             exp_w1, exp_w3, exp_w2, shared_w1, shared_w3, shared_w2, interpret=False):
    B, S, D = x.shape
    depth = w_ada.shape[0]
    for l in range(depth):
        mod = _mod(c, w_ada[l], b_ada[l], interpret)
        shift1, scale1, gate1, shift2, scale2, gate2 = [
            m.reshape(B, 1, D) for m in jnp.split(mod, 6, axis=-1)]
        tabs = _rope_tables(positions, interpret)
        p_rw, k, ki, qt, vt, qit, wit = _inproj(
            x, scale1, shift1, norm1_g[l], w_in[l], k_norm_g[l], idx_ln_w[l], idx_ln_b[l],
            q_norm_g[l], tabs, interpret)
        y_rw = _rwkv(p_rw, rw_mu[l], rw_w0[l], rw_w2[l], rw_a0[l], rw_a2[l], rw_g2[l], rw_k_k[l],
                     rw_k_a[l], rw_r_k[l], rw_ln_w[l], rw_ln_b[l], interpret)
        y_at = _dsa(k, vt, ki, qt, qit, wit, at_out_g[l], interpret)
        base, h2, eid_t, gw_t, rank_t, cnt = _post(
            x, y_rw, y_at, gate1, scale2, shift2, gate2, norm2_g[l], w_out[l], router_w[l],
            router_bias[l], shared_w1[l], shared_w3[l], shared_w2[l], interpret)
        T = B * S
        td = min(S, ROW_TILE)
        block_e, block_rows, n_used, dest_tiles, n_slots = _slot_plan(
            cnt[:, 0].astype(I32), eid_t, rank_t, td, interpret)
        xs = _dispatch(h2.reshape(T, D // 2), dest_tiles, n_slots, interpret)
        ys = _experts(xs, block_e, block_rows, n_used, exp_w1[l], exp_w3[l], exp_w2[l], interpret)
        gw_tok = gw_t.transpose(0, 2, 1).reshape(T, TOP_K)
        x = _combine(base.reshape(T, D), gate2, gw_tok, ys, dest_tiles, S // td,
                     interpret).reshape(B, S, D)
    return x


def kernel(x, c, positions, w_ada, b_ada, norm1_g, norm2_g, w_in, rw_mu, rw_w0, rw_w2, rw_a0, rw_a2, rw_g2, rw_k_k, rw_k_a, rw_r_k, rw_ln_w, rw_ln_b, q_norm_g, k_norm_g, idx_ln_w, idx_ln_b, at_out_g, w_out, router_w, router_bias, exp_w1, exp_w3, exp_w2, shared_w1, shared_w3, shared_w2):
    return _forward(x, c, positions, w_ada, b_ada, norm1_g, norm2_g, w_in, rw_mu, rw_w0, rw_w2,
                    rw_a0, rw_a2, rw_g2, rw_k_k, rw_k_a, rw_r_k, rw_ln_w, rw_ln_b, q_norm_g,
                    k_norm_g, idx_ln_w, idx_ln_b, at_out_g, w_out, router_w, router_bias,
                    exp_w1, exp_w3, exp_w2, shared_w1, shared_w3, shared_w2)
```

```python
import functools

import jax
import jax.numpy as jnp
import numpy as np
from jax import lax
from jax.experimental import pallas as pl
from jax.experimental.pallas import tpu as pltpu

F32 = jnp.float32
BF16 = jnp.bfloat16
I32 = jnp.int32
HIGHEST = lax.Precision.HIGHEST

LANES = 128
HEAD_DIM = 64
HALF = HEAD_DIM // 2
RW_HEADS = 8
RW_WIDTH = RW_HEADS * HEAD_DIM
AT_HEADS = 8
AT_WIDTH = AT_HEADS * HEAD_DIM
IDX_HEADS = 4
RW_LORA_W, RW_LORA_A, RW_LORA_G = 64, 64, 128
RW_COLS = 3 * RW_WIDTH + RW_LORA_W + RW_LORA_A + RW_LORA_G
KI_OFF = 3 * AT_WIDTH + IDX_HEADS * HEAD_DIM
WI_OFF = KI_OFF + HEAD_DIM
PT_ROWS = 2 * AT_WIDTH + IDX_HEADS * HEAD_DIM + 8
PK_COLS = AT_WIDTH + LANES
ROPE_THETA = 10000.0
NORM_EPS = 1e-6
LN_EPS = 1e-6
RW_GN_EPS = 64e-5
IDX_TOPK_MAX = 256
N_EXPERTS = 256
TOP_K = 8
N_GROUPS = 8
TOPK_GROUPS = 4
ROUTED_SCALE = 2.5
INT_MIN = -2 ** 31
PAIR_HI_MASK = -65536
NEG_BIG = -1e30

RW_CHUNK = 128
RW_BATCH = 2
QBLK = 128
SCORE_UNROLL = 4
ATTN_UNROLL = 4
COUNT_UNROLL = 4
EXP_BLK = 512
ROW_TILE = 256
VMEM_LIMIT = 48 * 1024 * 1024


def _cparams(sem):
    return pltpu.CompilerParams(dimension_semantics=sem, vmem_limit_bytes=VMEM_LIMIT)


def _sigmoid(x):
    return 1.0 / (1.0 + jnp.exp(-x))


def _silu(x):
    return x * _sigmoid(x)


def _dot(a, b):
    return jnp.dot(a, b, preferred_element_type=F32)


def _dot_split(a, b):
    hi = a.astype(BF16)
    lo = (a - hi.astype(F32)).astype(BF16)
    return _dot(hi, b) + _dot(lo, b)


def _pack_halves(x):
    w = x.shape[1] // 2
    lo = pltpu.bitcast(x[:, :w].astype(BF16).astype(F32), I32)
    hi = pltpu.bitcast(x[:, w:].astype(BF16).astype(F32), I32)
    return (hi & PAIR_HI_MASK) | lax.shift_right_logical(lo, 16)


def _unpack_halves(p):
    return pltpu.bitcast(p << 16, F32), pltpu.bitcast(p & PAIR_HI_MASK, F32)


def _dot_nt(a, b):
    return lax.dot_general(a, b, (((1,), (1,)), ((), ())), preferred_element_type=F32)


def _mod_kernel(c_ref, w_ref, b_ref, o_ref):
    c = c_ref[...]
    o_ref[...] = jnp.dot(_silu(c), w_ref[...], precision=HIGHEST,
                         preferred_element_type=F32) + b_ref[...]


def _mod(c, w_ada, b_ada, interpret):
    B, D = c.shape
    n = w_ada.shape[1] // D
    return pl.pallas_call(
        _mod_kernel,
        grid=(n,),
        in_specs=[pl.BlockSpec((B, D), lambda i: (0, 0)),
                  pl.BlockSpec((D, D), lambda i: (0, i)),
                  pl.BlockSpec((1, D), lambda i: (0, i))],
        out_specs=pl.BlockSpec((B, D), lambda i: (0, i)),
        out_shape=jax.ShapeDtypeStruct((B, n * D), F32),
        compiler_params=_cparams(("arbitrary",)),
        name="mod", interpret=interpret,
    )(c, w_ada, b_ada.reshape(1, -1))


def _rope_tab_kernel(pc_ref, pr_ref, cr_ref, sr_ref, ct_ref, st_ref):
    log_theta = float(np.log(ROPE_THETA))
    lane = lax.broadcasted_iota(I32, (1, LANES), 1)
    inv_r = jnp.exp((lane % HALF).astype(F32) * (-log_theta / HALF))
    ang = pc_ref[0].astype(F32) * inv_r
    cr_ref[0] = jnp.cos(ang)
    sr_ref[0] = jnp.where((lane % HEAD_DIM) < HALF, -jnp.sin(ang), jnp.sin(ang))
    sub = lax.broadcasted_iota(I32, (HALF, 1), 0)
    inv_c = jnp.exp(sub.astype(F32) * (-log_theta / HALF))
    ang_t = inv_c * pr_ref[0].astype(F32)
    ct_ref[0] = jnp.cos(ang_t)
    st_ref[0] = jnp.sin(ang_t)


def _rope_tables(positions, interpret):
    B, S = positions.shape
    ts = min(S, 512)
    return pl.pallas_call(
        _rope_tab_kernel,
        grid=(B, S // ts),
        in_specs=[pl.BlockSpec((1, ts, 1), lambda b, i: (b, i, 0)),
                  pl.BlockSpec((1, 1, ts), lambda b, i: (b, 0, i))],
        out_specs=[pl.BlockSpec((1, ts, LANES), lambda b, i: (b, i, 0)),
                   pl.BlockSpec((1, ts, LANES), lambda b, i: (b, i, 0)),
                   pl.BlockSpec((1, HALF, ts), lambda b, i: (b, 0, i)),
                   pl.BlockSpec((1, HALF, ts), lambda b, i: (b, 0, i))],
        out_shape=[jax.ShapeDtypeStruct((B, S, LANES), F32),
                   jax.ShapeDtypeStruct((B, S, LANES), F32),
                   jax.ShapeDtypeStruct((B, HALF, S), F32),
                   jax.ShapeDtypeStruct((B, HALF, S), F32)],
        compiler_params=_cparams(("arbitrary", "arbitrary")),
        name="rope_tab", interpret=interpret,
    )(positions.reshape(B, S, 1), positions.reshape(B, 1, S))


def _rope_rows(y, cos, sin_signed):
    lane = lax.broadcasted_iota(I32, (1, LANES), 1)
    partner = jnp.where((lane % HEAD_DIM) < HALF,
                        pltpu.roll(y, LANES - HALF, 1), pltpu.roll(y, HALF, 1))
    return y * cos + partner * sin_signed


def _rope_cols(y, cos_t, sin_t):
    x1, x2 = y[:, :HALF], y[:, HALF:]
    return jnp.concatenate([x1 * cos_t - x2 * sin_t, x2 * cos_t + x1 * sin_t], axis=1)


def _inproj_kernel(x_ref, sc_ref, sh_ref, g_ref, wrw_ref, wk_ref, wt_ref, kg_ref, iw_ref, ib_ref,
                   qg_ref, gsum_ref, cr_ref, sr_ref, ct_ref, st_ref,
                   prw_ref, k_ref, ki_ref, qt_ref, vt_ref, qit_ref, wit_ref):
    tm = x_ref.shape[1]
    x = x_ref[0]
    ms = jnp.mean(x * x, axis=-1, keepdims=True)
    h = x * lax.rsqrt(ms + NORM_EPS) * g_ref[...] * (1.0 + sc_ref[0]) + sh_ref[0]
    hb = h.astype(BF16)
    prw_ref[0] = _dot(hb, wrw_ref[...])
    pk = _dot(hb, wk_ref[...])
    pt = _dot_nt(wt_ref[...], hb)

    cos_r, sin_r = cr_ref[0], sr_ref[0]
    gsum = gsum_ref[...]
    inv_hd = 1.0 / HEAD_DIM
    for p in range(AT_WIDTH // LANES):
        xk = pk[:, p * LANES:(p + 1) * LANES]
        ss = jnp.dot(xk * xk, gsum, precision=HIGHEST, preferred_element_type=F32)
        y = xk * lax.rsqrt(ss * inv_hd + NORM_EPS) * kg_ref[...]
        k_ref[0, :, p * LANES:(p + 1) * LANES] = _rope_rows(y, cos_r, sin_r).astype(BF16)
    xi = pk[:, AT_WIDTH:AT_WIDTH + LANES]
    mu = jnp.dot(xi, gsum, precision=HIGHEST, preferred_element_type=F32) * inv_hd
    xc = xi - mu
    var = jnp.dot(xc * xc, gsum, precision=HIGHEST, preferred_element_type=F32) * inv_hd
    yi = xc * lax.rsqrt(var + LN_EPS) * iw_ref[...] + ib_ref[...]
    ki_ref[0] = _rope_rows(yi, cos_r, sin_r).astype(BF16)

    cos_t, sin_t = ct_ref[0][None], st_ref[0][None]
    xq = pt[0:AT_WIDTH].reshape(AT_HEADS, HEAD_DIM, tm)
    msq = jnp.mean(xq * xq, axis=1, keepdims=True)
    yq = xq * lax.rsqrt(msq + NORM_EPS) * qg_ref[...][None]
    yq = _rope_cols(yq, cos_t, sin_t) * (HEAD_DIM ** -0.5)
    zq = jnp.zeros((HEAD_DIM, tm), BF16)
    for hh in range(AT_HEADS):
        parts = [yq[hh].astype(BF16), zq] if hh % 2 == 0 else [zq, yq[hh].astype(BF16)]
        qt_ref[0, hh] = jnp.concatenate(parts, axis=0)
    vt = pt[AT_WIDTH:2 * AT_WIDTH].astype(BF16)
    for cblk in range(tm // QBLK):
        vt_ref[0, cblk] = vt[:, cblk * QBLK:(cblk + 1) * QBLK]
    xqi = pt[2 * AT_WIDTH:2 * AT_WIDTH + IDX_HEADS * HEAD_DIM].reshape(IDX_HEADS, HEAD_DIM, tm)
    yqi = _rope_cols(xqi, cos_t, sin_t)
    for hh in range(IDX_HEADS):
        qit_ref[0, hh] = jnp.concatenate([yqi[hh].astype(BF16), zq], axis=0)
    wit_ref[0] = pt[PT_ROWS - 8:PT_ROWS] * (IDX_HEADS ** -0.5 * HEAD_DIM ** -0.5)


def _inproj(x, scale1, shift1, norm1_g, w_in, k_norm_g, idx_ln_w, idx_ln_b, q_norm_g,
            tabs, interpret):
    B, S, D = x.shape
    tm = min(S, 256)
    cos_r, sin_r, cos_t, sin_t = tabs
    w_at = w_in[:, RW_COLS:]
    w_rw = w_in[:, :RW_COLS].astype(BF16)
    w_k = jnp.concatenate([w_at[:, AT_WIDTH:2 * AT_WIDTH], w_at[:, KI_OFF:KI_OFF + HEAD_DIM],
                           jnp.zeros((D, HEAD_DIM), F32)], axis=1).astype(BF16)
    w_t = jnp.concatenate([w_at[:, 0:AT_WIDTH], w_at[:, 2 * AT_WIDTH:3 * AT_WIDTH],
                           w_at[:, 3 * AT_WIDTH:KI_OFF], w_at[:, WI_OFF:WI_OFF + IDX_HEADS],
                           jnp.zeros((D, 8 - IDX_HEADS), F32)], axis=1).T.astype(BF16)
    kg = jnp.tile(k_norm_g, 2).reshape(1, LANES)
    zpad = jnp.zeros((HEAD_DIM,), F32)
    iw = jnp.concatenate([idx_ln_w, zpad]).reshape(1, LANES)
    ib = jnp.concatenate([idx_ln_b, zpad]).reshape(1, LANES)
    qg = q_norm_g.reshape(HEAD_DIM, 1)
    li = np.arange(LANES)
    gsum = jnp.asarray((li[:, None] // HEAD_DIM == li[None, :] // HEAD_DIM).astype(np.float32))

    full = lambda shape: pl.BlockSpec(shape, lambda b, i: (0,) * len(shape))
    return pl.pallas_call(
        _inproj_kernel,
        grid=(B, S // tm),
        in_specs=[pl.BlockSpec((1, tm, D), lambda b, i: (b, i, 0)),
                  pl.BlockSpec((1, 1, D), lambda b, i: (b, 0, 0)),
                  pl.BlockSpec((1, 1, D), lambda b, i: (b, 0, 0)),
                  full((1, D)), full((D, RW_COLS)), full((D, PK_COLS)), full((PT_ROWS, D)),
                  full((1, LANES)), full((1, LANES)), full((1, LANES)), full((HEAD_DIM, 1)),
                  full((LANES, LANES)),
                  pl.BlockSpec((1, tm, LANES), lambda b, i: (b, i, 0)),
                  pl.BlockSpec((1, tm, LANES), lambda b, i: (b, i, 0)),
                  pl.BlockSpec((1, HALF, tm), lambda b, i: (b, 0, i)),
                  pl.BlockSpec((1, HALF, tm), lambda b, i: (b, 0, i))],
        out_specs=[pl.BlockSpec((1, tm, RW_COLS), lambda b, i: (b, i, 0)),
                   pl.BlockSpec((1, tm, AT_WIDTH), lambda b, i: (b, i, 0)),
                   pl.BlockSpec((1, tm, LANES), lambda b, i: (b, i, 0)),
                   pl.BlockSpec((1, AT_HEADS, LANES, tm), lambda b, i: (b, 0, 0, i)),
                   pl.BlockSpec((1, tm // QBLK, AT_WIDTH, QBLK), lambda b, i: (b, i, 0, 0)),
                   pl.BlockSpec((1, IDX_HEADS, LANES, tm), lambda b, i: (b, 0, 0, i)),
                   pl.BlockSpec((1, 8, tm), lambda b, i: (b, 0, i))],
        out_shape=[jax.ShapeDtypeStruct((B, S, RW_COLS), F32),
                   jax.ShapeDtypeStruct((B, S, AT_WIDTH), BF16),
                   jax.ShapeDtypeStruct((B, S, LANES), BF16),
                   jax.ShapeDtypeStruct((B, AT_HEADS, LANES, S), BF16),
                   jax.ShapeDtypeStruct((B, S // QBLK, AT_WIDTH, QBLK), BF16),
                   jax.ShapeDtypeStruct((B, IDX_HEADS, LANES, S), BF16),
                   jax.ShapeDtypeStruct((B, 8, S), F32)],
        compiler_params=_cparams(("arbitrary", "arbitrary")),
        name="inproj", interpret=interpret,
    )(x, scale1, shift1, norm1_g.reshape(1, D), w_rw, w_k, w_t, kg, iw, ib, qg, gsum,
      cos_r, sin_r, cos_t, sin_t)


def _rwkv_kernel(p_ref, mu_ref, w0_ref, w2_ref, a0_ref, a2_ref, g2_ref, kk_ref, ka_ref, rk_ref,
                 lnw_ref, lnb_ref, gsum_ref, y_ref, s_ref, prev_ref, yt_ref):
    C = RW_CHUNK
    W = RW_WIDTH
    nb = p_ref.shape[0]

    @pl.when(pl.program_id(1) == 0)
    def _():
        s_ref[...] = jnp.zeros_like(s_ref)
        prev_ref[...] = jnp.zeros_like(prev_ref)

    row = lax.broadcasted_iota(I32, (C, 1), 0)
    gsum = gsum_ref[...]

    def prepare(bi):
        p = p_ref[bi]
        pprev = jnp.where(row == 0, prev_ref[bi:bi + 1, :], pltpu.roll(p, 1, 0))
        prev_ref[bi:bi + 1, :] = p[C - 1:C]
        ps = p + (pprev - p) * mu_ref[...]
        r, k, v = ps[:, 0:W], ps[:, W:2 * W], ps[:, 2 * W:3 * W]
        o = 3 * W
        wl = ps[:, o:o + RW_LORA_W]
        al = ps[:, o + RW_LORA_W:o + RW_LORA_W + RW_LORA_A]
        gl = ps[:, o + RW_LORA_W + RW_LORA_A:]
        z = w0_ref[...] + _dot(jnp.tanh(wl).astype(BF16), w2_ref[...])
        nz = -z
        softplus = jnp.maximum(nz, 0.0) + jnp.log(1.0 + jnp.exp(-jnp.abs(nz)))
        logw = -jnp.exp(-softplus - 0.5)
        a = _sigmoid(a0_ref[...] + _dot(al.astype(BF16), a2_ref[...]))
        g = _dot(_sigmoid(gl).astype(BF16), g2_ref[...])
        kk = k * kk_ref[...]
        ss = _dot((kk * kk).astype(BF16), gsum)
        kk = kk * (1.0 / jnp.maximum(jnp.sqrt(ss), 1e-12))
        k2 = k * (1.0 + (a - 1.0) * ka_ref[...])
        bb = kk * a
        cw = logw
        sh = 1
        while sh < C:
            cw = cw + jnp.where(row >= sh, pltpu.roll(cw, sh, 0), 0.0)
            sh *= 2
        cw_last = cw[C - 1:C]
        e_neg = jnp.exp(-cw)
        e_end = jnp.exp(cw_last - cw)
        return dict(r=r, v=v, k2=k2, g=g, rw=r * jnp.exp(cw), kkp=kk * jnp.exp(cw - logw),
                    bw=bb * e_neg, kw=k2 * e_neg, bend=bb * e_end, kend=k2 * e_end,
                    wc=jnp.exp(cw_last), vt=v.T.astype(BF16))

    pre = [prepare(bi) for bi in range(nb)]

    ri = lax.broadcasted_iota(I32, (C, C), 0)
    ci = lax.broadcasted_iota(I32, (C, C), 1)
    strict = ri < ci
    incl = ri <= ci
    incl2 = jnp.concatenate([incl, incl], axis=0)
    lane_half = lax.broadcasted_iota(I32, (1, LANES), 1) // HEAD_DIM

    chains = [(bi, h) for bi in range(nb) for h in range(RW_HEADS)]
    pair = lambda h: slice((h // 2) * LANES, (h // 2 + 1) * LANES)
    own = lambda h: lane_half == (h % 2)
    stack = lambda bi, top, bot, h: jnp.concatenate(
        [pre[bi][top][:, pair(h)], pre[bi][bot][:, pair(h)]], axis=0)
    lh = {(bi, q): stack(bi, 'kkp', 'rw', 2 * q).astype(BF16)
          for bi in range(nb) for q in range(RW_HEADS // 2)}
    rh = [jnp.where(own(h), stack(bi, 'bw', 'kw', h), 0.0).astype(BF16) for bi, h in chains]
    aat = [_dot_nt(rh[n], lh[bi, h // 2]) for n, (bi, h) in enumerate(chains)]
    s_old = [s_ref[bi * RW_HEADS + h] for bi, h in chains]
    sl = [_dot_nt(s_old[n].astype(BF16), lh[bi, h // 2]) for n, (bi, h) in enumerate(chains)]
    vt = [pre[bi]['vt'][h * HEAD_DIM:(h + 1) * HEAD_DIM] for bi, h in chains]
    ids = range(len(chains))
    akt = [jnp.where(strict, aat[n][C:, :C], 0.0).astype(BF16) for n in ids]
    m = [jnp.where(strict, aat[n][:C, :C], 0.0).astype(BF16) for n in ids]
    xs = [-(sl[n][:, :C] + _dot(vt[n], akt[n])) for n in ids]
    xs = [xs[n] - _dot_split(xs[n], m[n]) for n in ids]
    lvl = 2
    while lvl < C:
        m = [_dot(m[n], m[n]).astype(BF16) for n in ids]
        xs = [xs[n] + _dot_split(xs[n], m[n]) for n in ids]
        lvl *= 2
    zt = [jnp.concatenate([xs[n].astype(BF16), vt[n]], axis=1) for n in ids]
    for n in ids:
        ymat = jnp.where(incl2, aat[n][:, C:], 0.0).astype(BF16)
        yt_ref[n * HEAD_DIM:(n + 1) * HEAD_DIM, :] = sl[n][:, C:] + _dot(zt[n], ymat)
    for n, (bi, h) in enumerate(chains):
        endz = jnp.where(own(h), stack(bi, 'bend', 'kend', h), 0.0).astype(BF16)
        s_ref[n] = s_old[n] * pre[bi]['wc'][:, pair(h)] + _dot(zt[n], endz)

    lnw = lnw_ref[...].reshape(RW_HEADS, HEAD_DIM, 1)
    lnb = lnb_ref[...].reshape(RW_HEADS, HEAD_DIM, 1)
    for bi in range(nb):
        q = pre[bi]
        yt = yt_ref[bi * W:(bi + 1) * W, :].reshape(RW_HEADS, HEAD_DIM, C)
        mean = jnp.mean(yt, axis=1, keepdims=True)
        yc = yt - mean
        var = jnp.mean(yc * yc, axis=1, keepdims=True)
        yn = yc * lax.rsqrt(var + RW_GN_EPS) * lnw + lnb
        y = yn.reshape(W, C).T
        bonus = _dot((q['r'] * q['k2'] * rk_ref[...]).astype(BF16), gsum) * q['v']
        y_ref[bi] = ((y + bonus) * q['g']).astype(BF16)


def _rwkv(p_rw, rw_mu, rw_w0, rw_w2, rw_a0, rw_a2, rw_g2, rw_k_k, rw_k_a, rw_r_k, rw_ln_w, rw_ln_b,
          interpret):
    B, S, _ = p_rw.shape
    C, W = RW_CHUNK, RW_WIDTH
    li = np.arange(W)
    gsum = jnp.asarray((li[:, None] // HEAD_DIM == li[None, :] // HEAD_DIM).astype(np.float32)).astype(BF16)
    row = lambda a: a.reshape(1, -1)
    full = lambda shape: pl.BlockSpec(shape, lambda b, i: (0,) * len(shape))
    nb = RW_BATCH if B % RW_BATCH == 0 else 1
    return pl.pallas_call(
        _rwkv_kernel,
        grid=(B // nb, S // C),
        in_specs=[pl.BlockSpec((nb, C, RW_COLS), lambda b, i: (b, i, 0)),
                  full((1, RW_COLS)), full((1, W)), full((RW_LORA_W, W)), full((1, W)),
                  full((RW_LORA_A, W)), full((RW_LORA_G, W)), full((1, W)), full((1, W)),
                  full((1, W)), full((W, 1)), full((W, 1)), full((W, W))],
        out_specs=pl.BlockSpec((nb, C, W), lambda b, i: (b, i, 0)),
        out_shape=jax.ShapeDtypeStruct((B, S, W), BF16),
        scratch_shapes=[pltpu.VMEM((nb * RW_HEADS, HEAD_DIM, LANES), F32),
                        pltpu.VMEM((nb, RW_COLS), F32),
                        pltpu.VMEM((nb * W, C), F32)],
        compiler_params=_cparams(("arbitrary", "arbitrary")),
        name="rwkv", interpret=interpret,
    )(p_rw, row(rw_mu), row(rw_w0), rw_w2.astype(BF16), row(rw_a0), rw_a2.astype(BF16),
      rw_g2.astype(BF16), row(rw_k_k), row(rw_k_a), row(rw_r_k), rw_ln_w.reshape(W, 1),
      rw_ln_b.reshape(W, 1), gsum)


def _dsa_kernel(topk, nbits, k_ref, vt_ref, ki_ref, qt_ref, qit_ref, wit_ref, og_ref, o_ref,
                key_s, acc_s, m_s, l_s, thr_s):
    j = pl.program_id(1)
    nkb = j + 1
    lane = lax.broadcasted_iota(I32, (QBLK, QBLK), 1)
    sub = lax.broadcasted_iota(I32, (QBLK, QBLK), 0)
    qpos = j * QBLK + lane
    wit = wit_ref[0]

    def score_blocks(i, carry):
        kbs = [i * SCORE_UNROLL + u for u in range(SCORE_UNROLL)]
        kib = [ki_ref[0, pl.ds(pl.multiple_of(kb * QBLK, QBLK), QBLK), :] for kb in kbs]
        lg = [[_dot(kib[u], qit_ref[0, hh]) for hh in range(IDX_HEADS)] for u in range(SCORE_UNROLL)]
        for u, kb in enumerate(kbs):
            s = wit[0:1, :] * jnp.maximum(lg[u][0], 0.0)
            for hh in range(1, IDX_HEADS):
                s = s + wit[hh:hh + 1, :] * jnp.maximum(lg[u][hh], 0.0)
            s = jnp.where(s == 0.0, 0.0, s)
            bits = pltpu.bitcast(s, I32)
            skey = jnp.where(bits < 0, bits ^ 0x7FFFFFFF, bits)
            key_s[kb] = jnp.where(kb * QBLK + sub <= qpos, skey, INT_MIN)
        return carry

    lax.fori_loop(0, pl.cdiv(nkb, SCORE_UNROLL), score_blocks, 0)

    @pl.when(nkb * QBLK <= topk)
    def _():
        thr_s[0:1, :] = jnp.full((1, QBLK), INT_MIN, I32)
        thr_s[1:2, :] = jnp.zeros((1, QBLK), I32)

    @pl.when(nkb * QBLK > topk)
    def _():
        n_done = pl.cdiv(nkb, SCORE_UNROLL) * SCORE_UNROLL
        n_cnt = pl.cdiv(nkb, COUNT_UNROLL)

        def fill(kb, carry):
            key_s[kb] = jnp.full((QBLK, QBLK), INT_MIN, I32)
            return carry

        lax.fori_loop(n_done, n_cnt * COUNT_UNROLL, fill, 0)

        def count(preds):
            def body(i, accs):
                accs = list(accs)
                for u in range(COUNT_UNROLL):
                    kb = i * COUNT_UNROLL + u
                    ky = key_s[kb]
                    for n, pred in enumerate(preds):
                        hit = pred(ky, kb * QBLK + sub).astype(I32)
                        accs[n] = accs[n] + jnp.sum(hit.reshape(QBLK // 8, 8, QBLK), axis=0)
                return tuple(accs)
            accs = lax.fori_loop(0, n_cnt, body, tuple(jnp.zeros((8, QBLK), I32) for _ in preds))
            return [jnp.sum(a, axis=0, keepdims=True) for a in accs]

        c0, = count([lambda ky, ix: ky >= 0])
        t0 = jnp.where(c0 >= topk, 0, INT_MIN).astype(I32)

        def bit_step(i, t):
            cand = t | jnp.left_shift(jnp.int32(1), 30 - i)
            c, = count([lambda ky, ix: ky >= cand])
            return jnp.where(c >= topk, cand, t)

        thr = lax.fori_loop(0, 31, bit_step, t0)
        n_gt, n_eq = count([lambda ky, ix: ky > thr, lambda ky, ix: ky == thr])
        need = topk - n_gt
        thr_s[0:1, :] = thr
        thr_s[1:2, :] = jnp.full((1, QBLK), 2 ** nbits, I32)

        @pl.when(jnp.max(jnp.abs(n_eq - need)) > 0)
        def _():
            def idx_step(i, mm):
                cand = mm | jnp.left_shift(jnp.int32(1), nbits - 1 - i)
                c, = count([lambda ky, ix: (ky == thr) & (ix < cand)])
                return jnp.where(c < need, cand, mm)

            thr_s[1:2, :] = lax.fori_loop(0, nbits, idx_step, jnp.zeros((1, QBLK), I32))

    thr = thr_s[0:1, :]
    mm = thr_s[1:2, :]
    m_s[...] = jnp.full_like(m_s, NEG_BIG)
    l_s[...] = jnp.zeros_like(l_s)
    acc_s[...] = jnp.zeros_like(acc_s)

    qt2 = [jnp.concatenate([qt_ref[0, 2 * q], qt_ref[0, 2 * q + 1]], axis=1)
           for q in range(AT_HEADS // 2)]

    def attn_blocks(i, carry):
        kbs = [i * ATTN_UNROLL + u for u in range(ATTN_UNROLL)]
        sel = []
        for kb in kbs:
            skey = key_s[kb]
            kidx = kb * QBLK + sub
            sel.append((kidx <= qpos) & ((skey > thr) | ((skey == thr) & (kidx <= mm))))
        sel = jnp.concatenate(sel, axis=0)
        kblk = [k_ref[0, pl.ds(pl.multiple_of(kb * QBLK, QBLK), QBLK), :] for kb in kbs]
        vtb = jnp.concatenate([vt_ref[0, kb] for kb in kbs], axis=1)
        s2 = [[_dot(kblk[u][:, q * LANES:(q + 1) * LANES], qt2[q]) for u in range(ATTN_UNROLL)]
              for q in range(AT_HEADS // 2)]
        pexp, alpha = [], []
        for hh in range(AT_HEADS):
            half = slice((hh % 2) * QBLK, (hh % 2 + 1) * QBLK)
            s = jnp.concatenate([s2[hh // 2][u][:, half] for u in range(ATTN_UNROLL)], axis=0)
            s = jnp.where(sel, s, NEG_BIG)
            m_old = m_s[hh:hh + 1, :]
            m_new = jnp.maximum(m_old, jnp.max(s, axis=0, keepdims=True))
            pe = jnp.exp(s - m_new)
            al = jnp.exp(m_old - m_new)
            l_s[hh:hh + 1, :] = al * l_s[hh:hh + 1, :] + jnp.sum(pe, axis=0, keepdims=True)
            m_s[hh:hh + 1, :] = m_new
            pexp.append(pe.astype(BF16))
            alpha.append(al)
        for hh in range(AT_HEADS):
            hs = slice(hh * HEAD_DIM, (hh + 1) * HEAD_DIM)
            acc_s[hs, :] = alpha[hh] * acc_s[hs, :] + _dot(vtb[hs, :], pexp[hh])
        return carry

    lax.fori_loop(0, pl.cdiv(nkb, ATTN_UNROLL), attn_blocks, 0)

    for hh in range(AT_HEADS):
        hs = slice(hh * HEAD_DIM, (hh + 1) * HEAD_DIM)
        oh = acc_s[hs, :] * (1.0 / l_s[hh:hh + 1, :])
        ms = jnp.mean(oh * oh, axis=0, keepdims=True)
        acc_s[hs, :] = oh * lax.rsqrt(ms + NORM_EPS) * og_ref[hs, :]
    o_ref[0] = acc_s[...].T.astype(BF16)


def _dsa(k, vt, ki, qt, qit, wit, at_out_g, interpret):
    B, S, _ = k.shape
    nq = S // QBLK
    topk = min(IDX_TOPK_MAX, S // 4)
    nbits = int(np.log2(S))
    assert 2 ** nbits == S and nq % COUNT_UNROLL == 0 and SCORE_UNROLL == ATTN_UNROLL
    assert COUNT_UNROLL % SCORE_UNROLL == 0
    return pl.pallas_call(
        functools.partial(_dsa_kernel, topk, nbits),
        grid=(B, nq),
        in_specs=[pl.BlockSpec((1, S, AT_WIDTH), lambda b, j: (b, 0, 0)),
                  pl.BlockSpec((1, nq, AT_WIDTH, QBLK), lambda b, j: (b, 0, 0, 0)),
                  pl.BlockSpec((1, S, LANES), lambda b, j: (b, 0, 0)),
                  pl.BlockSpec((1, AT_HEADS, LANES, QBLK), lambda b, j: (b, 0, 0, j)),
                  pl.BlockSpec((1, IDX_HEADS, LANES, QBLK), lambda b, j: (b, 0, 0, j)),
                  pl.BlockSpec((1, 8, QBLK), lambda b, j: (b, 0, j)),
                  pl.BlockSpec((AT_WIDTH, 1), lambda b, j: (0, 0))],
        out_specs=pl.BlockSpec((1, QBLK, AT_WIDTH), lambda b, j: (b, j, 0)),
        out_shape=jax.ShapeDtypeStruct((B, S, AT_WIDTH), BF16),
        scratch_shapes=[pltpu.VMEM((nq, QBLK, QBLK), I32),
                        pltpu.VMEM((AT_WIDTH, QBLK), F32),
                        pltpu.VMEM((AT_HEADS, QBLK), F32),
                        pltpu.VMEM((AT_HEADS, QBLK), F32),
                        pltpu.VMEM((8, QBLK), I32)],
        compiler_params=_cparams(("arbitrary", "arbitrary")),
        name="dsa", interpret=interpret,
    )(k, vt, ki, qt, qit, wit, at_out_g.reshape(AT_WIDTH, 1))


def _first_max(vals, idx, axis, sentinel):
    m = jnp.max(vals, axis=axis, keepdims=True)
    return m, jnp.min(jnp.where(vals == m, idx, sentinel), axis=axis, keepdims=True)


def _route_cols(logits_t, bias_col):
    E, tm = logits_t.shape
    pg = E // N_GROUPS
    scores = _sigmoid(logits_t)
    biased = scores + bias_col
    b3 = biased.reshape(N_GROUPS, pg, tm)
    r3 = lax.broadcasted_iota(I32, (N_GROUPS, pg, tm), 1)
    m1, first = _first_max(b3, r3, 1, pg)
    m2 = jnp.max(jnp.where(r3 == first, -jnp.inf, b3), axis=1, keepdims=True)
    cur = (m1 + m2).reshape(N_GROUPS, tm)
    grow = lax.broadcasted_iota(I32, (N_GROUPS, tm), 0)
    gsel = jnp.zeros((N_GROUPS, tm), F32)
    for _ in range(TOPK_GROUPS):
        _, gi = _first_max(cur, grow, 0, N_GROUPS)
        hit = grow == gi
        gsel = jnp.where(hit, 1.0, gsel)
        cur = jnp.where(hit, -jnp.inf, cur)
    gmask = jnp.broadcast_to(gsel.reshape(N_GROUPS, 1, tm), (N_GROUPS, pg, tm)).reshape(E, tm)
    cur = jnp.where(gmask > 0.0, biased, -jnp.inf)
    row = lax.broadcasted_iota(I32, (E, tm), 0)
    onehot = jnp.zeros((E, tm), F32)
    eids, gws = [], []
    for _ in range(TOP_K):
        _, ei = _first_max(cur, row, 0, E)
        hit = row == ei
        eids.append(ei)
        gws.append(jnp.sum(jnp.where(hit, scores, 0.0), axis=0, keepdims=True))
        onehot = jnp.where(hit, 1.0, onehot)
        cur = jnp.where(hit, -jnp.inf, cur)
    eid = jnp.concatenate(eids, axis=0)
    gw = jnp.concatenate(gws, axis=0)
    gw = gw * (ROUTED_SCALE / jnp.sum(gw, axis=0, keepdims=True))
    return eid, gw, onehot


def _post_kernel(x_ref, yrw_ref, yat_ref, g1_ref, sc_ref, sh_ref, g2_ref, ng_ref, wo_ref, rwt_ref,
                 rb_ref, s1_ref, s3_ref, s2_ref, base_ref, h2_ref, eid_ref, gw_ref, rank_ref, cnt_ref):
    W = RW_WIDTH
    tm = x_ref.shape[1]
    E = rwt_ref.shape[0]

    @pl.when((pl.program_id(0) == 0) & (pl.program_id(1) == 0))
    def _():
        cnt_ref[...] = jnp.zeros_like(cnt_ref)

    mix = _dot(yrw_ref[0], wo_ref[0:W, :]) + _dot(yat_ref[0], wo_ref[W:, :])
    x1 = x_ref[0] + g1_ref[0] * mix
    ms = jnp.mean(x1 * x1, axis=-1, keepdims=True)
    h2 = x1 * lax.rsqrt(ms + NORM_EPS) * ng_ref[...] * (1.0 + sc_ref[0]) + sh_ref[0]
    hb = h2.astype(BF16)
    h2_ref[0] = _pack_halves(h2)
    act =(_silu(_dot(hb, s1_ref[...])) * _dot(hb, s3_ref[...])).astype(BF16)
    base_ref[0] = x1 + g2_ref[0] * _dot(act, s2_ref[...])

    logits_t = lax.dot_general(rwt_ref[...], h2, (((1,), (1,)), ((), ())), precision=HIGHEST,
                               preferred_element_type=F32)
    eid, gw, onehot = _route_cols(logits_t, rb_ref[...])
    eid_ref[0] = eid
    gw_ref[0] = gw
    ti = lax.broadcasted_iota(I32, (tm, tm), 0)
    tj = lax.broadcasted_iota(I32, (tm, tm), 1)
    before = _dot(onehot.astype(BF16), (ti < tj).astype(BF16)) + cnt_ref[:, 0:1]
    row = lax.broadcasted_iota(I32, (E, tm), 0)
    ranks = [jnp.sum(jnp.where(row == eid[kk:kk + 1, :], before, 0.0), axis=0, keepdims=True)
             for kk in range(TOP_K)]
    rank_ref[0] = jnp.concatenate(ranks, axis=0).astype(I32)
    cnt_ref[...] = cnt_ref[...] + jnp.sum(onehot, axis=1, keepdims=True)


def _post(x, y_rw, y_at, gate1, scale2, shift2, gate2, norm2_g, w_out, router_w, router_bias,
          sw1, sw3, sw2, interpret):
    B, S, D = x.shape
    tm = min(S, 256)
    E = router_w.shape[1]
    sd = sw1.shape[1]
    full = lambda shape: pl.BlockSpec(shape, lambda b, i: (0,) * len(shape))
    tok = lambda w: pl.BlockSpec((1, tm, w), lambda b, i: (b, i, 0))
    per_b = pl.BlockSpec((1, 1, D), lambda b, i: (b, 0, 0))
    col8 = pl.BlockSpec((1, TOP_K, tm), lambda b, i: (b, 0, i))
    return pl.pallas_call(
        _post_kernel,
        grid=(B, S // tm),
        in_specs=[tok(D), tok(RW_WIDTH), tok(AT_WIDTH), per_b, per_b, per_b, per_b, full((1, D)),
                  full((D, D)), full((E, D)), full((E, 1)), full((D, sd)), full((D, sd)),
                  full((sd, D))],
        out_specs=[tok(D), tok(D // 2), col8, col8, col8, full((E, LANES))],
        out_shape=[jax.ShapeDtypeStruct((B, S, D), F32),
                   jax.ShapeDtypeStruct((B, S, D // 2), I32),
                   jax.ShapeDtypeStruct((B, TOP_K, S), I32),
                   jax.ShapeDtypeStruct((B, TOP_K, S), F32),
                   jax.ShapeDtypeStruct((B, TOP_K, S), I32),
                   jax.ShapeDtypeStruct((E, LANES), F32)],
        compiler_params=_cparams(("arbitrary", "arbitrary")),
        name="post", interpret=interpret,
    )(x, y_rw, y_at, gate1, scale2, shift2, gate2, norm2_g.reshape(1, D), w_out.astype(BF16),
      router_w.T, router_bias.reshape(E, 1), sw1.astype(BF16), sw3.astype(BF16), sw2.astype(BF16))


def _expert_kernel(be_ref, nv_ref, nu_ref, xs_ref, w1_ref, w3_ref, w2_ref, o_ref, w13_s, w2_s):
    i = pl.program_id(0)
    blk, hw = xs_ref.shape
    F = w1_ref.shape[2]
    used = i < nu_ref[0]

    @pl.when(used & ((i == 0) | (be_ref[i] != be_ref[jnp.maximum(i - 1, 0)])))
    def _():
        w13_s[:, :F] = w1_ref[0].astype(BF16)
        w13_s[:, F:] = w3_ref[0].astype(BF16)
        w2_s[...] = w2_ref[0].astype(BF16)

    @pl.when(used)
    def _():
        live = lax.broadcasted_iota(I32, (blk, 1), 0) < nv_ref[i]
        x_lo, x_hi = _unpack_halves(jnp.where(live, xs_ref[...], 0))
        h13 =(_dot(x_lo.astype(BF16), w13_s[:hw, :]) + _dot(x_hi.astype(BF16), w13_s[hw:, :]))
        act = (_silu(h13[:, :F]) * h13[:, F:]).astype(BF16)
        o_ref[...] = _pack_halves(_dot(act, w2_s[...]))


def _experts(xs, block_e, block_rows, n_used, w1, w3, w2, interpret):
    P, hw = xs.shape
    E, D, F = w1.shape
    nb = P // EXP_BLK
    blk = lambda i, nu: jnp.minimum(i, nu[0] - 1)
    grid_spec = pltpu.PrefetchScalarGridSpec(
        num_scalar_prefetch=3,
        grid=(nb,),
        in_specs=[pl.BlockSpec((EXP_BLK, hw), lambda i, be, nv, nu: (blk(i, nu), 0)),
                  pl.BlockSpec((1, D, F), lambda i, be, nv, nu: (be[blk(i, nu)], 0, 0)),
                  pl.BlockSpec((1, D, F), lambda i, be, nv, nu: (be[blk(i, nu)], 0, 0)),
                  pl.BlockSpec((1, F, D), lambda i, be, nv, nu: (be[blk(i, nu)], 0, 0))],
        out_specs=pl.BlockSpec((EXP_BLK, hw), lambda i, be, nv, nu: (blk(i, nu), 0)),
        scratch_shapes=[pltpu.VMEM((D, 2 * F), BF16), pltpu.VMEM((F, D), BF16)],
    )
    return pl.pallas_call(
        _expert_kernel,
        grid_spec=grid_spec,
        out_shape=jax.ShapeDtypeStruct((P, hw), I32),
        compiler_params=_cparams(("arbitrary",)),
        name="experts", interpret=interpret,
    )(block_e, block_rows, n_used, xs, w1, w3, w2)


def _row_copy(src, s_row, dst, d_row, sem):
    return pltpu.make_async_copy(src.at[pl.ds(s_row, 1)], dst.at[pl.ds(d_row, 1)], sem)


def _dispatch_kernel(dest_ref, h2_ref, xs_ref, sem):
    td = h2_ref.shape[0]

    def issue(t, carry):
        for kk in range(TOP_K):
            _row_copy(h2_ref, t, xs_ref, dest_ref[0, 0, kk * td + t], sem).start(priority=kk % 2)
        return carry

    def drain(t, carry):
        for kk in range(TOP_K):
            _row_copy(h2_ref, t, xs_ref, dest_ref[0, 0, kk * td + t], sem).wait()
        return carry

    lax.fori_loop(0, td, issue, 0)
    lax.fori_loop(0, td, drain, 0)


def _dispatch(h2, dest_tiles, n_slots, interpret):
    T, D = h2.shape
    nt, _, n = dest_tiles.shape
    td = n // TOP_K
    return pl.pallas_call(
        _dispatch_kernel,
        grid=(nt,),
        in_specs=[pl.BlockSpec((1, 1, n), lambda i: (i, 0, 0), memory_space=pltpu.SMEM),
                  pl.BlockSpec((td, D), lambda i: (i, 0))],
        out_specs=pl.BlockSpec(memory_space=pl.ANY),
        out_shape=jax.ShapeDtypeStruct((n_slots, D), h2.dtype),
        scratch_shapes=[pltpu.SemaphoreType.DMA(())],
        compiler_params=_cparams(("arbitrary",)),
        name="dispatch", interpret=interpret,
    )(dest_tiles, h2)


def _combine_kernel(dest_ref, base_ref, g2_ref, gw_ref, ys_ref, o_ref, buf, sem):
    td = base_ref.shape[0]

    def issue(t, carry):
        for kk in range(TOP_K):
            _row_copy(ys_ref, dest_ref[0, 0, kk * td + t], buf.at[kk], t, sem).start(priority=kk % 2)
        return carry

    def drain(t, carry):
        for kk in range(TOP_K):
            _row_copy(ys_ref, dest_ref[0, 0, kk * td + t], buf.at[kk], t, sem).wait()
        return carry

    lax.fori_loop(0, td, issue, 0)
    lax.fori_loop(0, td, drain, 0)
    gw = gw_ref[...]
    acc_lo, acc_hi = _unpack_halves(buf[0])
    acc_lo, acc_hi = gw[:, 0:1] * acc_lo, gw[:, 0:1] * acc_hi
    for kk in range(1, TOP_K):
        y_lo, y_hi = _unpack_halves(buf[kk])
        acc_lo = acc_lo + gw[:, kk:kk + 1] * y_lo
        acc_hi = acc_hi + gw[:, kk:kk + 1] * y_hi
    o_ref[...] = base_ref[...] + g2_ref[0] * jnp.concatenate([acc_lo, acc_hi], axis=1)


def _combine(base, gate2, gw_tok, ys, dest_tiles, tiles_per_batch, interpret):
    T, D = base.shape
    nt, _, n = dest_tiles.shape
    td = n // TOP_K
    return pl.pallas_call(
        _combine_kernel,
        grid=(nt,),
        in_specs=[pl.BlockSpec((1, 1, n), lambda i: (i, 0, 0), memory_space=pltpu.SMEM),
                  pl.BlockSpec((td, D), lambda i: (i, 0)),
                  pl.BlockSpec((1, 1, D), lambda i: (i // tiles_per_batch, 0, 0)),
                  pl.BlockSpec((td, TOP_K), lambda i: (i, 0)),
                  pl.BlockSpec(memory_space=pl.ANY)],
        out_specs=pl.BlockSpec((td, D), lambda i: (i, 0)),
        out_shape=jax.ShapeDtypeStruct((T, D), F32),
        scratch_shapes=[pltpu.VMEM((TOP_K, td, D // 2), I32), pltpu.SemaphoreType.DMA(())],
        compiler_params=_cparams(("arbitrary",)),
        name="combine", interpret=interpret,
    )(dest_tiles, base, gate2, gw_tok, ys)


def _slots_kernel(eid_ref, rank_ref, pstart_ref, dest_ref):
    td = eid_ref.shape[2]
    E = pstart_ref.shape[0]
    row = lax.broadcasted_iota(I32, (E, td), 0)
    pstart = pstart_ref[...]
    eid = eid_ref[0]
    for kk in range(TOP_K):
        base = jnp.sum(jnp.where(row == eid[kk:kk + 1, :], pstart, 0), axis=0, keepdims=True)
        dest_ref[0, :, kk * td:(kk + 1) * td] = base + rank_ref[0, kk:kk + 1, :]


def _slot_plan(counts, eid_t, rank_t, td, interpret):
    B, _, S = eid_t.shape
    E = counts.shape[0]
    padded = (counts + EXP_BLK - 1) // EXP_BLK * EXP_BLK
    pend = jnp.cumsum(padded)
    pstart = (pend - padded).astype(I32)
    nb = -(-(B * S * TOP_K + E * (EXP_BLK - 1)) // EXP_BLK)
    first_row = jnp.arange(nb, dtype=I32) * EXP_BLK
    block_e = jnp.sum(pend[None, :] <= first_row[:, None], axis=1)
    block_e = jnp.minimum(block_e, E - 1).astype(I32)
    block_rows = jnp.clip(pstart[block_e] + counts[block_e] - first_row, 0, EXP_BLK).astype(I32)
    nt = S // td
    dest_tiles = pl.pallas_call(
        _slots_kernel,
        grid=(B, nt),
        in_specs=[pl.BlockSpec((1, TOP_K, td), lambda b, i: (b, 0, i)),
                  pl.BlockSpec((1, TOP_K, td), lambda b, i: (b, 0, i)),
                  pl.BlockSpec((E, 1), lambda b, i: (0, 0))],
        out_specs=pl.BlockSpec((1, 1, TOP_K * td), lambda b, i: (b * nt + i, 0, 0)),
        out_shape=jax.ShapeDtypeStruct((B * nt, 1, TOP_K * td), I32),
        compiler_params=_cparams(("arbitrary", "arbitrary")),
        name="slots", interpret=interpret,
    )(eid_t, rank_t, pstart.reshape(E, 1))
    n_used = (pend[-1:] // EXP_BLK).astype(I32)
    return block_e, block_rows, n_used, dest_tiles, nb * EXP_BLK


def _forward(x, c, positions, w_ada, b_ada, norm1_g, norm2_g, w_in, rw_mu, rw_w0, rw_w2,
             rw_a0, rw_a2, rw_g2, rw_k_k, rw_k_a, rw_r_k, rw_ln_w, rw_ln_b, q_norm_g,
             k_norm_g, idx_ln_w, idx_ln_b, at_out_g, w_out, router_w, router_bias,
             exp_w1, exp_w3, exp_w2, shared_w1, shared_w3, shared_w2, interpret=False):
    B, S, D = x.shape
    depth = w_ada.shape[0]
    for l in range(depth):
        mod = _mod(c, w_ada[l], b_ada[l], interpret)
        shift1, scale1, gate1, shift2, scale2, gate2 = [
            m.reshape(B, 1, D) for m in jnp.split(mod, 6, axis=-1)]
        tabs = _rope_tables(positions, interpret)
        p_rw, k, ki, qt, vt, qit, wit = _inproj(
            x, scale1, shift1, norm1_g[l], w_in[l], k_norm_g[l], idx_ln_w[l], idx_ln_b[l],
            q_norm_g[l], tabs, interpret)
        y_rw = _rwkv(p_rw, rw_mu[l], rw_w0[l], rw_w2[l], rw_a0[l], rw_a2[l], rw_g2[l], rw_k_k[l],
                     rw_k_a[l], rw_r_k[l], rw_ln_w[l], rw_ln_b[l], interpret)
        y_at = _dsa(k, vt, ki, qt, qit, wit, at_out_g[l], interpret)
        base, h2, eid_t, gw_t, rank_t, cnt = _post(
            x, y_rw, y_at, gate1, scale2, shift2, gate2, norm2_g[l], w_out[l], router_w[l],
            router_bias[l], shared_w1[l], shared_w3[l], shared_w2[l], interpret)
        T = B * S
        td = min(S, ROW_TILE)
        block_e, block_rows, n_used, dest_tiles, n_slots = _slot_plan(
            cnt[:, 0].astype(I32), eid_t, rank_t, td, interpret)
        xs = _dispatch(h2.reshape(T, D // 2), dest_tiles, n_slots, interpret)
        ys = _experts(xs, block_e, block_rows, n_used, exp_w1[l], exp_w3[l], exp_w2[l], interpret)
        gw_tok = gw_t.transpose(0, 2, 1).reshape(T, TOP_K)
        x = _combine(base.reshape(T, D), gate2, gw_tok, ys, dest_tiles, S // td,
                     interpret).reshape(B, S, D)
    return x


def kernel(x, c, positions, w_ada, b_ada, norm1_g, norm2_g, w_in, rw_mu, rw_w0, rw_w2, rw_a0, rw_a2, rw_g2, rw_k_k, rw_k_a, rw_r_k, rw_ln_w, rw_ln_b, q_norm_g, k_norm_g, idx_ln_w, idx_ln_b, at_out_g, w_out, router_w, router_bias, exp_w1, exp_w3, exp_w2, shared_w1, shared_w3, shared_w2):
    return _forward(x, c, positions, w_ada, b_ada, norm1_g, norm2_g, w_in, rw_mu, rw_w0, rw_w2,
                    rw_a0, rw_a2, rw_g2, rw_k_k, rw_k_a, rw_r_k, rw_ln_w, rw_ln_b, q_norm_g,
                    k_norm_g, idx_ln_w, idx_ln_b, at_out_g, w_out, router_w, router_bias,
                    exp_w1, exp_w3, exp_w2, shared_w1, shared_w3, shared_w2)
```

```python
import functools

import jax
import jax.numpy as jnp
import numpy as np
from jax import lax
from jax.experimental import pallas as pl
from jax.experimental.pallas import tpu as pltpu

F32 = jnp.float32
BF16 = jnp.bfloat16
I32 = jnp.int32
HIGHEST = lax.Precision.HIGHEST

LANES = 128
HEAD_DIM = 64
HALF = HEAD_DIM // 2
RW_HEADS = 8
RW_WIDTH = RW_HEADS * HEAD_DIM
AT_HEADS = 8
AT_WIDTH = AT_HEADS * HEAD_DIM
IDX_HEADS = 4
RW_LORA_W, RW_LORA_A, RW_LORA_G = 64, 64, 128
RW_COLS = 3 * RW_WIDTH + RW_LORA_W + RW_LORA_A + RW_LORA_G
KI_OFF = 3 * AT_WIDTH + IDX_HEADS * HEAD_DIM
WI_OFF = KI_OFF + HEAD_DIM
PT_ROWS = 2 * AT_WIDTH + IDX_HEADS * HEAD_DIM + 8
PK_COLS = AT_WIDTH + LANES
ROPE_THETA = 10000.0
NORM_EPS = 1e-6
LN_EPS = 1e-6
RW_GN_EPS = 64e-5
IDX_TOPK_MAX = 256
N_EXPERTS = 256
TOP_K = 8
N_GROUPS = 8
TOPK_GROUPS = 4
ROUTED_SCALE = 2.5
INT_MIN = -2 ** 31
PAIR_HI_MASK = -65536
NEG_BIG = -1e30

RW_CHUNK = 128
RW_BATCH = 2
QBLK = 128
SCORE_UNROLL = 4
ATTN_UNROLL = 4
COUNT_UNROLL = 4
EXP_BLK = 512
ROW_TILE = 256
VMEM_LIMIT = 48 * 1024 * 1024


def _cparams(sem):
    return pltpu.CompilerParams(dimension_semantics=sem, vmem_limit_bytes=VMEM_LIMIT)


def _sigmoid(x):
    return 1.0 / (1.0 + jnp.exp(-x))


def _silu(x):
    return x * _sigmoid(x)


def _dot(a, b):
    return jnp.dot(a, b, preferred_element_type=F32)


def _dot_split(a, b):
    hi = a.astype(BF16)
    lo = (a - hi.astype(F32)).astype(BF16)
    return _dot(hi, b) + _dot(lo, b)


def _pack_halves(x):
    w = x.shape[1] // 2
    lo = pltpu.bitcast(x[:, :w].astype(BF16).astype(F32), I32)
    hi = pltpu.bitcast(x[:, w:].astype(BF16).astype(F32), I32)
    return (hi & PAIR_HI_MASK) | lax.shift_right_logical(lo, 16)


def _unpack_halves(p):
    return pltpu.bitcast(p << 16, F32), pltpu.bitcast(p & PAIR_HI_MASK, F32)


def _dot_nt(a, b):
    return lax.dot_general(a, b, (((1,), (1,)), ((), ())), preferred_element_type=F32)


def _mod_kernel(c_ref, w_ref, b_ref, o_ref):
    c = c_ref[...]
    o_ref[...] = jnp.dot(_silu(c), w_ref[...], precision=HIGHEST,
                         preferred_element_type=F32) + b_ref[...]


def _mod(c, w_ada, b_ada, interpret):
    B, D = c.shape
    n = w_ada.shape[1] // D
    return pl.pallas_call(
        _mod_kernel,
        grid=(n,),
        in_specs=[pl.BlockSpec((B, D), lambda i: (0, 0)),
                  pl.BlockSpec((D, D), lambda i: (0, i)),
                  pl.BlockSpec((1, D), lambda i: (0, i))],
        out_specs=pl.BlockSpec((B, D), lambda i: (0, i)),
        out_shape=jax.ShapeDtypeStruct((B, n * D), F32),
        compiler_params=_cparams(("arbitrary",)),
        name="mod", interpret=interpret,
    )(c, w_ada, b_ada.reshape(1, -1))


def _rope_tab_kernel(pc_ref, pr_ref, cr_ref, sr_ref, ct_ref, st_ref):
    log_theta = float(np.log(ROPE_THETA))
    lane = lax.broadcasted_iota(I32, (1, LANES), 1)
    inv_r = jnp.exp((lane % HALF).astype(F32) * (-log_theta / HALF))
    ang = pc_ref[0].astype(F32) * inv_r
    cr_ref[0] = jnp.cos(ang)
    sr_ref[0] = jnp.where((lane % HEAD_DIM) < HALF, -jnp.sin(ang), jnp.sin(ang))
    sub = lax.broadcasted_iota(I32, (HALF, 1), 0)
    inv_c = jnp.exp(sub.astype(F32) * (-log_theta / HALF))
    ang_t = inv_c * pr_ref[0].astype(F32)
    ct_ref[0] = jnp.cos(ang_t)
    st_ref[0] = jnp.sin(ang_t)


def _rope_tables(positions, interpret):
    B, S = positions.shape
    ts = min(S, 512)
    return pl.pallas_call(
        _rope_tab_kernel,
        grid=(B, S // ts),
        in_specs=[pl.BlockSpec((1, ts, 1), lambda b, i: (b, i, 0)),
                  pl.BlockSpec((1, 1, ts), lambda b, i: (b, 0, i))],
        out_specs=[pl.BlockSpec((1, ts, LANES), lambda b, i: (b, i, 0)),
                   pl.BlockSpec((1, ts, LANES), lambda b, i: (b, i, 0)),
                   pl.BlockSpec((1, HALF, ts), lambda b, i: (b, 0, i)),
                   pl.BlockSpec((1, HALF, ts), lambda b, i: (b, 0, i))],
        out_shape=[jax.ShapeDtypeStruct((B, S, LANES), F32),
                   jax.ShapeDtypeStruct((B, S, LANES), F32),
                   jax.ShapeDtypeStruct((B, HALF, S), F32),
                   jax.ShapeDtypeStruct((B, HALF, S), F32)],
        compiler_params=_cparams(("arbitrary", "arbitrary")),
        name="rope_tab", interpret=interpret,
    )(positions.reshape(B, S, 1), positions.reshape(B, 1, S))


def _rope_rows(y, cos, sin_signed):
    lane = lax.broadcasted_iota(I32, (1, LANES), 1)
    partner = jnp.where((lane % HEAD_DIM) < HALF,
                        pltpu.roll(y, LANES - HALF, 1), pltpu.roll(y, HALF, 1))
    return y * cos + partner * sin_signed


def _rope_cols(y, cos_t, sin_t):
    x1, x2 = y[:, :HALF], y[:, HALF:]
    return jnp.concatenate([x1 * cos_t - x2 * sin_t, x2 * cos_t + x1 * sin_t], axis=1)


def _inproj_kernel(x_ref, sc_ref, sh_ref, g_ref, wrw_ref, wk_ref, wt_ref, kg_ref, iw_ref, ib_ref,
                   qg_ref, gsum_ref, cr_ref, sr_ref, ct_ref, st_ref,
                   prw_ref, k_ref, ki_ref, qt_ref, vt_ref, qit_ref, wit_ref):
    tm = x_ref.shape[1]
    x = x_ref[0]
    ms = jnp.mean(x * x, axis=-1, keepdims=True)
    h = x * lax.rsqrt(ms + NORM_EPS) * g_ref[...] * (1.0 + sc_ref[0]) + sh_ref[0]
    hb = h.astype(BF16)
    prw_ref[0] = _dot(hb, wrw_ref[...])
    pk = _dot(hb, wk_ref[...])
    pt = _dot_nt(wt_ref[...], hb)

    cos_r, sin_r = cr_ref[0], sr_ref[0]
    gsum = gsum_ref[...]
    inv_hd = 1.0 / HEAD_DIM
    for p in range(AT_WIDTH // LANES):
        xk = pk[:, p * LANES:(p + 1) * LANES]
        ss = jnp.dot(xk * xk, gsum, precision=HIGHEST, preferred_element_type=F32)
        y = xk * lax.rsqrt(ss * inv_hd + NORM_EPS) * kg_ref[...]
        k_ref[0, :, p * LANES:(p + 1) * LANES] = _rope_rows(y, cos_r, sin_r).astype(BF16)
    xi = pk[:, AT_WIDTH:AT_WIDTH + LANES]
    mu = jnp.dot(xi, gsum, precision=HIGHEST, preferred_element_type=F32) * inv_hd
    xc = xi - mu
    var = jnp.dot(xc * xc, gsum, precision=HIGHEST, preferred_element_type=F32) * inv_hd
    yi = xc * lax.rsqrt(var + LN_EPS) * iw_ref[...] + ib_ref[...]
    ki_ref[0] = _rope_rows(yi, cos_r, sin_r).astype(BF16)

    cos_t, sin_t = ct_ref[0][None], st_ref[0][None]
    xq = pt[0:AT_WIDTH].reshape(AT_HEADS, HEAD_DIM, tm)
    msq = jnp.mean(xq * xq, axis=1, keepdims=True)
    yq = xq * lax.rsqrt(msq + NORM_EPS) * qg_ref[...][None]
    yq = _rope_cols(yq, cos_t, sin_t) * (HEAD_DIM ** -0.5)
    zq = jnp.zeros((HEAD_DIM, tm), BF16)
    for hh in range(AT_HEADS):
        parts = [yq[hh].astype(BF16), zq] if hh % 2 == 0 else [zq, yq[hh].astype(BF16)]
        qt_ref[0, hh] = jnp.concatenate(parts, axis=0)
    vt = pt[AT_WIDTH:2 * AT_WIDTH].astype(BF16)
    for cblk in range(tm // QBLK):
        vt_ref[0, cblk] = vt[:, cblk * QBLK:(cblk + 1) * QBLK]
    xqi = pt[2 * AT_WIDTH:2 * AT_WIDTH + IDX_HEADS * HEAD_DIM].reshape(IDX_HEADS, HEAD_DIM, tm)
    yqi = _rope_cols(xqi, cos_t, sin_t)
    for hh in range(IDX_HEADS):
        qit_ref[0, hh] = jnp.concatenate([yqi[hh].astype(BF16), zq], axis=0)
    wit_ref[0] = pt[PT_ROWS - 8:PT_ROWS] * (IDX_HEADS ** -0.5 * HEAD_DIM ** -0.5)


def _inproj(x, scale1, shift1, norm1_g, w_in, k_norm_g, idx_ln_w, idx_ln_b, q_norm_g,
            tabs, interpret):
    B, S, D = x.shape
    tm = min(S, 256)
    cos_r, sin_r, cos_t, sin_t = tabs
    w_at = w_in[:, RW_COLS:]
    w_rw = w_in[:, :RW_COLS].astype(BF16)
    w_k = jnp.concatenate([w_at[:, AT_WIDTH:2 * AT_WIDTH], w_at[:, KI_OFF:KI_OFF + HEAD_DIM],
                           jnp.zeros((D, HEAD_DIM), F32)], axis=1).astype(BF16)
    w_t = jnp.concatenate([w_at[:, 0:AT_WIDTH], w_at[:, 2 * AT_WIDTH:3 * AT_WIDTH],
                           w_at[:, 3 * AT_WIDTH:KI_OFF], w_at[:, WI_OFF:WI_OFF + IDX_HEADS],
                           jnp.zeros((D, 8 - IDX_HEADS), F32)], axis=1).T.astype(BF16)
    kg = jnp.tile(k_norm_g, 2).reshape(1, LANES)
    zpad = jnp.zeros((HEAD_DIM,), F32)
    iw = jnp.concatenate([idx_ln_w, zpad]).reshape(1, LANES)
    ib = jnp.concatenate([idx_ln_b, zpad]).reshape(1, LANES)
    qg = q_norm_g.reshape(HEAD_DIM, 1)
    li = np.arange(LANES)
    gsum = jnp.asarray((li[:, None] // HEAD_DIM == li[None, :] // HEAD_DIM).astype(np.float32))

    full = lambda shape: pl.BlockSpec(shape, lambda b, i: (0,) * len(shape))
    return pl.pallas_call(
        _inproj_kernel,
        grid=(B, S // tm),
        in_specs=[pl.BlockSpec((1, tm, D), lambda b, i: (b, i, 0)),
                  pl.BlockSpec((1, 1, D), lambda b, i: (b, 0, 0)),
                  pl.BlockSpec((1, 1, D), lambda b, i: (b, 0, 0)),
                  full((1, D)), full((D, RW_COLS)), full((D, PK_COLS)), full((PT_ROWS, D)),
                  full((1, LANES)), full((1, LANES)), full((1, LANES)), full((HEAD_DIM, 1)),
                  full((LANES, LANES)),
                  pl.BlockSpec((1, tm, LANES), lambda b, i: (b, i, 0)),
                  pl.BlockSpec((1, tm, LANES), lambda b, i: (b, i, 0)),
                  pl.BlockSpec((1, HALF, tm), lambda b, i: (b, 0, i)),
                  pl.BlockSpec((1, HALF, tm), lambda b, i: (b, 0, i))],
        out_specs=[pl.BlockSpec((1, tm, RW_COLS), lambda b, i: (b, i, 0)),
                   pl.BlockSpec((1, tm, AT_WIDTH), lambda b, i: (b, i, 0)),
                   pl.BlockSpec((1, tm, LANES), lambda b, i: (b, i, 0)),
                   pl.BlockSpec((1, AT_HEADS, LANES, tm), lambda b, i: (b, 0, 0, i)),
                   pl.BlockSpec((1, tm // QBLK, AT_WIDTH, QBLK), lambda b, i: (b, i, 0, 0)),
                   pl.BlockSpec((1, IDX_HEADS, LANES, tm), lambda b, i: (b, 0, 0, i)),
                   pl.BlockSpec((1, 8, tm), lambda b, i: (b, 0, i))],
        out_shape=[jax.ShapeDtypeStruct((B, S, RW_COLS), F32),
                   jax.ShapeDtypeStruct((B, S, AT_WIDTH), BF16),
                   jax.ShapeDtypeStruct((B, S, LANES), BF16),
                   jax.ShapeDtypeStruct((B, AT_HEADS, LANES, S), BF16),
                   jax.ShapeDtypeStruct((B, S // QBLK, AT_WIDTH, QBLK), BF16),
                   jax.ShapeDtypeStruct((B, IDX_HEADS, LANES, S), BF16),
                   jax.ShapeDtypeStruct((B, 8, S), F32)],
        compiler_params=_cparams(("arbitrary", "arbitrary")),
        name="inproj", interpret=interpret,
    )(x, scale1, shift1, norm1_g.reshape(1, D), w_rw, w_k, w_t, kg, iw, ib, qg, gsum,
      cos_r, sin_r, cos_t, sin_t)


def _rwkv_kernel(p_ref, mu_ref, w0_ref, w2_ref, a0_ref, a2_ref, g2_ref, kk_ref, ka_ref, rk_ref,
                 lnw_ref, lnb_ref, gsum_ref, y_ref, s_ref, prev_ref, yt_ref):
    C = RW_CHUNK
    W = RW_WIDTH
    nb = p_ref.shape[0]

    @pl.when(pl.program_id(1) == 0)
    def _():
        s_ref[...] = jnp.zeros_like(s_ref)
        prev_ref[...] = jnp.zeros_like(prev_ref)

    row = lax.broadcasted_iota(I32, (C, 1), 0)
    gsum = gsum_ref[...]

    def prepare(bi):
        p = p_ref[bi]
        pprev = jnp.where(row == 0, prev_ref[bi:bi + 1, :], pltpu.roll(p, 1, 0))
        prev_ref[bi:bi + 1, :] = p[C - 1:C]
        ps = p + (pprev - p) * mu_ref[...]
        r, k, v = ps[:, 0:W], ps[:, W:2 * W], ps[:, 2 * W:3 * W]
        o = 3 * W
        wl = ps[:, o:o + RW_LORA_W]
        al = ps[:, o + RW_LORA_W:o + RW_LORA_W + RW_LORA_A]
        gl = ps[:, o + RW_LORA_W + RW_LORA_A:]
        z = w0_ref[...] + _dot(jnp.tanh(wl).astype(BF16), w2_ref[...])
        nz = -z
        softplus = jnp.maximum(nz, 0.0) + jnp.log(1.0 + jnp.exp(-jnp.abs(nz)))
        logw = -jnp.exp(-softplus - 0.5)
        a = _sigmoid(a0_ref[...] + _dot(al.astype(BF16), a2_ref[...]))
        g = _dot(_sigmoid(gl).astype(BF16), g2_ref[...])
        kk = k * kk_ref[...]
        ss = _dot((kk * kk).astype(BF16), gsum)
        kk = kk * (1.0 / jnp.maximum(jnp.sqrt(ss), 1e-12))
        k2 = k * (1.0 + (a - 1.0) * ka_ref[...])
        bb = kk * a
        cw = logw
        sh = 1
        while sh < C:
            cw = cw + jnp.where(row >= sh, pltpu.roll(cw, sh, 0), 0.0)
            sh *= 2
        cw_last = cw[C - 1:C]
        e_neg = jnp.exp(-cw)
        e_end = jnp.exp(cw_last - cw)
        return dict(r=r, v=v, k2=k2, g=g, rw=r * jnp.exp(cw), kkp=kk * jnp.exp(cw - logw),
                    bw=bb * e_neg, kw=k2 * e_neg, bend=bb * e_end, kend=k2 * e_end,
                    wc=jnp.exp(cw_last), vt=v.T.astype(BF16))

    pre = [prepare(bi) for bi in range(nb)]

    ri = lax.broadcasted_iota(I32, (C, C), 0)
    ci = lax.broadcasted_iota(I32, (C, C), 1)
    strict = ri < ci
    incl = ri <= ci
    incl2 = jnp.concatenate([incl, incl], axis=0)
    lane_half = lax.broadcasted_iota(I32, (1, LANES), 1) // HEAD_DIM

    chains = [(bi, h) for bi in range(nb) for h in range(RW_HEADS)]
    pair = lambda h: slice((h // 2) * LANES, (h // 2 + 1) * LANES)
    own = lambda h: lane_half == (h % 2)
    stack = lambda bi, top, bot, h: jnp.concatenate(
        [pre[bi][top][:, pair(h)], pre[bi][bot][:, pair(h)]], axis=0)
    lh = {(bi, q): stack(bi, 'kkp', 'rw', 2 * q).astype(BF16)
          for bi in range(nb) for q in range(RW_HEADS // 2)}
    rh = [jnp.where(own(h), stack(bi, 'bw', 'kw', h), 0.0).astype(BF16) for bi, h in chains]
    aat = [_dot_nt(rh[n], lh[bi, h // 2]) for n, (bi, h) in enumerate(chains)]
    s_old = [s_ref[bi * RW_HEADS + h] for bi, h in chains]
    sl = [_dot_nt(s_old[n].astype(BF16), lh[bi, h // 2]) for n, (bi, h) in enumerate(chains)]
    vt = [pre[bi]['vt'][h * HEAD_DIM:(h + 1) * HEAD_DIM] for bi, h in chains]
    ids = range(len(chains))
    akt = [jnp.where(strict, aat[n][C:, :C], 0.0).astype(BF16) for n in ids]
    m = [jnp.where(strict, aat[n][:C, :C], 0.0).astype(BF16) for n in ids]
    xs = [-(sl[n][:, :C] + _dot(vt[n], akt[n])) for n in ids]
    xs = [xs[n] - _dot_split(xs[n], m[n]) for n in ids]
    lvl = 2
    while lvl < C:
        m = [_dot(m[n], m[n]).astype(BF16) for n in ids]
        xs = [xs[n] + _dot_split(xs[n], m[n]) for n in ids]
        lvl *= 2
    zt = [jnp.concatenate([xs[n].astype(BF16), vt[n]], axis=1) for n in ids]
    for n in ids:
        ymat = jnp.where(incl2, aat[n][:, C:], 0.0).astype(BF16)
        yt_ref[n * HEAD_DIM:(n + 1) * HEAD_DIM, :] = sl[n][:, C:] + _dot(zt[n], ymat)
    for n, (bi, h) in enumerate(chains):
        endz = jnp.where(own(h), stack(bi, 'bend', 'kend', h), 0.0).astype(BF16)
        s_ref[n] = s_old[n] * pre[bi]['wc'][:, pair(h)] + _dot(zt[n], endz)

    lnw = lnw_ref[...].reshape(RW_HEADS, HEAD_DIM, 1)
    lnb = lnb_ref[...].reshape(RW_HEADS, HEAD_DIM, 1)
    for bi in range(nb):
        q = pre[bi]
        yt = yt_ref[bi * W:(bi + 1) * W, :].reshape(RW_HEADS, HEAD_DIM, C)
        mean = jnp.mean(yt, axis=1, keepdims=True)
        yc = yt - mean
        var = jnp.mean(yc * yc, axis=1, keepdims=True)
        yn = yc * lax.rsqrt(var + RW_GN_EPS) * lnw + lnb
        y = yn.reshape(W, C).T
        bonus = _dot((q['r'] * q['k2'] * rk_ref[...]).astype(BF16), gsum) * q['v']
        y_ref[bi] = ((y + bonus) * q['g']).astype(BF16)


def _rwkv(p_rw, rw_mu, rw_w0, rw_w2, rw_a0, rw_a2, rw_g2, rw_k_k, rw_k_a, rw_r_k, rw_ln_w, rw_ln_b,
          interpret):
    B, S, _ = p_rw.shape
    C, W = RW_CHUNK, RW_WIDTH
    li = np.arange(W)
    gsum = jnp.asarray((li[:, None] // HEAD_DIM == li[None, :] // HEAD_DIM).astype(np.float32)).astype(BF16)
    row = lambda a: a.reshape(1, -1)
    full = lambda shape: pl.BlockSpec(shape, lambda b, i: (0,) * len(shape))
    nb = RW_BATCH if B % RW_BATCH == 0 else 1
    return pl.pallas_call(
        _rwkv_kernel,
        grid=(B // nb, S // C),
        in_specs=[pl.BlockSpec((nb, C, RW_COLS), lambda b, i: (b, i, 0)),
                  full((1, RW_COLS)), full((1, W)), full((RW_LORA_W, W)), full((1, W)),
                  full((RW_LORA_A, W)), full((RW_LORA_G, W)), full((1, W)), full((1, W)),
                  full((1, W)), full((W, 1)), full((W, 1)), full((W, W))],
        out_specs=pl.BlockSpec((nb, C, W), lambda b, i: (b, i, 0)),
        out_shape=jax.ShapeDtypeStruct((B, S, W), BF16),
        scratch_shapes=[pltpu.VMEM((nb * RW_HEADS, HEAD_DIM, LANES), F32),
                        pltpu.VMEM((nb, RW_COLS), F32),
                        pltpu.VMEM((nb * W, C), F32)],
        compiler_params=_cparams(("arbitrary", "arbitrary")),
        name="rwkv", interpret=interpret,
    )(p_rw, row(rw_mu), row(rw_w0), rw_w2.astype(BF16), row(rw_a0), rw_a2.astype(BF16),
      rw_g2.astype(BF16), row(rw_k_k), row(rw_k_a), row(rw_r_k), rw_ln_w.reshape(W, 1),
      rw_ln_b.reshape(W, 1), gsum)


def _dsa_kernel(topk, nbits, k_ref, vt_ref, ki_ref, qt_ref, qit_ref, wit_ref, og_ref, o_ref,
                key_s, acc_s, m_s, l_s, thr_s):
    j = pl.program_id(1)
    nkb = j + 1
    lane = lax.broadcasted_iota(I32, (QBLK, QBLK), 1)
    sub = lax.broadcasted_iota(I32, (QBLK, QBLK), 0)
    qpos = j * QBLK + lane
    wit = wit_ref[0]

    def score_blocks(i, carry):
        kbs = [i * SCORE_UNROLL + u for u in range(SCORE_UNROLL)]
        kib = [ki_ref[0, pl.ds(pl.multiple_of(kb * QBLK, QBLK), QBLK), :] for kb in kbs]
        lg = [[_dot(kib[u], qit_ref[0, hh]) for hh in range(IDX_HEADS)] for u in range(SCORE_UNROLL)]
        for u, kb in enumerate(kbs):
            s = wit[0:1, :] * jnp.maximum(lg[u][0], 0.0)
            for hh in range(1, IDX_HEADS):
                s = s + wit[hh:hh + 1, :] * jnp.maximum(lg[u][hh], 0.0)
            s = jnp.where(s == 0.0, 0.0, s)
            bits = pltpu.bitcast(s, I32)
            skey = jnp.where(bits < 0, bits ^ 0x7FFFFFFF, bits)
            key_s[kb] = jnp.where(kb * QBLK + sub <= qpos, skey, INT_MIN)
        return carry

    lax.fori_loop(0, pl.cdiv(nkb, SCORE_UNROLL), score_blocks, 0)

    @pl.when(nkb * QBLK <= topk)
    def _():
        thr_s[0:1, :] = jnp.full((1, QBLK), INT_MIN, I32)
        thr_s[1:2, :] = jnp.zeros((1, QBLK), I32)

    @pl.when(nkb * QBLK > topk)
    def _():
        n_done = pl.cdiv(nkb, SCORE_UNROLL) * SCORE_UNROLL
        n_cnt = pl.cdiv(nkb, COUNT_UNROLL)

        def fill(kb, carry):
            key_s[kb] = jnp.full((QBLK, QBLK), INT_MIN, I32)
            return carry

        lax.fori_loop(n_done, n_cnt * COUNT_UNROLL, fill, 0)

        def count(preds):
            def body(i, accs):
                accs = list(accs)
                for u in range(COUNT_UNROLL):
                    kb = i * COUNT_UNROLL + u
                    ky = key_s[kb]
                    for n, pred in enumerate(preds):
                        hit = pred(ky, kb * QBLK + sub).astype(I32)
                        accs[n] = accs[n] + jnp.sum(hit.reshape(QBLK // 8, 8, QBLK), axis=0)
                return tuple(accs)
            accs = lax.fori_loop(0, n_cnt, body, tuple(jnp.zeros((8, QBLK), I32) for _ in preds))
            return [jnp.sum(a, axis=0, keepdims=True) for a in accs]

        c0, = count([lambda ky, ix: ky >= 0])
        t0 = jnp.where(c0 >= topk, 0, INT_MIN).astype(I32)

        def bit_step(i, t):
            cand = t | jnp.left_shift(jnp.int32(1), 30 - i)
            c, = count([lambda ky, ix: ky >= cand])
            return jnp.where(c >= topk, cand, t)

        thr = lax.fori_loop(0, 31, bit_step, t0)
        n_gt, n_eq = count([lambda ky, ix: ky > thr, lambda ky, ix: ky == thr])
        need = topk - n_gt
        thr_s[0:1, :] = thr
        thr_s[1:2, :] = jnp.full((1, QBLK), 2 ** nbits, I32)

        @pl.when(jnp.max(jnp.abs(n_eq - need)) > 0)
        def _():
            def idx_step(i, mm):
                cand = mm | jnp.left_shift(jnp.int32(1), nbits - 1 - i)
                c, = count([lambda ky, ix: (ky == thr) & (ix < cand)])
                return jnp.where(c < need, cand, mm)

            thr_s[1:2, :] = lax.fori_loop(0, nbits, idx_step, jnp.zeros((1, QBLK), I32))

    thr = thr_s[0:1, :]
    mm = thr_s[1:2, :]
    m_s[...] = jnp.full_like(m_s, NEG_BIG)
    l_s[...] = jnp.zeros_like(l_s)
    acc_s[...] = jnp.zeros_like(acc_s)

    qt2 = [jnp.concatenate([qt_ref[0, 2 * q], qt_ref[0, 2 * q + 1]], axis=1)
           for q in range(AT_HEADS // 2)]

    def attn_blocks(i, carry):
        kbs = [i * ATTN_UNROLL + u for u in range(ATTN_UNROLL)]
        sel = []
        for kb in kbs:
            skey = key_s[kb]
            kidx = kb * QBLK + sub
            sel.append((kidx <= qpos) & ((skey > thr) | ((skey == thr) & (kidx <= mm))))
        sel = jnp.concatenate(sel, axis=0)
        kblk = [k_ref[0, pl.ds(pl.multiple_of(kb * QBLK, QBLK), QBLK), :] for kb in kbs]
        vtb = jnp.concatenate([vt_ref[0, kb] for kb in kbs], axis=1)
        s2 = [[_dot(kblk[u][:, q * LANES:(q + 1) * LANES], qt2[q]) for u in range(ATTN_UNROLL)]
              for q in range(AT_HEADS // 2)]
        pexp, alpha = [], []
        for hh in range(AT_HEADS):
            half = slice((hh % 2) * QBLK, (hh % 2 + 1) * QBLK)
            s = jnp.concatenate([s2[hh // 2][u][:, half] for u in range(ATTN_UNROLL)], axis=0)
            s = jnp.where(sel, s, NEG_BIG)
            m_old = m_s[hh:hh + 1, :]
            m_new = jnp.maximum(m_old, jnp.max(s, axis=0, keepdims=True))
            pe = jnp.exp(s - m_new)
            al = jnp.exp(m_old - m_new)
            l_s[hh:hh + 1, :] = al * l_s[hh:hh + 1, :] + jnp.sum(pe, axis=0, keepdims=True)
            m_s[hh:hh + 1, :] = m_new
            pexp.append(pe.astype(BF16))
            alpha.append(al)
        for hh in range(AT_HEADS):
            hs = slice(hh * HEAD_DIM, (hh + 1) * HEAD_DIM)
            acc_s[hs, :] = alpha[hh] * acc_s[hs, :] + _dot(vtb[hs, :], pexp[hh])
        return carry

    lax.fori_loop(0, pl.cdiv(nkb, ATTN_UNROLL), attn_blocks, 0)

    for hh in range(AT_HEADS):
        hs = slice(hh * HEAD_DIM, (hh + 1) * HEAD_DIM)
        oh = acc_s[hs, :] * (1.0 / l_s[hh:hh + 1, :])
        ms = jnp.mean(oh * oh, axis=0, keepdims=True)
        acc_s[hs, :] = oh * lax.rsqrt(ms + NORM_EPS) * og_ref[hs, :]
    o_ref[0] = acc_s[...].T.astype(BF16)


def _dsa(k, vt, ki, qt, qit, wit, at_out_g, interpret):
    B, S, _ = k.shape
    nq = S // QBLK
    topk = min(IDX_TOPK_MAX, S // 4)
    nbits = int(np.log2(S))
    assert 2 ** nbits == S and nq % COUNT_UNROLL == 0 and SCORE_UNROLL == ATTN_UNROLL
    assert COUNT_UNROLL % SCORE_UNROLL == 0
    return pl.pallas_call(
        functools.partial(_dsa_kernel, topk, nbits),
        grid=(B, nq),
        in_specs=[pl.BlockSpec((1, S, AT_WIDTH), lambda b, j: (b, 0, 0)),
                  pl.BlockSpec((1, nq, AT_WIDTH, QBLK), lambda b, j: (b, 0, 0, 0)),
                  pl.BlockSpec((1, S, LANES), lambda b, j: (b, 0, 0)),
                  pl.BlockSpec((1, AT_HEADS, LANES, QBLK), lambda b, j: (b, 0, 0, j)),
                  pl.BlockSpec((1, IDX_HEADS, LANES, QBLK), lambda b, j: (b, 0, 0, j)),
                  pl.BlockSpec((1, 8, QBLK), lambda b, j: (b, 0, j)),
                  pl.BlockSpec((AT_WIDTH, 1), lambda b, j: (0, 0))],
        out_specs=pl.BlockSpec((1, QBLK, AT_WIDTH), lambda b, j: (b, j, 0)),
        out_shape=jax.ShapeDtypeStruct((B, S, AT_WIDTH), BF16),
        scratch_shapes=[pltpu.VMEM((nq, QBLK, QBLK), I32),
                        pltpu.VMEM((AT_WIDTH, QBLK), F32),
                        pltpu.VMEM((AT_HEADS, QBLK), F32),
                        pltpu.VMEM((AT_HEADS, QBLK), F32),
                        pltpu.VMEM((8, QBLK), I32)],
        compiler_params=_cparams(("arbitrary", "arbitrary")),
        name="dsa", interpret=interpret,
    )(k, vt, ki, qt, qit, wit, at_out_g.reshape(AT_WIDTH, 1))


def _first_max(vals, idx, axis, sentinel):
    m = jnp.max(vals, axis=axis, keepdims=True)
    return m, jnp.min(jnp.where(vals == m, idx, sentinel), axis=axis, keepdims=True)


def _route_cols(logits_t, bias_col):
    E, tm = logits_t.shape
    pg = E // N_GROUPS
    scores = _sigmoid(logits_t)
    biased = scores + bias_col
    b3 = biased.reshape(N_GROUPS, pg, tm)
    r3 = lax.broadcasted_iota(I32, (N_GROUPS, pg, tm), 1)
    m1, first = _first_max(b3, r3, 1, pg)
    m2 = jnp.max(jnp.where(r3 == first, -jnp.inf, b3), axis=1, keepdims=True)
    cur = (m1 + m2).reshape(N_GROUPS, tm)
    grow = lax.broadcasted_iota(I32, (N_GROUPS, tm), 0)
    gsel = jnp.zeros((N_GROUPS, tm), F32)
    for _ in range(TOPK_GROUPS):
        _, gi = _first_max(cur, grow, 0, N_GROUPS)
        hit = grow == gi
        gsel = jnp.where(hit, 1.0, gsel)
        cur = jnp.where(hit, -jnp.inf, cur)
    gmask = jnp.broadcast_to(gsel.reshape(N_GROUPS, 1, tm), (N_GROUPS, pg, tm)).reshape(E, tm)
    cur = jnp.where(gmask > 0.0, biased, -jnp.inf)
    row = lax.broadcasted_iota(I32, (E, tm), 0)
    onehot = jnp.zeros((E, tm), F32)
    eids, gws = [], []
    for _ in range(TOP_K):
        _, ei = _first_max(cur, row, 0, E)
        hit = row == ei
        eids.append(ei)
        gws.append(jnp.sum(jnp.where(hit, scores, 0.0), axis=0, keepdims=True))
        onehot = jnp.where(hit, 1.0, onehot)
        cur = jnp.where(hit, -jnp.inf, cur)
    eid = jnp.concatenate(eids, axis=0)
    gw = jnp.concatenate(gws, axis=0)
    gw = gw * (ROUTED_SCALE / jnp.sum(gw, axis=0, keepdims=True))
    return eid, gw, onehot


def _post_kernel(x_ref, yrw_ref, yat_ref, g1_ref, sc_ref, sh_ref, g2_ref, ng_ref, wo_ref, rwt_ref,
                 rb_ref, s1_ref, s3_ref, s2_ref, base_ref, h2_ref, eid_ref, gw_ref, rank_ref, cnt_ref):
    W = RW_WIDTH
    tm = x_ref.shape[1]
    E = rwt_ref.shape[0]

    @pl.when((pl.program_id(0) == 0) & (pl.program_id(1) == 0))
    def _():
        cnt_ref[...] = jnp.zeros_like(cnt_ref)

    mix = _dot(yrw_ref[0], wo_ref[0:W, :]) + _dot(yat_ref[0], wo_ref[W:, :])
    x1 = x_ref[0] + g1_ref[0] * mix
    ms = jnp.mean(x1 * x1, axis=-1, keepdims=True)
    h2 = x1 * lax.rsqrt(ms + NORM_EPS) * ng_ref[...] * (1.0 + sc_ref[0]) + sh_ref[0]
    hb = h2.astype(BF16)
    h2_ref[0] = _pack_halves(h2)
    act =(_silu(_dot(hb, s1_ref[...])) * _dot(hb, s3_ref[...])).astype(BF16)
    base_ref[0] = x1 + g2_ref[0] * _dot(act, s2_ref[...])

    logits_t = lax.dot_general(rwt_ref[...], h2, (((1,), (1,)), ((), ())), precision=HIGHEST,
                               preferred_element_type=F32)
    eid, gw, onehot = _route_cols(logits_t, rb_ref[...])
    eid_ref[0] = eid
    gw_ref[0] = gw
    ti = lax.broadcasted_iota(I32, (tm, tm), 0)
    tj = lax.broadcasted_iota(I32, (tm, tm), 1)
    before = _dot(onehot.astype(BF16), (ti < tj).astype(BF16)) + cnt_ref[:, 0:1]
    row = lax.broadcasted_iota(I32, (E, tm), 0)
    ranks = [jnp.sum(jnp.where(row == eid[kk:kk + 1, :], before, 0.0), axis=0, keepdims=True)
             for kk in range(TOP_K)]
    rank_ref[0] = jnp.concatenate(ranks, axis=0).astype(I32)
    cnt_ref[...] = cnt_ref[...] + jnp.sum(onehot, axis=1, keepdims=True)


def _post(x, y_rw, y_at, gate1, scale2, shift2, gate2, norm2_g, w_out, router_w, router_bias,
          sw1, sw3, sw2, interpret):
    B, S, D = x.shape
    tm = min(S, 256)
    E = router_w.shape[1]
    sd = sw1.shape[1]
    full = lambda shape: pl.BlockSpec(shape, lambda b, i: (0,) * len(shape))
    tok = lambda w: pl.BlockSpec((1, tm, w), lambda b, i: (b, i, 0))
    per_b = pl.BlockSpec((1, 1, D), lambda b, i: (b, 0, 0))
    col8 = pl.BlockSpec((1, TOP_K, tm), lambda b, i: (b, 0, i))
    return pl.pallas_call(
        _post_kernel,
        grid=(B, S // tm),
        in_specs=[tok(D), tok(RW_WIDTH), tok(AT_WIDTH), per_b, per_b, per_b, per_b, full((1, D)),
                  full((D, D)), full((E, D)), full((E, 1)), full((D, sd)), full((D, sd)),
                  full((sd, D))],
        out_specs=[tok(D), tok(D // 2), col8, col8, col8, full((E, LANES))],
        out_shape=[jax.ShapeDtypeStruct((B, S, D), F32),
                   jax.ShapeDtypeStruct((B, S, D // 2), I32),
                   jax.ShapeDtypeStruct((B, TOP_K, S), I32),
                   jax.ShapeDtypeStruct((B, TOP_K, S), F32),
                   jax.ShapeDtypeStruct((B, TOP_K, S), I32),
                   jax.ShapeDtypeStruct((E, LANES), F32)],
        compiler_params=_cparams(("arbitrary", "arbitrary")),
        name="post", interpret=interpret,
    )(x, y_rw, y_at, gate1, scale2, shift2, gate2, norm2_g.reshape(1, D), w_out.astype(BF16),
      router_w.T, router_bias.reshape(E, 1), sw1.astype(BF16), sw3.astype(BF16), sw2.astype(BF16))


def _expert_kernel(be_ref, nv_ref, nu_ref, xs_ref, w1_ref, w3_ref, w2_ref, o_ref, w13_s, w2_s):
    i = pl.program_id(0)
    blk, _, hw = xs_ref.shape
    F = w1_ref.shape[2]
    used = i < nu_ref[0]

    @pl.when(used & ((i == 0) | (be_ref[i] != be_ref[jnp.maximum(i - 1, 0)])))
    def _():
        w13_s[:, :F] = w1_ref[0].astype(BF16)
        w13_s[:, F:] = w3_ref[0].astype(BF16)
        w2_s[...] = w2_ref[0].astype(BF16)

    @pl.when(used)
    def _():
        live = lax.broadcasted_iota(I32, (blk, 1), 0) < nv_ref[i]
        x_lo, x_hi = _unpack_halves(jnp.where(live, xs_ref[:, 0, :], 0))
        h13 =(_dot(x_lo.astype(BF16), w13_s[:hw, :]) + _dot(x_hi.astype(BF16), w13_s[hw:, :]))
        act = (_silu(h13[:, :F]) * h13[:, F:]).astype(BF16)
        o_ref[:, 0, :] = _pack_halves(_dot(act, w2_s[...]))


def _experts(xs, block_e, block_rows, n_used, w1, w3, w2, interpret):
    P, _, hw = xs.shape
    E, D, F = w1.shape
    nb = P // EXP_BLK
    blk = lambda i, nu: jnp.minimum(i, nu[0] - 1)
    grid_spec = pltpu.PrefetchScalarGridSpec(
        num_scalar_prefetch=3,
        grid=(nb,),
        in_specs=[pl.BlockSpec((EXP_BLK, 1, hw), lambda i, be, nv, nu: (blk(i, nu), 0, 0)),
                  pl.BlockSpec((1, D, F), lambda i, be, nv, nu: (be[blk(i, nu)], 0, 0)),
                  pl.BlockSpec((1, D, F), lambda i, be, nv, nu: (be[blk(i, nu)], 0, 0)),
                  pl.BlockSpec((1, F, D), lambda i, be, nv, nu: (be[blk(i, nu)], 0, 0))],
        out_specs=pl.BlockSpec((EXP_BLK, 1, hw), lambda i, be, nv, nu: (blk(i, nu), 0, 0)),
        scratch_shapes=[pltpu.VMEM((D, 2 * F), BF16), pltpu.VMEM((F, D), BF16)],
    )
    return pl.pallas_call(
        _expert_kernel,
        grid_spec=grid_spec,
        out_shape=jax.ShapeDtypeStruct((P, 1, hw), I32),
        compiler_params=_cparams(("arbitrary",)),
        name="experts", interpret=interpret,
    )(block_e, block_rows, n_used, xs, w1, w3, w2)


def _row_out(tile, t, slots, slot, sem):
    return pltpu.make_async_copy(tile.at[pl.ds(t, 1)], slots.at[slot], sem)


def _row_in(slots, slot, tile, t, sem):
    return pltpu.make_async_copy(slots.at[slot], tile.at[pl.ds(t, 1)], sem)


def _dispatch_kernel(dest_ref, h2_ref, xs_ref, sem):
    td = h2_ref.shape[0]

    def issue(t, carry):
        for kk in range(TOP_K):
            _row_out(h2_ref, t, xs_ref, dest_ref[0, 0, kk * td + t], sem).start(priority=kk % 2)
        return carry

    def drain(t, carry):
        for kk in range(TOP_K):
            _row_out(h2_ref, t, xs_ref, dest_ref[0, 0, kk * td + t], sem).wait()
        return carry

    lax.fori_loop(0, td, issue, 0)
    lax.fori_loop(0, td, drain, 0)


def _dispatch(h2, dest_tiles, n_slots, interpret):
    T, D = h2.shape
    nt, _, n = dest_tiles.shape
    td = n // TOP_K
    return pl.pallas_call(
        _dispatch_kernel,
        grid=(nt,),
        in_specs=[pl.BlockSpec((1, 1, n), lambda i: (i, 0, 0), memory_space=pltpu.SMEM),
                  pl.BlockSpec((td, D), lambda i: (i, 0))],
        out_specs=pl.BlockSpec(memory_space=pl.ANY),
        out_shape=jax.ShapeDtypeStruct((n_slots, 1, D), h2.dtype),
        scratch_shapes=[pltpu.SemaphoreType.DMA(())],
        compiler_params=_cparams(("arbitrary",)),
        name="dispatch", interpret=interpret,
    )(dest_tiles, h2)


def _combine_kernel(dest_ref, base_ref, g2_ref, gw_ref, ys_ref, o_ref, buf, sem):
    td = base_ref.shape[0]

    def issue(t, carry):
        for kk in range(TOP_K):
            _row_in(ys_ref, dest_ref[0, 0, kk * td + t], buf.at[kk], t, sem).start(priority=kk % 2)
        return carry

    def drain(t, carry):
        for kk in range(TOP_K):
            _row_in(ys_ref, dest_ref[0, 0, kk * td + t], buf.at[kk], t, sem).wait()
        return carry

    lax.fori_loop(0, td, issue, 0)
    lax.fori_loop(0, td, drain, 0)
    gw = gw_ref[...]
    acc_lo, acc_hi = _unpack_halves(buf[0])
    acc_lo, acc_hi = gw[:, 0:1] * acc_lo, gw[:, 0:1] * acc_hi
    for kk in range(1, TOP_K):
        y_lo, y_hi = _unpack_halves(buf[kk])
        acc_lo = acc_lo + gw[:, kk:kk + 1] * y_lo
        acc_hi = acc_hi + gw[:, kk:kk + 1] * y_hi
    o_ref[...] = base_ref[...] + g2_ref[0] * jnp.concatenate([acc_lo, acc_hi], axis=1)


def _combine(base, gate2, gw_tok, ys, dest_tiles, tiles_per_batch, interpret):
    T, D = base.shape
    nt, _, n = dest_tiles.shape
    td = n // TOP_K
    return pl.pallas_call(
        _combine_kernel,
        grid=(nt,),
        in_specs=[pl.BlockSpec((1, 1, n), lambda i: (i, 0, 0), memory_space=pltpu.SMEM),
                  pl.BlockSpec((td, D), lambda i: (i, 0)),
                  pl.BlockSpec((1, 1, D), lambda i: (i // tiles_per_batch, 0, 0)),
                  pl.BlockSpec((td, TOP_K), lambda i: (i, 0)),
                  pl.BlockSpec(memory_space=pl.ANY)],
        out_specs=pl.BlockSpec((td, D), lambda i: (i, 0)),
        out_shape=jax.ShapeDtypeStruct((T, D), F32),
        scratch_shapes=[pltpu.VMEM((TOP_K, td, D // 2), I32), pltpu.SemaphoreType.DMA(())],
        compiler_params=_cparams(("arbitrary",)),
        name="combine", interpret=interpret,
    )(dest_tiles, base, gate2, gw_tok, ys)


def _slots_kernel(eid_ref, rank_ref, pstart_ref, dest_ref):
    td = eid_ref.shape[2]
    E = pstart_ref.shape[0]
    row = lax.broadcasted_iota(I32, (E, td), 0)
    pstart = pstart_ref[...]
    eid = eid_ref[0]
    for kk in range(TOP_K):
        base = jnp.sum(jnp.where(row == eid[kk:kk + 1, :], pstart, 0), axis=0, keepdims=True)
        dest_ref[0, :, kk * td:(kk + 1) * td] = base + rank_ref[0, kk:kk + 1, :]


def _slot_plan(counts, eid_t, rank_t, td, interpret):
    B, _, S = eid_t.shape
    E = counts.shape[0]
    padded = (counts + EXP_BLK - 1) // EXP_BLK * EXP_BLK
    pend = jnp.cumsum(padded)
    pstart = (pend - padded).astype(I32)
    nb = -(-(B * S * TOP_K + E * (EXP_BLK - 1)) // EXP_BLK)
    first_row = jnp.arange(nb, dtype=I32) * EXP_BLK
    block_e = jnp.sum(pend[None, :] <= first_row[:, None], axis=1)
    block_e = jnp.minimum(block_e, E - 1).astype(I32)
    block_rows = jnp.clip(pstart[block_e] + counts[block_e] - first_row, 0, EXP_BLK).astype(I32)
    nt = S // td
    dest_tiles = pl.pallas_call(
        _slots_kernel,
        grid=(B, nt),
        in_specs=[pl.BlockSpec((1, TOP_K, td), lambda b, i: (b, 0, i)),
                  pl.BlockSpec((1, TOP_K, td), lambda b, i: (b, 0, i)),
                  pl.BlockSpec((E, 1), lambda b, i: (0, 0))],
        out_specs=pl.BlockSpec((1, 1, TOP_K * td), lambda b, i: (b * nt + i, 0, 0)),
        out_shape=jax.ShapeDtypeStruct((B * nt, 1, TOP_K * td), I32),
        compiler_params=_cparams(("arbitrary", "arbitrary")),
        name="slots", interpret=interpret,
    )(eid_t, rank_t, pstart.reshape(E, 1))
    n_used = (pend[-1:] // EXP_BLK).astype(I32)
    return block_e, block_rows, n_used, dest_tiles, nb * EXP_BLK


def _forward(x, c, positions, w_ada, b_ada, norm1_g, norm2_g, w_in, rw_mu, rw_w0, rw_w2,
             rw_a0, rw_a2, rw_g2, rw_k_k, rw_k_a, rw_r_k, rw_ln_w, rw_ln_b, q_norm_g,
             k_norm_g, idx_ln_w, idx_ln_b, at_out_g, w_out, router_w, router_bias,
             exp_w1, exp_w3, exp_w2, shared_w1, shared_w3, shared_w2, interpret=False):
    B, S, D = x.shape
    depth = w_ada.shape[0]
    for l in range(depth):
        mod = _mod(c, w_ada[l], b_ada[l], interpret)
        shift1, scale1, gate1, shift2, scale2, gate2 = [
            m.reshape(B, 1, D) for m in jnp.split(mod, 6, axis=-1)]
        tabs = _rope_tables(positions, interpret)
        p_rw, k, ki, qt, vt, qit, wit = _inproj(
            x, scale1, shift1, norm1_g[l], w_in[l], k_norm_g[l], idx_ln_w[l], idx_ln_b[l],
            q_norm_g[l], tabs, interpret)
        y_rw = _rwkv(p_rw, rw_mu[l], rw_w0[l], rw_w2[l], rw_a0[l], rw_a2[l], rw_g2[l], rw_k_k[l],
                     rw_k_a[l], rw_r_k[l], rw_ln_w[l], rw_ln_b[l], interpret)
        y_at = _dsa(k, vt, ki, qt, qit, wit, at_out_g[l], interpret)
        base, h2, eid_t, gw_t, rank_t, cnt = _post(
            x, y_rw, y_at, gate1, scale2, shift2, gate2, norm2_g[l], w_out[l], router_w[l],
            router_bias[l], shared_w1[l], shared_w3[l], shared_w2[l], interpret)
        T = B * S
        td = min(S, ROW_TILE)
        block_e, block_rows, n_used, dest_tiles, n_slots = _slot_plan(
            cnt[:, 0].astype(I32), eid_t, rank_t, td, interpret)
        xs = _dispatch(h2.reshape(T, D // 2), dest_tiles, n_slots, interpret)
        ys = _experts(xs, block_e, block_rows, n_used, exp_w1[l], exp_w3[l], exp_w2[l], interpret)
        gw_tok = gw_t.transpose(0, 2, 1).reshape(T, TOP_K)
        x = _combine(base.reshape(T, D), gate2, gw_tok, ys, dest_tiles, S // td,
                     interpret).reshape(B, S, D)
    return x


def kernel(x, c, positions, w_ada, b_ada, norm1_g, norm2_g, w_in, rw_mu, rw_w0, rw_w2, rw_a0, rw_a2, rw_g2, rw_k_k, rw_k_a, rw_r_k, rw_ln_w, rw_ln_b, q_norm_g, k_norm_g, idx_ln_w, idx_ln_b, at_out_g, w_out, router_w, router_bias, exp_w1, exp_w3, exp_w2, shared_w1, shared_w3, shared_w2):
    return _forward(x, c, positions, w_ada, b_ada, norm1_g, norm2_g, w_in, rw_mu, rw_w0, rw_w2,
                    rw_a0, rw_a2, rw_g2, rw_k_k, rw_k_a, rw_r_k, rw_ln_w, rw_ln_b, q_norm_g,
                    k_norm_g, idx_ln_w, idx_ln_b, at_out_g, w_out, router_w, router_bias,
                    exp_w1, exp_w3, exp_w2, shared_w1, shared_w3, shared_w2)
```

```python
import functools

import jax
import jax.numpy as jnp
import numpy as np
from jax import lax
from jax.experimental import pallas as pl
from jax.experimental.pallas import tpu as pltpu

F32 = jnp.float32
BF16 = jnp.bfloat16
I32 = jnp.int32
HIGHEST = lax.Precision.HIGHEST

LANES = 128
HEAD_DIM = 64
HALF = HEAD_DIM // 2
RW_HEADS = 8
RW_WIDTH = RW_HEADS * HEAD_DIM
AT_HEADS = 8
AT_WIDTH = AT_HEADS * HEAD_DIM
IDX_HEADS = 4
RW_LORA_W, RW_LORA_A, RW_LORA_G = 64, 64, 128
RW_COLS = 3 * RW_WIDTH + RW_LORA_W + RW_LORA_A + RW_LORA_G
KI_OFF = 3 * AT_WIDTH + IDX_HEADS * HEAD_DIM
WI_OFF = KI_OFF + HEAD_DIM
PT_ROWS = 2 * AT_WIDTH + IDX_HEADS * HEAD_DIM + 8
PK_COLS = AT_WIDTH + LANES
ROPE_THETA = 10000.0
NORM_EPS = 1e-6
LN_EPS = 1e-6
RW_GN_EPS = 64e-5
IDX_TOPK_MAX = 256
N_EXPERTS = 256
TOP_K = 8
N_GROUPS = 8
TOPK_GROUPS = 4
ROUTED_SCALE = 2.5
INT_MIN = -2 ** 31
PAIR_HI_MASK = -65536
NEG_BIG = -1e30

RW_CHUNK = 128
RW_BATCH = 2
QBLK = 128
SCORE_UNROLL = 4
ATTN_UNROLL = 4
COUNT_UNROLL = 4
EXP_BLK = 512
ROW_TILE = 256
VMEM_LIMIT = 48 * 1024 * 1024


def _cparams(sem):
    return pltpu.CompilerParams(dimension_semantics=sem, vmem_limit_bytes=VMEM_LIMIT)


def _sigmoid(x):
    return 1.0 / (1.0 + jnp.exp(-x))


def _silu(x):
    return x * _sigmoid(x)


def _dot(a, b):
    return jnp.dot(a, b, preferred_element_type=F32)


def _dot_split(a, b):
    hi = a.astype(BF16)
    lo = (a - hi.astype(F32)).astype(BF16)
    return _dot(hi, b) + _dot(lo, b)


def _pack_halves(x):
    w = x.shape[1] // 2
    lo = pltpu.bitcast(x[:, :w].astype(BF16).astype(F32), I32)
    hi = pltpu.bitcast(x[:, w:].astype(BF16).astype(F32), I32)
    return (hi & PAIR_HI_MASK) | lax.shift_right_logical(lo, 16)


def _unpack_halves(p):
    return pltpu.bitcast(p << 16, F32), pltpu.bitcast(p & PAIR_HI_MASK, F32)


def _dot_nt(a, b):
    return lax.dot_general(a, b, (((1,), (1,)), ((), ())), preferred_element_type=F32)


def _mod_kernel(c_ref, w_ref, b_ref, o_ref):
    c = c_ref[...]
    o_ref[...] = jnp.dot(_silu(c), w_ref[...], precision=HIGHEST,
                         preferred_element_type=F32) + b_ref[...]


def _mod(c, w_ada, b_ada, interpret):
    B, D = c.shape
    n = w_ada.shape[1] // D
    return pl.pallas_call(
        _mod_kernel,
        grid=(n,),
        in_specs=[pl.BlockSpec((B, D), lambda i: (0, 0)),
                  pl.BlockSpec((D, D), lambda i: (0, i)),
                  pl.BlockSpec((1, D), lambda i: (0, i))],
        out_specs=pl.BlockSpec((B, D), lambda i: (0, i)),
        out_shape=jax.ShapeDtypeStruct((B, n * D), F32),
        compiler_params=_cparams(("arbitrary",)),
        name="mod", interpret=interpret,
    )(c, w_ada, b_ada.reshape(1, -1))


def _rope_tab_kernel(pc_ref, pr_ref, cr_ref, sr_ref, ct_ref, st_ref):
    log_theta = float(np.log(ROPE_THETA))
    lane = lax.broadcasted_iota(I32, (1, LANES), 1)
    inv_r = jnp.exp((lane % HALF).astype(F32) * (-log_theta / HALF))
    ang = pc_ref[0].astype(F32) * inv_r
    cr_ref[0] = jnp.cos(ang)
    sr_ref[0] = jnp.where((lane % HEAD_DIM) < HALF, -jnp.sin(ang), jnp.sin(ang))
    sub = lax.broadcasted_iota(I32, (HALF, 1), 0)
    inv_c = jnp.exp(sub.astype(F32) * (-log_theta / HALF))
    ang_t = inv_c * pr_ref[0].astype(F32)
    ct_ref[0] = jnp.cos(ang_t)
    st_ref[0] = jnp.sin(ang_t)


def _rope_tables(positions, interpret):
    B, S = positions.shape
    ts = min(S, 512)
    return pl.pallas_call(
        _rope_tab_kernel,
        grid=(B, S // ts),
        in_specs=[pl.BlockSpec((1, ts, 1), lambda b, i: (b, i, 0)),
                  pl.BlockSpec((1, 1, ts), lambda b, i: (b, 0, i))],
        out_specs=[pl.BlockSpec((1, ts, LANES), lambda b, i: (b, i, 0)),
                   pl.BlockSpec((1, ts, LANES), lambda b, i: (b, i, 0)),
                   pl.BlockSpec((1, HALF, ts), lambda b, i: (b, 0, i)),
                   pl.BlockSpec((1, HALF, ts), lambda b, i: (b, 0, i))],
        out_shape=[jax.ShapeDtypeStruct((B, S, LANES), F32),
                   jax.ShapeDtypeStruct((B, S, LANES), F32),
                   jax.ShapeDtypeStruct((B, HALF, S), F32),
                   jax.ShapeDtypeStruct((B, HALF, S), F32)],
        compiler_params=_cparams(("arbitrary", "arbitrary")),
        name="rope_tab", interpret=interpret,
    )(positions.reshape(B, S, 1), positions.reshape(B, 1, S))


def _rope_rows(y, cos, sin_signed):
    lane = lax.broadcasted_iota(I32, (1, LANES), 1)
    partner = jnp.where((lane % HEAD_DIM) < HALF,
                        pltpu.roll(y, LANES - HALF, 1), pltpu.roll(y, HALF, 1))
    return y * cos + partner * sin_signed


def _rope_cols(y, cos_t, sin_t):
    x1, x2 = y[:, :HALF], y[:, HALF:]
    return jnp.concatenate([x1 * cos_t - x2 * sin_t, x2 * cos_t + x1 * sin_t], axis=1)


def _inproj_kernel(x_ref, sc_ref, sh_ref, g_ref, wrw_ref, wk_ref, wt_ref, kg_ref, iw_ref, ib_ref,
                   qg_ref, gsum_ref, cr_ref, sr_ref, ct_ref, st_ref,
                   prw_ref, k_ref, ki_ref, qt_ref, vt_ref, qit_ref, wit_ref):
    tm = x_ref.shape[1]
    x = x_ref[0]
    ms = jnp.mean(x * x, axis=-1, keepdims=True)
    h = x * lax.rsqrt(ms + NORM_EPS) * g_ref[...] * (1.0 + sc_ref[0]) + sh_ref[0]
    hb = h.astype(BF16)
    prw_ref[0] = _dot(hb, wrw_ref[...])
    pk = _dot(hb, wk_ref[...])
    pt = _dot_nt(wt_ref[...], hb)

    cos_r, sin_r = cr_ref[0], sr_ref[0]
    gsum = gsum_ref[...]
    inv_hd = 1.0 / HEAD_DIM
    for p in range(AT_WIDTH // LANES):
        xk = pk[:, p * LANES:(p + 1) * LANES]
        ss = jnp.dot(xk * xk, gsum, precision=HIGHEST, preferred_element_type=F32)
        y = xk * lax.rsqrt(ss * inv_hd + NORM_EPS) * kg_ref[...]
        k_ref[0, :, p * LANES:(p + 1) * LANES] = _rope_rows(y, cos_r, sin_r).astype(BF16)
    xi = pk[:, AT_WIDTH:AT_WIDTH + LANES]
    mu = jnp.dot(xi, gsum, precision=HIGHEST, preferred_element_type=F32) * inv_hd
    xc = xi - mu
    var = jnp.dot(xc * xc, gsum, precision=HIGHEST, preferred_element_type=F32) * inv_hd
    yi = xc * lax.rsqrt(var + LN_EPS) * iw_ref[...] + ib_ref[...]
    ki_ref[0] = _rope_rows(yi, cos_r, sin_r).astype(BF16)

    cos_t, sin_t = ct_ref[0][None], st_ref[0][None]
    xq = pt[0:AT_WIDTH].reshape(AT_HEADS, HEAD_DIM, tm)
    msq = jnp.mean(xq * xq, axis=1, keepdims=True)
    yq = xq * lax.rsqrt(msq + NORM_EPS) * qg_ref[...][None]
    yq = _rope_cols(yq, cos_t, sin_t) * (HEAD_DIM ** -0.5)
    zq = jnp.zeros((HEAD_DIM, tm), BF16)
    for hh in range(AT_HEADS):
        parts = [yq[hh].astype(BF16), zq] if hh % 2 == 0 else [zq, yq[hh].astype(BF16)]
        qt_ref[0, hh] = jnp.concatenate(parts, axis=0)
    vt = pt[AT_WIDTH:2 * AT_WIDTH].astype(BF16)
    for cblk in range(tm // QBLK):
        vt_ref[0, cblk] = vt[:, cblk * QBLK:(cblk + 1) * QBLK]
    xqi = pt[2 * AT_WIDTH:2 * AT_WIDTH + IDX_HEADS * HEAD_DIM].reshape(IDX_HEADS, HEAD_DIM, tm)
    yqi = _rope_cols(xqi, cos_t, sin_t)
    for hh in range(IDX_HEADS):
        qit_ref[0, hh] = jnp.concatenate([yqi[hh].astype(BF16), zq], axis=0)
    wit_ref[0] = pt[PT_ROWS - 8:PT_ROWS] * (IDX_HEADS ** -0.5 * HEAD_DIM ** -0.5)


def _inproj(x, scale1, shift1, norm1_g, w_in, k_norm_g, idx_ln_w, idx_ln_b, q_norm_g,
            tabs, interpret):
    B, S, D = x.shape
    tm = min(S, 256)
    cos_r, sin_r, cos_t, sin_t = tabs
    w_at = w_in[:, RW_COLS:]
    w_rw = w_in[:, :RW_COLS].astype(BF16)
    w_k = jnp.concatenate([w_at[:, AT_WIDTH:2 * AT_WIDTH], w_at[:, KI_OFF:KI_OFF + HEAD_DIM],
                           jnp.zeros((D, HEAD_DIM), F32)], axis=1).astype(BF16)
    w_t = jnp.concatenate([w_at[:, 0:AT_WIDTH], w_at[:, 2 * AT_WIDTH:3 * AT_WIDTH],
                           w_at[:, 3 * AT_WIDTH:KI_OFF], w_at[:, WI_OFF:WI_OFF + IDX_HEADS],
                           jnp.zeros((D, 8 - IDX_HEADS), F32)], axis=1).T.astype(BF16)
    kg = jnp.tile(k_norm_g, 2).reshape(1, LANES)
    zpad = jnp.zeros((HEAD_DIM,), F32)
    iw = jnp.concatenate([idx_ln_w, zpad]).reshape(1, LANES)
    ib = jnp.concatenate([idx_ln_b, zpad]).reshape(1, LANES)
    qg = q_norm_g.reshape(HEAD_DIM, 1)
    li = np.arange(LANES)
    gsum = jnp.asarray((li[:, None] // HEAD_DIM == li[None, :] // HEAD_DIM).astype(np.float32))

    full = lambda shape: pl.BlockSpec(shape, lambda b, i: (0,) * len(shape))
    return pl.pallas_call(
        _inproj_kernel,
        grid=(B, S // tm),
        in_specs=[pl.BlockSpec((1, tm, D), lambda b, i: (b, i, 0)),
                  pl.BlockSpec((1, 1, D), lambda b, i: (b, 0, 0)),
                  pl.BlockSpec((1, 1, D), lambda b, i: (b, 0, 0)),
                  full((1, D)), full((D, RW_COLS)), full((D, PK_COLS)), full((PT_ROWS, D)),
                  full((1, LANES)), full((1, LANES)), full((1, LANES)), full((HEAD_DIM, 1)),
                  full((LANES, LANES)),
                  pl.BlockSpec((1, tm, LANES), lambda b, i: (b, i, 0)),
                  pl.BlockSpec((1, tm, LANES), lambda b, i: (b, i, 0)),
                  pl.BlockSpec((1, HALF, tm), lambda b, i: (b, 0, i)),
                  pl.BlockSpec((1, HALF, tm), lambda b, i: (b, 0, i))],
        out_specs=[pl.BlockSpec((1, tm, RW_COLS), lambda b, i: (b, i, 0)),
                   pl.BlockSpec((1, tm, AT_WIDTH), lambda b, i: (b, i, 0)),
                   pl.BlockSpec((1, tm, LANES), lambda b, i: (b, i, 0)),
                   pl.BlockSpec((1, AT_HEADS, LANES, tm), lambda b, i: (b, 0, 0, i)),
                   pl.BlockSpec((1, tm // QBLK, AT_WIDTH, QBLK), lambda b, i: (b, i, 0, 0)),
                   pl.BlockSpec((1, IDX_HEADS, LANES, tm), lambda b, i: (b, 0, 0, i)),
                   pl.BlockSpec((1, 8, tm), lambda b, i: (b, 0, i))],
        out_shape=[jax.ShapeDtypeStruct((B, S, RW_COLS), F32),
                   jax.ShapeDtypeStruct((B, S, AT_WIDTH), BF16),
                   jax.ShapeDtypeStruct((B, S, LANES), BF16),
                   jax.ShapeDtypeStruct((B, AT_HEADS, LANES, S), BF16),
                   jax.ShapeDtypeStruct((B, S // QBLK, AT_WIDTH, QBLK), BF16),
                   jax.ShapeDtypeStruct((B, IDX_HEADS, LANES, S), BF16),
                   jax.ShapeDtypeStruct((B, 8, S), F32)],
        compiler_params=_cparams(("arbitrary", "arbitrary")),
        name="inproj", interpret=interpret,
    )(x, scale1, shift1, norm1_g.reshape(1, D), w_rw, w_k, w_t, kg, iw, ib, qg, gsum,
      cos_r, sin_r, cos_t, sin_t)


def _rwkv_kernel(p_ref, mu_ref, w0_ref, w2_ref, a0_ref, a2_ref, g2_ref, kk_ref, ka_ref, rk_ref,
                 lnw_ref, lnb_ref, gsum_ref, y_ref, s_ref, prev_ref, yt_ref):
    C = RW_CHUNK
    W = RW_WIDTH
    nb = p_ref.shape[0]

    @pl.when(pl.program_id(1) == 0)
    def _():
        s_ref[...] = jnp.zeros_like(s_ref)
        prev_ref[...] = jnp.zeros_like(prev_ref)

    row = lax.broadcasted_iota(I32, (C, 1), 0)
    gsum = gsum_ref[...]

    def prepare(bi):
        p = p_ref[bi]
        pprev = jnp.where(row == 0, prev_ref[bi:bi + 1, :], pltpu.roll(p, 1, 0))
        prev_ref[bi:bi + 1, :] = p[C - 1:C]
        ps = p + (pprev - p) * mu_ref[...]
        r, k, v = ps[:, 0:W], ps[:, W:2 * W], ps[:, 2 * W:3 * W]
        o = 3 * W
        wl = ps[:, o:o + RW_LORA_W]
        al = ps[:, o + RW_LORA_W:o + RW_LORA_W + RW_LORA_A]
        gl = ps[:, o + RW_LORA_W + RW_LORA_A:]
        z = w0_ref[...] + _dot(jnp.tanh(wl).astype(BF16), w2_ref[...])
        nz = -z
        softplus = jnp.maximum(nz, 0.0) + jnp.log(1.0 + jnp.exp(-jnp.abs(nz)))
        logw = -jnp.exp(-softplus - 0.5)
        a = _sigmoid(a0_ref[...] + _dot(al.astype(BF16), a2_ref[...]))
        g = _dot(_sigmoid(gl).astype(BF16), g2_ref[...])
        kk = k * kk_ref[...]
        ss = _dot((kk * kk).astype(BF16), gsum)
        kk = kk * (1.0 / jnp.maximum(jnp.sqrt(ss), 1e-12))
        k2 = k * (1.0 + (a - 1.0) * ka_ref[...])
        bb = kk * a
        cw = logw
        sh = 1
        while sh < C:
            cw = cw + jnp.where(row >= sh, pltpu.roll(cw, sh, 0), 0.0)
            sh *= 2
        cw_last = cw[C - 1:C]
        e_neg = jnp.exp(-cw)
        e_end = jnp.exp(cw_last - cw)
        return dict(r=r, v=v, k2=k2, g=g, rw=r * jnp.exp(cw), kkp=kk * jnp.exp(cw - logw),
                    bw=bb * e_neg, kw=k2 * e_neg, bend=bb * e_end, kend=k2 * e_end,
                    wc=jnp.exp(cw_last), vt=v.T.astype(BF16))

    pre = [prepare(bi) for bi in range(nb)]

    ri = lax.broadcasted_iota(I32, (C, C), 0)
    ci = lax.broadcasted_iota(I32, (C, C), 1)
    strict = ri < ci
    incl = ri <= ci
    incl2 = jnp.concatenate([incl, incl], axis=0)
    lane_half = lax.broadcasted_iota(I32, (1, LANES), 1) // HEAD_DIM

    chains = [(bi, h) for bi in range(nb) for h in range(RW_HEADS)]
    pair = lambda h: slice((h // 2) * LANES, (h // 2 + 1) * LANES)
    own = lambda h: lane_half == (h % 2)
    stack = lambda bi, top, bot, h: jnp.concatenate(
        [pre[bi][top][:, pair(h)], pre[bi][bot][:, pair(h)]], axis=0)
    lh = {(bi, q): stack(bi, 'kkp', 'rw', 2 * q).astype(BF16)
          for bi in range(nb) for q in range(RW_HEADS // 2)}
    rh = [jnp.where(own(h), stack(bi, 'bw', 'kw', h), 0.0).astype(BF16) for bi, h in chains]
    aat = [_dot_nt(rh[n], lh[bi, h // 2]) for n, (bi, h) in enumerate(chains)]
    s_old = [s_ref[bi * RW_HEADS + h] for bi, h in chains]
    sl = [_dot_nt(s_old[n].astype(BF16), lh[bi, h // 2]) for n, (bi, h) in enumerate(chains)]
    vt = [pre[bi]['vt'][h * HEAD_DIM:(h + 1) * HEAD_DIM] for bi, h in chains]
    ids = range(len(chains))
    akt = [jnp.where(strict, aat[n][C:, :C], 0.0).astype(BF16) for n in ids]
    m = [jnp.where(strict, aat[n][:C, :C], 0.0).astype(BF16) for n in ids]
    xs = [-(sl[n][:, :C] + _dot(vt[n], akt[n])) for n in ids]
    xs = [xs[n] - _dot_split(xs[n], m[n]) for n in ids]
    lvl = 2
    while lvl < C:
        m = [_dot(m[n], m[n]).astype(BF16) for n in ids]
        xs = [xs[n] + _dot_split(xs[n], m[n]) for n in ids]
        lvl *= 2
    zt = [jnp.concatenate([xs[n].astype(BF16), vt[n]], axis=1) for n in ids]
    for n in ids:
        ymat = jnp.where(incl2, aat[n][:, C:], 0.0).astype(BF16)
        yt_ref[n * HEAD_DIM:(n + 1) * HEAD_DIM, :] = sl[n][:, C:] + _dot(zt[n], ymat)
    for n, (bi, h) in enumerate(chains):
        endz = jnp.where(own(h), stack(bi, 'bend', 'kend', h), 0.0).astype(BF16)
        s_ref[n] = s_old[n] * pre[bi]['wc'][:, pair(h)] + _dot(zt[n], endz)

    lnw = lnw_ref[...].reshape(RW_HEADS, HEAD_DIM, 1)
    lnb = lnb_ref[...].reshape(RW_HEADS, HEAD_DIM, 1)
    for bi in range(nb):
        q = pre[bi]
        yt = yt_ref[bi * W:(bi + 1) * W, :].reshape(RW_HEADS, HEAD_DIM, C)
        mean = jnp.mean(yt, axis=1, keepdims=True)
        yc = yt - mean
        var = jnp.mean(yc * yc, axis=1, keepdims=True)
        yn = yc * lax.rsqrt(var + RW_GN_EPS) * lnw + lnb
        y = yn.reshape(W, C).T
        bonus = _dot((q['r'] * q['k2'] * rk_ref[...]).astype(BF16), gsum) * q['v']
        y_ref[bi] = ((y + bonus) * q['g']).astype(BF16)


def _rwkv(p_rw, rw_mu, rw_w0, rw_w2, rw_a0, rw_a2, rw_g2, rw_k_k, rw_k_a, rw_r_k, rw_ln_w, rw_ln_b,
          interpret):
    B, S, _ = p_rw.shape
    C, W = RW_CHUNK, RW_WIDTH
    li = np.arange(W)
    gsum = jnp.asarray((li[:, None] // HEAD_DIM == li[None, :] // HEAD_DIM).astype(np.float32)).astype(BF16)
    row = lambda a: a.reshape(1, -1)
    full = lambda shape: pl.BlockSpec(shape, lambda b, i: (0,) * len(shape))
    nb = RW_BATCH if B % RW_BATCH == 0 else 1
    return pl.pallas_call(
        _rwkv_kernel,
        grid=(B // nb, S // C),
        in_specs=[pl.BlockSpec((nb, C, RW_COLS), lambda b, i: (b, i, 0)),
                  full((1, RW_COLS)), full((1, W)), full((RW_LORA_W, W)), full((1, W)),
                  full((RW_LORA_A, W)), full((RW_LORA_G, W)), full((1, W)), full((1, W)),
                  full((1, W)), full((W, 1)), full((W, 1)), full((W, W))],
        out_specs=pl.BlockSpec((nb, C, W), lambda b, i: (b, i, 0)),
        out_shape=jax.ShapeDtypeStruct((B, S, W), BF16),
        scratch_shapes=[pltpu.VMEM((nb * RW_HEADS, HEAD_DIM, LANES), F32),
                        pltpu.VMEM((nb, RW_COLS), F32),
                        pltpu.VMEM((nb * W, C), F32)],
        compiler_params=_cparams(("arbitrary", "arbitrary")),
        name="rwkv", interpret=interpret,
    )(p_rw, row(rw_mu), row(rw_w0), rw_w2.astype(BF16), row(rw_a0), rw_a2.astype(BF16),
      rw_g2.astype(BF16), row(rw_k_k), row(rw_k_a), row(rw_r_k), rw_ln_w.reshape(W, 1),
      rw_ln_b.reshape(W, 1), gsum)


def _dsa_kernel(topk, nbits, k_ref, vt_ref, ki_ref, qt_ref, qit_ref, wit_ref, og_ref, o_ref,
                key_s, acc_s, m_s, l_s, thr_s):
    j = pl.program_id(1)
    nkb = j + 1
    lane = lax.broadcasted_iota(I32, (QBLK, QBLK), 1)
    sub = lax.broadcasted_iota(I32, (QBLK, QBLK), 0)
    qpos = j * QBLK + lane
    wit = wit_ref[0]

    def score_blocks(i, carry):
        kbs = [i * SCORE_UNROLL + u for u in range(SCORE_UNROLL)]
        kib = [ki_ref[0, pl.ds(pl.multiple_of(kb * QBLK, QBLK), QBLK), :] for kb in kbs]
        lg = [[_dot(kib[u], qit_ref[0, hh]) for hh in range(IDX_HEADS)] for u in range(SCORE_UNROLL)]
        for u, kb in enumerate(kbs):
            s = wit[0:1, :] * jnp.maximum(lg[u][0], 0.0)
            for hh in range(1, IDX_HEADS):
                s = s + wit[hh:hh + 1, :] * jnp.maximum(lg[u][hh], 0.0)
            s = jnp.where(s == 0.0, 0.0, s)
            bits = pltpu.bitcast(s, I32)
            skey = jnp.where(bits < 0, bits ^ 0x7FFFFFFF, bits)
            key_s[kb] = jnp.where(kb * QBLK + sub <= qpos, skey, INT_MIN)
        return carry

    lax.fori_loop(0, pl.cdiv(nkb, SCORE_UNROLL), score_blocks, 0)

    @pl.when(nkb * QBLK <= topk)
    def _():
        thr_s[0:1, :] = jnp.full((1, QBLK), INT_MIN, I32)
        thr_s[1:2, :] = jnp.zeros((1, QBLK), I32)

    @pl.when(nkb * QBLK > topk)
    def _():
        n_done = pl.cdiv(nkb, SCORE_UNROLL) * SCORE_UNROLL
        n_cnt = pl.cdiv(nkb, COUNT_UNROLL)

        def fill(kb, carry):
            key_s[kb] = jnp.full((QBLK, QBLK), INT_MIN, I32)
            return carry

        lax.fori_loop(n_done, n_cnt * COUNT_UNROLL, fill, 0)

        def count(preds):
            def body(i, accs):
                accs = list(accs)
                for u in range(COUNT_UNROLL):
                    kb = i * COUNT_UNROLL + u
                    ky = key_s[kb]
                    for n, pred in enumerate(preds):
                        hit = pred(ky, kb * QBLK + sub).astype(I32)
                        accs[n] = accs[n] + jnp.sum(hit.reshape(QBLK // 8, 8, QBLK), axis=0)
                return tuple(accs)
            accs = lax.fori_loop(0, n_cnt, body, tuple(jnp.zeros((8, QBLK), I32) for _ in preds))
            return [jnp.sum(a, axis=0, keepdims=True) for a in accs]

        c0, = count([lambda ky, ix: ky >= 0])
        t0 = jnp.where(c0 >= topk, 0, INT_MIN).astype(I32)

        def bit_step(i, t):
            cand = t | jnp.left_shift(jnp.int32(1), 30 - i)
            c, = count([lambda ky, ix: ky >= cand])
            return jnp.where(c >= topk, cand, t)

        thr = lax.fori_loop(0, 31, bit_step, t0)
        n_gt, n_eq = count([lambda ky, ix: ky > thr, lambda ky, ix: ky == thr])
        need = topk - n_gt
        thr_s[0:1, :] = thr
        thr_s[1:2, :] = jnp.full((1, QBLK), 2 ** nbits, I32)

        @pl.when(jnp.max(jnp.abs(n_eq - need)) > 0)
        def _():
            def idx_step(i, mm):
                cand = mm | jnp.left_shift(jnp.int32(1), nbits - 1 - i)
                c, = count([lambda ky, ix: (ky == thr) & (ix < cand)])
                return jnp.where(c < need, cand, mm)

            thr_s[1:2, :] = lax.fori_loop(0, nbits, idx_step, jnp.zeros((1, QBLK), I32))

    thr = thr_s[0:1, :]
    mm = thr_s[1:2, :]
    m_s[...] = jnp.full_like(m_s, NEG_BIG)
    l_s[...] = jnp.zeros_like(l_s)
    acc_s[...] = jnp.zeros_like(acc_s)

    qt2 = [jnp.concatenate([qt_ref[0, 2 * q], qt_ref[0, 2 * q + 1]], axis=1)
           for q in range(AT_HEADS // 2)]

    def attn_blocks(i, carry):
        kbs = [i * ATTN_UNROLL + u for u in range(ATTN_UNROLL)]
        sel = []
        for kb in kbs:
            skey = key_s[kb]
            kidx = kb * QBLK + sub
            sel.append((kidx <= qpos) & ((skey > thr) | ((skey == thr) & (kidx <= mm))))
        sel = jnp.concatenate(sel, axis=0)
        kblk = [k_ref[0, pl.ds(pl.multiple_of(kb * QBLK, QBLK), QBLK), :] for kb in kbs]
        vtb = jnp.concatenate([vt_ref[0, kb] for kb in kbs], axis=1)
        s2 = [[_dot(kblk[u][:, q * LANES:(q + 1) * LANES], qt2[q]) for u in range(ATTN_UNROLL)]
              for q in range(AT_HEADS // 2)]
        pexp, alpha = [], []
        for hh in range(AT_HEADS):
            half = slice((hh % 2) * QBLK, (hh % 2 + 1) * QBLK)
            s = jnp.concatenate([s2[hh // 2][u][:, half] for u in range(ATTN_UNROLL)], axis=0)
            s = jnp.where(sel, s, NEG_BIG)
            m_old = m_s[hh:hh + 1, :]
            m_new = jnp.maximum(m_old, jnp.max(s, axis=0, keepdims=True))
            pe = jnp.exp(s - m_new)
            al = jnp.exp(m_old - m_new)
            l_s[hh:hh + 1, :] = al * l_s[hh:hh + 1, :] + jnp.sum(pe, axis=0, keepdims=True)
            m_s[hh:hh + 1, :] = m_new
            pexp.append(pe.astype(BF16))
            alpha.append(al)
        for hh in range(AT_HEADS):
            hs = slice(hh * HEAD_DIM, (hh + 1) * HEAD_DIM)
            acc_s[hs, :] = alpha[hh] * acc_s[hs, :] + _dot(vtb[hs, :], pexp[hh])
        return carry

    lax.fori_loop(0, pl.cdiv(nkb, ATTN_UNROLL), attn_blocks, 0)

    for hh in range(AT_HEADS):
        hs = slice(hh * HEAD_DIM, (hh + 1) * HEAD_DIM)
        oh = acc_s[hs, :] * (1.0 / l_s[hh:hh + 1, :])
        ms = jnp.mean(oh * oh, axis=0, keepdims=True)
        acc_s[hs, :] = oh * lax.rsqrt(ms + NORM_EPS) * og_ref[hs, :]
    o_ref[0] = acc_s[...].T.astype(BF16)


def _dsa(k, vt, ki, qt, qit, wit, at_out_g, interpret):
    B, S, _ = k.shape
    nq = S // QBLK
    topk = min(IDX_TOPK_MAX, S // 4)
    nbits = int(np.log2(S))
    assert 2 ** nbits == S and nq % COUNT_UNROLL == 0 and SCORE_UNROLL == ATTN_UNROLL
    assert COUNT_UNROLL % SCORE_UNROLL == 0
    return pl.pallas_call(
        functools.partial(_dsa_kernel, topk, nbits),
        grid=(B, nq),
        in_specs=[pl.BlockSpec((1, S, AT_WIDTH), lambda b, j: (b, 0, 0)),
                  pl.BlockSpec((1, nq, AT_WIDTH, QBLK), lambda b, j: (b, 0, 0, 0)),
                  pl.BlockSpec((1, S, LANES), lambda b, j: (b, 0, 0)),
                  pl.BlockSpec((1, AT_HEADS, LANES, QBLK), lambda b, j: (b, 0, 0, j)),
                  pl.BlockSpec((1, IDX_HEADS, LANES, QBLK), lambda b, j: (b, 0, 0, j)),
                  pl.BlockSpec((1, 8, QBLK), lambda b, j: (b, 0, j)),
                  pl.BlockSpec((AT_WIDTH, 1), lambda b, j: (0, 0))],
        out_specs=pl.BlockSpec((1, QBLK, AT_WIDTH), lambda b, j: (b, j, 0)),
        out_shape=jax.ShapeDtypeStruct((B, S, AT_WIDTH), BF16),
        scratch_shapes=[pltpu.VMEM((nq, QBLK, QBLK), I32),
                        pltpu.VMEM((AT_WIDTH, QBLK), F32),
                        pltpu.VMEM((AT_HEADS, QBLK), F32),
                        pltpu.VMEM((AT_HEADS, QBLK), F32),
                        pltpu.VMEM((8, QBLK), I32)],
        compiler_params=_cparams(("arbitrary", "arbitrary")),
        name="dsa", interpret=interpret,
    )(k, vt, ki, qt, qit, wit, at_out_g.reshape(AT_WIDTH, 1))


def _first_max(vals, idx, axis, sentinel):
    m = jnp.max(vals, axis=axis, keepdims=True)
    return m, jnp.min(jnp.where(vals == m, idx, sentinel), axis=axis, keepdims=True)


def _route_cols(logits_t, bias_col):
    E, tm = logits_t.shape
    pg = E // N_GROUPS
    scores = _sigmoid(logits_t)
    biased = scores + bias_col
    b3 = biased.reshape(N_GROUPS, pg, tm)
    r3 = lax.broadcasted_iota(I32, (N_GROUPS, pg, tm), 1)
    m1, first = _first_max(b3, r3, 1, pg)
    m2 = jnp.max(jnp.where(r3 == first, -jnp.inf, b3), axis=1, keepdims=True)
    cur = (m1 + m2).reshape(N_GROUPS, tm)
    grow = lax.broadcasted_iota(I32, (N_GROUPS, tm), 0)
    gsel = jnp.zeros((N_GROUPS, tm), F32)
    for _ in range(TOPK_GROUPS):
        _, gi = _first_max(cur, grow, 0, N_GROUPS)
        hit = grow == gi
        gsel = jnp.where(hit, 1.0, gsel)
        cur = jnp.where(hit, -jnp.inf, cur)
    gmask = jnp.broadcast_to(gsel.reshape(N_GROUPS, 1, tm), (N_GROUPS, pg, tm)).reshape(E, tm)
    cur = jnp.where(gmask > 0.0, biased, -jnp.inf)
    row = lax.broadcasted_iota(I32, (E, tm), 0)
    onehot = jnp.zeros((E, tm), F32)
    eids, gws = [], []
    for _ in range(TOP_K):
        _, ei = _first_max(cur, row, 0, E)
        hit = row == ei
        eids.append(ei)
        gws.append(jnp.sum(jnp.where(hit, scores, 0.0), axis=0, keepdims=True))
        onehot = jnp.where(hit, 1.0, onehot)
        cur = jnp.where(hit, -jnp.inf, cur)
    eid = jnp.concatenate(eids, axis=0)
    gw = jnp.concatenate(gws, axis=0)
    gw = gw * (ROUTED_SCALE / jnp.sum(gw, axis=0, keepdims=True))
    return eid, gw, onehot


def _post_kernel(x_ref, yrw_ref, yat_ref, g1_ref, sc_ref, sh_ref, g2_ref, ng_ref, wo_ref, rwt_ref,
                 rb_ref, s1_ref, s3_ref, s2_ref, base_ref, h2_ref, eid_ref, gw_ref, rank_ref, cnt_ref):
    W = RW_WIDTH
    tm = x_ref.shape[1]
    E = rwt_ref.shape[0]

    @pl.when((pl.program_id(0) == 0) & (pl.program_id(1) == 0))
    def _():
        cnt_ref[...] = jnp.zeros_like(cnt_ref)

    mix = _dot(yrw_ref[0], wo_ref[0:W, :]) + _dot(yat_ref[0], wo_ref[W:, :])
    x1 = x_ref[0] + g1_ref[0] * mix
    ms = jnp.mean(x1 * x1, axis=-1, keepdims=True)
    h2 = x1 * lax.rsqrt(ms + NORM_EPS) * ng_ref[...] * (1.0 + sc_ref[0]) + sh_ref[0]
    hb = h2.astype(BF16)
    h2_ref[0, :, 0, :] = _pack_halves(h2)
    act =(_silu(_dot(hb, s1_ref[...])) * _dot(hb, s3_ref[...])).astype(BF16)
    base_ref[0] = x1 + g2_ref[0] * _dot(act, s2_ref[...])

    logits_t = lax.dot_general(rwt_ref[...], h2, (((1,), (1,)), ((), ())), precision=HIGHEST,
                               preferred_element_type=F32)
    eid, gw, onehot = _route_cols(logits_t, rb_ref[...])
    eid_ref[0] = eid
    gw_ref[0] = gw
    ti = lax.broadcasted_iota(I32, (tm, tm), 0)
    tj = lax.broadcasted_iota(I32, (tm, tm), 1)
    before = _dot(onehot.astype(BF16), (ti < tj).astype(BF16)) + cnt_ref[:, 0:1]
    row = lax.broadcasted_iota(I32, (E, tm), 0)
    ranks = [jnp.sum(jnp.where(row == eid[kk:kk + 1, :], before, 0.0), axis=0, keepdims=True)
             for kk in range(TOP_K)]
    rank_ref[0] = jnp.concatenate(ranks, axis=0).astype(I32)
    cnt_ref[...] = cnt_ref[...] + jnp.sum(onehot, axis=1, keepdims=True)


def _post(x, y_rw, y_at, gate1, scale2, shift2, gate2, norm2_g, w_out, router_w, router_bias,
          sw1, sw3, sw2, interpret):
    B, S, D = x.shape
    tm = min(S, 256)
    E = router_w.shape[1]
    sd = sw1.shape[1]
    full = lambda shape: pl.BlockSpec(shape, lambda b, i: (0,) * len(shape))
    tok = lambda w: pl.BlockSpec((1, tm, w), lambda b, i: (b, i, 0))
    per_b = pl.BlockSpec((1, 1, D), lambda b, i: (b, 0, 0))
    col8 = pl.BlockSpec((1, TOP_K, tm), lambda b, i: (b, 0, i))
    return pl.pallas_call(
        _post_kernel,
        grid=(B, S // tm),
        in_specs=[tok(D), tok(RW_WIDTH), tok(AT_WIDTH), per_b, per_b, per_b, per_b, full((1, D)),
                  full((D, D)), full((E, D)), full((E, 1)), full((D, sd)), full((D, sd)),
                  full((sd, D))],
        out_specs=[tok(D), pl.BlockSpec((1, tm, 1, D // 2), lambda b, i: (b, i, 0, 0)),
                   col8, col8, col8, full((E, LANES))],
        out_shape=[jax.ShapeDtypeStruct((B, S, D), F32),
                   jax.ShapeDtypeStruct((B, S, 1, D // 2), I32),
                   jax.ShapeDtypeStruct((B, TOP_K, S), I32),
                   jax.ShapeDtypeStruct((B, TOP_K, S), F32),
                   jax.ShapeDtypeStruct((B, TOP_K, S), I32),
                   jax.ShapeDtypeStruct((E, LANES), F32)],
        compiler_params=_cparams(("arbitrary", "arbitrary")),
        name="post", interpret=interpret,
    )(x, y_rw, y_at, gate1, scale2, shift2, gate2, norm2_g.reshape(1, D), w_out.astype(BF16),
      router_w.T, router_bias.reshape(E, 1), sw1.astype(BF16), sw3.astype(BF16), sw2.astype(BF16))


def _expert_kernel(be_ref, nv_ref, nu_ref, xs_ref, w1_ref, w3_ref, w2_ref, o_ref, w13_s, w2_s):
    i = pl.program_id(0)
    blk, _, hw = xs_ref.shape
    F = w1_ref.shape[2]
    used = i < nu_ref[0]

    @pl.when(used & ((i == 0) | (be_ref[i] != be_ref[jnp.maximum(i - 1, 0)])))
    def _():
        w13_s[:, :F] = w1_ref[0].astype(BF16)
        w13_s[:, F:] = w3_ref[0].astype(BF16)
        w2_s[...] = w2_ref[0].astype(BF16)

    @pl.when(used)
    def _():
        live = lax.broadcasted_iota(I32, (blk, 1), 0) < nv_ref[i]
        x_lo, x_hi = _unpack_halves(jnp.where(live, xs_ref[:, 0, :], 0))
        h13 =(_dot(x_lo.astype(BF16), w13_s[:hw, :]) + _dot(x_hi.astype(BF16), w13_s[hw:, :]))
        act = (_silu(h13[:, :F]) * h13[:, F:]).astype(BF16)
        o_ref[:, 0, :] = _pack_halves(_dot(act, w2_s[...]))


def _experts(xs, block_e, block_rows, n_used, w1, w3, w2, interpret):
    P, _, hw = xs.shape
    E, D, F = w1.shape
    nb = P // EXP_BLK
    blk = lambda i, nu: jnp.minimum(i, nu[0] - 1)
    grid_spec = pltpu.PrefetchScalarGridSpec(
        num_scalar_prefetch=3,
        grid=(nb,),
        in_specs=[pl.BlockSpec((EXP_BLK, 1, hw), lambda i, be, nv, nu: (blk(i, nu), 0, 0)),
                  pl.BlockSpec((1, D, F), lambda i, be, nv, nu: (be[blk(i, nu)], 0, 0)),
                  pl.BlockSpec((1, D, F), lambda i, be, nv, nu: (be[blk(i, nu)], 0, 0)),
                  pl.BlockSpec((1, F, D), lambda i, be, nv, nu: (be[blk(i, nu)], 0, 0))],
        out_specs=pl.BlockSpec((EXP_BLK, 1, hw), lambda i, be, nv, nu: (blk(i, nu), 0, 0)),
        scratch_shapes=[pltpu.VMEM((D, 2 * F), BF16), pltpu.VMEM((F, D), BF16)],
    )
    return pl.pallas_call(
        _expert_kernel,
        grid_spec=grid_spec,
        out_shape=jax.ShapeDtypeStruct((P, 1, hw), I32),
        compiler_params=_cparams(("arbitrary",)),
        name="experts", interpret=interpret,
    )(block_e, block_rows, n_used, xs, w1, w3, w2)


def _row_out(tile, t, slots, slot, sem):
    return pltpu.make_async_copy(tile.at[t], slots.at[slot], sem)


def _row_in(slots, slot, tile, t, sem):
    return pltpu.make_async_copy(slots.at[slot], tile.at[t], sem)


def _dispatch_kernel(dest_ref, h2_ref, xs_ref, sem):
    td = h2_ref.shape[0]

    def issue(t, carry):
        for kk in range(TOP_K):
            _row_out(h2_ref, t, xs_ref, dest_ref[0, 0, kk * td + t], sem).start(priority=kk % 2)
        return carry

    def drain(t, carry):
        for kk in range(TOP_K):
            _row_out(h2_ref, t, xs_ref, dest_ref[0, 0, kk * td + t], sem).wait()
        return carry

    lax.fori_loop(0, td, issue, 0)
    lax.fori_loop(0, td, drain, 0)


def _dispatch(h2, dest_tiles, n_slots, interpret):
    T, _, D = h2.shape
    nt, _, n = dest_tiles.shape
    td = n // TOP_K
    return pl.pallas_call(
        _dispatch_kernel,
        grid=(nt,),
        in_specs=[pl.BlockSpec((1, 1, n), lambda i: (i, 0, 0), memory_space=pltpu.SMEM),
                  pl.BlockSpec((td, 1, D), lambda i: (i, 0, 0))],
        out_specs=pl.BlockSpec(memory_space=pl.ANY),
        out_shape=jax.ShapeDtypeStruct((n_slots, 1, D), h2.dtype),
        scratch_shapes=[pltpu.SemaphoreType.DMA(())],
        compiler_params=_cparams(("arbitrary",)),
        name="dispatch", interpret=interpret,
    )(dest_tiles, h2)


def _combine_kernel(dest_ref, base_ref, g2_ref, gw_ref, ys_ref, o_ref, buf, sem):
    td = base_ref.shape[0]

    def issue(t, carry):
        for kk in range(TOP_K):
            _row_in(ys_ref, dest_ref[0, 0, kk * td + t], buf.at[kk], t, sem).start(priority=kk % 2)
        return carry

    def drain(t, carry):
        for kk in range(TOP_K):
            _row_in(ys_ref, dest_ref[0, 0, kk * td + t], buf.at[kk], t, sem).wait()
        return carry

    lax.fori_loop(0, td, issue, 0)
    lax.fori_loop(0, td, drain, 0)
    gw = gw_ref[...]
    acc_lo, acc_hi = _unpack_halves(buf[0, :, 0, :])
    acc_lo, acc_hi = gw[:, 0:1] * acc_lo, gw[:, 0:1] * acc_hi
    for kk in range(1, TOP_K):
        y_lo, y_hi = _unpack_halves(buf[kk, :, 0, :])
        acc_lo = acc_lo + gw[:, kk:kk + 1] * y_lo
        acc_hi = acc_hi + gw[:, kk:kk + 1] * y_hi
    o_ref[...] = base_ref[...] + g2_ref[0] * jnp.concatenate([acc_lo, acc_hi], axis=1)


def _combine(base, gate2, gw_tok, ys, dest_tiles, tiles_per_batch, interpret):
    T, D = base.shape
    nt, _, n = dest_tiles.shape
    td = n // TOP_K
    return pl.pallas_call(
        _combine_kernel,
        grid=(nt,),
        in_specs=[pl.BlockSpec((1, 1, n), lambda i: (i, 0, 0), memory_space=pltpu.SMEM),
                  pl.BlockSpec((td, D), lambda i: (i, 0)),
                  pl.BlockSpec((1, 1, D), lambda i: (i // tiles_per_batch, 0, 0)),
                  pl.BlockSpec((td, TOP_K), lambda i: (i, 0)),
                  pl.BlockSpec(memory_space=pl.ANY)],
        out_specs=pl.BlockSpec((td, D), lambda i: (i, 0)),
        out_shape=jax.ShapeDtypeStruct((T, D), F32),
        scratch_shapes=[pltpu.VMEM((TOP_K, td, 1, D // 2), I32), pltpu.SemaphoreType.DMA(())],
        compiler_params=_cparams(("arbitrary",)),
        name="combine", interpret=interpret,
    )(dest_tiles, base, gate2, gw_tok, ys)


def _slots_kernel(eid_ref, rank_ref, pstart_ref, dest_ref):
    td = eid_ref.shape[2]
    E = pstart_ref.shape[0]
    row = lax.broadcasted_iota(I32, (E, td), 0)
    pstart = pstart_ref[...]
    eid = eid_ref[0]
    for kk in range(TOP_K):
        base = jnp.sum(jnp.where(row == eid[kk:kk + 1, :], pstart, 0), axis=0, keepdims=True)
        dest_ref[0, :, kk * td:(kk + 1) * td] = base + rank_ref[0, kk:kk + 1, :]


def _slot_plan(counts, eid_t, rank_t, td, interpret):
    B, _, S = eid_t.shape
    E = counts.shape[0]
    padded = (counts + EXP_BLK - 1) // EXP_BLK * EXP_BLK
    pend = jnp.cumsum(padded)
    pstart = (pend - padded).astype(I32)
    nb = -(-(B * S * TOP_K + E * (EXP_BLK - 1)) // EXP_BLK)
    first_row = jnp.arange(nb, dtype=I32) * EXP_BLK
    block_e = jnp.sum(pend[None, :] <= first_row[:, None], axis=1)
    block_e = jnp.minimum(block_e, E - 1).astype(I32)
    block_rows = jnp.clip(pstart[block_e] + counts[block_e] - first_row, 0, EXP_BLK).astype(I32)
    nt = S // td
    dest_tiles = pl.pallas_call(
        _slots_kernel,
        grid=(B, nt),
        in_specs=[pl.BlockSpec((1, TOP_K, td), lambda b, i: (b, 0, i)),
                  pl.BlockSpec((1, TOP_K, td), lambda b, i: (b, 0, i)),
                  pl.BlockSpec((E, 1), lambda b, i: (0, 0))],
        out_specs=pl.BlockSpec((1, 1, TOP_K * td), lambda b, i: (b * nt + i, 0, 0)),
        out_shape=jax.ShapeDtypeStruct((B * nt, 1, TOP_K * td), I32),
        compiler_params=_cparams(("arbitrary", "arbitrary")),
        name="slots", interpret=interpret,
    )(eid_t, rank_t, pstart.reshape(E, 1))
    n_used = (pend[-1:] // EXP_BLK).astype(I32)
    return block_e, block_rows, n_used, dest_tiles, nb * EXP_BLK


def _forward(x, c, positions, w_ada, b_ada, norm1_g, norm2_g, w_in, rw_mu, rw_w0, rw_w2,
             rw_a0, rw_a2, rw_g2, rw_k_k, rw_k_a, rw_r_k, rw_ln_w, rw_ln_b, q_norm_g,
             k_norm_g, idx_ln_w, idx_ln_b, at_out_g, w_out, router_w, router_bias,
             exp_w1, exp_w3, exp_w2, shared_w1, shared_w3, shared_w2, interpret=False):
    B, S, D = x.shape
    depth = w_ada.shape[0]
    for l in range(depth):
        mod = _mod(c, w_ada[l], b_ada[l], interpret)
        shift1, scale1, gate1, shift2, scale2, gate2 = [
            m.reshape(B, 1, D) for m in jnp.split(mod, 6, axis=-1)]
        tabs = _rope_tables(positions, interpret)
        p_rw, k, ki, qt, vt, qit, wit = _inproj(
            x, scale1, shift1, norm1_g[l], w_in[l], k_norm_g[l], idx_ln_w[l], idx_ln_b[l],
            q_norm_g[l], tabs, interpret)
        y_rw = _rwkv(p_rw, rw_mu[l], rw_w0[l], rw_w2[l], rw_a0[l], rw_a2[l], rw_g2[l], rw_k_k[l],
                     rw_k_a[l], rw_r_k[l], rw_ln_w[l], rw_ln_b[l], interpret)
        y_at = _dsa(k, vt, ki, qt, qit, wit, at_out_g[l], interpret)
        base, h2, eid_t, gw_t, rank_t, cnt = _post(
            x, y_rw, y_at, gate1, scale2, shift2, gate2, norm2_g[l], w_out[l], router_w[l],
            router_bias[l], shared_w1[l], shared_w3[l], shared_w2[l], interpret)
        T = B * S
        td = min(S, ROW_TILE)
        block_e, block_rows, n_used, dest_tiles, n_slots = _slot_plan(
            cnt[:, 0].astype(I32), eid_t, rank_t, td, interpret)
        xs = _dispatch(h2.reshape(T, 1, D // 2), dest_tiles, n_slots, interpret)
        ys = _experts(xs, block_e, block_rows, n_used, exp_w1[l], exp_w3[l], exp_w2[l], interpret)
        gw_tok = gw_t.transpose(0, 2, 1).reshape(T, TOP_K)
        x = _combine(base.reshape(T, D), gate2, gw_tok, ys, dest_tiles, S // td,
                     interpret).reshape(B, S, D)
    return x


def kernel(x, c, positions, w_ada, b_ada, norm1_g, norm2_g, w_in, rw_mu, rw_w0, rw_w2, rw_a0, rw_a2, rw_g2, rw_k_k, rw_k_a, rw_r_k, rw_ln_w, rw_ln_b, q_norm_g, k_norm_g, idx_ln_w, idx_ln_b, at_out_g, w_out, router_w, router_bias, exp_w1, exp_w3, exp_w2, shared_w1, shared_w3, shared_w2):
    return _forward(x, c, positions, w_ada, b_ada, norm1_g, norm2_g, w_in, rw_mu, rw_w0, rw_w2,
                    rw_a0, rw_a2, rw_g2, rw_k_k, rw_k_a, rw_r_k, rw_ln_w, rw_ln_b, q_norm_g,
                    k_norm_g, idx_ln_w, idx_ln_b, at_out_g, w_out, router_w, router_bias,
                    exp_w1, exp_w3, exp_w2, shared_w1, shared_w3, shared_w2)
```

```python
import functools

import jax
import jax.numpy as jnp
import numpy as np
from jax import lax
from jax.experimental import pallas as pl
from jax.experimental.pallas import tpu as pltpu

F32 = jnp.float32
BF16 = jnp.bfloat16
I32 = jnp.int32
HIGHEST = lax.Precision.HIGHEST

LANES = 128
HEAD_DIM = 64
HALF = HEAD_DIM // 2
RW_HEADS = 8
RW_WIDTH = RW_HEADS * HEAD_DIM
AT_HEADS = 8
AT_WIDTH = AT_HEADS * HEAD_DIM
IDX_HEADS = 4
RW_LORA_W, RW_LORA_A, RW_LORA_G = 64, 64, 128
RW_COLS = 3 * RW_WIDTH + RW_LORA_W + RW_LORA_A + RW_LORA_G
KI_OFF = 3 * AT_WIDTH + IDX_HEADS * HEAD_DIM
WI_OFF = KI_OFF + HEAD_DIM
PT_ROWS = 2 * AT_WIDTH + IDX_HEADS * HEAD_DIM + 8
PK_COLS = AT_WIDTH + LANES
ROPE_THETA = 10000.0
NORM_EPS = 1e-6
LN_EPS = 1e-6
RW_GN_EPS = 64e-5
IDX_TOPK_MAX = 256
N_EXPERTS = 256
TOP_K = 8
N_GROUPS = 8
TOPK_GROUPS = 4
ROUTED_SCALE = 2.5
INT_MIN = -2 ** 31
PAIR_HI_MASK = -65536
NEG_BIG = -1e30

RW_CHUNK = 128
RW_BATCH = 2
QBLK = 128
SCORE_UNROLL = 4
ATTN_UNROLL = 4
COUNT_UNROLL = 4
EXP_BLK = 512
ROW_TILE = 256
VMEM_LIMIT = 48 * 1024 * 1024


def _cparams(sem):
    return pltpu.CompilerParams(dimension_semantics=sem, vmem_limit_bytes=VMEM_LIMIT)


def _sigmoid(x):
    return 1.0 / (1.0 + jnp.exp(-x))


def _silu(x):
    return x * _sigmoid(x)


def _dot(a, b):
    return jnp.dot(a, b, preferred_element_type=F32)


def _dot_split(a, b):
    hi = a.astype(BF16)
    lo = (a - hi.astype(F32)).astype(BF16)
    return _dot(hi, b) + _dot(lo, b)


def _pack_halves(x):
    w = x.shape[1] // 2
    lo = pltpu.bitcast(x[:, :w].astype(BF16).astype(F32), I32)
    hi = pltpu.bitcast(x[:, w:].astype(BF16).astype(F32), I32)
    return (hi & PAIR_HI_MASK) | lax.shift_right_logical(lo, 16)


def _unpack_halves(p):
    return pltpu.bitcast(p << 16, F32), pltpu.bitcast(p & PAIR_HI_MASK, F32)


def _dot_nt(a, b):
    return lax.dot_general(a, b, (((1,), (1,)), ((), ())), preferred_element_type=F32)


def _mod_kernel(c_ref, w_ref, b_ref, o_ref):
    c = c_ref[...]
    o_ref[...] = jnp.dot(_silu(c), w_ref[...], precision=HIGHEST,
                         preferred_element_type=F32) + b_ref[...]


def _mod(c, w_ada, b_ada, interpret):
    B, D = c.shape
    n = w_ada.shape[1] // D
    return pl.pallas_call(
        _mod_kernel,
        grid=(n,),
        in_specs=[pl.BlockSpec((B, D), lambda i: (0, 0)),
                  pl.BlockSpec((D, D), lambda i: (0, i)),
                  pl.BlockSpec((1, D), lambda i: (0, i))],
        out_specs=pl.BlockSpec((B, D), lambda i: (0, i)),
        out_shape=jax.ShapeDtypeStruct((B, n * D), F32),
        compiler_params=_cparams(("arbitrary",)),
        name="mod", interpret=interpret,
    )(c, w_ada, b_ada.reshape(1, -1))


def _rope_tab_kernel(pc_ref, pr_ref, cr_ref, sr_ref, ct_ref, st_ref):
    log_theta = float(np.log(ROPE_THETA))
    lane = lax.broadcasted_iota(I32, (1, LANES), 1)
    inv_r = jnp.exp((lane % HALF).astype(F32) * (-log_theta / HALF))
    ang = pc_ref[0].astype(F32) * inv_r
    cr_ref[0] = jnp.cos(ang)
    sr_ref[0] = jnp.where((lane % HEAD_DIM) < HALF, -jnp.sin(ang), jnp.sin(ang))
    sub = lax.broadcasted_iota(I32, (HALF, 1), 0)
    inv_c = jnp.exp(sub.astype(F32) * (-log_theta / HALF))
    ang_t = inv_c * pr_ref[0].astype(F32)
    ct_ref[0] = jnp.cos(ang_t)
    st_ref[0] = jnp.sin(ang_t)


def _rope_tables(positions, interpret):
    B, S = positions.shape
    ts = min(S, 512)
    return pl.pallas_call(
        _rope_tab_kernel,
        grid=(B, S // ts),
        in_specs=[pl.BlockSpec((1, ts, 1), lambda b, i: (b, i, 0)),
                  pl.BlockSpec((1, 1, ts), lambda b, i: (b, 0, i))],
        out_specs=[pl.BlockSpec((1, ts, LANES), lambda b, i: (b, i, 0)),
                   pl.BlockSpec((1, ts, LANES), lambda b, i: (b, i, 0)),
                   pl.BlockSpec((1, HALF, ts), lambda b, i: (b, 0, i)),
                   pl.BlockSpec((1, HALF, ts), lambda b, i: (b, 0, i))],
        out_shape=[jax.ShapeDtypeStruct((B, S, LANES), F32),
                   jax.ShapeDtypeStruct((B, S, LANES), F32),
                   jax.ShapeDtypeStruct((B, HALF, S), F32),
                   jax.ShapeDtypeStruct((B, HALF, S), F32)],
        compiler_params=_cparams(("arbitrary", "arbitrary")),
        name="rope_tab", interpret=interpret,
    )(positions.reshape(B, S, 1), positions.reshape(B, 1, S))


def _rope_rows(y, cos, sin_signed):
    lane = lax.broadcasted_iota(I32, (1, LANES), 1)
    partner = jnp.where((lane % HEAD_DIM) < HALF,
                        pltpu.roll(y, LANES - HALF, 1), pltpu.roll(y, HALF, 1))
    return y * cos + partner * sin_signed


def _rope_cols(y, cos_t, sin_t):
    x1, x2 = y[:, :HALF], y[:, HALF:]
    return jnp.concatenate([x1 * cos_t - x2 * sin_t, x2 * cos_t + x1 * sin_t], axis=1)


def _inproj_kernel(x_ref, sc_ref, sh_ref, g_ref, wrw_ref, wk_ref, wt_ref, kg_ref, iw_ref, ib_ref,
                   qg_ref, gsum_ref, cr_ref, sr_ref, ct_ref, st_ref,
                   prw_ref, k_ref, ki_ref, qt_ref, vt_ref, qit_ref, wit_ref):
    tm = x_ref.shape[1]
    x = x_ref[0]
    ms = jnp.mean(x * x, axis=-1, keepdims=True)
    h = x * lax.rsqrt(ms + NORM_EPS) * g_ref[...] * (1.0 + sc_ref[0]) + sh_ref[0]
    hb = h.astype(BF16)
    prw_ref[0] = _dot(hb, wrw_ref[...])
    pk = _dot(hb, wk_ref[...])
    pt = _dot_nt(wt_ref[...], hb)

    cos_r, sin_r = cr_ref[0], sr_ref[0]
    gsum = gsum_ref[...]
    inv_hd = 1.0 / HEAD_DIM
    for p in range(AT_WIDTH // LANES):
        xk = pk[:, p * LANES:(p + 1) * LANES]
        ss = jnp.dot(xk * xk, gsum, precision=HIGHEST, preferred_element_type=F32)
        y = xk * lax.rsqrt(ss * inv_hd + NORM_EPS) * kg_ref[...]
        k_ref[0, :, p * LANES:(p + 1) * LANES] = _rope_rows(y, cos_r, sin_r).astype(BF16)
    xi = pk[:, AT_WIDTH:AT_WIDTH + LANES]
    mu = jnp.dot(xi, gsum, precision=HIGHEST, preferred_element_type=F32) * inv_hd
    xc = xi - mu
    var = jnp.dot(xc * xc, gsum, precision=HIGHEST, preferred_element_type=F32) * inv_hd
    yi = xc * lax.rsqrt(var + LN_EPS) * iw_ref[...] + ib_ref[...]
    ki_ref[0] = _rope_rows(yi, cos_r, sin_r).astype(BF16)

    cos_t, sin_t = ct_ref[0][None], st_ref[0][None]
    xq = pt[0:AT_WIDTH].reshape(AT_HEADS, HEAD_DIM, tm)
    msq = jnp.mean(xq * xq, axis=1, keepdims=True)
    yq = xq * lax.rsqrt(msq + NORM_EPS) * qg_ref[...][None]
    yq = _rope_cols(yq, cos_t, sin_t) * (HEAD_DIM ** -0.5)
    zq = jnp.zeros((HEAD_DIM, tm), BF16)
    for hh in range(AT_HEADS):
        parts = [yq[hh].astype(BF16), zq] if hh % 2 == 0 else [zq, yq[hh].astype(BF16)]
        qt_ref[0, hh] = jnp.concatenate(parts, axis=0)
    vt = pt[AT_WIDTH:2 * AT_WIDTH].astype(BF16)
    for cblk in range(tm // QBLK):
        vt_ref[0, cblk] = vt[:, cblk * QBLK:(cblk + 1) * QBLK]
    xqi = pt[2 * AT_WIDTH:2 * AT_WIDTH + IDX_HEADS * HEAD_DIM].reshape(IDX_HEADS, HEAD_DIM, tm)
    yqi = _rope_cols(xqi, cos_t, sin_t)
    for hh in range(IDX_HEADS):
        qit_ref[0, hh] = jnp.concatenate([yqi[hh].astype(BF16), zq], axis=0)
    wit_ref[0] = pt[PT_ROWS - 8:PT_ROWS] * (IDX_HEADS ** -0.5 * HEAD_DIM ** -0.5)


def _inproj(x, scale1, shift1, norm1_g, w_in, k_norm_g, idx_ln_w, idx_ln_b, q_norm_g,
            tabs, interpret):
    B, S, D = x.shape
    tm = min(S, 256)
    cos_r, sin_r, cos_t, sin_t = tabs
    w_at = w_in[:, RW_COLS:]
    w_rw = w_in[:, :RW_COLS].astype(BF16)
    w_k = jnp.concatenate([w_at[:, AT_WIDTH:2 * AT_WIDTH], w_at[:, KI_OFF:KI_OFF + HEAD_DIM],
                           jnp.zeros((D, HEAD_DIM), F32)], axis=1).astype(BF16)
    w_t = jnp.concatenate([w_at[:, 0:AT_WIDTH], w_at[:, 2 * AT_WIDTH:3 * AT_WIDTH],
                           w_at[:, 3 * AT_WIDTH:KI_OFF], w_at[:, WI_OFF:WI_OFF + IDX_HEADS],
                           jnp.zeros((D, 8 - IDX_HEADS), F32)], axis=1).T.astype(BF16)
    kg = jnp.tile(k_norm_g, 2).reshape(1, LANES)
    zpad = jnp.zeros((HEAD_DIM,), F32)
    iw = jnp.concatenate([idx_ln_w, zpad]).reshape(1, LANES)
    ib = jnp.concatenate([idx_ln_b, zpad]).reshape(1, LANES)
    qg = q_norm_g.reshape(HEAD_DIM, 1)
    li = np.arange(LANES)
    gsum = jnp.asarray((li[:, None] // HEAD_DIM == li[None, :] // HEAD_DIM).astype(np.float32))

    full = lambda shape: pl.BlockSpec(shape, lambda b, i: (0,) * len(shape))
    return pl.pallas_call(
        _inproj_kernel,
        grid=(B, S // tm),
        in_specs=[pl.BlockSpec((1, tm, D), lambda b, i: (b, i, 0)),
                  pl.BlockSpec((1, 1, D), lambda b, i: (b, 0, 0)),
                  pl.BlockSpec((1, 1, D), lambda b, i: (b, 0, 0)),
                  full((1, D)), full((D, RW_COLS)), full((D, PK_COLS)), full((PT_ROWS, D)),
                  full((1, LANES)), full((1, LANES)), full((1, LANES)), full((HEAD_DIM, 1)),
                  full((LANES, LANES)),
                  pl.BlockSpec((1, tm, LANES), lambda b, i: (b, i, 0)),
                  pl.BlockSpec((1, tm, LANES), lambda b, i: (b, i, 0)),
                  pl.BlockSpec((1, HALF, tm), lambda b, i: (b, 0, i)),
                  pl.BlockSpec((1, HALF, tm), lambda b, i: (b, 0, i))],
        out_specs=[pl.BlockSpec((1, tm, RW_COLS), lambda b, i: (b, i, 0)),
                   pl.BlockSpec((1, tm, AT_WIDTH), lambda b, i: (b, i, 0)),
                   pl.BlockSpec((1, tm, LANES), lambda b, i: (b, i, 0)),
                   pl.BlockSpec((1, AT_HEADS, LANES, tm), lambda b, i: (b, 0, 0, i)),
                   pl.BlockSpec((1, tm // QBLK, AT_WIDTH, QBLK), lambda b, i: (b, i, 0, 0)),
                   pl.BlockSpec((1, IDX_HEADS, LANES, tm), lambda b, i: (b, 0, 0, i)),
                   pl.BlockSpec((1, 8, tm), lambda b, i: (b, 0, i))],
        out_shape=[jax.ShapeDtypeStruct((B, S, RW_COLS), F32),
                   jax.ShapeDtypeStruct((B, S, AT_WIDTH), BF16),
                   jax.ShapeDtypeStruct((B, S, LANES), BF16),
                   jax.ShapeDtypeStruct((B, AT_HEADS, LANES, S), BF16),
                   jax.ShapeDtypeStruct((B, S // QBLK, AT_WIDTH, QBLK), BF16),
                   jax.ShapeDtypeStruct((B, IDX_HEADS, LANES, S), BF16),
                   jax.ShapeDtypeStruct((B, 8, S), F32)],
        compiler_params=_cparams(("arbitrary", "arbitrary")),
        name="inproj", interpret=interpret,
    )(x, scale1, shift1, norm1_g.reshape(1, D), w_rw, w_k, w_t, kg, iw, ib, qg, gsum,
      cos_r, sin_r, cos_t, sin_t)


def _rwkv_kernel(p_ref, mu_ref, w0_ref, w2_ref, a0_ref, a2_ref, g2_ref, kk_ref, ka_ref, rk_ref,
                 lnw_ref, lnb_ref, gsum_ref, y_ref, s_ref, prev_ref, yt_ref):
    C = RW_CHUNK
    W = RW_WIDTH
    nb = p_ref.shape[0]

    @pl.when(pl.program_id(1) == 0)
    def _():
        s_ref[...] = jnp.zeros_like(s_ref)
        prev_ref[...] = jnp.zeros_like(prev_ref)

    row = lax.broadcasted_iota(I32, (C, 1), 0)
    gsum = gsum_ref[...]

    def prepare(bi):
        p = p_ref[bi]
        pprev = jnp.where(row == 0, prev_ref[bi:bi + 1, :], pltpu.roll(p, 1, 0))
        prev_ref[bi:bi + 1, :] = p[C - 1:C]
        ps = p + (pprev - p) * mu_ref[...]
        r, k, v = ps[:, 0:W], ps[:, W:2 * W], ps[:, 2 * W:3 * W]
        o = 3 * W
        wl = ps[:, o:o + RW_LORA_W]
        al = ps[:, o + RW_LORA_W:o + RW_LORA_W + RW_LORA_A]
        gl = ps[:, o + RW_LORA_W + RW_LORA_A:]
        z = w0_ref[...] + _dot(jnp.tanh(wl).astype(BF16), w2_ref[...])
        nz = -z
        softplus = jnp.maximum(nz, 0.0) + jnp.log(1.0 + jnp.exp(-jnp.abs(nz)))
        logw = -jnp.exp(-softplus - 0.5)
        a = _sigmoid(a0_ref[...] + _dot(al.astype(BF16), a2_ref[...]))
        g = _dot(_sigmoid(gl).astype(BF16), g2_ref[...])
        kk = k * kk_ref[...]
        ss = _dot((kk * kk).astype(BF16), gsum)
        kk = kk * (1.0 / jnp.maximum(jnp.sqrt(ss), 1e-12))
        k2 = k * (1.0 + (a - 1.0) * ka_ref[...])
        bb = kk * a
        cw = logw
        sh = 1
        while sh < C:
            cw = cw + jnp.where(row >= sh, pltpu.roll(cw, sh, 0), 0.0)
            sh *= 2
        cw_last = cw[C - 1:C]
        e_neg = jnp.exp(-cw)
        e_end = jnp.exp(cw_last - cw)
        return dict(r=r, v=v, k2=k2, g=g, rw=r * jnp.exp(cw), kkp=kk * jnp.exp(cw - logw),
                    bw=bb * e_neg, kw=k2 * e_neg, bend=bb * e_end, kend=k2 * e_end,
                    wc=jnp.exp(cw_last), vt=v.T.astype(BF16))

    pre = [prepare(bi) for bi in range(nb)]

    ri = lax.broadcasted_iota(I32, (C, C), 0)
    ci = lax.broadcasted_iota(I32, (C, C), 1)
    strict = ri < ci
    incl = ri <= ci
    incl2 = jnp.concatenate([incl, incl], axis=0)
    lane_half = lax.broadcasted_iota(I32, (1, LANES), 1) // HEAD_DIM

    chains = [(bi, h) for bi in range(nb) for h in range(RW_HEADS)]
    pair = lambda h: slice((h // 2) * LANES, (h // 2 + 1) * LANES)
    own = lambda h: lane_half == (h % 2)
    stack = lambda bi, top, bot, h: jnp.concatenate(
        [pre[bi][top][:, pair(h)], pre[bi][bot][:, pair(h)]], axis=0)
    lh = {(bi, q): stack(bi, 'kkp', 'rw', 2 * q).astype(BF16)
          for bi in range(nb) for q in range(RW_HEADS // 2)}
    rh = [jnp.where(own(h), stack(bi, 'bw', 'kw', h), 0.0).astype(BF16) for bi, h in chains]
    aat = [_dot_nt(rh[n], lh[bi, h // 2]) for n, (bi, h) in enumerate(chains)]
    s_old = [s_ref[bi * RW_HEADS + h] for bi, h in chains]
    sl = [_dot_nt(s_old[n].astype(BF16), lh[bi, h // 2]) for n, (bi, h) in enumerate(chains)]
    vt = [pre[bi]['vt'][h * HEAD_DIM:(h + 1) * HEAD_DIM] for bi, h in chains]
    ids = range(len(chains))
    akt = [jnp.where(strict, aat[n][C:, :C], 0.0).astype(BF16) for n in ids]
    m = [jnp.where(strict, aat[n][:C, :C], 0.0).astype(BF16) for n in ids]
    xs = [-(sl[n][:, :C] + _dot(vt[n], akt[n])) for n in ids]
    xs = [xs[n] - _dot_split(xs[n], m[n]) for n in ids]
    lvl = 2
    while lvl < C:
        m = [_dot(m[n], m[n]).astype(BF16) for n in ids]
        xs = [xs[n] + _dot_split(xs[n], m[n]) for n in ids]
        lvl *= 2
    zt = [jnp.concatenate([xs[n].astype(BF16), vt[n]], axis=1) for n in ids]
    for n in ids:
        ymat = jnp.where(incl2, aat[n][:, C:], 0.0).astype(BF16)
        yt_ref[n * HEAD_DIM:(n + 1) * HEAD_DIM, :] = sl[n][:, C:] + _dot(zt[n], ymat)
    for n, (bi, h) in enumerate(chains):
        endz = jnp.where(own(h), stack(bi, 'bend', 'kend', h), 0.0).astype(BF16)
        s_ref[n] = s_old[n] * pre[bi]['wc'][:, pair(h)] + _dot(zt[n], endz)

    lnw = lnw_ref[...].reshape(RW_HEADS, HEAD_DIM, 1)
    lnb = lnb_ref[...].reshape(RW_HEADS, HEAD_DIM, 1)
    for bi in range(nb):
        q = pre[bi]
        yt = yt_ref[bi * W:(bi + 1) * W, :].reshape(RW_HEADS, HEAD_DIM, C)
        mean = jnp.mean(yt, axis=1, keepdims=True)
        yc = yt - mean
        var = jnp.mean(yc * yc, axis=1, keepdims=True)
        yn = yc * lax.rsqrt(var + RW_GN_EPS) * lnw + lnb
        y = yn.reshape(W, C).T
        bonus = _dot((q['r'] * q['k2'] * rk_ref[...]).astype(BF16), gsum) * q['v']
        y_ref[bi] = ((y + bonus) * q['g']).astype(BF16)


def _rwkv(p_rw, rw_mu, rw_w0, rw_w2, rw_a0, rw_a2, rw_g2, rw_k_k, rw_k_a, rw_r_k, rw_ln_w, rw_ln_b,
          interpret):
    B, S, _ = p_rw.shape
    C, W = RW_CHUNK, RW_WIDTH
    li = np.arange(W)
    gsum = jnp.asarray((li[:, None] // HEAD_DIM == li[None, :] // HEAD_DIM).astype(np.float32)).astype(BF16)
    row = lambda a: a.reshape(1, -1)
    full = lambda shape: pl.BlockSpec(shape, lambda b, i: (0,) * len(shape))
    nb = RW_BATCH if B % RW_BATCH == 0 else 1
    return pl.pallas_call(
        _rwkv_kernel,
        grid=(B // nb, S // C),
        in_specs=[pl.BlockSpec((nb, C, RW_COLS), lambda b, i: (b, i, 0)),
                  full((1, RW_COLS)), full((1, W)), full((RW_LORA_W, W)), full((1, W)),
                  full((RW_LORA_A, W)), full((RW_LORA_G, W)), full((1, W)), full((1, W)),
                  full((1, W)), full((W, 1)), full((W, 1)), full((W, W))],
        out_specs=pl.BlockSpec((nb, C, W), lambda b, i: (b, i, 0)),
        out_shape=jax.ShapeDtypeStruct((B, S, W), BF16),
        scratch_shapes=[pltpu.VMEM((nb * RW_HEADS, HEAD_DIM, LANES), F32),
                        pltpu.VMEM((nb, RW_COLS), F32),
                        pltpu.VMEM((nb * W, C), F32)],
        compiler_params=_cparams(("arbitrary", "arbitrary")),
        name="rwkv", interpret=interpret,
    )(p_rw, row(rw_mu), row(rw_w0), rw_w2.astype(BF16), row(rw_a0), rw_a2.astype(BF16),
      rw_g2.astype(BF16), row(rw_k_k), row(rw_k_a), row(rw_r_k), rw_ln_w.reshape(W, 1),
      rw_ln_b.reshape(W, 1), gsum)


def _dsa_kernel(topk, nbits, k_ref, vt_ref, ki_ref, qt_ref, qit_ref, wit_ref, og_ref, o_ref,
                key_s, acc_s, m_s, l_s, thr_s):
    j = pl.program_id(1)
    nkb = j + 1
    lane = lax.broadcasted_iota(I32, (QBLK, QBLK), 1)
    sub = lax.broadcasted_iota(I32, (QBLK, QBLK), 0)
    qpos = j * QBLK + lane
    wit = wit_ref[0]

    def score_blocks(i, carry):
        kbs = [i * SCORE_UNROLL + u for u in range(SCORE_UNROLL)]
        kib = [ki_ref[0, pl.ds(pl.multiple_of(kb * QBLK, QBLK), QBLK), :] for kb in kbs]
        lg = [[_dot(kib[u], qit_ref[0, hh]) for hh in range(IDX_HEADS)] for u in range(SCORE_UNROLL)]
        for u, kb in enumerate(kbs):
            s = wit[0:1, :] * jnp.maximum(lg[u][0], 0.0)
            for hh in range(1, IDX_HEADS):
                s = s + wit[hh:hh + 1, :] * jnp.maximum(lg[u][hh], 0.0)
            s = jnp.where(s == 0.0, 0.0, s)
            bits = pltpu.bitcast(s, I32)
            skey = jnp.where(bits < 0, bits ^ 0x7FFFFFFF, bits)
            key_s[kb] = jnp.where(kb * QBLK + sub <= qpos, skey, INT_MIN)
        return carry

    lax.fori_loop(0, pl.cdiv(nkb, SCORE_UNROLL), score_blocks, 0)

    @pl.when(nkb * QBLK <= topk)
    def _():
        thr_s[0:1, :] = jnp.full((1, QBLK), INT_MIN, I32)
        thr_s[1:2, :] = jnp.zeros((1, QBLK), I32)

    @pl.when(nkb * QBLK > topk)
    def _():
        n_done = pl.cdiv(nkb, SCORE_UNROLL) * SCORE_UNROLL
        n_cnt = pl.cdiv(nkb, COUNT_UNROLL)

        def fill(kb, carry):
            key_s[kb] = jnp.full((QBLK, QBLK), INT_MIN, I32)
            return carry

        lax.fori_loop(n_done, n_cnt * COUNT_UNROLL, fill, 0)

        def count(preds):
            def body(i, accs):
                accs = list(accs)
                for u in range(COUNT_UNROLL):
                    kb = i * COUNT_UNROLL + u
                    ky = key_s[kb]
                    for n, pred in enumerate(preds):
                        hit = pred(ky, kb * QBLK + sub).astype(I32)
                        accs[n] = accs[n] + jnp.sum(hit.reshape(QBLK // 8, 8, QBLK), axis=0)
                return tuple(accs)
            accs = lax.fori_loop(0, n_cnt, body, tuple(jnp.zeros((8, QBLK), I32) for _ in preds))
            return [jnp.sum(a, axis=0, keepdims=True) for a in accs]

        c0, = count([lambda ky, ix: ky >= 0])
        t0 = jnp.where(c0 >= topk, 0, INT_MIN).astype(I32)

        def bit_step(i, t):
            cand = t | jnp.left_shift(jnp.int32(1), 30 - i)
            c, = count([lambda ky, ix: ky >= cand])
            return jnp.where(c >= topk, cand, t)

        thr = lax.fori_loop(0, 31, bit_step, t0)
        n_gt, n_eq = count([lambda ky, ix: ky > thr, lambda ky, ix: ky == thr])
        need = topk - n_gt
        thr_s[0:1, :] = thr
        thr_s[1:2, :] = jnp.full((1, QBLK), 2 ** nbits, I32)

        @pl.when(jnp.max(jnp.abs(n_eq - need)) > 0)
        def _():
            def idx_step(i, mm):
                cand = mm | jnp.left_shift(jnp.int32(1), nbits - 1 - i)
                c, = count([lambda ky, ix: (ky == thr) & (ix < cand)])
                return jnp.where(c < need, cand, mm)

            thr_s[1:2, :] = lax.fori_loop(0, nbits, idx_step, jnp.zeros((1, QBLK), I32))

    thr = thr_s[0:1, :]
    mm = thr_s[1:2, :]
    m_s[...] = jnp.full_like(m_s, NEG_BIG)
    l_s[...] = jnp.zeros_like(l_s)
    acc_s[...] = jnp.zeros_like(acc_s)

    qt2 = [jnp.concatenate([qt_ref[0, 2 * q], qt_ref[0, 2 * q + 1]], axis=1)
           for q in range(AT_HEADS // 2)]

    def attn_blocks(i, carry):
        kbs = [i * ATTN_UNROLL + u for u in range(ATTN_UNROLL)]
        sel = []
        for kb in kbs:
            skey = key_s[kb]
            kidx = kb * QBLK + sub
            sel.append((kidx <= qpos) & ((skey > thr) | ((skey == thr) & (kidx <= mm))))
        sel = jnp.concatenate(sel, axis=0)
        kblk = [k_ref[0, pl.ds(pl.multiple_of(kb * QBLK, QBLK), QBLK), :] for kb in kbs]
        vtb = jnp.concatenate([vt_ref[0, kb] for kb in kbs], axis=1)
        s2 = [[_dot(kblk[u][:, q * LANES:(q + 1) * LANES], qt2[q]) for u in range(ATTN_UNROLL)]
              for q in range(AT_HEADS // 2)]
        pexp, alpha = [], []
        for hh in range(AT_HEADS):
            half = slice((hh % 2) * QBLK, (hh % 2 + 1) * QBLK)
            s = jnp.concatenate([s2[hh // 2][u][:, half] for u in range(ATTN_UNROLL)], axis=0)
            s = jnp.where(sel, s, NEG_BIG)
            m_old = m_s[hh:hh + 1, :]
            m_new = jnp.maximum(m_old, jnp.max(s, axis=0, keepdims=True))
            pe = jnp.exp(s - m_new)
            al = jnp.exp(m_old - m_new)
            l_s[hh:hh + 1, :] = al * l_s[hh:hh + 1, :] + jnp.sum(pe, axis=0, keepdims=True)
            m_s[hh:hh + 1, :] = m_new
            pexp.append(pe.astype(BF16))
            alpha.append(al)
        for hh in range(AT_HEADS):
            hs = slice(hh * HEAD_DIM, (hh + 1) * HEAD_DIM)
            acc_s[hs, :] = alpha[hh] * acc_s[hs, :] + _dot(vtb[hs, :], pexp[hh])
        return carry

    lax.fori_loop(0, pl.cdiv(nkb, ATTN_UNROLL), attn_blocks, 0)

    for hh in range(AT_HEADS):
        hs = slice(hh * HEAD_DIM, (hh + 1) * HEAD_DIM)
        oh = acc_s[hs, :] * (1.0 / l_s[hh:hh + 1, :])
        ms = jnp.mean(oh * oh, axis=0, keepdims=True)
        acc_s[hs, :] = oh * lax.rsqrt(ms + NORM_EPS) * og_ref[hs, :]
    o_ref[0] = acc_s[...].T.astype(BF16)


def _dsa(k, vt, ki, qt, qit, wit, at_out_g, interpret):
    B, S, _ = k.shape
    nq = S // QBLK
    topk = min(IDX_TOPK_MAX, S // 4)
    nbits = int(np.log2(S))
    assert 2 ** nbits == S and nq % COUNT_UNROLL == 0 and SCORE_UNROLL == ATTN_UNROLL
    assert COUNT_UNROLL % SCORE_UNROLL == 0
    return pl.pallas_call(
        functools.partial(_dsa_kernel, topk, nbits),
        grid=(B, nq),
        in_specs=[pl.BlockSpec((1, S, AT_WIDTH), lambda b, j: (b, 0, 0)),
                  pl.BlockSpec((1, nq, AT_WIDTH, QBLK), lambda b, j: (b, 0, 0, 0)),
                  pl.BlockSpec((1, S, LANES), lambda b, j: (b, 0, 0)),
                  pl.BlockSpec((1, AT_HEADS, LANES, QBLK), lambda b, j: (b, 0, 0, j)),
                  pl.BlockSpec((1, IDX_HEADS, LANES, QBLK), lambda b, j: (b, 0, 0, j)),
                  pl.BlockSpec((1, 8, QBLK), lambda b, j: (b, 0, j)),
                  pl.BlockSpec((AT_WIDTH, 1), lambda b, j: (0, 0))],
        out_specs=pl.BlockSpec((1, QBLK, AT_WIDTH), lambda b, j: (b, j, 0)),
        out_shape=jax.ShapeDtypeStruct((B, S, AT_WIDTH), BF16),
        scratch_shapes=[pltpu.VMEM((nq, QBLK, QBLK), I32),
                        pltpu.VMEM((AT_WIDTH, QBLK), F32),
                        pltpu.VMEM((AT_HEADS, QBLK), F32),
                        pltpu.VMEM((AT_HEADS, QBLK), F32),
                        pltpu.VMEM((8, QBLK), I32)],
        compiler_params=_cparams(("arbitrary", "arbitrary")),
        name="dsa", interpret=interpret,
    )(k, vt, ki, qt, qit, wit, at_out_g.reshape(AT_WIDTH, 1))


def _first_max(vals, idx, axis, sentinel):
    m = jnp.max(vals, axis=axis, keepdims=True)
    return m, jnp.min(jnp.where(vals == m, idx, sentinel), axis=axis, keepdims=True)


def _route_cols(logits_t, bias_col):
    E, tm = logits_t.shape
    pg = E // N_GROUPS
    scores = _sigmoid(logits_t)
    biased = scores + bias_col
    b3 = biased.reshape(N_GROUPS, pg, tm)
    r3 = lax.broadcasted_iota(I32, (N_GROUPS, pg, tm), 1)
    m1, first = _first_max(b3, r3, 1, pg)
    m2 = jnp.max(jnp.where(r3 == first, -jnp.inf, b3), axis=1, keepdims=True)
    cur = (m1 + m2).reshape(N_GROUPS, tm)
    grow = lax.broadcasted_iota(I32, (N_GROUPS, tm), 0)
    gsel = jnp.zeros((N_GROUPS, tm), F32)
    for _ in range(TOPK_GROUPS):
        _, gi = _first_max(cur, grow, 0, N_GROUPS)
        hit = grow == gi
        gsel = jnp.where(hit, 1.0, gsel)
        cur = jnp.where(hit, -jnp.inf, cur)
    gmask = jnp.broadcast_to(gsel.reshape(N_GROUPS, 1, tm), (N_GROUPS, pg, tm)).reshape(E, tm)
    cur = jnp.where(gmask > 0.0, biased, -jnp.inf)
    row = lax.broadcasted_iota(I32, (E, tm), 0)
    onehot = jnp.zeros((E, tm), F32)
    eids, gws = [], []
    for _ in range(TOP_K):
        _, ei = _first_max(cur, row, 0, E)
        hit = row == ei
        eids.append(ei)
        gws.append(jnp.sum(jnp.where(hit, scores, 0.0), axis=0, keepdims=True))
        onehot = jnp.where(hit, 1.0, onehot)
        cur = jnp.where(hit, -jnp.inf, cur)
    eid = jnp.concatenate(eids, axis=0)
    gw = jnp.concatenate(gws, axis=0)
    gw = gw * (ROUTED_SCALE / jnp.sum(gw, axis=0, keepdims=True))
    return eid, gw, onehot


def _post_kernel(x_ref, yrw_ref, yat_ref, g1_ref, sc_ref, sh_ref, g2_ref, ng_ref, wo_ref, rwt_ref,
                 rb_ref, s1_ref, s3_ref, s2_ref, base_ref, h2_ref, eid_ref, gw_ref, rank_ref, cnt_ref):
    W = RW_WIDTH
    tm = x_ref.shape[1]
    E = rwt_ref.shape[0]

    @pl.when((pl.program_id(0) == 0) & (pl.program_id(1) == 0))
    def _():
        cnt_ref[...] = jnp.zeros_like(cnt_ref)

    mix = _dot(yrw_ref[0], wo_ref[0:W, :]) + _dot(yat_ref[0], wo_ref[W:, :])
    x1 = x_ref[0] + g1_ref[0] * mix
    ms = jnp.mean(x1 * x1, axis=-1, keepdims=True)
    h2 = x1 * lax.rsqrt(ms + NORM_EPS) * ng_ref[...] * (1.0 + sc_ref[0]) + sh_ref[0]
    hb = h2.astype(BF16)
    h2_ref[0, :, 0, :] = _pack_halves(h2)
    act =(_silu(_dot(hb, s1_ref[...])) * _dot(hb, s3_ref[...])).astype(BF16)
    base_ref[0] = x1 + g2_ref[0] * _dot(act, s2_ref[...])

    logits_t = lax.dot_general(rwt_ref[...], h2, (((1,), (1,)), ((), ())), precision=HIGHEST,
                               preferred_element_type=F32)
    eid, gw, onehot = _route_cols(logits_t, rb_ref[...])
    eid_ref[0] = eid
    gw_ref[0] = gw
    ti = lax.broadcasted_iota(I32, (tm, tm), 0)
    tj = lax.broadcasted_iota(I32, (tm, tm), 1)
    before = _dot(onehot.astype(BF16), (ti < tj).astype(BF16)) + cnt_ref[:, 0:1]
    row = lax.broadcasted_iota(I32, (E, tm), 0)
    ranks = [jnp.sum(jnp.where(row == eid[kk:kk + 1, :], before, 0.0), axis=0, keepdims=True)
             for kk in range(TOP_K)]
    rank_ref[0] = jnp.concatenate(ranks, axis=0).astype(I32)
    cnt_ref[...] = cnt_ref[...] + jnp.sum(onehot, axis=1, keepdims=True)


def _post(x, y_rw, y_at, gate1, scale2, shift2, gate2, norm2_g, w_out, router_w, router_bias,
          sw1, sw3, sw2, interpret):
    B, S, D = x.shape
    tm = min(S, 256)
    E = router_w.shape[1]
    sd = sw1.shape[1]
    full = lambda shape: pl.BlockSpec(shape, lambda b, i: (0,) * len(shape))
    tok = lambda w: pl.BlockSpec((1, tm, w), lambda b, i: (b, i, 0))
    per_b = pl.BlockSpec((1, 1, D), lambda b, i: (b, 0, 0))
    col8 = pl.BlockSpec((1, TOP_K, tm), lambda b, i: (b, 0, i))
    return pl.pallas_call(
        _post_kernel,
        grid=(B, S // tm),
        in_specs=[tok(D), tok(RW_WIDTH), tok(AT_WIDTH), per_b, per_b, per_b, per_b, full((1, D)),
                  full((D, D)), full((E, D)), full((E, 1)), full((D, sd)), full((D, sd)),
                  full((sd, D))],
        out_specs=[tok(D), pl.BlockSpec((1, tm, 1, D // 2), lambda b, i: (b, i, 0, 0)),
                   col8, col8, col8, full((E, LANES))],
        out_shape=[jax.ShapeDtypeStruct((B, S, D), F32),
                   jax.ShapeDtypeStruct((B, S, 1, D // 2), I32),
                   jax.ShapeDtypeStruct((B, TOP_K, S), I32),
                   jax.ShapeDtypeStruct((B, TOP_K, S), F32),
                   jax.ShapeDtypeStruct((B, TOP_K, S), I32),
                   jax.ShapeDtypeStruct((E, LANES), F32)],
        compiler_params=_cparams(("arbitrary", "arbitrary")),
        name="post", interpret=interpret,
    )(x, y_rw, y_at, gate1, scale2, shift2, gate2, norm2_g.reshape(1, D), w_out.astype(BF16),
      router_w.T, router_bias.reshape(E, 1), sw1.astype(BF16), sw3.astype(BF16), sw2.astype(BF16))


def _expert_kernel(be_ref, nv_ref, nu_ref, xs_ref, w1_ref, w3_ref, w2_ref, o_ref, w13_s, w2_s):
    i = pl.program_id(0)
    blk, _, hw = xs_ref.shape
    F = w1_ref.shape[2]
    used = i < nu_ref[0]

    @pl.when(used & ((i == 0) | (be_ref[i] != be_ref[jnp.maximum(i - 1, 0)])))
    def _():
        w13_s[:, :F] = w1_ref[0].astype(BF16)
        w13_s[:, F:] = w3_ref[0].astype(BF16)
        w2_s[...] = w2_ref[0].astype(BF16)

    @pl.when(used)
    def _():
        live = lax.broadcasted_iota(I32, (blk, 1), 0) < nv_ref[i]
        x_lo, x_hi = _unpack_halves(jnp.where(live, xs_ref[:, 0, :], 0))
        h13 =(_dot(x_lo.astype(BF16), w13_s[:hw, :]) + _dot(x_hi.astype(BF16), w13_s[hw:, :]))
        act = (_silu(h13[:, :F]) * h13[:, F:]).astype(BF16)
        o_ref[:, 0, :] = _pack_halves(_dot(act, w2_s[...]))


def _experts(xs, block_e, block_rows, n_used, w1, w3, w2, interpret):
    P, _, hw = xs.shape
    E, D, F = w1.shape
    nb = P // EXP_BLK
    blk = lambda i, nu: jnp.minimum(i, nu[0] - 1)
    grid_spec = pltpu.PrefetchScalarGridSpec(
        num_scalar_prefetch=3,
        grid=(nb,),
        in_specs=[pl.BlockSpec((EXP_BLK, 1, hw), lambda i, be, nv, nu: (blk(i, nu), 0, 0)),
                  pl.BlockSpec((1, D, F), lambda i, be, nv, nu: (be[blk(i, nu)], 0, 0)),
                  pl.BlockSpec((1, D, F), lambda i, be, nv, nu: (be[blk(i, nu)], 0, 0)),
                  pl.BlockSpec((1, F, D), lambda i, be, nv, nu: (be[blk(i, nu)], 0, 0))],
        out_specs=pl.BlockSpec((EXP_BLK, 1, hw), lambda i, be, nv, nu: (blk(i, nu), 0, 0)),
        scratch_shapes=[pltpu.VMEM((D, 2 * F), BF16), pltpu.VMEM((F, D), BF16)],
    )
    return pl.pallas_call(
        _expert_kernel,
        grid_spec=grid_spec,
        out_shape=jax.ShapeDtypeStruct((P, 1, hw), I32),
        compiler_params=_cparams(("arbitrary",)),
        name="experts", interpret=interpret,
    )(block_e, block_rows, n_used, xs, w1, w3, w2)


def _row_out(tile, t, slots, slot, sem):
    return pltpu.make_async_copy(tile.at[t], slots.at[slot], sem)


def _row_in(slots, slot, tile, t, sem):
    return pltpu.make_async_copy(slots.at[slot], tile.at[pl.ds(t, 1)], sem)


def _dispatch_kernel(dest_ref, h2_ref, xs_ref, sem):
    td = h2_ref.shape[0]

    def issue(t, carry):
        for kk in range(TOP_K):
            _row_out(h2_ref, t, xs_ref, dest_ref[0, 0, kk * td + t], sem).start(priority=kk % 2)
        return carry

    def drain(t, carry):
        for kk in range(TOP_K):
            _row_out(h2_ref, t, xs_ref, dest_ref[0, 0, kk * td + t], sem).wait()
        return carry

    lax.fori_loop(0, td, issue, 0)
    lax.fori_loop(0, td, drain, 0)


def _dispatch(h2, dest_tiles, n_slots, interpret):
    T, _, D = h2.shape
    nt, _, n = dest_tiles.shape
    td = n // TOP_K
    return pl.pallas_call(
        _dispatch_kernel,
        grid=(nt,),
        in_specs=[pl.BlockSpec((1, 1, n), lambda i: (i, 0, 0), memory_space=pltpu.SMEM),
                  pl.BlockSpec((td, 1, D), lambda i: (i, 0, 0))],
        out_specs=pl.BlockSpec(memory_space=pl.ANY),
        out_shape=jax.ShapeDtypeStruct((n_slots, 1, D), h2.dtype),
        scratch_shapes=[pltpu.SemaphoreType.DMA(())],
        compiler_params=_cparams(("arbitrary",)),
        name="dispatch", interpret=interpret,
    )(dest_tiles, h2)


def _combine_kernel(dest_ref, base_ref, g2_ref, gw_ref, ys_ref, o_ref, buf, sem):
    td = base_ref.shape[0]

    def issue(t, carry):
        for kk in range(TOP_K):
            _row_in(ys_ref, dest_ref[0, 0, kk * td + t], buf.at[kk], t, sem).start(priority=kk % 2)
        return carry

    def drain(t, carry):
        for kk in range(TOP_K):
            _row_in(ys_ref, dest_ref[0, 0, kk * td + t], buf.at[kk], t, sem).wait()
        return carry

    lax.fori_loop(0, td, issue, 0)
    lax.fori_loop(0, td, drain, 0)
    gw = gw_ref[...]
    acc_lo, acc_hi = _unpack_halves(buf[0])
    acc_lo, acc_hi = gw[:, 0:1] * acc_lo, gw[:, 0:1] * acc_hi
    for kk in range(1, TOP_K):
        y_lo, y_hi = _unpack_halves(buf[kk])
        acc_lo = acc_lo + gw[:, kk:kk + 1] * y_lo
        acc_hi = acc_hi + gw[:, kk:kk + 1] * y_hi
    o_ref[...] = base_ref[...] + g2_ref[0] * jnp.concatenate([acc_lo, acc_hi], axis=1)


def _combine(base, gate2, gw_tok, ys, dest_tiles, tiles_per_batch, interpret):
    T, D = base.shape
    nt, _, n = dest_tiles.shape
    td = n // TOP_K
    return pl.pallas_call(
        _combine_kernel,
        grid=(nt,),
        in_specs=[pl.BlockSpec((1, 1, n), lambda i: (i, 0, 0), memory_space=pltpu.SMEM),
                  pl.BlockSpec((td, D), lambda i: (i, 0)),
                  pl.BlockSpec((1, 1, D), lambda i: (i // tiles_per_batch, 0, 0)),
                  pl.BlockSpec((td, TOP_K), lambda i: (i, 0)),
                  pl.BlockSpec(memory_space=pl.ANY)],
        out_specs=pl.BlockSpec((td, D), lambda i: (i, 0)),
        out_shape=jax.ShapeDtypeStruct((T, D), F32),
        scratch_shapes=[pltpu.VMEM((TOP_K, td, D // 2), I32), pltpu.SemaphoreType.DMA(())],
        compiler_params=_cparams(("arbitrary",)),
        name="combine", interpret=interpret,
    )(dest_tiles, base, gate2, gw_tok, ys)


def _slots_kernel(eid_ref, rank_ref, pstart_ref, dest_ref):
    td = eid_ref.shape[2]
    E = pstart_ref.shape[0]
    row = lax.broadcasted_iota(I32, (E, td), 0)
    pstart = pstart_ref[...]
    eid = eid_ref[0]
    for kk in range(TOP_K):
        base = jnp.sum(jnp.where(row == eid[kk:kk + 1, :], pstart, 0), axis=0, keepdims=True)
        dest_ref[0, :, kk * td:(kk + 1) * td] = base + rank_ref[0, kk:kk + 1, :]


def _slot_plan(counts, eid_t, rank_t, td, interpret):
    B, _, S = eid_t.shape
    E = counts.shape[0]
    padded = (counts + EXP_BLK - 1) // EXP_BLK * EXP_BLK
    pend = jnp.cumsum(padded)
    pstart = (pend - padded).astype(I32)
    nb = -(-(B * S * TOP_K + E * (EXP_BLK - 1)) // EXP_BLK)
    first_row = jnp.arange(nb, dtype=I32) * EXP_BLK
    block_e = jnp.sum(pend[None, :] <= first_row[:, None], axis=1)
    block_e = jnp.minimum(block_e, E - 1).astype(I32)
    block_rows = jnp.clip(pstart[block_e] + counts[block_e] - first_row, 0, EXP_BLK).astype(I32)
    nt = S // td
    dest_tiles = pl.pallas_call(
        _slots_kernel,
        grid=(B, nt),
        in_specs=[pl.BlockSpec((1, TOP_K, td), lambda b, i: (b, 0, i)),
                  pl.BlockSpec((1, TOP_K, td), lambda b, i: (b, 0, i)),
                  pl.BlockSpec((E, 1), lambda b, i: (0, 0))],
        out_specs=pl.BlockSpec((1, 1, TOP_K * td), lambda b, i: (b * nt + i, 0, 0)),
        out_shape=jax.ShapeDtypeStruct((B * nt, 1, TOP_K * td), I32),
        compiler_params=_cparams(("arbitrary", "arbitrary")),
        name="slots", interpret=interpret,
    )(eid_t, rank_t, pstart.reshape(E, 1))
    n_used = (pend[-1:] // EXP_BLK).astype(I32)
    return block_e, block_rows, n_used, dest_tiles, nb * EXP_BLK


def _forward(x, c, positions, w_ada, b_ada, norm1_g, norm2_g, w_in, rw_mu, rw_w0, rw_w2,
             rw_a0, rw_a2, rw_g2, rw_k_k, rw_k_a, rw_r_k, rw_ln_w, rw_ln_b, q_norm_g,
             k_norm_g, idx_ln_w, idx_ln_b, at_out_g, w_out, router_w, router_bias,
             exp_w1, exp_w3, exp_w2, shared_w1, shared_w3, shared_w2, interpret=False):
    B, S, D = x.shape
    depth = w_ada.shape[0]
    for l in range(depth):
        mod = _mod(c, w_ada[l], b_ada[l], interpret)
        shift1, scale1, gate1, shift2, scale2, gate2 = [
            m.reshape(B, 1, D) for m in jnp.split(mod, 6, axis=-1)]
        tabs = _rope_tables(positions, interpret)
        p_rw, k, ki, qt, vt, qit, wit = _inproj(
            x, scale1, shift1, norm1_g[l], w_in[l], k_norm_g[l], idx_ln_w[l], idx_ln_b[l],
            q_norm_g[l], tabs, interpret)
        y_rw = _rwkv(p_rw, rw_mu[l], rw_w0[l], rw_w2[l], rw_a0[l], rw_a2[l], rw_g2[l], rw_k_k[l],
                     rw_k_a[l], rw_r_k[l], rw_ln_w[l], rw_ln_b[l], interpret)
        y_at = _dsa(k, vt, ki, qt, qit, wit, at_out_g[l], interpret)
        base, h2, eid_t, gw_t, rank_t, cnt = _post(
            x, y_rw, y_at, gate1, scale2, shift2, gate2, norm2_g[l], w_out[l], router_w[l],
            router_bias[l], shared_w1[l], shared_w3[l], shared_w2[l], interpret)
        T = B * S
        td = min(S, ROW_TILE)
        block_e, block_rows, n_used, dest_tiles, n_slots = _slot_plan(
            cnt[:, 0].astype(I32), eid_t, rank_t, td, interpret)
        xs = _dispatch(h2.reshape(T, 1, D // 2), dest_tiles, n_slots, interpret)
        ys = _experts(xs, block_e, block_rows, n_used, exp_w1[l], exp_w3[l], exp_w2[l], interpret)
        gw_tok = gw_t.transpose(0, 2, 1).reshape(T, TOP_K)
        x = _combine(base.reshape(T, D), gate2, gw_tok, ys, dest_tiles, S // td,
                     interpret).reshape(B, S, D)
    return x


def kernel(x, c, positions, w_ada, b_ada, norm1_g, norm2_g, w_in, rw_mu, rw_w0, rw_w2, rw_a0, rw_a2, rw_g2, rw_k_k, rw_k_a, rw_r_k, rw_ln_w, rw_ln_b, q_norm_g, k_norm_g, idx_ln_w, idx_ln_b, at_out_g, w_out, router_w, router_bias, exp_w1, exp_w3, exp_w2, shared_w1, shared_w3, shared_w2):
    return _forward(x, c, positions, w_ada, b_ada, norm1_g, norm2_g, w_in, rw_mu, rw_w0, rw_w2,
                    rw_a0, rw_a2, rw_g2, rw_k_k, rw_k_a, rw_r_k, rw_ln_w, rw_ln_b, q_norm_g,
                    k_norm_g, idx_ln_w, idx_ln_b, at_out_g, w_out, router_w, router_bias,
                    exp_w1, exp_w3, exp_w2, shared_w1, shared_w3, shared_w2)
```

```python
import functools

import jax
import jax.numpy as jnp
import numpy as np
from jax import lax
from jax.experimental import pallas as pl
from jax.experimental.pallas import tpu as pltpu

F32 = jnp.float32
BF16 = jnp.bfloat16
I32 = jnp.int32
HIGHEST = lax.Precision.HIGHEST

LANES = 128
HEAD_DIM = 64
HALF = HEAD_DIM // 2
RW_HEADS = 8
RW_WIDTH = RW_HEADS * HEAD_DIM
AT_HEADS = 8
AT_WIDTH = AT_HEADS * HEAD_DIM
IDX_HEADS = 4
RW_LORA_W, RW_LORA_A, RW_LORA_G = 64, 64, 128
RW_COLS = 3 * RW_WIDTH + RW_LORA_W + RW_LORA_A + RW_LORA_G
KI_OFF = 3 * AT_WIDTH + IDX_HEADS * HEAD_DIM
WI_OFF = KI_OFF + HEAD_DIM
PT_ROWS = 2 * AT_WIDTH + IDX_HEADS * HEAD_DIM + 8
PK_COLS = AT_WIDTH + LANES
ROPE_THETA = 10000.0
NORM_EPS = 1e-6
LN_EPS = 1e-6
RW_GN_EPS = 64e-5
IDX_TOPK_MAX = 256
N_EXPERTS = 256
TOP_K = 8
N_GROUPS = 8
TOPK_GROUPS = 4
ROUTED_SCALE = 2.5
INT_MIN = -2 ** 31
PAIR_HI_MASK = -65536
NEG_BIG = -1e30

RW_CHUNK = 128
RW_BATCH = 2
QBLK = 128
SCORE_UNROLL = 4
ATTN_UNROLL = 4
COUNT_UNROLL = 4
EXP_BLK = 512
ROW_TILE = 256
VMEM_LIMIT = 48 * 1024 * 1024


def _cparams(sem):
    return pltpu.CompilerParams(dimension_semantics=sem, vmem_limit_bytes=VMEM_LIMIT)


def _sigmoid(x):
    return 1.0 / (1.0 + jnp.exp(-x))


def _silu(x):
    return x * _sigmoid(x)


def _dot(a, b):
    return jnp.dot(a, b, preferred_element_type=F32)


def _dot_split(a, b):
    hi = a.astype(BF16)
    lo = (a - hi.astype(F32)).astype(BF16)
    return _dot(hi, b) + _dot(lo, b)


def _pack_halves(x):
    w = x.shape[1] // 2
    lo = pltpu.bitcast(x[:, :w].astype(BF16).astype(F32), I32)
    hi = pltpu.bitcast(x[:, w:].astype(BF16).astype(F32), I32)
    return (hi & PAIR_HI_MASK) | lax.shift_right_logical(lo, 16)


def _unpack_halves(p):
    return pltpu.bitcast(p << 16, F32), pltpu.bitcast(p & PAIR_HI_MASK, F32)


def _dot_nt(a, b):
    return lax.dot_general(a, b, (((1,), (1,)), ((), ())), preferred_element_type=F32)


def _mod_kernel(c_ref, w_ref, b_ref, o_ref):
    c = c_ref[...]
    o_ref[...] = jnp.dot(_silu(c), w_ref[...], precision=HIGHEST,
                         preferred_element_type=F32) + b_ref[...]


def _mod(c, w_ada, b_ada, interpret):
    B, D = c.shape
    n = w_ada.shape[1] // D
    return pl.pallas_call(
        _mod_kernel,
        grid=(n,),
        in_specs=[pl.BlockSpec((B, D), lambda i: (0, 0)),
                  pl.BlockSpec((D, D), lambda i: (0, i)),
                  pl.BlockSpec((1, D), lambda i: (0, i))],
        out_specs=pl.BlockSpec((B, D), lambda i: (0, i)),
        out_shape=jax.ShapeDtypeStruct((B, n * D), F32),
        compiler_params=_cparams(("arbitrary",)),
        name="mod", interpret=interpret,
    )(c, w_ada, b_ada.reshape(1, -1))


def _rope_tab_kernel(pc_ref, pr_ref, cr_ref, sr_ref, ct_ref, st_ref):
    log_theta = float(np.log(ROPE_THETA))
    lane = lax.broadcasted_iota(I32, (1, LANES), 1)
    inv_r = jnp.exp((lane % HALF).astype(F32) * (-log_theta / HALF))
    ang = pc_ref[0].astype(F32) * inv_r
    cr_ref[0] = jnp.cos(ang)
    sr_ref[0] = jnp.where((lane % HEAD_DIM) < HALF, -jnp.sin(ang), jnp.sin(ang))
    sub = lax.broadcasted_iota(I32, (HALF, 1), 0)
    inv_c = jnp.exp(sub.astype(F32) * (-log_theta / HALF))
    ang_t = inv_c * pr_ref[0].astype(F32)
    ct_ref[0] = jnp.cos(ang_t)
    st_ref[0] = jnp.sin(ang_t)


def _rope_tables(positions, interpret):
    B, S = positions.shape
    ts = min(S, 512)
    return pl.pallas_call(
        _rope_tab_kernel,
        grid=(B, S // ts),
        in_specs=[pl.BlockSpec((1, ts, 1), lambda b, i: (b, i, 0)),
                  pl.BlockSpec((1, 1, ts), lambda b, i: (b, 0, i))],
        out_specs=[pl.BlockSpec((1, ts, LANES), lambda b, i: (b, i, 0)),
                   pl.BlockSpec((1, ts, LANES), lambda b, i: (b, i, 0)),
                   pl.BlockSpec((1, HALF, ts), lambda b, i: (b, 0, i)),
                   pl.BlockSpec((1, HALF, ts), lambda b, i: (b, 0, i))],
        out_shape=[jax.ShapeDtypeStruct((B, S, LANES), F32),
                   jax.ShapeDtypeStruct((B, S, LANES), F32),
                   jax.ShapeDtypeStruct((B, HALF, S), F32),
                   jax.ShapeDtypeStruct((B, HALF, S), F32)],
        compiler_params=_cparams(("arbitrary", "arbitrary")),
        name="rope_tab", interpret=interpret,
    )(positions.reshape(B, S, 1), positions.reshape(B, 1, S))


def _rope_rows(y, cos, sin_signed):
    lane = lax.broadcasted_iota(I32, (1, LANES), 1)
    partner = jnp.where((lane % HEAD_DIM) < HALF,
                        pltpu.roll(y, LANES - HALF, 1), pltpu.roll(y, HALF, 1))
    return y * cos + partner * sin_signed


def _rope_cols(y, cos_t, sin_t):
    x1, x2 = y[:, :HALF], y[:, HALF:]
    return jnp.concatenate([x1 * cos_t - x2 * sin_t, x2 * cos_t + x1 * sin_t], axis=1)


def _inproj_kernel(x_ref, sc_ref, sh_ref, g_ref, wrw_ref, wk_ref, wt_ref, kg_ref, iw_ref, ib_ref,
                   qg_ref, gsum_ref, cr_ref, sr_ref, ct_ref, st_ref,
                   prw_ref, k_ref, ki_ref, qt_ref, vt_ref, qit_ref, wit_ref):
    tm = x_ref.shape[1]
    x = x_ref[0]
    ms = jnp.mean(x * x, axis=-1, keepdims=True)
    h = x * lax.rsqrt(ms + NORM_EPS) * g_ref[...] * (1.0 + sc_ref[0]) + sh_ref[0]
    hb = h.astype(BF16)
    prw_ref[0] = _dot(hb, wrw_ref[...])
    pk = _dot(hb, wk_ref[...])
    pt = _dot_nt(wt_ref[...], hb)

    cos_r, sin_r = cr_ref[0], sr_ref[0]
    gsum = gsum_ref[...]
    inv_hd = 1.0 / HEAD_DIM
    for p in range(AT_WIDTH // LANES):
        xk = pk[:, p * LANES:(p + 1) * LANES]
        ss = jnp.dot(xk * xk, gsum, precision=HIGHEST, preferred_element_type=F32)
        y = xk * lax.rsqrt(ss * inv_hd + NORM_EPS) * kg_ref[...]
        k_ref[0, :, p * LANES:(p + 1) * LANES] = _rope_rows(y, cos_r, sin_r).astype(BF16)
    xi = pk[:, AT_WIDTH:AT_WIDTH + LANES]
    mu = jnp.dot(xi, gsum, precision=HIGHEST, preferred_element_type=F32) * inv_hd
    xc = xi - mu
    var = jnp.dot(xc * xc, gsum, precision=HIGHEST, preferred_element_type=F32) * inv_hd
    yi = xc * lax.rsqrt(var + LN_EPS) * iw_ref[...] + ib_ref[...]
    ki_ref[0] = _rope_rows(yi, cos_r, sin_r).astype(BF16)

    cos_t, sin_t = ct_ref[0][None], st_ref[0][None]
    xq = pt[0:AT_WIDTH].reshape(AT_HEADS, HEAD_DIM, tm)
    msq = jnp.mean(xq * xq, axis=1, keepdims=True)
    yq = xq * lax.rsqrt(msq + NORM_EPS) * qg_ref[...][None]
    yq = _rope_cols(yq, cos_t, sin_t) * (HEAD_DIM ** -0.5)
    zq = jnp.zeros((HEAD_DIM, tm), BF16)
    for hh in range(AT_HEADS):
        parts = [yq[hh].astype(BF16), zq] if hh % 2 == 0 else [zq, yq[hh].astype(BF16)]
        qt_ref[0, hh] = jnp.concatenate(parts, axis=0)
    vt = pt[AT_WIDTH:2 * AT_WIDTH].astype(BF16)
    for cblk in range(tm // QBLK):
        vt_ref[0, cblk] = vt[:, cblk * QBLK:(cblk + 1) * QBLK]
    xqi = pt[2 * AT_WIDTH:2 * AT_WIDTH + IDX_HEADS * HEAD_DIM].reshape(IDX_HEADS, HEAD_DIM, tm)
    yqi = _rope_cols(xqi, cos_t, sin_t)
    for hh in range(IDX_HEADS):
        qit_ref[0, hh] = jnp.concatenate([yqi[hh].astype(BF16), zq], axis=0)
    wit_ref[0] = pt[PT_ROWS - 8:PT_ROWS] * (IDX_HEADS ** -0.5 * HEAD_DIM ** -0.5)


def _inproj(x, scale1, shift1, norm1_g, w_in, k_norm_g, idx_ln_w, idx_ln_b, q_norm_g,
            tabs, interpret):
    B, S, D = x.shape
    tm = min(S, 256)
    cos_r, sin_r, cos_t, sin_t = tabs
    w_at = w_in[:, RW_COLS:]
    w_rw = w_in[:, :RW_COLS].astype(BF16)
    w_k = jnp.concatenate([w_at[:, AT_WIDTH:2 * AT_WIDTH], w_at[:, KI_OFF:KI_OFF + HEAD_DIM],
                           jnp.zeros((D, HEAD_DIM), F32)], axis=1).astype(BF16)
    w_t = jnp.concatenate([w_at[:, 0:AT_WIDTH], w_at[:, 2 * AT_WIDTH:3 * AT_WIDTH],
                           w_at[:, 3 * AT_WIDTH:KI_OFF], w_at[:, WI_OFF:WI_OFF + IDX_HEADS],
                           jnp.zeros((D, 8 - IDX_HEADS), F32)], axis=1).T.astype(BF16)
    kg = jnp.tile(k_norm_g, 2).reshape(1, LANES)
    zpad = jnp.zeros((HEAD_DIM,), F32)
    iw = jnp.concatenate([idx_ln_w, zpad]).reshape(1, LANES)
    ib = jnp.concatenate([idx_ln_b, zpad]).reshape(1, LANES)
    qg = q_norm_g.reshape(HEAD_DIM, 1)
    li = np.arange(LANES)
    gsum = jnp.asarray((li[:, None] // HEAD_DIM == li[None, :] // HEAD_DIM).astype(np.float32))

    full = lambda shape: pl.BlockSpec(shape, lambda b, i: (0,) * len(shape))
    return pl.pallas_call(
        _inproj_kernel,
        grid=(B, S // tm),
        in_specs=[pl.BlockSpec((1, tm, D), lambda b, i: (b, i, 0)),
                  pl.BlockSpec((1, 1, D), lambda b, i: (b, 0, 0)),
                  pl.BlockSpec((1, 1, D), lambda b, i: (b, 0, 0)),
                  full((1, D)), full((D, RW_COLS)), full((D, PK_COLS)), full((PT_ROWS, D)),
                  full((1, LANES)), full((1, LANES)), full((1, LANES)), full((HEAD_DIM, 1)),
                  full((LANES, LANES)),
                  pl.BlockSpec((1, tm, LANES), lambda b, i: (b, i, 0)),
                  pl.BlockSpec((1, tm, LANES), lambda b, i: (b, i, 0)),
                  pl.BlockSpec((1, HALF, tm), lambda b, i: (b, 0, i)),
                  pl.BlockSpec((1, HALF, tm), lambda b, i: (b, 0, i))],
        out_specs=[pl.BlockSpec((1, tm, RW_COLS), lambda b, i: (b, i, 0)),
                   pl.BlockSpec((1, tm, AT_WIDTH), lambda b, i: (b, i, 0)),
                   pl.BlockSpec((1, tm, LANES), lambda b, i: (b, i, 0)),
                   pl.BlockSpec((1, AT_HEADS, LANES, tm), lambda b, i: (b, 0, 0, i)),
                   pl.BlockSpec((1, tm // QBLK, AT_WIDTH, QBLK), lambda b, i: (b, i, 0, 0)),
                   pl.BlockSpec((1, IDX_HEADS, LANES, tm), lambda b, i: (b, 0, 0, i)),
                   pl.BlockSpec((1, 8, tm), lambda b, i: (b, 0, i))],
        out_shape=[jax.ShapeDtypeStruct((B, S, RW_COLS), F32),
                   jax.ShapeDtypeStruct((B, S, AT_WIDTH), BF16),
                   jax.ShapeDtypeStruct((B, S, LANES), BF16),
                   jax.ShapeDtypeStruct((B, AT_HEADS, LANES, S), BF16),
                   jax.ShapeDtypeStruct((B, S // QBLK, AT_WIDTH, QBLK), BF16),
                   jax.ShapeDtypeStruct((B, IDX_HEADS, LANES, S), BF16),
                   jax.ShapeDtypeStruct((B, 8, S), F32)],
        compiler_params=_cparams(("arbitrary", "arbitrary")),
        name="inproj", interpret=interpret,
    )(x, scale1, shift1, norm1_g.reshape(1, D), w_rw, w_k, w_t, kg, iw, ib, qg, gsum,
      cos_r, sin_r, cos_t, sin_t)


def _rwkv_kernel(p_ref, mu_ref, w0_ref, w2_ref, a0_ref, a2_ref, g2_ref, kk_ref, ka_ref, rk_ref,
                 lnw_ref, lnb_ref, gsum_ref, y_ref, s_ref, prev_ref, yt_ref):
    C = RW_CHUNK
    W = RW_WIDTH
    nb = p_ref.shape[0]

    @pl.when(pl.program_id(1) == 0)
    def _():
        s_ref[...] = jnp.zeros_like(s_ref)
        prev_ref[...] = jnp.zeros_like(prev_ref)

    row = lax.broadcasted_iota(I32, (C, 1), 0)
    gsum = gsum_ref[...]

    def prepare(bi):
        p = p_ref[bi]
        pprev = jnp.where(row == 0, prev_ref[bi:bi + 1, :], pltpu.roll(p, 1, 0))
        prev_ref[bi:bi + 1, :] = p[C - 1:C]
        ps = p + (pprev - p) * mu_ref[...]
        r, k, v = ps[:, 0:W], ps[:, W:2 * W], ps[:, 2 * W:3 * W]
        o = 3 * W
        wl = ps[:, o:o + RW_LORA_W]
        al = ps[:, o + RW_LORA_W:o + RW_LORA_W + RW_LORA_A]
        gl = ps[:, o + RW_LORA_W + RW_LORA_A:]
        z = w0_ref[...] + _dot(jnp.tanh(wl).astype(BF16), w2_ref[...])
        nz = -z
        softplus = jnp.maximum(nz, 0.0) + jnp.log(1.0 + jnp.exp(-jnp.abs(nz)))
        logw = -jnp.exp(-softplus - 0.5)
        a = _sigmoid(a0_ref[...] + _dot(al.astype(BF16), a2_ref[...]))
        g = _dot(_sigmoid(gl).astype(BF16), g2_ref[...])
        kk = k * kk_ref[...]
        ss = _dot((kk * kk).astype(BF16), gsum)
        kk = kk * (1.0 / jnp.maximum(jnp.sqrt(ss), 1e-12))
        k2 = k * (1.0 + (a - 1.0) * ka_ref[...])
        bb = kk * a
        cw = logw
        sh = 1
        while sh < C:
            cw = cw + jnp.where(row >= sh, pltpu.roll(cw, sh, 0), 0.0)
            sh *= 2
        cw_last = cw[C - 1:C]
        e_neg = jnp.exp(-cw)
        e_end = jnp.exp(cw_last - cw)
        return dict(r=r, v=v, k2=k2, g=g, rw=r * jnp.exp(cw), kkp=kk * jnp.exp(cw - logw),
                    bw=bb * e_neg, kw=k2 * e_neg, bend=bb * e_end, kend=k2 * e_end,
                    wc=jnp.exp(cw_last), vt=v.T.astype(BF16))

    pre = [prepare(bi) for bi in range(nb)]

    ri = lax.broadcasted_iota(I32, (C, C), 0)
    ci = lax.broadcasted_iota(I32, (C, C), 1)
    strict = ri < ci
    incl = ri <= ci
    incl2 = jnp.concatenate([incl, incl], axis=0)
    lane_half = lax.broadcasted_iota(I32, (1, LANES), 1) // HEAD_DIM

    chains = [(bi, h) for bi in range(nb) for h in range(RW_HEADS)]
    pair = lambda h: slice((h // 2) * LANES, (h // 2 + 1) * LANES)
    own = lambda h: lane_half == (h % 2)
    stack = lambda bi, top, bot, h: jnp.concatenate(
        [pre[bi][top][:, pair(h)], pre[bi][bot][:, pair(h)]], axis=0)
    lh = {(bi, q): stack(bi, 'kkp', 'rw', 2 * q).astype(BF16)
          for bi in range(nb) for q in range(RW_HEADS // 2)}
    rh = [jnp.where(own(h), stack(bi, 'bw', 'kw', h), 0.0).astype(BF16) for bi, h in chains]
    aat = [_dot_nt(rh[n], lh[bi, h // 2]) for n, (bi, h) in enumerate(chains)]
    s_old = [s_ref[bi * RW_HEADS + h] for bi, h in chains]
    sl = [_dot_nt(s_old[n].astype(BF16), lh[bi, h // 2]) for n, (bi, h) in enumerate(chains)]
    vt = [pre[bi]['vt'][h * HEAD_DIM:(h + 1) * HEAD_DIM] for bi, h in chains]
    ids = range(len(chains))
    akt = [jnp.where(strict, aat[n][C:, :C], 0.0).astype(BF16) for n in ids]
    m = [jnp.where(strict, aat[n][:C, :C], 0.0).astype(BF16) for n in ids]
    xs = [-(sl[n][:, :C] + _dot(vt[n], akt[n])) for n in ids]
    xs = [xs[n] - _dot_split(xs[n], m[n]) for n in ids]
    lvl = 2
    while lvl < C:
        m = [_dot(m[n], m[n]).astype(BF16) for n in ids]
        xs = [xs[n] + _dot_split(xs[n], m[n]) for n in ids]
        lvl *= 2
    zt = [jnp.concatenate([xs[n].astype(BF16), vt[n]], axis=1) for n in ids]
    for n in ids:
        ymat = jnp.where(incl2, aat[n][:, C:], 0.0).astype(BF16)
        yt_ref[n * HEAD_DIM:(n + 1) * HEAD_DIM, :] = sl[n][:, C:] + _dot(zt[n], ymat)
    for n, (bi, h) in enumerate(chains):
        endz = jnp.where(own(h), stack(bi, 'bend', 'kend', h), 0.0).astype(BF16)
        s_ref[n] = s_old[n] * pre[bi]['wc'][:, pair(h)] + _dot(zt[n], endz)

    lnw = lnw_ref[...].reshape(RW_HEADS, HEAD_DIM, 1)
    lnb = lnb_ref[...].reshape(RW_HEADS, HEAD_DIM, 1)
    for bi in range(nb):
        q = pre[bi]
        yt = yt_ref[bi * W:(bi + 1) * W, :].reshape(RW_HEADS, HEAD_DIM, C)
        mean = jnp.mean(yt, axis=1, keepdims=True)
        yc = yt - mean
        var = jnp.mean(yc * yc, axis=1, keepdims=True)
        yn = yc * lax.rsqrt(var + RW_GN_EPS) * lnw + lnb
        y = yn.reshape(W, C).T
        bonus = _dot((q['r'] * q['k2'] * rk_ref[...]).astype(BF16), gsum) * q['v']
        y_ref[bi] = ((y + bonus) * q['g']).astype(BF16)


def _rwkv(p_rw, rw_mu, rw_w0, rw_w2, rw_a0, rw_a2, rw_g2, rw_k_k, rw_k_a, rw_r_k, rw_ln_w, rw_ln_b,
          interpret):
    B, S, _ = p_rw.shape
    C, W = RW_CHUNK, RW_WIDTH
    li = np.arange(W)
    gsum = jnp.asarray((li[:, None] // HEAD_DIM == li[None, :] // HEAD_DIM).astype(np.float32)).astype(BF16)
    row = lambda a: a.reshape(1, -1)
    full = lambda shape: pl.BlockSpec(shape, lambda b, i: (0,) * len(shape))
    nb = RW_BATCH if B % RW_BATCH == 0 else 1
    return pl.pallas_call(
        _rwkv_kernel,
        grid=(B // nb, S // C),
        in_specs=[pl.BlockSpec((nb, C, RW_COLS), lambda b, i: (b, i, 0)),
                  full((1, RW_COLS)), full((1, W)), full((RW_LORA_W, W)), full((1, W)),
                  full((RW_LORA_A, W)), full((RW_LORA_G, W)), full((1, W)), full((1, W)),
                  full((1, W)), full((W, 1)), full((W, 1)), full((W, W))],
        out_specs=pl.BlockSpec((nb, C, W), lambda b, i: (b, i, 0)),
        out_shape=jax.ShapeDtypeStruct((B, S, W), BF16),
        scratch_shapes=[pltpu.VMEM((nb * RW_HEADS, HEAD_DIM, LANES), F32),
                        pltpu.VMEM((nb, RW_COLS), F32),
                        pltpu.VMEM((nb * W, C), F32)],
        compiler_params=_cparams(("arbitrary", "arbitrary")),
        name="rwkv", interpret=interpret,
    )(p_rw, row(rw_mu), row(rw_w0), rw_w2.astype(BF16), row(rw_a0), rw_a2.astype(BF16),
      rw_g2.astype(BF16), row(rw_k_k), row(rw_k_a), row(rw_r_k), rw_ln_w.reshape(W, 1),
      rw_ln_b.reshape(W, 1), gsum)


def _dsa_kernel(topk, nbits, k_ref, vt_ref, ki_ref, qt_ref, qit_ref, wit_ref, og_ref, o_ref,
                key_s, acc_s, m_s, l_s, thr_s):
    j = pl.program_id(1)
    nkb = j + 1
    lane = lax.broadcasted_iota(I32, (QBLK, QBLK), 1)
    sub = lax.broadcasted_iota(I32, (QBLK, QBLK), 0)
    qpos = j * QBLK + lane
    wit = wit_ref[0]

    def score_blocks(i, carry):
        kbs = [i * SCORE_UNROLL + u for u in range(SCORE_UNROLL)]
        kib = [ki_ref[0, pl.ds(pl.multiple_of(kb * QBLK, QBLK), QBLK), :] for kb in kbs]
        lg = [[_dot(kib[u], qit_ref[0, hh]) for hh in range(IDX_HEADS)] for u in range(SCORE_UNROLL)]
        for u, kb in enumerate(kbs):
            s = wit[0:1, :] * jnp.maximum(lg[u][0], 0.0)
            for hh in range(1, IDX_HEADS):
                s = s + wit[hh:hh + 1, :] * jnp.maximum(lg[u][hh], 0.0)
            s = jnp.where(s == 0.0, 0.0, s)
            bits = pltpu.bitcast(s, I32)
            skey = jnp.where(bits < 0, bits ^ 0x7FFFFFFF, bits)
            key_s[kb] = jnp.where(kb * QBLK + sub <= qpos, skey, INT_MIN)
        return carry

    lax.fori_loop(0, pl.cdiv(nkb, SCORE_UNROLL), score_blocks, 0)

    @pl.when(nkb * QBLK <= topk)
    def _():
        thr_s[0:1, :] = jnp.full((1, QBLK), INT_MIN, I32)
        thr_s[1:2, :] = jnp.zeros((1, QBLK), I32)

    @pl.when(nkb * QBLK > topk)
    def _():
        n_done = pl.cdiv(nkb, SCORE_UNROLL) * SCORE_UNROLL
        n_cnt = pl.cdiv(nkb, COUNT_UNROLL)

        def fill(kb, carry):
            key_s[kb] = jnp.full((QBLK, QBLK), INT_MIN, I32)
            return carry

        lax.fori_loop(n_done, n_cnt * COUNT_UNROLL, fill, 0)

        def count(preds):
            def body(i, accs):
                accs = list(accs)
                for u in range(COUNT_UNROLL):
                    kb = i * COUNT_UNROLL + u
                    ky = key_s[kb]
                    for n, pred in enumerate(preds):
                        hit = pred(ky, kb * QBLK + sub).astype(I32)
                        accs[n] = accs[n] + jnp.sum(hit.reshape(QBLK // 8, 8, QBLK), axis=0)
                return tuple(accs)
            accs = lax.fori_loop(0, n_cnt, body, tuple(jnp.zeros((8, QBLK), I32) for _ in preds))
            return [jnp.sum(a, axis=0, keepdims=True) for a in accs]

        c0, = count([lambda ky, ix: ky >= 0])
        t0 = jnp.where(c0 >= topk, 0, INT_MIN).astype(I32)

        def bit_step(i, t):
            cand = t | jnp.left_shift(jnp.int32(1), 30 - i)
            c, = count([lambda ky, ix: ky >= cand])
            return jnp.where(c >= topk, cand, t)

        thr = lax.fori_loop(0, 31, bit_step, t0)
        n_gt, n_eq = count([lambda ky, ix: ky > thr, lambda ky, ix: ky == thr])
        need = topk - n_gt
        thr_s[0:1, :] = thr
        thr_s[1:2, :] = jnp.full((1, QBLK), 2 ** nbits, I32)

        @pl.when(jnp.max(jnp.abs(n_eq - need)) > 0)
        def _():
            def idx_step(i, mm):
                cand = mm | jnp.left_shift(jnp.int32(1), nbits - 1 - i)
                c, = count([lambda ky, ix: (ky == thr) & (ix < cand)])
                return jnp.where(c < need, cand, mm)

            thr_s[1:2, :] = lax.fori_loop(0, nbits, idx_step, jnp.zeros((1, QBLK), I32))

    thr = thr_s[0:1, :]
    mm = thr_s[1:2, :]
    m_s[...] = jnp.full_like(m_s, NEG_BIG)
    l_s[...] = jnp.zeros_like(l_s)
    acc_s[...] = jnp.zeros_like(acc_s)

    qt2 = [jnp.concatenate([qt_ref[0, 2 * q], qt_ref[0, 2 * q + 1]], axis=1)
           for q in range(AT_HEADS // 2)]

    def attn_blocks(i, carry):
        kbs = [i * ATTN_UNROLL + u for u in range(ATTN_UNROLL)]
        sel = []
        for kb in kbs:
            skey = key_s[kb]
            kidx = kb * QBLK + sub
            sel.append((kidx <= qpos) & ((skey > thr) | ((skey == thr) & (kidx <= mm))))
        sel = jnp.concatenate(sel, axis=0)
        kblk = [k_ref[0, pl.ds(pl.multiple_of(kb * QBLK, QBLK), QBLK), :] for kb in kbs]
        vtb = jnp.concatenate([vt_ref[0, kb] for kb in kbs], axis=1)
        s2 = [[_dot(kblk[u][:, q * LANES:(q + 1) * LANES], qt2[q]) for u in range(ATTN_UNROLL)]
              for q in range(AT_HEADS // 2)]
        pexp, alpha = [], []
        for hh in range(AT_HEADS):
            half = slice((hh % 2) * QBLK, (hh % 2 + 1) * QBLK)
            s = jnp.concatenate([s2[hh // 2][u][:, half] for u in range(ATTN_UNROLL)], axis=0)
            s = jnp.where(sel, s, NEG_BIG)
            m_old = m_s[hh:hh + 1, :]
            m_new = jnp.maximum(m_old, jnp.max(s, axis=0, keepdims=True))
            pe = jnp.exp(s - m_new)
            al = jnp.exp(m_old - m_new)
            l_s[hh:hh + 1, :] = al * l_s[hh:hh + 1, :] + jnp.sum(pe, axis=0, keepdims=True)
            m_s[hh:hh + 1, :] = m_new
            pexp.append(pe.astype(BF16))
            alpha.append(al)
        for hh in range(AT_HEADS):
            hs = slice(hh * HEAD_DIM, (hh + 1) * HEAD_DIM)
            acc_s[hs, :] = alpha[hh] * acc_s[hs, :] + _dot(vtb[hs, :], pexp[hh])
        return carry

    lax.fori_loop(0, pl.cdiv(nkb, ATTN_UNROLL), attn_blocks, 0)

    for hh in range(AT_HEADS):
        hs = slice(hh * HEAD_DIM, (hh + 1) * HEAD_DIM)
        oh = acc_s[hs, :] * (1.0 / l_s[hh:hh + 1, :])
        ms = jnp.mean(oh * oh, axis=0, keepdims=True)
        acc_s[hs, :] = oh * lax.rsqrt(ms + NORM_EPS) * og_ref[hs, :]
    o_ref[0] = acc_s[...].T.astype(BF16)


def _dsa(k, vt, ki, qt, qit, wit, at_out_g, interpret):
    B, S, _ = k.shape
    nq = S // QBLK
    topk = min(IDX_TOPK_MAX, S // 4)
    nbits = int(np.log2(S))
    assert 2 ** nbits == S and nq % COUNT_UNROLL == 0 and SCORE_UNROLL == ATTN_UNROLL
    assert COUNT_UNROLL % SCORE_UNROLL == 0
    return pl.pallas_call(
        functools.partial(_dsa_kernel, topk, nbits),
        grid=(B, nq),
        in_specs=[pl.BlockSpec((1, S, AT_WIDTH), lambda b, j: (b, 0, 0)),
                  pl.BlockSpec((1, nq, AT_WIDTH, QBLK), lambda b, j: (b, 0, 0, 0)),
                  pl.BlockSpec((1, S, LANES), lambda b, j: (b, 0, 0)),
                  pl.BlockSpec((1, AT_HEADS, LANES, QBLK), lambda b, j: (b, 0, 0, j)),
                  pl.BlockSpec((1, IDX_HEADS, LANES, QBLK), lambda b, j: (b, 0, 0, j)),
                  pl.BlockSpec((1, 8, QBLK), lambda b, j: (b, 0, j)),
                  pl.BlockSpec((AT_WIDTH, 1), lambda b, j: (0, 0))],
        out_specs=pl.BlockSpec((1, QBLK, AT_WIDTH), lambda b, j: (b, j, 0)),
        out_shape=jax.ShapeDtypeStruct((B, S, AT_WIDTH), BF16),
        scratch_shapes=[pltpu.VMEM((nq, QBLK, QBLK), I32),
                        pltpu.VMEM((AT_WIDTH, QBLK), F32),
                        pltpu.VMEM((AT_HEADS, QBLK), F32),
                        pltpu.VMEM((AT_HEADS, QBLK), F32),
                        pltpu.VMEM((8, QBLK), I32)],
        compiler_params=_cparams(("arbitrary", "arbitrary")),
        name="dsa", interpret=interpret,
    )(k, vt, ki, qt, qit, wit, at_out_g.reshape(AT_WIDTH, 1))


def _first_max(vals, idx, axis, sentinel):
    m = jnp.max(vals, axis=axis, keepdims=True)
    return m, jnp.min(jnp.where(vals == m, idx, sentinel), axis=axis, keepdims=True)


def _route_cols(logits_t, bias_col):
    E, tm = logits_t.shape
    pg = E // N_GROUPS
    scores = _sigmoid(logits_t)
    biased = scores + bias_col
    b3 = biased.reshape(N_GROUPS, pg, tm)
    r3 = lax.broadcasted_iota(I32, (N_GROUPS, pg, tm), 1)
    m1, first = _first_max(b3, r3, 1, pg)
    m2 = jnp.max(jnp.where(r3 == first, -jnp.inf, b3), axis=1, keepdims=True)
    cur = (m1 + m2).reshape(N_GROUPS, tm)
    grow = lax.broadcasted_iota(I32, (N_GROUPS, tm), 0)
    gsel = jnp.zeros((N_GROUPS, tm), F32)
    for _ in range(TOPK_GROUPS):
        _, gi = _first_max(cur, grow, 0, N_GROUPS)
        hit = grow == gi
        gsel = jnp.where(hit, 1.0, gsel)
        cur = jnp.where(hit, -jnp.inf, cur)
    gmask = jnp.broadcast_to(gsel.reshape(N_GROUPS, 1, tm), (N_GROUPS, pg, tm)).reshape(E, tm)
    cur = jnp.where(gmask > 0.0, biased, -jnp.inf)
    row = lax.broadcasted_iota(I32, (E, tm), 0)
    onehot = jnp.zeros((E, tm), F32)
    eids, gws = [], []
    for _ in range(TOP_K):
        _, ei = _first_max(cur, row, 0, E)
        hit = row == ei
        eids.append(ei)
        gws.append(jnp.sum(jnp.where(hit, scores, 0.0), axis=0, keepdims=True))
        onehot = jnp.where(hit, 1.0, onehot)
        cur = jnp.where(hit, -jnp.inf, cur)
    eid = jnp.concatenate(eids, axis=0)
    gw = jnp.concatenate(gws, axis=0)
    gw = gw * (ROUTED_SCALE / jnp.sum(gw, axis=0, keepdims=True))
    return eid, gw, onehot


def _post_kernel(x_ref, yrw_ref, yat_ref, g1_ref, sc_ref, sh_ref, g2_ref, ng_ref, wo_ref, rwt_ref,
                 rb_ref, s1_ref, s3_ref, s2_ref, base_ref, h2_ref, eid_ref, gw_ref, rank_ref, cnt_ref):
    W = RW_WIDTH
    tm = x_ref.shape[1]
    E = rwt_ref.shape[0]

    @pl.when((pl.program_id(0) == 0) & (pl.program_id(1) == 0))
    def _():
        cnt_ref[...] = jnp.zeros_like(cnt_ref)

    mix = _dot(yrw_ref[0], wo_ref[0:W, :]) + _dot(yat_ref[0], wo_ref[W:, :])
    x1 = x_ref[0] + g1_ref[0] * mix
    ms = jnp.mean(x1 * x1, axis=-1, keepdims=True)
    h2 = x1 * lax.rsqrt(ms + NORM_EPS) * ng_ref[...] * (1.0 + sc_ref[0]) + sh_ref[0]
    hb = h2.astype(BF16)
    h2_ref[0, :, 0, :] = _pack_halves(h2)
    act =(_silu(_dot(hb, s1_ref[...])) * _dot(hb, s3_ref[...])).astype(BF16)
    base_ref[0] = x1 + g2_ref[0] * _dot(act, s2_ref[...])

    logits_t = lax.dot_general(rwt_ref[...], h2, (((1,), (1,)), ((), ())), precision=HIGHEST,
                               preferred_element_type=F32)
    eid, gw, onehot = _route_cols(logits_t, rb_ref[...])
    eid_ref[0] = eid
    gw_ref[0] = gw
    ti = lax.broadcasted_iota(I32, (tm, tm), 0)
    tj = lax.broadcasted_iota(I32, (tm, tm), 1)
    before = _dot(onehot.astype(BF16), (ti < tj).astype(BF16)) + cnt_ref[:, 0:1]
    row = lax.broadcasted_iota(I32, (E, tm), 0)
    ranks = [jnp.sum(jnp.where(row == eid[kk:kk + 1, :], before, 0.0), axis=0, keepdims=True)
             for kk in range(TOP_K)]
    rank_ref[0] = jnp.concatenate(ranks, axis=0).astype(I32)
    cnt_ref[...] = cnt_ref[...] + jnp.sum(onehot, axis=1, keepdims=True)


def _post(x, y_rw, y_at, gate1, scale2, shift2, gate2, norm2_g, w_out, router_w, router_bias,
          sw1, sw3, sw2, interpret):
    B, S, D = x.shape
    tm = min(S, 256)
    E = router_w.shape[1]
    sd = sw1.shape[1]
    full = lambda shape: pl.BlockSpec(shape, lambda b, i: (0,) * len(shape))
    tok = lambda w: pl.BlockSpec((1, tm, w), lambda b, i: (b, i, 0))
    per_b = pl.BlockSpec((1, 1, D), lambda b, i: (b, 0, 0))
    col8 = pl.BlockSpec((1, TOP_K, tm), lambda b, i: (b, 0, i))
    return pl.pallas_call(
        _post_kernel,
        grid=(B, S // tm),
        in_specs=[tok(D), tok(RW_WIDTH), tok(AT_WIDTH), per_b, per_b, per_b, per_b, full((1, D)),
                  full((D, D)), full((E, D)), full((E, 1)), full((D, sd)), full((D, sd)),
                  full((sd, D))],
        out_specs=[tok(D), pl.BlockSpec((1, tm, 1, D // 2), lambda b, i: (b, i, 0, 0)),
                   col8, col8, col8, full((E, LANES))],
        out_shape=[jax.ShapeDtypeStruct((B, S, D), F32),
                   jax.ShapeDtypeStruct((B, S, 1, D // 2), I32),
                   jax.ShapeDtypeStruct((B, TOP_K, S), I32),
                   jax.ShapeDtypeStruct((B, TOP_K, S), F32),
                   jax.ShapeDtypeStruct((B, TOP_K, S), I32),
                   jax.ShapeDtypeStruct((E, LANES), F32)],
        compiler_params=_cparams(("arbitrary", "arbitrary")),
        name="post", interpret=interpret,
    )(x, y_rw, y_at, gate1, scale2, shift2, gate2, norm2_g.reshape(1, D), w_out.astype(BF16),
      router_w.T, router_bias.reshape(E, 1), sw1.astype(BF16), sw3.astype(BF16), sw2.astype(BF16))


def _expert_kernel(be_ref, nv_ref, nu_ref, xs_ref, w1_ref, w3_ref, w2_ref, o_ref, w13_s, w2_s):
    i = pl.program_id(0)
    blk, _, hw = xs_ref.shape
    F = w1_ref.shape[2]
    used = i < nu_ref[0]

    @pl.when(used & ((i == 0) | (be_ref[i] != be_ref[jnp.maximum(i - 1, 0)])))
    def _():
        w13_s[:, :F] = w1_ref[0].astype(BF16)
        w13_s[:, F:] = w3_ref[0].astype(BF16)
        w2_s[...] = w2_ref[0].astype(BF16)

    @pl.when(used)
    def _():
        live = lax.broadcasted_iota(I32, (blk, 1), 0) < nv_ref[i]
        x_lo, x_hi = _unpack_halves(jnp.where(live, xs_ref[:, 0, :], 0))
        h13 =(_dot(x_lo.astype(BF16), w13_s[:hw, :]) + _dot(x_hi.astype(BF16), w13_s[hw:, :]))
        act = (_silu(h13[:, :F]) * h13[:, F:]).astype(BF16)
        packed = _pack_halves(_dot(act, w2_s[...]))
        for c in range(o_ref.shape[1]):
            o_ref[:, c, :] = packed[:, c * LANES:(c + 1) * LANES]


def _experts(xs, block_e, block_rows, n_used, w1, w3, w2, interpret):
    P, _, hw = xs.shape
    E, D, F = w1.shape
    nb = P // EXP_BLK
    blk = lambda i, nu: jnp.minimum(i, nu[0] - 1)
    grid_spec = pltpu.PrefetchScalarGridSpec(
        num_scalar_prefetch=3,
        grid=(nb,),
        in_specs=[pl.BlockSpec((EXP_BLK, 1, hw), lambda i, be, nv, nu: (blk(i, nu), 0, 0)),
                  pl.BlockSpec((1, D, F), lambda i, be, nv, nu: (be[blk(i, nu)], 0, 0)),
                  pl.BlockSpec((1, D, F), lambda i, be, nv, nu: (be[blk(i, nu)], 0, 0)),
                  pl.BlockSpec((1, F, D), lambda i, be, nv, nu: (be[blk(i, nu)], 0, 0))],
        out_specs=pl.BlockSpec((EXP_BLK, hw // LANES, LANES), lambda i, be, nv, nu: (blk(i, nu), 0, 0)),
        scratch_shapes=[pltpu.VMEM((D, 2 * F), BF16), pltpu.VMEM((F, D), BF16)],
    )
    return pl.pallas_call(
        _expert_kernel,
        grid_spec=grid_spec,
        out_shape=jax.ShapeDtypeStruct((P, hw // LANES, LANES), I32),
        compiler_params=_cparams(("arbitrary",)),
        name="experts", interpret=interpret,
    )(block_e, block_rows, n_used, xs, w1, w3, w2)


def _row_out(tile, t, slots, slot, sem):
    return pltpu.make_async_copy(tile.at[t], slots.at[slot], sem)


def _dispatch_kernel(dest_ref, h2_ref, xs_ref, sem):
    td = h2_ref.shape[0]

    def issue(t, carry):
        for kk in range(TOP_K):
            _row_out(h2_ref, t, xs_ref, dest_ref[0, 0, kk * td + t], sem).start(priority=kk % 2)
        return carry

    def drain(t, carry):
        for kk in range(TOP_K):
            _row_out(h2_ref, t, xs_ref, dest_ref[0, 0, kk * td + t], sem).wait()
        return carry

    lax.fori_loop(0, td, issue, 0)
    lax.fori_loop(0, td, drain, 0)


def _dispatch(h2, dest_tiles, n_slots, interpret):
    T, _, D = h2.shape
    nt, _, n = dest_tiles.shape
    td = n // TOP_K
    return pl.pallas_call(
        _dispatch_kernel,
        grid=(nt,),
        in_specs=[pl.BlockSpec((1, 1, n), lambda i: (i, 0, 0), memory_space=pltpu.SMEM),
                  pl.BlockSpec((td, 1, D), lambda i: (i, 0, 0))],
        out_specs=pl.BlockSpec(memory_space=pl.ANY),
        out_shape=jax.ShapeDtypeStruct((n_slots, 1, D), h2.dtype),
        scratch_shapes=[pltpu.SemaphoreType.DMA(())],
        compiler_params=_cparams(("arbitrary",)),
        name="dispatch", interpret=interpret,
    )(dest_tiles, h2)


def _combine_kernel(dest_ref, base_ref, g2_ref, gw_ref, ys_ref, o_ref, buf, acc_s, sem):
    td = base_ref.shape[0]
    rq = ys_ref.shape[1]

    def row(t, kk):
        return pltpu.make_async_copy(ys_ref.at[dest_ref[0, 0, kk * td + t]],
                                     buf.at[kk, pl.ds(pl.multiple_of(t * rq, rq), rq), :], sem)

    def issue(t, carry):
        for kk in range(TOP_K):
            row(t, kk).start(priority=kk % 2)
        return carry

    def drain(t, carry):
        for kk in range(TOP_K):
            row(t, kk).wait()
        return carry

    lax.fori_loop(0, td, issue, 0)
    lax.fori_loop(0, td, drain, 0)
    gw = gw_ref[...]
    acc_lo, acc_hi = _unpack_halves(buf[0])
    acc_lo, acc_hi = gw[:, 0:1] * acc_lo, gw[:, 0:1] * acc_hi
    for kk in range(1, TOP_K):
        y_lo, y_hi = _unpack_halves(buf[kk])
        acc_lo = acc_lo + gw[:, kk:kk + 1] * y_lo
        acc_hi = acc_hi + gw[:, kk:kk + 1] * y_hi
    acc_s[0] = acc_lo
    acc_s[1] = acc_hi
    g2 = g2_ref[0]
    for half in range(2):
        for c in range(rq):
            cols = slice((half * rq + c) * LANES, (half * rq + c + 1) * LANES)
            part = acc_s[half, pl.ds(c, td, stride=rq), :]
            o_ref[:, cols] = base_ref[:, cols] + g2[:, cols] * part


def _combine(base, gate2, gw_tok, ys, dest_tiles, tiles_per_batch, interpret):
    T, D = base.shape
    nt, _, n = dest_tiles.shape
    td = n // TOP_K
    rq = ys.shape[1]
    gw_rows = jnp.repeat(gw_tok, rq, axis=0)
    return pl.pallas_call(
        _combine_kernel,
        grid=(nt,),
        in_specs=[pl.BlockSpec((1, 1, n), lambda i: (i, 0, 0), memory_space=pltpu.SMEM),
                  pl.BlockSpec((td, D), lambda i: (i, 0)),
                  pl.BlockSpec((1, 1, D), lambda i: (i // tiles_per_batch, 0, 0)),
                  pl.BlockSpec((td * rq, TOP_K), lambda i: (i, 0)),
                  pl.BlockSpec(memory_space=pl.ANY)],
        out_specs=pl.BlockSpec((td, D), lambda i: (i, 0)),
        out_shape=jax.ShapeDtypeStruct((T, D), F32),
        scratch_shapes=[pltpu.VMEM((TOP_K, td * rq, LANES), I32),
                        pltpu.VMEM((2, td * rq, LANES), F32),
                        pltpu.SemaphoreType.DMA(())],
        compiler_params=_cparams(("arbitrary",)),
        name="combine", interpret=interpret,
    )(dest_tiles, base, gate2, gw_rows, ys)


def _slots_kernel(eid_ref, rank_ref, pstart_ref, dest_ref):
    td = eid_ref.shape[2]
    E = pstart_ref.shape[0]
    row = lax.broadcasted_iota(I32, (E, td), 0)
    pstart = pstart_ref[...]
    eid = eid_ref[0]
    for kk in range(TOP_K):
        base = jnp.sum(jnp.where(row == eid[kk:kk + 1, :], pstart, 0), axis=0, keepdims=True)
        dest_ref[0, :, kk * td:(kk + 1) * td] = base + rank_ref[0, kk:kk + 1, :]


def _slot_plan(counts, eid_t, rank_t, td, interpret):
    B, _, S = eid_t.shape
    E = counts.shape[0]
    padded = (counts + EXP_BLK - 1) // EXP_BLK * EXP_BLK
    pend = jnp.cumsum(padded)
    pstart = (pend - padded).astype(I32)
    nb = -(-(B * S * TOP_K + E * (EXP_BLK - 1)) // EXP_BLK)
    first_row = jnp.arange(nb, dtype=I32) * EXP_BLK
    block_e = jnp.sum(pend[None, :] <= first_row[:, None], axis=1)
    block_e = jnp.minimum(block_e, E - 1).astype(I32)
    block_rows = jnp.clip(pstart[block_e] + counts[block_e] - first_row, 0, EXP_BLK).astype(I32)
    nt = S // td
    dest_tiles = pl.pallas_call(
        _slots_kernel,
        grid=(B, nt),
        in_specs=[pl.BlockSpec((1, TOP_K, td), lambda b, i: (b, 0, i)),
                  pl.BlockSpec((1, TOP_K, td), lambda b, i: (b, 0, i)),
                  pl.BlockSpec((E, 1), lambda b, i: (0, 0))],
        out_specs=pl.BlockSpec((1, 1, TOP_K * td), lambda b, i: (b * nt + i, 0, 0)),
        out_shape=jax.ShapeDtypeStruct((B * nt, 1, TOP_K * td), I32),
        compiler_params=_cparams(("arbitrary", "arbitrary")),
        name="slots", interpret=interpret,
    )(eid_t, rank_t, pstart.reshape(E, 1))
    n_used = (pend[-1:] // EXP_BLK).astype(I32)
    return block_e, block_rows, n_used, dest_tiles, nb * EXP_BLK


def _forward(x, c, positions, w_ada, b_ada, norm1_g, norm2_g, w_in, rw_mu, rw_w0, rw_w2,
             rw_a0, rw_a2, rw_g2, rw_k_k, rw_k_a, rw_r_k, rw_ln_w, rw_ln_b, q_norm_g,
             k_norm_g, idx_ln_w, idx_ln_b, at_out_g, w_out, router_w, router_bias,
             exp_w1, exp_w3, exp_w2, shared_w1, shared_w3, shared_w2, interpret=False):
    B, S, D = x.shape
    depth = w_ada.shape[0]
    for l in range(depth):
        mod = _mod(c, w_ada[l], b_ada[l], interpret)
        shift1, scale1, gate1, shift2, scale2, gate2 = [
            m.reshape(B, 1, D) for m in jnp.split(mod, 6, axis=-1)]
        tabs = _rope_tables(positions, interpret)
        p_rw, k, ki, qt, vt, qit, wit = _inproj(
            x, scale1, shift1, norm1_g[l], w_in[l], k_norm_g[l], idx_ln_w[l], idx_ln_b[l],
            q_norm_g[l], tabs, interpret)
        y_rw = _rwkv(p_rw, rw_mu[l], rw_w0[l], rw_w2[l], rw_a0[l], rw_a2[l], rw_g2[l], rw_k_k[l],
                     rw_k_a[l], rw_r_k[l], rw_ln_w[l], rw_ln_b[l], interpret)
        y_at = _dsa(k, vt, ki, qt, qit, wit, at_out_g[l], interpret)
        base, h2, eid_t, gw_t, rank_t, cnt = _post(
            x, y_rw, y_at, gate1, scale2, shift2, gate2, norm2_g[l], w_out[l], router_w[l],
            router_bias[l], shared_w1[l], shared_w3[l], shared_w2[l], interpret)
        T = B * S
        td = min(S, ROW_TILE)
        block_e, block_rows, n_used, dest_tiles, n_slots = _slot_plan(
            cnt[:, 0].astype(I32), eid_t, rank_t, td, interpret)
        xs = _dispatch(h2.reshape(T, 1, D // 2), dest_tiles, n_slots, interpret)
        ys = _experts(xs, block_e, block_rows, n_used, exp_w1[l], exp_w3[l], exp_w2[l], interpret)
        gw_tok = gw_t.transpose(0, 2, 1).reshape(T, TOP_K)
        x = _combine(base.reshape(T, D), gate2, gw_tok, ys, dest_tiles, S // td,
                     interpret).reshape(B, S, D)
    return x


def kernel(x, c, positions, w_ada, b_ada, norm1_g, norm2_g, w_in, rw_mu, rw_w0, rw_w2, rw_a0, rw_a2, rw_g2, rw_k_k, rw_k_a, rw_r_k, rw_ln_w, rw_ln_b, q_norm_g, k_norm_g, idx_ln_w, idx_ln_b, at_out_g, w_out, router_w, router_bias, exp_w1, exp_w3, exp_w2, shared_w1, shared_w3, shared_w2):
    return _forward(x, c, positions, w_ada, b_ada, norm1_g, norm2_g, w_in, rw_mu, rw_w0, rw_w2,
                    rw_a0, rw_a2, rw_g2, rw_k_k, rw_k_a, rw_r_k, rw_ln_w, rw_ln_b, q_norm_g,
                    k_norm_g, idx_ln_w, idx_ln_b, at_out_g, w_out, router_w, router_bias,
                    exp_w1, exp_w3, exp_w2, shared_w1, shared_w3, shared_w2)
```

```python
import functools

import jax
import jax.numpy as jnp
import numpy as np
from jax import lax
from jax.experimental import pallas as pl
from jax.experimental.pallas import tpu as pltpu

F32 = jnp.float32
BF16 = jnp.bfloat16
I32 = jnp.int32
HIGHEST = lax.Precision.HIGHEST

LANES = 128
HEAD_DIM = 64
HALF = HEAD_DIM // 2
RW_HEADS = 8
RW_WIDTH = RW_HEADS * HEAD_DIM
AT_HEADS = 8
AT_WIDTH = AT_HEADS * HEAD_DIM
IDX_HEADS = 4
RW_LORA_W, RW_LORA_A, RW_LORA_G = 64, 64, 128
RW_COLS = 3 * RW_WIDTH + RW_LORA_W + RW_LORA_A + RW_LORA_G
KI_OFF = 3 * AT_WIDTH + IDX_HEADS * HEAD_DIM
WI_OFF = KI_OFF + HEAD_DIM
PT_ROWS = 2 * AT_WIDTH + IDX_HEADS * HEAD_DIM + 8
PK_COLS = AT_WIDTH + LANES
ROPE_THETA = 10000.0
NORM_EPS = 1e-6
LN_EPS = 1e-6
RW_GN_EPS = 64e-5
IDX_TOPK_MAX = 256
N_EXPERTS = 256
TOP_K = 8
N_GROUPS = 8
TOPK_GROUPS = 4
ROUTED_SCALE = 2.5
INT_MIN = -2 ** 31
PAIR_HI_MASK = -65536
NEG_BIG = -1e30

RW_CHUNK = 128
RW_BATCH = 4
QBLK = 128
SCORE_UNROLL = 4
ATTN_UNROLL = 4
COUNT_UNROLL = 4
EXP_BLK = 512
ROW_TILE = 256
VMEM_LIMIT = 48 * 1024 * 1024


def _cparams(sem):
    return pltpu.CompilerParams(dimension_semantics=sem, vmem_limit_bytes=VMEM_LIMIT)


def _sigmoid(x):
    return 1.0 / (1.0 + jnp.exp(-x))


def _silu(x):
    return x * _sigmoid(x)


def _dot(a, b):
    return jnp.dot(a, b, preferred_element_type=F32)


def _dot_split(a, b):
    hi = a.astype(BF16)
    lo = (a - hi.astype(F32)).astype(BF16)
    return _dot(hi, b) + _dot(lo, b)


def _pack_halves(x):
    w = x.shape[1] // 2
    lo = pltpu.bitcast(x[:, :w].astype(BF16).astype(F32), I32)
    hi = pltpu.bitcast(x[:, w:].astype(BF16).astype(F32), I32)
    return (hi & PAIR_HI_MASK) | lax.shift_right_logical(lo, 16)


def _unpack_halves(p):
    return pltpu.bitcast(p << 16, F32), pltpu.bitcast(p & PAIR_HI_MASK, F32)


def _dot_nt(a, b):
    return lax.dot_general(a, b, (((1,), (1,)), ((), ())), preferred_element_type=F32)


def _mod_kernel(c_ref, w_ref, b_ref, o_ref):
    c = c_ref[...]
    o_ref[...] = jnp.dot(_silu(c), w_ref[...], precision=HIGHEST,
                         preferred_element_type=F32) + b_ref[...]


def _mod(c, w_ada, b_ada, interpret):
    B, D = c.shape
    n = w_ada.shape[1] // D
    return pl.pallas_call(
        _mod_kernel,
        grid=(n,),
        in_specs=[pl.BlockSpec((B, D), lambda i: (0, 0)),
                  pl.BlockSpec((D, D), lambda i: (0, i)),
                  pl.BlockSpec((1, D), lambda i: (0, i))],
        out_specs=pl.BlockSpec((B, D), lambda i: (0, i)),
        out_shape=jax.ShapeDtypeStruct((B, n * D), F32),
        compiler_params=_cparams(("arbitrary",)),
        name="mod", interpret=interpret,
    )(c, w_ada, b_ada.reshape(1, -1))


def _rope_tab_kernel(pc_ref, pr_ref, cr_ref, sr_ref, ct_ref, st_ref):
    log_theta = float(np.log(ROPE_THETA))
    lane = lax.broadcasted_iota(I32, (1, LANES), 1)
    inv_r = jnp.exp((lane % HALF).astype(F32) * (-log_theta / HALF))
    ang = pc_ref[0].astype(F32) * inv_r
    cr_ref[0] = jnp.cos(ang)
    sr_ref[0] = jnp.where((lane % HEAD_DIM) < HALF, -jnp.sin(ang), jnp.sin(ang))
    sub = lax.broadcasted_iota(I32, (HALF, 1), 0)
    inv_c = jnp.exp(sub.astype(F32) * (-log_theta / HALF))
    ang_t = inv_c * pr_ref[0].astype(F32)
    ct_ref[0] = jnp.cos(ang_t)
    st_ref[0] = jnp.sin(ang_t)


def _rope_tables(positions, interpret):
    B, S = positions.shape
    ts = min(S, 512)
    return pl.pallas_call(
        _rope_tab_kernel,
        grid=(B, S // ts),
        in_specs=[pl.BlockSpec((1, ts, 1), lambda b, i: (b, i, 0)),
                  pl.BlockSpec((1, 1, ts), lambda b, i: (b, 0, i))],
        out_specs=[pl.BlockSpec((1, ts, LANES), lambda b, i: (b, i, 0)),
                   pl.BlockSpec((1, ts, LANES), lambda b, i: (b, i, 0)),
                   pl.BlockSpec((1, HALF, ts), lambda b, i: (b, 0, i)),
                   pl.BlockSpec((1, HALF, ts), lambda b, i: (b, 0, i))],
        out_shape=[jax.ShapeDtypeStruct((B, S, LANES), F32),
                   jax.ShapeDtypeStruct((B, S, LANES), F32),
                   jax.ShapeDtypeStruct((B, HALF, S), F32),
                   jax.ShapeDtypeStruct((B, HALF, S), F32)],
        compiler_params=_cparams(("arbitrary", "arbitrary")),
        name="rope_tab", interpret=interpret,
    )(positions.reshape(B, S, 1), positions.reshape(B, 1, S))


def _rope_rows(y, cos, sin_signed):
    lane = lax.broadcasted_iota(I32, (1, LANES), 1)
    partner = jnp.where((lane % HEAD_DIM) < HALF,
                        pltpu.roll(y, LANES - HALF, 1), pltpu.roll(y, HALF, 1))
    return y * cos + partner * sin_signed


def _rope_cols(y, cos_t, sin_t):
    x1, x2 = y[:, :HALF], y[:, HALF:]
    return jnp.concatenate([x1 * cos_t - x2 * sin_t, x2 * cos_t + x1 * sin_t], axis=1)


def _inproj_kernel(x_ref, sc_ref, sh_ref, g_ref, wrw_ref, wk_ref, wt_ref, kg_ref, iw_ref, ib_ref,
                   qg_ref, gsum_ref, cr_ref, sr_ref, ct_ref, st_ref,
                   prw_ref, k_ref, ki_ref, qt_ref, vt_ref, qit_ref, wit_ref):
    tm = x_ref.shape[1]
    x = x_ref[0]
    ms = jnp.mean(x * x, axis=-1, keepdims=True)
    h = x * lax.rsqrt(ms + NORM_EPS) * g_ref[...] * (1.0 + sc_ref[0]) + sh_ref[0]
    hb = h.astype(BF16)
    prw_ref[0] = _dot(hb, wrw_ref[...])
    pk = _dot(hb, wk_ref[...])
    pt = _dot_nt(wt_ref[...], hb)

    cos_r, sin_r = cr_ref[0], sr_ref[0]
    gsum = gsum_ref[...]
    inv_hd = 1.0 / HEAD_DIM
    for p in range(AT_WIDTH // LANES):
        xk = pk[:, p * LANES:(p + 1) * LANES]
        ss = _dot_split(xk * xk, gsum)
        y = xk * lax.rsqrt(ss * inv_hd + NORM_EPS) * kg_ref[...]
        k_ref[0, :, p * LANES:(p + 1) * LANES] = _rope_rows(y, cos_r, sin_r).astype(BF16)
    xi = pk[:, AT_WIDTH:AT_WIDTH + LANES]
    mu = _dot_split(xi, gsum) * inv_hd
    xc = xi - mu
    var = _dot_split(xc * xc, gsum) * inv_hd
    yi = xc * lax.rsqrt(var + LN_EPS) * iw_ref[...] + ib_ref[...]
    ki_ref[0] = _rope_rows(yi, cos_r, sin_r).astype(BF16)

    cos_t, sin_t = ct_ref[0][None], st_ref[0][None]
    xq = pt[0:AT_WIDTH].reshape(AT_HEADS, HEAD_DIM, tm)
    msq = jnp.mean(xq * xq, axis=1, keepdims=True)
    yq = xq * lax.rsqrt(msq + NORM_EPS) * qg_ref[...][None]
    yq = _rope_cols(yq, cos_t, sin_t) * (HEAD_DIM ** -0.5)
    zq = jnp.zeros((HEAD_DIM, tm), BF16)
    for hh in range(AT_HEADS):
        parts = [yq[hh].astype(BF16), zq] if hh % 2 == 0 else [zq, yq[hh].astype(BF16)]
        qt_ref[0, hh] = jnp.concatenate(parts, axis=0)
    vt = pt[AT_WIDTH:2 * AT_WIDTH].astype(BF16)
    for cblk in range(tm // QBLK):
        vt_ref[0, cblk] = vt[:, cblk * QBLK:(cblk + 1) * QBLK]
    xqi = pt[2 * AT_WIDTH:2 * AT_WIDTH + IDX_HEADS * HEAD_DIM].reshape(IDX_HEADS, HEAD_DIM, tm)
    yqi = _rope_cols(xqi, cos_t, sin_t)
    for hh in range(IDX_HEADS):
        qit_ref[0, hh] = jnp.concatenate([yqi[hh].astype(BF16), zq], axis=0)
    wit_ref[0] = pt[PT_ROWS - 8:PT_ROWS] * (IDX_HEADS ** -0.5 * HEAD_DIM ** -0.5)


def _inproj(x, scale1, shift1, norm1_g, w_in, k_norm_g, idx_ln_w, idx_ln_b, q_norm_g,
            tabs, interpret):
    B, S, D = x.shape
    tm = min(S, 256)
    cos_r, sin_r, cos_t, sin_t = tabs
    w_at = w_in[:, RW_COLS:]
    w_rw = w_in[:, :RW_COLS].astype(BF16)
    w_k = jnp.concatenate([w_at[:, AT_WIDTH:2 * AT_WIDTH], w_at[:, KI_OFF:KI_OFF + HEAD_DIM],
                           jnp.zeros((D, HEAD_DIM), F32)], axis=1).astype(BF16)
    w_t = jnp.concatenate([w_at[:, 0:AT_WIDTH], w_at[:, 2 * AT_WIDTH:3 * AT_WIDTH],
                           w_at[:, 3 * AT_WIDTH:KI_OFF], w_at[:, WI_OFF:WI_OFF + IDX_HEADS],
                           jnp.zeros((D, 8 - IDX_HEADS), F32)], axis=1).T.astype(BF16)
    kg = jnp.tile(k_norm_g, 2).reshape(1, LANES)
    zpad = jnp.zeros((HEAD_DIM,), F32)
    iw = jnp.concatenate([idx_ln_w, zpad]).reshape(1, LANES)
    ib = jnp.concatenate([idx_ln_b, zpad]).reshape(1, LANES)
    qg = q_norm_g.reshape(HEAD_DIM, 1)
    li = np.arange(LANES)
    gsum = jnp.asarray((li[:, None] // HEAD_DIM == li[None, :] // HEAD_DIM).astype(np.float32)).astype(BF16)

    full = lambda shape: pl.BlockSpec(shape, lambda b, i: (0,) * len(shape))
    return pl.pallas_call(
        _inproj_kernel,
        grid=(B, S // tm),
        in_specs=[pl.BlockSpec((1, tm, D), lambda b, i: (b, i, 0)),
                  pl.BlockSpec((1, 1, D), lambda b, i: (b, 0, 0)),
                  pl.BlockSpec((1, 1, D), lambda b, i: (b, 0, 0)),
                  full((1, D)), full((D, RW_COLS)), full((D, PK_COLS)), full((PT_ROWS, D)),
                  full((1, LANES)), full((1, LANES)), full((1, LANES)), full((HEAD_DIM, 1)),
                  full((LANES, LANES)),
                  pl.BlockSpec((1, tm, LANES), lambda b, i: (b, i, 0)),
                  pl.BlockSpec((1, tm, LANES), lambda b, i: (b, i, 0)),
                  pl.BlockSpec((1, HALF, tm), lambda b, i: (b, 0, i)),
                  pl.BlockSpec((1, HALF, tm), lambda b, i: (b, 0, i))],
        out_specs=[pl.BlockSpec((1, tm, RW_COLS), lambda b, i: (b, i, 0)),
                   pl.BlockSpec((1, tm, AT_WIDTH), lambda b, i: (b, i, 0)),
                   pl.BlockSpec((1, tm, LANES), lambda b, i: (b, i, 0)),
                   pl.BlockSpec((1, AT_HEADS, LANES, tm), lambda b, i: (b, 0, 0, i)),
                   pl.BlockSpec((1, tm // QBLK, AT_WIDTH, QBLK), lambda b, i: (b, i, 0, 0)),
                   pl.BlockSpec((1, IDX_HEADS, LANES, tm), lambda b, i: (b, 0, 0, i)),
                   pl.BlockSpec((1, 8, tm), lambda b, i: (b, 0, i))],
        out_shape=[jax.ShapeDtypeStruct((B, S, RW_COLS), F32),
                   jax.ShapeDtypeStruct((B, S, AT_WIDTH), BF16),
                   jax.ShapeDtypeStruct((B, S, LANES), BF16),
                   jax.ShapeDtypeStruct((B, AT_HEADS, LANES, S), BF16),
                   jax.ShapeDtypeStruct((B, S // QBLK, AT_WIDTH, QBLK), BF16),
                   jax.ShapeDtypeStruct((B, IDX_HEADS, LANES, S), BF16),
                   jax.ShapeDtypeStruct((B, 8, S), F32)],
        compiler_params=_cparams(("arbitrary", "arbitrary")),
        name="inproj", interpret=interpret,
    )(x, scale1, shift1, norm1_g.reshape(1, D), w_rw, w_k, w_t, kg, iw, ib, qg, gsum,
      cos_r, sin_r, cos_t, sin_t)


def _rwkv_kernel(p_ref, mu_ref, w0_ref, w2_ref, a0_ref, a2_ref, g2_ref, kk_ref, ka_ref, rk_ref,
                 lnw_ref, lnb_ref, gsum_ref, y_ref, s_ref, prev_ref, yt_ref):
    C = RW_CHUNK
    W = RW_WIDTH
    nb = p_ref.shape[0]

    @pl.when(pl.program_id(1) == 0)
    def _():
        s_ref[...] = jnp.zeros_like(s_ref)
        prev_ref[...] = jnp.zeros_like(prev_ref)

    row = lax.broadcasted_iota(I32, (C, 1), 0)
    gsum = gsum_ref[...]

    def prepare(bi):
        p = p_ref[bi]
        pprev = jnp.where(row == 0, prev_ref[bi:bi + 1, :], pltpu.roll(p, 1, 0))
        prev_ref[bi:bi + 1, :] = p[C - 1:C]
        ps = p + (pprev - p) * mu_ref[...]
        r, k, v = ps[:, 0:W], ps[:, W:2 * W], ps[:, 2 * W:3 * W]
        o = 3 * W
        wl = ps[:, o:o + RW_LORA_W]
        al = ps[:, o + RW_LORA_W:o + RW_LORA_W + RW_LORA_A]
        gl = ps[:, o + RW_LORA_W + RW_LORA_A:]
        z = w0_ref[...] + _dot(jnp.tanh(wl).astype(BF16), w2_ref[...])
        nz = -z
        softplus = jnp.maximum(nz, 0.0) + jnp.log(1.0 + jnp.exp(-jnp.abs(nz)))
        logw = -jnp.exp(-softplus - 0.5)
        a = _sigmoid(a0_ref[...] + _dot(al.astype(BF16), a2_ref[...]))
        g = _dot(_sigmoid(gl).astype(BF16), g2_ref[...])
        kk = k * kk_ref[...]
        ss = _dot((kk * kk).astype(BF16), gsum)
        kk = kk * (1.0 / jnp.maximum(jnp.sqrt(ss), 1e-12))
        k2 = k * (1.0 + (a - 1.0) * ka_ref[...])
        bb = kk * a
        cw = logw
        sh = 1
        while sh < C:
            cw = cw + jnp.where(row >= sh, pltpu.roll(cw, sh, 0), 0.0)
            sh *= 2
        cw_last = cw[C - 1:C]
        e_neg = jnp.exp(-cw)
        e_end = jnp.exp(cw_last - cw)
        return dict(r=r, v=v, k2=k2, g=g, rw=r * jnp.exp(cw), kkp=kk * jnp.exp(cw - logw),
                    bw=bb * e_neg, kw=k2 * e_neg, bend=bb * e_end, kend=k2 * e_end,
                    wc=jnp.exp(cw_last), vt=v.T.astype(BF16))

    pre = [prepare(bi) for bi in range(nb)]

    ri = lax.broadcasted_iota(I32, (C, C), 0)
    ci = lax.broadcasted_iota(I32, (C, C), 1)
    strict = ri < ci
    incl = ri <= ci
    incl2 = jnp.concatenate([incl, incl], axis=0)
    lane_half = lax.broadcasted_iota(I32, (1, LANES), 1) // HEAD_DIM

    chains = [(bi, h) for bi in range(nb) for h in range(RW_HEADS)]
    pair = lambda h: slice((h // 2) * LANES, (h // 2 + 1) * LANES)
    own = lambda h: lane_half == (h % 2)
    stack = lambda bi, top, bot, h: jnp.concatenate(
        [pre[bi][top][:, pair(h)], pre[bi][bot][:, pair(h)]], axis=0)
    lh = {(bi, q): stack(bi, 'kkp', 'rw', 2 * q).astype(BF16)
          for bi in range(nb) for q in range(RW_HEADS // 2)}
    rh = [jnp.where(own(h), stack(bi, 'bw', 'kw', h), 0.0).astype(BF16) for bi, h in chains]
    aat = [_dot_nt(rh[n], lh[bi, h // 2]) for n, (bi, h) in enumerate(chains)]
    s_old = [s_ref[bi * RW_HEADS + h] for bi, h in chains]
    sl = [_dot_nt(s_old[n].astype(BF16), lh[bi, h // 2]) for n, (bi, h) in enumerate(chains)]
    vt = [pre[bi]['vt'][h * HEAD_DIM:(h + 1) * HEAD_DIM] for bi, h in chains]
    ids = range(len(chains))
    akt = [jnp.where(strict, aat[n][C:, :C], 0.0).astype(BF16) for n in ids]
    m = [jnp.where(strict, aat[n][:C, :C], 0.0).astype(BF16) for n in ids]
    xs = [-(sl[n][:, :C] + _dot(vt[n], akt[n])) for n in ids]
    xs = [xs[n] - _dot_split(xs[n], m[n]) for n in ids]
    lvl = 2
    while lvl < C:
        m = [_dot(m[n], m[n]).astype(BF16) for n in ids]
        xs = [xs[n] + _dot_split(xs[n], m[n]) for n in ids]
        lvl *= 2
    zt = [jnp.concatenate([xs[n].astype(BF16), vt[n]], axis=1) for n in ids]
    for n in ids:
        ymat = jnp.where(incl2, aat[n][:, C:], 0.0).astype(BF16)
        yt_ref[n * HEAD_DIM:(n + 1) * HEAD_DIM, :] = sl[n][:, C:] + _dot(zt[n], ymat)
    for n, (bi, h) in enumerate(chains):
        endz = jnp.where(own(h), stack(bi, 'bend', 'kend', h), 0.0).astype(BF16)
        s_ref[n] = s_old[n] * pre[bi]['wc'][:, pair(h)] + _dot(zt[n], endz)

    lnw = lnw_ref[...].reshape(RW_HEADS, HEAD_DIM, 1)
    lnb = lnb_ref[...].reshape(RW_HEADS, HEAD_DIM, 1)
    for bi in range(nb):
        q = pre[bi]
        yt = yt_ref[bi * W:(bi + 1) * W, :].reshape(RW_HEADS, HEAD_DIM, C)
        mean = jnp.mean(yt, axis=1, keepdims=True)
        yc = yt - mean
        var = jnp.mean(yc * yc, axis=1, keepdims=True)
        yn = yc * lax.rsqrt(var + RW_GN_EPS) * lnw + lnb
        y = yn.reshape(W, C).T
        bonus = _dot((q['r'] * q['k2'] * rk_ref[...]).astype(BF16), gsum) * q['v']
        y_ref[bi] = ((y + bonus) * q['g']).astype(BF16)


def _rwkv(p_rw, rw_mu, rw_w0, rw_w2, rw_a0, rw_a2, rw_g2, rw_k_k, rw_k_a, rw_r_k, rw_ln_w, rw_ln_b,
          interpret):
    B, S, _ = p_rw.shape
    C, W = RW_CHUNK, RW_WIDTH
    li = np.arange(W)
    gsum = jnp.asarray((li[:, None] // HEAD_DIM == li[None, :] // HEAD_DIM).astype(np.float32)).astype(BF16)
    row = lambda a: a.reshape(1, -1)
    full = lambda shape: pl.BlockSpec(shape, lambda b, i: (0,) * len(shape))
    nb = RW_BATCH if B % RW_BATCH == 0 else 1
    return pl.pallas_call(
        _rwkv_kernel,
        grid=(B // nb, S // C),
        in_specs=[pl.BlockSpec((nb, C, RW_COLS), lambda b, i: (b, i, 0)),
                  full((1, RW_COLS)), full((1, W)), full((RW_LORA_W, W)), full((1, W)),
                  full((RW_LORA_A, W)), full((RW_LORA_G, W)), full((1, W)), full((1, W)),
                  full((1, W)), full((W, 1)), full((W, 1)), full((W, W))],
        out_specs=pl.BlockSpec((nb, C, W), lambda b, i: (b, i, 0)),
        out_shape=jax.ShapeDtypeStruct((B, S, W), BF16),
        scratch_shapes=[pltpu.VMEM((nb * RW_HEADS, HEAD_DIM, LANES), F32),
                        pltpu.VMEM((nb, RW_COLS), F32),
                        pltpu.VMEM((nb * W, C), F32)],
        compiler_params=_cparams(("arbitrary", "arbitrary")),
        name="rwkv", interpret=interpret,
    )(p_rw, row(rw_mu), row(rw_w0), rw_w2.astype(BF16), row(rw_a0), rw_a2.astype(BF16),
      rw_g2.astype(BF16), row(rw_k_k), row(rw_k_a), row(rw_r_k), rw_ln_w.reshape(W, 1),
      rw_ln_b.reshape(W, 1), gsum)


def _dsa_kernel(topk, nbits, k_ref, vt_ref, ki_ref, qt_ref, qit_ref, wit_ref, og_ref, o_ref,
                key_s, acc_s, m_s, l_s, thr_s):
    j = pl.program_id(1)
    nkb = j + 1
    lane = lax.broadcasted_iota(I32, (QBLK, QBLK), 1)
    sub = lax.broadcasted_iota(I32, (QBLK, QBLK), 0)
    qpos = j * QBLK + lane
    wit = wit_ref[0]

    def score_blocks(i, carry):
        kbs = [i * SCORE_UNROLL + u for u in range(SCORE_UNROLL)]
        kib = [ki_ref[0, pl.ds(pl.multiple_of(kb * QBLK, QBLK), QBLK), :] for kb in kbs]
        lg = [[_dot(kib[u], qit_ref[0, hh]) for hh in range(IDX_HEADS)] for u in range(SCORE_UNROLL)]
        for u, kb in enumerate(kbs):
            s = wit[0:1, :] * jnp.maximum(lg[u][0], 0.0)
            for hh in range(1, IDX_HEADS):
                s = s + wit[hh:hh + 1, :] * jnp.maximum(lg[u][hh], 0.0)
            s = jnp.where(s == 0.0, 0.0, s)
            bits = pltpu.bitcast(s, I32)
            skey = jnp.where(bits < 0, bits ^ 0x7FFFFFFF, bits)
            key_s[kb] = jnp.where(kb * QBLK + sub <= qpos, skey, INT_MIN)
        return carry

    lax.fori_loop(0, pl.cdiv(nkb, SCORE_UNROLL), score_blocks, 0)

    @pl.when(nkb * QBLK <= topk)
    def _():
        thr_s[0:1, :] = jnp.full((1, QBLK), INT_MIN, I32)
        thr_s[1:2, :] = jnp.zeros((1, QBLK), I32)

    @pl.when(nkb * QBLK > topk)
    def _():
        n_done = pl.cdiv(nkb, SCORE_UNROLL) * SCORE_UNROLL
        n_cnt = pl.cdiv(nkb, COUNT_UNROLL)

        def fill(kb, carry):
            key_s[kb] = jnp.full((QBLK, QBLK), INT_MIN, I32)
            return carry

        lax.fori_loop(n_done, n_cnt * COUNT_UNROLL, fill, 0)

        def count(preds):
            def body(i, accs):
                accs = list(accs)
                for u in range(COUNT_UNROLL):
                    kb = i * COUNT_UNROLL + u
                    ky = key_s[kb]
                    for n, pred in enumerate(preds):
                        hit = pred(ky, kb * QBLK + sub).astype(I32)
                        accs[n] = accs[n] + jnp.sum(hit.reshape(QBLK // 8, 8, QBLK), axis=0)
                return tuple(accs)
            accs = lax.fori_loop(0, n_cnt, body, tuple(jnp.zeros((8, QBLK), I32) for _ in preds))
            return [jnp.sum(a, axis=0, keepdims=True) for a in accs]

        c0, = count([lambda ky, ix: ky >= 0])
        t0 = jnp.where(c0 >= topk, 0, INT_MIN).astype(I32)

        def bit_step(i, t):
            cand = t | jnp.left_shift(jnp.int32(1), 30 - i)
            c, = count([lambda ky, ix: ky >= cand])
            return jnp.where(c >= topk, cand, t)

        thr = lax.fori_loop(0, 31, bit_step, t0)
        n_gt, n_eq = count([lambda ky, ix: ky > thr, lambda ky, ix: ky == thr])
        need = topk - n_gt
        thr_s[0:1, :] = thr
        thr_s[1:2, :] = jnp.full((1, QBLK), 2 ** nbits, I32)

        @pl.when(jnp.max(jnp.abs(n_eq - need)) > 0)
        def _():
            def idx_step(i, mm):
                cand = mm | jnp.left_shift(jnp.int32(1), nbits - 1 - i)
                c, = count([lambda ky, ix: (ky == thr) & (ix < cand)])
                return jnp.where(c < need, cand, mm)

            thr_s[1:2, :] = lax.fori_loop(0, nbits, idx_step, jnp.zeros((1, QBLK), I32))

    thr = thr_s[0:1, :]
    mm = thr_s[1:2, :]
    m_s[...] = jnp.full_like(m_s, NEG_BIG)
    l_s[...] = jnp.zeros_like(l_s)
    acc_s[...] = jnp.zeros_like(acc_s)

    qt2 = [jnp.concatenate([qt_ref[0, 2 * q], qt_ref[0, 2 * q + 1]], axis=1)
           for q in range(AT_HEADS // 2)]

    def attn_blocks(i, carry):
        kbs = [i * ATTN_UNROLL + u for u in range(ATTN_UNROLL)]
        sel = []
        for kb in kbs:
            skey = key_s[kb]
            kidx = kb * QBLK + sub
            sel.append((kidx <= qpos) & ((skey > thr) | ((skey == thr) & (kidx <= mm))))
        sel = jnp.concatenate(sel, axis=0)
        kblk = [k_ref[0, pl.ds(pl.multiple_of(kb * QBLK, QBLK), QBLK), :] for kb in kbs]
        vtb = jnp.concatenate([vt_ref[0, kb] for kb in kbs], axis=1)
        s2 = [[_dot(kblk[u][:, q * LANES:(q + 1) * LANES], qt2[q]) for u in range(ATTN_UNROLL)]
              for q in range(AT_HEADS // 2)]
        pexp, alpha = [], []
        for hh in range(AT_HEADS):
            half = slice((hh % 2) * QBLK, (hh % 2 + 1) * QBLK)
            s = jnp.concatenate([s2[hh // 2][u][:, half] for u in range(ATTN_UNROLL)], axis=0)
            s = jnp.where(sel, s, NEG_BIG)
            m_old = m_s[hh:hh + 1, :]
            m_new = jnp.maximum(m_old, jnp.max(s, axis=0, keepdims=True))
            pe = jnp.exp(s - m_new)
            al = jnp.exp(m_old - m_new)
            l_s[hh:hh + 1, :] = al * l_s[hh:hh + 1, :] + jnp.sum(pe, axis=0, keepdims=True)
            m_s[hh:hh + 1, :] = m_new
            pexp.append(pe.astype(BF16))
            alpha.append(al)
        for hh in range(AT_HEADS):
            hs = slice(hh * HEAD_DIM, (hh + 1) * HEAD_DIM)
            acc_s[hs, :] = alpha[hh] * acc_s[hs, :] + _dot(vtb[hs, :], pexp[hh])
        return carry

    lax.fori_loop(0, pl.cdiv(nkb, ATTN_UNROLL), attn_blocks, 0)

    for hh in range(AT_HEADS):
        hs = slice(hh * HEAD_DIM, (hh + 1) * HEAD_DIM)
        oh = acc_s[hs, :] * (1.0 / l_s[hh:hh + 1, :])
        ms = jnp.mean(oh * oh, axis=0, keepdims=True)
        acc_s[hs, :] = oh * lax.rsqrt(ms + NORM_EPS) * og_ref[hs, :]
    o_ref[0] = acc_s[...].T.astype(BF16)


def _dsa(k, vt, ki, qt, qit, wit, at_out_g, interpret):
    B, S, _ = k.shape
    nq = S // QBLK
    topk = min(IDX_TOPK_MAX, S // 4)
    nbits = int(np.log2(S))
    assert 2 ** nbits == S and nq % COUNT_UNROLL == 0 and SCORE_UNROLL == ATTN_UNROLL
    assert COUNT_UNROLL % SCORE_UNROLL == 0
    return pl.pallas_call(
        functools.partial(_dsa_kernel, topk, nbits),
        grid=(B, nq),
        in_specs=[pl.BlockSpec((1, S, AT_WIDTH), lambda b, j: (b, 0, 0)),
                  pl.BlockSpec((1, nq, AT_WIDTH, QBLK), lambda b, j: (b, 0, 0, 0)),
                  pl.BlockSpec((1, S, LANES), lambda b, j: (b, 0, 0)),
                  pl.BlockSpec((1, AT_HEADS, LANES, QBLK), lambda b, j: (b, 0, 0, j)),
                  pl.BlockSpec((1, IDX_HEADS, LANES, QBLK), lambda b, j: (b, 0, 0, j)),
                  pl.BlockSpec((1, 8, QBLK), lambda b, j: (b, 0, j)),
                  pl.BlockSpec((AT_WIDTH, 1), lambda b, j: (0, 0))],
        out_specs=pl.BlockSpec((1, QBLK, AT_WIDTH), lambda b, j: (b, j, 0)),
        out_shape=jax.ShapeDtypeStruct((B, S, AT_WIDTH), BF16),
        scratch_shapes=[pltpu.VMEM((nq, QBLK, QBLK), I32),
                        pltpu.VMEM((AT_WIDTH, QBLK), F32),
                        pltpu.VMEM((AT_HEADS, QBLK), F32),
                        pltpu.VMEM((AT_HEADS, QBLK), F32),
                        pltpu.VMEM((8, QBLK), I32)],
        compiler_params=_cparams(("arbitrary", "arbitrary")),
        name="dsa", interpret=interpret,
    )(k, vt, ki, qt, qit, wit, at_out_g.reshape(AT_WIDTH, 1))


def _first_max(vals, idx, axis, sentinel):
    m = jnp.max(vals, axis=axis, keepdims=True)
    return m, jnp.min(jnp.where(vals == m, idx, sentinel), axis=axis, keepdims=True)


def _route_cols(logits_t, bias_col):
    E, tm = logits_t.shape
    pg = E // N_GROUPS
    scores = _sigmoid(logits_t)
    biased = scores + bias_col
    b3 = biased.reshape(N_GROUPS, pg, tm)
    r3 = lax.broadcasted_iota(I32, (N_GROUPS, pg, tm), 1)
    m1, first = _first_max(b3, r3, 1, pg)
    m2 = jnp.max(jnp.where(r3 == first, -jnp.inf, b3), axis=1, keepdims=True)
    cur = (m1 + m2).reshape(N_GROUPS, tm)
    grow = lax.broadcasted_iota(I32, (N_GROUPS, tm), 0)
    gsel = jnp.zeros((N_GROUPS, tm), F32)
    for _ in range(TOPK_GROUPS):
        _, gi = _first_max(cur, grow, 0, N_GROUPS)
        hit = grow == gi
        gsel = jnp.where(hit, 1.0, gsel)
        cur = jnp.where(hit, -jnp.inf, cur)
    gmask = jnp.broadcast_to(gsel.reshape(N_GROUPS, 1, tm), (N_GROUPS, pg, tm)).reshape(E, tm)
    cur = jnp.where(gmask > 0.0, biased, -jnp.inf)
    row = lax.broadcasted_iota(I32, (E, tm), 0)
    onehot = jnp.zeros((E, tm), F32)
    eids, gws = [], []
    for _ in range(TOP_K):
        _, ei = _first_max(cur, row, 0, E)
        hit = row == ei
        eids.append(ei)
        gws.append(jnp.sum(jnp.where(hit, scores, 0.0), axis=0, keepdims=True))
        onehot = jnp.where(hit, 1.0, onehot)
        cur = jnp.where(hit, -jnp.inf, cur)
    eid = jnp.concatenate(eids, axis=0)
    gw = jnp.concatenate(gws, axis=0)
    gw = gw * (ROUTED_SCALE / jnp.sum(gw, axis=0, keepdims=True))
    return eid, gw, onehot


def _post_kernel(x_ref, yrw_ref, yat_ref, g1_ref, sc_ref, sh_ref, g2_ref, ng_ref, wo_ref, rwh_ref,
                 rwl_ref, rb_ref, s1_ref, s3_ref, s2_ref, base_ref, h2_ref, eid_ref, gw_ref, rank_ref,
                 cnt_ref):
    W = RW_WIDTH
    tm = x_ref.shape[1]
    E = rwh_ref.shape[0]

    @pl.when((pl.program_id(0) == 0) & (pl.program_id(1) == 0))
    def _():
        cnt_ref[...] = jnp.zeros_like(cnt_ref)

    mix = _dot(yrw_ref[0], wo_ref[0:W, :]) + _dot(yat_ref[0], wo_ref[W:, :])
    x1 = x_ref[0] + g1_ref[0] * mix
    ms = jnp.mean(x1 * x1, axis=-1, keepdims=True)
    h2 = x1 * lax.rsqrt(ms + NORM_EPS) * ng_ref[...] * (1.0 + sc_ref[0]) + sh_ref[0]
    hb = h2.astype(BF16)
    h2_ref[0, :, 0, :] = _pack_halves(h2)
    act =(_silu(_dot(hb, s1_ref[...])) * _dot(hb, s3_ref[...])).astype(BF16)
    base_ref[0] = x1 + g2_ref[0] * _dot(act, s2_ref[...])

    h_lo = (h2 - hb.astype(F32)).astype(BF16)
    logits_t = (_dot_nt(rwh_ref[...], hb) + _dot_nt(rwh_ref[...], h_lo)
                + _dot_nt(rwl_ref[...], hb))
    eid, gw, onehot = _route_cols(logits_t, rb_ref[...])
    eid_ref[0] = eid
    gw_ref[0] = gw
    ti = lax.broadcasted_iota(I32, (tm, tm), 0)
    tj = lax.broadcasted_iota(I32, (tm, tm), 1)
    before = _dot(onehot.astype(BF16), (ti < tj).astype(BF16)) + cnt_ref[:, 0:1]
    row = lax.broadcasted_iota(I32, (E, tm), 0)
    ranks = [jnp.sum(jnp.where(row == eid[kk:kk + 1, :], before, 0.0), axis=0, keepdims=True)
             for kk in range(TOP_K)]
    rank_ref[0] = jnp.concatenate(ranks, axis=0).astype(I32)
    cnt_ref[...] = cnt_ref[...] + jnp.sum(onehot, axis=1, keepdims=True)


def _post(x, y_rw, y_at, gate1, scale2, shift2, gate2, norm2_g, w_out, router_w, router_bias,
          sw1, sw3, sw2, interpret):
    B, S, D = x.shape
    tm = min(S, 256)
    E = router_w.shape[1]
    sd = sw1.shape[1]
    full = lambda shape: pl.BlockSpec(shape, lambda b, i: (0,) * len(shape))
    tok = lambda w: pl.BlockSpec((1, tm, w), lambda b, i: (b, i, 0))
    per_b = pl.BlockSpec((1, 1, D), lambda b, i: (b, 0, 0))
    col8 = pl.BlockSpec((1, TOP_K, tm), lambda b, i: (b, 0, i))
    rw_t = router_w.T
    rw_hi = rw_t.astype(BF16)
    rw_lo = (rw_t - rw_hi.astype(F32)).astype(BF16)
    return pl.pallas_call(
        _post_kernel,
        grid=(B, S // tm),
        in_specs=[tok(D), tok(RW_WIDTH), tok(AT_WIDTH), per_b, per_b, per_b, per_b, full((1, D)),
                  full((D, D)), full((E, D)), full((E, D)), full((E, 1)), full((D, sd)), full((D, sd)),
                  full((sd, D))],
        out_specs=[tok(D), pl.BlockSpec((1, tm, 1, D // 2), lambda b, i: (b, i, 0, 0)),
                   col8, col8, col8, full((E, LANES))],
        out_shape=[jax.ShapeDtypeStruct((B, S, D), F32),
                   jax.ShapeDtypeStruct((B, S, 1, D // 2), I32),
                   jax.ShapeDtypeStruct((B, TOP_K, S), I32),
                   jax.ShapeDtypeStruct((B, TOP_K, S), F32),
                   jax.ShapeDtypeStruct((B, TOP_K, S), I32),
                   jax.ShapeDtypeStruct((E, LANES), F32)],
        compiler_params=_cparams(("arbitrary", "arbitrary")),
        name="post", interpret=interpret,
    )(x, y_rw, y_at, gate1, scale2, shift2, gate2, norm2_g.reshape(1, D), w_out.astype(BF16),
      rw_hi, rw_lo, router_bias.reshape(E, 1), sw1.astype(BF16), sw3.astype(BF16), sw2.astype(BF16))


def _expert_kernel(be_ref, nv_ref, nu_ref, xs_ref, w1_ref, w3_ref, w2_ref, o_ref, w13_s, w2_s):
    i = pl.program_id(0)
    blk, _, hw = xs_ref.shape
    F = w1_ref.shape[2]
    used = i < nu_ref[0]

    @pl.when(used & ((i == 0) | (be_ref[i] != be_ref[jnp.maximum(i - 1, 0)])))
    def _():
        w13_s[:, :F] = w1_ref[0].astype(BF16)
        w13_s[:, F:] = w3_ref[0].astype(BF16)
        w2_s[...] = w2_ref[0].astype(BF16)

    @pl.when(used)
    def _():
        live = lax.broadcasted_iota(I32, (blk, 1), 0) < nv_ref[i]
        x_lo, x_hi = _unpack_halves(jnp.where(live, xs_ref[:, 0, :], 0))
        h13 =(_dot(x_lo.astype(BF16), w13_s[:hw, :]) + _dot(x_hi.astype(BF16), w13_s[hw:, :]))
        act = (_silu(h13[:, :F]) * h13[:, F:]).astype(BF16)
        o_ref[:, 0, :] = _pack_halves(_dot(act, w2_s[...]))


def _experts(xs, block_e, block_rows, n_used, w1, w3, w2, interpret):
    P, _, hw = xs.shape
    E, D, F = w1.shape
    nb = P // EXP_BLK
    blk = lambda i, nu: jnp.minimum(i, nu[0] - 1)
    grid_spec = pltpu.PrefetchScalarGridSpec(
        num_scalar_prefetch=3,
        grid=(nb,),
        in_specs=[pl.BlockSpec((EXP_BLK, 1, hw), lambda i, be, nv, nu: (blk(i, nu), 0, 0)),
                  pl.BlockSpec((1, D, F), lambda i, be, nv, nu: (be[blk(i, nu)], 0, 0)),
                  pl.BlockSpec((1, D, F), lambda i, be, nv, nu: (be[blk(i, nu)], 0, 0)),
                  pl.BlockSpec((1, F, D), lambda i, be, nv, nu: (be[blk(i, nu)], 0, 0))],
        out_specs=pl.BlockSpec((EXP_BLK, 1, hw), lambda i, be, nv, nu: (blk(i, nu), 0, 0)),
        scratch_shapes=[pltpu.VMEM((D, 2 * F), BF16), pltpu.VMEM((F, D), BF16)],
    )
    return pl.pallas_call(
        _expert_kernel,
        grid_spec=grid_spec,
        out_shape=jax.ShapeDtypeStruct((P, 1, hw), I32),
        compiler_params=_cparams(("arbitrary",)),
        name="experts", interpret=interpret,
    )(block_e, block_rows, n_used, xs, w1, w3, w2)


def _row_out(tile, t, slots, slot, sem):
    return pltpu.make_async_copy(tile.at[t], slots.at[slot], sem)


def _row_in(slots, slot, tile, t, sem):
    return pltpu.make_async_copy(slots.at[slot], tile.at[pl.ds(t, 1)], sem)


def _dispatch_kernel(dest_ref, h2_ref, xs_ref, sem):
    td = h2_ref.shape[0]

    def issue(t, carry):
        for kk in range(TOP_K):
            _row_out(h2_ref, t, xs_ref, dest_ref[0, 0, kk * td + t], sem).start(priority=kk % 2)
        return carry

    def drain(t, carry):
        for kk in range(TOP_K):
            _row_out(h2_ref, t, xs_ref, dest_ref[0, 0, kk * td + t], sem).wait()
        return carry

    lax.fori_loop(0, td, issue, 0)
    lax.fori_loop(0, td, drain, 0)


def _dispatch(h2, dest_tiles, n_slots, interpret):
    T, _, D = h2.shape
    nt, _, n = dest_tiles.shape
    td = n // TOP_K
    return pl.pallas_call(
        _dispatch_kernel,
        grid=(nt,),
        in_specs=[pl.BlockSpec((1, 1, n), lambda i: (i, 0, 0), memory_space=pltpu.SMEM),
                  pl.BlockSpec((td, 1, D), lambda i: (i, 0, 0))],
        out_specs=pl.BlockSpec(memory_space=pl.ANY),
        out_shape=jax.ShapeDtypeStruct((n_slots, 1, D), h2.dtype),
        scratch_shapes=[pltpu.SemaphoreType.DMA(())],
        compiler_params=_cparams(("arbitrary",)),
        name="dispatch", interpret=interpret,
    )(dest_tiles, h2)


def _combine_kernel(dest_ref, base_ref, g2_ref, gw_ref, ys_ref, o_ref, buf, sem):
    td = base_ref.shape[0]

    def issue(t, carry):
        for kk in range(TOP_K):
            _row_in(ys_ref, dest_ref[0, 0, kk * td + t], buf.at[kk], t, sem).start(priority=kk % 2)
        return carry

    def drain(t, carry):
        for kk in range(TOP_K):
            _row_in(ys_ref, dest_ref[0, 0, kk * td + t], buf.at[kk], t, sem).wait()
        return carry

    lax.fori_loop(0, td, issue, 0)
    lax.fori_loop(0, td, drain, 0)
    gw = gw_ref[...]
    acc_lo, acc_hi = _unpack_halves(buf[0])
    acc_lo, acc_hi = gw[:, 0:1] * acc_lo, gw[:, 0:1] * acc_hi
    for kk in range(1, TOP_K):
        y_lo, y_hi = _unpack_halves(buf[kk])
        acc_lo = acc_lo + gw[:, kk:kk + 1] * y_lo
        acc_hi = acc_hi + gw[:, kk:kk + 1] * y_hi
    o_ref[...] = base_ref[...] + g2_ref[0] * jnp.concatenate([acc_lo, acc_hi], axis=1)


def _combine(base, gate2, gw_tok, ys, dest_tiles, tiles_per_batch, interpret):
    T, D = base.shape
    nt, _, n = dest_tiles.shape
    td = n // TOP_K
    return pl.pallas_call(
        _combine_kernel,
        grid=(nt,),
        in_specs=[pl.BlockSpec((1, 1, n), lambda i: (i, 0, 0), memory_space=pltpu.SMEM),
                  pl.BlockSpec((td, D), lambda i: (i, 0)),
                  pl.BlockSpec((1, 1, D), lambda i: (i // tiles_per_batch, 0, 0)),
                  pl.BlockSpec((td, TOP_K), lambda i: (i, 0)),
                  pl.BlockSpec(memory_space=pl.ANY)],
        out_specs=pl.BlockSpec((td, D), lambda i: (i, 0)),
        out_shape=jax.ShapeDtypeStruct((T, D), F32),
        scratch_shapes=[pltpu.VMEM((TOP_K, td, D // 2), I32), pltpu.SemaphoreType.DMA(())],
        compiler_params=_cparams(("arbitrary",)),
        name="combine", interpret=interpret,
    )(dest_tiles, base, gate2, gw_tok, ys)


def _slots_kernel(eid_ref, rank_ref, pstart_ref, dest_ref):
    td = eid_ref.shape[2]
    E = pstart_ref.shape[0]
    row = lax.broadcasted_iota(I32, (E, td), 0)
    pstart = pstart_ref[...]
    eid = eid_ref[0]
    for kk in range(TOP_K):
        base = jnp.sum(jnp.where(row == eid[kk:kk + 1, :], pstart, 0), axis=0, keepdims=True)
        dest_ref[0, :, kk * td:(kk + 1) * td] = base + rank_ref[0, kk:kk + 1, :]


def _slot_plan(counts, eid_t, rank_t, td, interpret):
    B, _, S = eid_t.shape
    E = counts.shape[0]
    padded = (counts + EXP_BLK - 1) // EXP_BLK * EXP_BLK
    pend = jnp.cumsum(padded)
    pstart = (pend - padded).astype(I32)
    nb = -(-(B * S * TOP_K + E * (EXP_BLK - 1)) // EXP_BLK)
    first_row = jnp.arange(nb, dtype=I32) * EXP_BLK
    block_e = jnp.sum(pend[None, :] <= first_row[:, None], axis=1)
    block_e = jnp.minimum(block_e, E - 1).astype(I32)
    block_rows = jnp.clip(pstart[block_e] + counts[block_e] - first_row, 0, EXP_BLK).astype(I32)
    nt = S // td
    dest_tiles = pl.pallas_call(
        _slots_kernel,
        grid=(B, nt),
        in_specs=[pl.BlockSpec((1, TOP_K, td), lambda b, i: (b, 0, i)),
                  pl.BlockSpec((1, TOP_K, td), lambda b, i: (b, 0, i)),
                  pl.BlockSpec((E, 1), lambda b, i: (0, 0))],
        out_specs=pl.BlockSpec((1, 1, TOP_K * td), lambda b, i: (b * nt + i, 0, 0)),
        out_shape=jax.ShapeDtypeStruct((B * nt, 1, TOP_K * td), I32),
        compiler_params=_cparams(("arbitrary", "arbitrary")),
        name="slots", interpret=interpret,
    )(eid_t, rank_t, pstart.reshape(E, 1))
    n_used = (pend[-1:] // EXP_BLK).astype(I32)
    return block_e, block_rows, n_used, dest_tiles, nb * EXP_BLK


def _forward(x, c, positions, w_ada, b_ada, norm1_g, norm2_g, w_in, rw_mu, rw_w0, rw_w2,
             rw_a0, rw_a2, rw_g2, rw_k_k, rw_k_a, rw_r_k, rw_ln_w, rw_ln_b, q_norm_g,
             k_norm_g, idx_ln_w, idx_ln_b, at_out_g, w_out, router_w, router_bias,
             exp_w1, exp_w3, exp_w2, shared_w1, shared_w3, shared_w2, interpret=False):
    B, S, D = x.shape
    depth = w_ada.shape[0]
    for l in range(depth):
        mod = _mod(c, w_ada[l], b_ada[l], interpret)
        shift1, scale1, gate1, shift2, scale2, gate2 = [
            m.reshape(B, 1, D) for m in jnp.split(mod, 6, axis=-1)]
        tabs = _rope_tables(positions, interpret)
        p_rw, k, ki, qt, vt, qit, wit = _inproj(
            x, scale1, shift1, norm1_g[l], w_in[l], k_norm_g[l], idx_ln_w[l], idx_ln_b[l],
            q_norm_g[l], tabs, interpret)
        y_rw = _rwkv(p_rw, rw_mu[l], rw_w0[l], rw_w2[l], rw_a0[l], rw_a2[l], rw_g2[l], rw_k_k[l],
                     rw_k_a[l], rw_r_k[l], rw_ln_w[l], rw_ln_b[l], interpret)
        y_at = _dsa(k, vt, ki, qt, qit, wit, at_out_g[l], interpret)
        base, h2, eid_t, gw_t, rank_t, cnt = _post(
            x, y_rw, y_at, gate1, scale2, shift2, gate2, norm2_g[l], w_out[l], router_w[l],
            router_bias[l], shared_w1[l], shared_w3[l], shared_w2[l], interpret)
        T = B * S
        td = min(S, ROW_TILE)
        block_e, block_rows, n_used, dest_tiles, n_slots = _slot_plan(
            cnt[:, 0].astype(I32), eid_t, rank_t, td, interpret)
        xs = _dispatch(h2.reshape(T, 1, D // 2), dest_tiles, n_slots, interpret)
        ys = _experts(xs, block_e, block_rows, n_used, exp_w1[l], exp_w3[l], exp_w2[l], interpret)
        gw_tok = gw_t.transpose(0, 2, 1).reshape(T, TOP_K)
        x = _combine(base.reshape(T, D), gate2, gw_tok, ys, dest_tiles, S // td,
                     interpret).reshape(B, S, D)
    return x


def kernel(x, c, positions, w_ada, b_ada, norm1_g, norm2_g, w_in, rw_mu, rw_w0, rw_w2, rw_a0, rw_a2, rw_g2, rw_k_k, rw_k_a, rw_r_k, rw_ln_w, rw_ln_b, q_norm_g, k_norm_g, idx_ln_w, idx_ln_b, at_out_g, w_out, router_w, router_bias, exp_w1, exp_w3, exp_w2, shared_w1, shared_w3, shared_w2):
    return _forward(x, c, positions, w_ada, b_ada, norm1_g, norm2_g, w_in, rw_mu, rw_w0, rw_w2,
                    rw_a0, rw_a2, rw_g2, rw_k_k, rw_k_a, rw_r_k, rw_ln_w, rw_ln_b, q_norm_g,
                    k_norm_g, idx_ln_w, idx_ln_b, at_out_g, w_out, router_w, router_bias,
                    exp_w1, exp_w3, exp_w2, shared_w1, shared_w3, shared_w2)
```

```python
import functools

import jax
import jax.numpy as jnp
import numpy as np
from jax import lax
from jax.experimental import pallas as pl
from jax.experimental.pallas import tpu as pltpu

F32 = jnp.float32
BF16 = jnp.bfloat16
I32 = jnp.int32
HIGHEST = lax.Precision.HIGHEST

LANES = 128
HEAD_DIM = 64
HALF = HEAD_DIM // 2
RW_HEADS = 8
RW_WIDTH = RW_HEADS * HEAD_DIM
AT_HEADS = 8
AT_WIDTH = AT_HEADS * HEAD_DIM
IDX_HEADS = 4
RW_LORA_W, RW_LORA_A, RW_LORA_G = 64, 64, 128
RW_COLS = 3 * RW_WIDTH + RW_LORA_W + RW_LORA_A + RW_LORA_G
KI_OFF = 3 * AT_WIDTH + IDX_HEADS * HEAD_DIM
WI_OFF = KI_OFF + HEAD_DIM
PT_ROWS = 2 * AT_WIDTH + IDX_HEADS * HEAD_DIM + 8
PK_COLS = AT_WIDTH + LANES
ROPE_THETA = 10000.0
NORM_EPS = 1e-6
LN_EPS = 1e-6
RW_GN_EPS = 64e-5
IDX_TOPK_MAX = 256
N_EXPERTS = 256
TOP_K = 8
N_GROUPS = 8
TOPK_GROUPS = 4
ROUTED_SCALE = 2.5
INT_MIN = -2 ** 31
PAIR_HI_MASK = -65536
NEG_BIG = -1e30

RW_CHUNK = 128
RW_BATCH = 4
QBLK = 128
SCORE_UNROLL = 4
ATTN_UNROLL = 4
COUNT_UNROLL = 4
EXP_BLK = 512
ROW_TILE = 512
LOG2_E = 1.4426950408889634
VMEM_LIMIT = 48 * 1024 * 1024


def _cparams(sem):
    return pltpu.CompilerParams(dimension_semantics=sem, vmem_limit_bytes=VMEM_LIMIT)


def _sigmoid(x):
    return 1.0 / (1.0 + jnp.exp(-x))


def _silu(x):
    return x * _sigmoid(x)


def _dot(a, b):
    return jnp.dot(a, b, preferred_element_type=F32)


def _dot_split(a, b):
    hi = a.astype(BF16)
    lo = (a - hi.astype(F32)).astype(BF16)
    return _dot(hi, b) + _dot(lo, b)


def _pack_halves(x):
    w = x.shape[1] // 2
    lo = pltpu.bitcast(x[:, :w].astype(BF16).astype(F32), I32)
    hi = pltpu.bitcast(x[:, w:].astype(BF16).astype(F32), I32)
    return (hi & PAIR_HI_MASK) | lax.shift_right_logical(lo, 16)


def _unpack_halves(p):
    return pltpu.bitcast(p << 16, F32), pltpu.bitcast(p & PAIR_HI_MASK, F32)


def _dot_nt(a, b):
    return lax.dot_general(a, b, (((1,), (1,)), ((), ())), preferred_element_type=F32)


def _mod_kernel(c_ref, w_ref, b_ref, o_ref):
    c = c_ref[...]
    o_ref[...] = jnp.dot(_silu(c), w_ref[...], precision=HIGHEST,
                         preferred_element_type=F32) + b_ref[...]


def _mod(c, w_ada, b_ada, interpret):
    B, D = c.shape
    n = w_ada.shape[1] // D
    return pl.pallas_call(
        _mod_kernel,
        grid=(n,),
        in_specs=[pl.BlockSpec((B, D), lambda i: (0, 0)),
                  pl.BlockSpec((D, D), lambda i: (0, i)),
                  pl.BlockSpec((1, D), lambda i: (0, i))],
        out_specs=pl.BlockSpec((B, D), lambda i: (0, i)),
        out_shape=jax.ShapeDtypeStruct((B, n * D), F32),
        compiler_params=_cparams(("arbitrary",)),
        name="mod", interpret=interpret,
    )(c, w_ada, b_ada.reshape(1, -1))


def _rope_tab_kernel(pc_ref, pr_ref, cr_ref, sr_ref, ct_ref, st_ref):
    log_theta = float(np.log(ROPE_THETA))
    lane = lax.broadcasted_iota(I32, (1, LANES), 1)
    inv_r = jnp.exp((lane % HALF).astype(F32) * (-log_theta / HALF))
    ang = pc_ref[0].astype(F32) * inv_r
    cr_ref[0] = jnp.cos(ang)
    sr_ref[0] = jnp.where((lane % HEAD_DIM) < HALF, -jnp.sin(ang), jnp.sin(ang))
    sub = lax.broadcasted_iota(I32, (HALF, 1), 0)
    inv_c = jnp.exp(sub.astype(F32) * (-log_theta / HALF))
    ang_t = inv_c * pr_ref[0].astype(F32)
    ct_ref[0] = jnp.cos(ang_t)
    st_ref[0] = jnp.sin(ang_t)


def _rope_tables(positions, interpret):
    B, S = positions.shape
    ts = min(S, 512)
    return pl.pallas_call(
        _rope_tab_kernel,
        grid=(B, S // ts),
        in_specs=[pl.BlockSpec((1, ts, 1), lambda b, i: (b, i, 0)),
                  pl.BlockSpec((1, 1, ts), lambda b, i: (b, 0, i))],
        out_specs=[pl.BlockSpec((1, ts, LANES), lambda b, i: (b, i, 0)),
                   pl.BlockSpec((1, ts, LANES), lambda b, i: (b, i, 0)),
                   pl.BlockSpec((1, HALF, ts), lambda b, i: (b, 0, i)),
                   pl.BlockSpec((1, HALF, ts), lambda b, i: (b, 0, i))],
        out_shape=[jax.ShapeDtypeStruct((B, S, LANES), F32),
                   jax.ShapeDtypeStruct((B, S, LANES), F32),
                   jax.ShapeDtypeStruct((B, HALF, S), F32),
                   jax.ShapeDtypeStruct((B, HALF, S), F32)],
        compiler_params=_cparams(("arbitrary", "arbitrary")),
        name="rope_tab", interpret=interpret,
    )(positions.reshape(B, S, 1), positions.reshape(B, 1, S))


def _rope_rows(y, cos, sin_signed):
    lane = lax.broadcasted_iota(I32, (1, LANES), 1)
    partner = jnp.where((lane % HEAD_DIM) < HALF,
                        pltpu.roll(y, LANES - HALF, 1), pltpu.roll(y, HALF, 1))
    return y * cos + partner * sin_signed


def _rope_cols(y, cos_t, sin_t):
    x1, x2 = y[:, :HALF], y[:, HALF:]
    return jnp.concatenate([x1 * cos_t - x2 * sin_t, x2 * cos_t + x1 * sin_t], axis=1)


def _inproj_kernel(x_ref, sc_ref, sh_ref, g_ref, wrw_ref, wk_ref, wt_ref, kg_ref, iw_ref, ib_ref,
                   qg_ref, gsum_ref, cr_ref, sr_ref, ct_ref, st_ref,
                   prw_ref, k_ref, ki_ref, qt_ref, vt_ref, qit_ref, wit_ref):
    tm = x_ref.shape[1]
    x = x_ref[0]
    ms = jnp.mean(x * x, axis=-1, keepdims=True)
    h = x * lax.rsqrt(ms + NORM_EPS) * g_ref[...] * (1.0 + sc_ref[0]) + sh_ref[0]
    hb = h.astype(BF16)
    prw_ref[0] = _dot(hb, wrw_ref[...])
    pk = _dot(hb, wk_ref[...])
    pt = _dot_nt(wt_ref[...], hb)

    cos_r, sin_r = cr_ref[0], sr_ref[0]
    gsum = gsum_ref[...]
    inv_hd = 1.0 / HEAD_DIM
    for p in range(AT_WIDTH // LANES):
        xk = pk[:, p * LANES:(p + 1) * LANES]
        ss = _dot_split(xk * xk, gsum)
        y = xk * lax.rsqrt(ss * inv_hd + NORM_EPS) * kg_ref[...]
        k_ref[0, :, p * LANES:(p + 1) * LANES] = _rope_rows(y, cos_r, sin_r).astype(BF16)
    xi = pk[:, AT_WIDTH:AT_WIDTH + LANES]
    mu = _dot_split(xi, gsum) * inv_hd
    xc = xi - mu
    var = _dot_split(xc * xc, gsum) * inv_hd
    yi = xc * lax.rsqrt(var + LN_EPS) * iw_ref[...] + ib_ref[...]
    ki_ref[0] = _rope_rows(yi, cos_r, sin_r).astype(BF16)

    cos_t, sin_t = ct_ref[0][None], st_ref[0][None]
    xq = pt[0:AT_WIDTH].reshape(AT_HEADS, HEAD_DIM, tm)
    msq = jnp.mean(xq * xq, axis=1, keepdims=True)
    yq = xq * lax.rsqrt(msq + NORM_EPS) * qg_ref[...][None]
    yq = _rope_cols(yq, cos_t, sin_t) * (HEAD_DIM ** -0.5 * LOG2_E)
    zq = jnp.zeros((HEAD_DIM, tm), BF16)
    for hh in range(AT_HEADS):
        parts = [yq[hh].astype(BF16), zq] if hh % 2 == 0 else [zq, yq[hh].astype(BF16)]
        qt_ref[0, hh] = jnp.concatenate(parts, axis=0)
    vt = pt[AT_WIDTH:2 * AT_WIDTH].astype(BF16)
    for cblk in range(tm // QBLK):
        vt_ref[0, cblk] = vt[:, cblk * QBLK:(cblk + 1) * QBLK]
    xqi = pt[2 * AT_WIDTH:2 * AT_WIDTH + IDX_HEADS * HEAD_DIM].reshape(IDX_HEADS, HEAD_DIM, tm)
    yqi = _rope_cols(xqi, cos_t, sin_t)
    for hh in range(IDX_HEADS):
        qit_ref[0, hh] = jnp.concatenate([yqi[hh].astype(BF16), zq], axis=0)
    wit_ref[0] = pt[PT_ROWS - 8:PT_ROWS] * (IDX_HEADS ** -0.5 * HEAD_DIM ** -0.5)


def _inproj(x, scale1, shift1, norm1_g, w_in, k_norm_g, idx_ln_w, idx_ln_b, q_norm_g,
            tabs, interpret):
    B, S, D = x.shape
    tm = min(S, 256)
    cos_r, sin_r, cos_t, sin_t = tabs
    w_at = w_in[:, RW_COLS:]
    w_rw = w_in[:, :RW_COLS].astype(BF16)
    w_k = jnp.concatenate([w_at[:, AT_WIDTH:2 * AT_WIDTH], w_at[:, KI_OFF:KI_OFF + HEAD_DIM],
                           jnp.zeros((D, HEAD_DIM), F32)], axis=1).astype(BF16)
    w_t = jnp.concatenate([w_at[:, 0:AT_WIDTH], w_at[:, 2 * AT_WIDTH:3 * AT_WIDTH],
                           w_at[:, 3 * AT_WIDTH:KI_OFF], w_at[:, WI_OFF:WI_OFF + IDX_HEADS],
                           jnp.zeros((D, 8 - IDX_HEADS), F32)], axis=1).T.astype(BF16)
    kg = jnp.tile(k_norm_g, 2).reshape(1, LANES)
    zpad = jnp.zeros((HEAD_DIM,), F32)
    iw = jnp.concatenate([idx_ln_w, zpad]).reshape(1, LANES)
    ib = jnp.concatenate([idx_ln_b, zpad]).reshape(1, LANES)
    qg = q_norm_g.reshape(HEAD_DIM, 1)
    li = np.arange(LANES)
    gsum = jnp.asarray((li[:, None] // HEAD_DIM == li[None, :] // HEAD_DIM).astype(np.float32)).astype(BF16)

    full = lambda shape: pl.BlockSpec(shape, lambda b, i: (0,) * len(shape))
    return pl.pallas_call(
        _inproj_kernel,
        grid=(B, S // tm),
        in_specs=[pl.BlockSpec((1, tm, D), lambda b, i: (b, i, 0)),
                  pl.BlockSpec((1, 1, D), lambda b, i: (b, 0, 0)),
                  pl.BlockSpec((1, 1, D), lambda b, i: (b, 0, 0)),
                  full((1, D)), full((D, RW_COLS)), full((D, PK_COLS)), full((PT_ROWS, D)),
                  full((1, LANES)), full((1, LANES)), full((1, LANES)), full((HEAD_DIM, 1)),
                  full((LANES, LANES)),
                  pl.BlockSpec((1, tm, LANES), lambda b, i: (b, i, 0)),
                  pl.BlockSpec((1, tm, LANES), lambda b, i: (b, i, 0)),
                  pl.BlockSpec((1, HALF, tm), lambda b, i: (b, 0, i)),
                  pl.BlockSpec((1, HALF, tm), lambda b, i: (b, 0, i))],
        out_specs=[pl.BlockSpec((1, tm, RW_COLS), lambda b, i: (b, i, 0)),
                   pl.BlockSpec((1, tm, AT_WIDTH), lambda b, i: (b, i, 0)),
                   pl.BlockSpec((1, tm, LANES), lambda b, i: (b, i, 0)),
                   pl.BlockSpec((1, AT_HEADS, LANES, tm), lambda b, i: (b, 0, 0, i)),
                   pl.BlockSpec((1, tm // QBLK, AT_WIDTH, QBLK), lambda b, i: (b, i, 0, 0)),
                   pl.BlockSpec((1, IDX_HEADS, LANES, tm), lambda b, i: (b, 0, 0, i)),
                   pl.BlockSpec((1, 8, tm), lambda b, i: (b, 0, i))],
        out_shape=[jax.ShapeDtypeStruct((B, S, RW_COLS), F32),
                   jax.ShapeDtypeStruct((B, S, AT_WIDTH), BF16),
                   jax.ShapeDtypeStruct((B, S, LANES), BF16),
                   jax.ShapeDtypeStruct((B, AT_HEADS, LANES, S), BF16),
                   jax.ShapeDtypeStruct((B, S // QBLK, AT_WIDTH, QBLK), BF16),
                   jax.ShapeDtypeStruct((B, IDX_HEADS, LANES, S), BF16),
                   jax.ShapeDtypeStruct((B, 8, S), F32)],
        compiler_params=_cparams(("arbitrary", "arbitrary")),
        name="inproj", interpret=interpret,
    )(x, scale1, shift1, norm1_g.reshape(1, D), w_rw, w_k, w_t, kg, iw, ib, qg, gsum,
      cos_r, sin_r, cos_t, sin_t)


def _rwkv_kernel(p_ref, mu_ref, w0_ref, w2_ref, a0_ref, a2_ref, g2_ref, kk_ref, ka_ref, rk_ref,
                 lnw_ref, lnb_ref, gsum_ref, y_ref, s_ref, prev_ref, yt_ref):
    C = RW_CHUNK
    W = RW_WIDTH
    nb = p_ref.shape[0]

    @pl.when(pl.program_id(1) == 0)
    def _():
        s_ref[...] = jnp.zeros_like(s_ref)
        prev_ref[...] = jnp.zeros_like(prev_ref)

    row = lax.broadcasted_iota(I32, (C, 1), 0)
    gsum = gsum_ref[...]

    def prepare(bi):
        p = p_ref[bi]
        pprev = jnp.where(row == 0, prev_ref[bi:bi + 1, :], pltpu.roll(p, 1, 0))
        prev_ref[bi:bi + 1, :] = p[C - 1:C]
        ps = p + (pprev - p) * mu_ref[...]
        r, k, v = ps[:, 0:W], ps[:, W:2 * W], ps[:, 2 * W:3 * W]
        o = 3 * W
        wl = ps[:, o:o + RW_LORA_W]
        al = ps[:, o + RW_LORA_W:o + RW_LORA_W + RW_LORA_A]
        gl = ps[:, o + RW_LORA_W + RW_LORA_A:]
        z = w0_ref[...] + _dot(jnp.tanh(wl).astype(BF16), w2_ref[...])
        nz = -z
        softplus = jnp.maximum(nz, 0.0) + jnp.log(1.0 + jnp.exp(-jnp.abs(nz)))
        logw = -jnp.exp(-softplus - 0.5)
        a = _sigmoid(a0_ref[...] + _dot(al.astype(BF16), a2_ref[...]))
        g = _dot(_sigmoid(gl).astype(BF16), g2_ref[...])
        kk = k * kk_ref[...]
        ss = _dot((kk * kk).astype(BF16), gsum)
        kk = kk * (1.0 / jnp.maximum(jnp.sqrt(ss), 1e-12))
        k2 = k * (1.0 + (a - 1.0) * ka_ref[...])
        bb = kk * a
        cw = logw
        sh = 1
        while sh < C:
            cw = cw + jnp.where(row >= sh, pltpu.roll(cw, sh, 0), 0.0)
            sh *= 2
        cw_last = cw[C - 1:C]
        e_neg = jnp.exp(-cw)
        e_end = jnp.exp(cw_last - cw)
        return dict(r=r, v=v, k2=k2, g=g, rw=r * jnp.exp(cw), kkp=kk * jnp.exp(cw - logw),
                    bw=bb * e_neg, kw=k2 * e_neg, bend=bb * e_end, kend=k2 * e_end,
                    wc=jnp.exp(cw_last), vt=v.T.astype(BF16))

    pre = [prepare(bi) for bi in range(nb)]

    ri = lax.broadcasted_iota(I32, (C, C), 0)
    ci = lax.broadcasted_iota(I32, (C, C), 1)
    strict = ri < ci
    incl = ri <= ci
    incl2 = jnp.concatenate([incl, incl], axis=0)
    lane_half = lax.broadcasted_iota(I32, (1, LANES), 1) // HEAD_DIM

    chains = [(bi, h) for bi in range(nb) for h in range(RW_HEADS)]
    pair = lambda h: slice((h // 2) * LANES, (h // 2 + 1) * LANES)
    own = lambda h: lane_half == (h % 2)
    stack = lambda bi, top, bot, h: jnp.concatenate(
        [pre[bi][top][:, pair(h)], pre[bi][bot][:, pair(h)]], axis=0)
    lh = {(bi, q): stack(bi, 'kkp', 'rw', 2 * q).astype(BF16)
          for bi in range(nb) for q in range(RW_HEADS // 2)}
    rh = [jnp.where(own(h), stack(bi, 'bw', 'kw', h), 0.0).astype(BF16) for bi, h in chains]
    aat = [_dot_nt(rh[n], lh[bi, h // 2]) for n, (bi, h) in enumerate(chains)]
    s_old = [s_ref[bi * RW_HEADS + h] for bi, h in chains]
    sl = [_dot_nt(s_old[n].astype(BF16), lh[bi, h // 2]) for n, (bi, h) in enumerate(chains)]
    vt = [pre[bi]['vt'][h * HEAD_DIM:(h + 1) * HEAD_DIM] for bi, h in chains]
    ids = range(len(chains))
    akt = [jnp.where(strict, aat[n][C:, :C], 0.0).astype(BF16) for n in ids]
    m = [jnp.where(strict, aat[n][:C, :C], 0.0).astype(BF16) for n in ids]
    xs = [-(sl[n][:, :C] + _dot(vt[n], akt[n])) for n in ids]
    xs = [xs[n] - _dot_split(xs[n], m[n]) for n in ids]
    lvl = 2
    while lvl < C:
        m = [_dot(m[n], m[n]).astype(BF16) for n in ids]
        xs = [xs[n] + _dot_split(xs[n], m[n]) for n in ids]
        lvl *= 2
    zt = [jnp.concatenate([xs[n].astype(BF16), vt[n]], axis=1) for n in ids]
    for n in ids:
        ymat = jnp.where(incl2, aat[n][:, C:], 0.0).astype(BF16)
        yt_ref[n * HEAD_DIM:(n + 1) * HEAD_DIM, :] = sl[n][:, C:] + _dot(zt[n], ymat)
    for n, (bi, h) in enumerate(chains):
        endz = jnp.where(own(h), stack(bi, 'bend', 'kend', h), 0.0).astype(BF16)
        s_ref[n] = s_old[n] * pre[bi]['wc'][:, pair(h)] + _dot(zt[n], endz)

    lnw = lnw_ref[...].reshape(RW_HEADS, HEAD_DIM, 1)
    lnb = lnb_ref[...].reshape(RW_HEADS, HEAD_DIM, 1)
    for bi in range(nb):
        q = pre[bi]
        yt = yt_ref[bi * W:(bi + 1) * W, :].reshape(RW_HEADS, HEAD_DIM, C)
        mean = jnp.mean(yt, axis=1, keepdims=True)
        yc = yt - mean
        var = jnp.mean(yc * yc, axis=1, keepdims=True)
        yn = yc * lax.rsqrt(var + RW_GN_EPS) * lnw + lnb
        y = yn.reshape(W, C).T
        bonus = _dot((q['r'] * q['k2'] * rk_ref[...]).astype(BF16), gsum) * q['v']
        y_ref[bi] = ((y + bonus) * q['g']).astype(BF16)


def _rwkv(p_rw, rw_mu, rw_w0, rw_w2, rw_a0, rw_a2, rw_g2, rw_k_k, rw_k_a, rw_r_k, rw_ln_w, rw_ln_b,
          interpret):
    B, S, _ = p_rw.shape
    C, W = RW_CHUNK, RW_WIDTH
    li = np.arange(W)
    gsum = jnp.asarray((li[:, None] // HEAD_DIM == li[None, :] // HEAD_DIM).astype(np.float32)).astype(BF16)
    row = lambda a: a.reshape(1, -1)
    full = lambda shape: pl.BlockSpec(shape, lambda b, i: (0,) * len(shape))
    nb = RW_BATCH if B % RW_BATCH == 0 else 1
    return pl.pallas_call(
        _rwkv_kernel,
        grid=(B // nb, S // C),
        in_specs=[pl.BlockSpec((nb, C, RW_COLS), lambda b, i: (b, i, 0)),
                  full((1, RW_COLS)), full((1, W)), full((RW_LORA_W, W)), full((1, W)),
                  full((RW_LORA_A, W)), full((RW_LORA_G, W)), full((1, W)), full((1, W)),
                  full((1, W)), full((W, 1)), full((W, 1)), full((W, W))],
        out_specs=pl.BlockSpec((nb, C, W), lambda b, i: (b, i, 0)),
        out_shape=jax.ShapeDtypeStruct((B, S, W), BF16),
        scratch_shapes=[pltpu.VMEM((nb * RW_HEADS, HEAD_DIM, LANES), F32),
                        pltpu.VMEM((nb, RW_COLS), F32),
                        pltpu.VMEM((nb * W, C), F32)],
        compiler_params=_cparams(("arbitrary", "arbitrary")),
        name="rwkv", interpret=interpret,
    )(p_rw, row(rw_mu), row(rw_w0), rw_w2.astype(BF16), row(rw_a0), rw_a2.astype(BF16),
      rw_g2.astype(BF16), row(rw_k_k), row(rw_k_a), row(rw_r_k), rw_ln_w.reshape(W, 1),
      rw_ln_b.reshape(W, 1), gsum)


def _dsa_kernel(topk, nbits, k_ref, vt_ref, ki_ref, qt_ref, qit_ref, wit_ref, og_ref, o_ref,
                key_s, acc_s, m_s, l_s, thr_s):
    j = pl.program_id(1)
    nkb = j + 1
    lane = lax.broadcasted_iota(I32, (QBLK, QBLK), 1)
    sub = lax.broadcasted_iota(I32, (QBLK, QBLK), 0)
    qpos = j * QBLK + lane
    wit = wit_ref[0]

    def score_blocks(i, carry):
        kbs = [i * SCORE_UNROLL + u for u in range(SCORE_UNROLL)]
        kib = [ki_ref[0, pl.ds(pl.multiple_of(kb * QBLK, QBLK), QBLK), :] for kb in kbs]
        lg = [[_dot(kib[u], qit_ref[0, hh]) for hh in range(IDX_HEADS)] for u in range(SCORE_UNROLL)]
        for u, kb in enumerate(kbs):
            s = wit[0:1, :] * jnp.maximum(lg[u][0], 0.0)
            for hh in range(1, IDX_HEADS):
                s = s + wit[hh:hh + 1, :] * jnp.maximum(lg[u][hh], 0.0)
            s = jnp.where(s == 0.0, 0.0, s)
            bits = pltpu.bitcast(s, I32)
            skey = jnp.where(bits < 0, bits ^ 0x7FFFFFFF, bits)
            key_s[kb] = jnp.where(kb * QBLK + sub <= qpos, skey, INT_MIN)
        return carry

    lax.fori_loop(0, pl.cdiv(nkb, SCORE_UNROLL), score_blocks, 0)

    @pl.when(nkb * QBLK <= topk)
    def _():
        thr_s[0:1, :] = jnp.full((1, QBLK), INT_MIN, I32)
        thr_s[1:2, :] = jnp.zeros((1, QBLK), I32)

    @pl.when(nkb * QBLK > topk)
    def _():
        n_done = pl.cdiv(nkb, SCORE_UNROLL) * SCORE_UNROLL
        n_cnt = pl.cdiv(nkb, COUNT_UNROLL)

        def fill(kb, carry):
            key_s[kb] = jnp.full((QBLK, QBLK), INT_MIN, I32)
            return carry

        lax.fori_loop(n_done, n_cnt * COUNT_UNROLL, fill, 0)

        def count(preds):
            def body(i, accs):
                accs = list(accs)
                for u in range(COUNT_UNROLL):
                    kb = i * COUNT_UNROLL + u
                    ky = key_s[kb]
                    for n, pred in enumerate(preds):
                        hit = pred(ky, kb * QBLK + sub).astype(I32)
                        accs[n] = accs[n] + jnp.sum(hit.reshape(QBLK // 8, 8, QBLK), axis=0)
                return tuple(accs)
            accs = lax.fori_loop(0, n_cnt, body, tuple(jnp.zeros((8, QBLK), I32) for _ in preds))
            return [jnp.sum(a, axis=0, keepdims=True) for a in accs]

        c0, = count([lambda ky, ix: ky >= 0])
        t0 = jnp.where(c0 >= topk, 0, INT_MIN).astype(I32)

        def bit_step(i, t):
            cand = t | jnp.left_shift(jnp.int32(1), 30 - i)
            c, = count([lambda ky, ix: ky >= cand])
            return jnp.where(c >= topk, cand, t)

        thr = lax.fori_loop(0, 31, bit_step, t0)
        n_gt, n_eq = count([lambda ky, ix: ky > thr, lambda ky, ix: ky == thr])
        need = topk - n_gt
        thr_s[0:1, :] = thr
        thr_s[1:2, :] = jnp.full((1, QBLK), 2 ** nbits, I32)

        @pl.when(jnp.max(jnp.abs(n_eq - need)) > 0)
        def _():
            def idx_step(i, mm):
                cand = mm | jnp.left_shift(jnp.int32(1), nbits - 1 - i)
                c, = count([lambda ky, ix: (ky == thr) & (ix < cand)])
                return jnp.where(c < need, cand, mm)

            thr_s[1:2, :] = lax.fori_loop(0, nbits, idx_step, jnp.zeros((1, QBLK), I32))

    thr = thr_s[0:1, :]
    mm = thr_s[1:2, :]
    m_s[...] = jnp.full_like(m_s, NEG_BIG)
    l_s[...] = jnp.zeros_like(l_s)
    acc_s[...] = jnp.zeros_like(acc_s)

    qt2 = [jnp.concatenate([qt_ref[0, 2 * q], qt_ref[0, 2 * q + 1]], axis=1)
           for q in range(AT_HEADS // 2)]

    def attn_blocks(i, carry):
        kbs = [i * ATTN_UNROLL + u for u in range(ATTN_UNROLL)]
        sel = []
        for kb in kbs:
            skey = key_s[kb]
            kidx = kb * QBLK + sub
            sel.append((kidx <= qpos) & ((skey > thr) | ((skey == thr) & (kidx <= mm))))
        sel = jnp.concatenate(sel, axis=0)
        kblk = [k_ref[0, pl.ds(pl.multiple_of(kb * QBLK, QBLK), QBLK), :] for kb in kbs]
        vtb = jnp.concatenate([vt_ref[0, kb] for kb in kbs], axis=1)
        s2 = [[_dot(kblk[u][:, q * LANES:(q + 1) * LANES], qt2[q]) for u in range(ATTN_UNROLL)]
              for q in range(AT_HEADS // 2)]
        pexp, alpha = [], []
        for hh in range(AT_HEADS):
            half = slice((hh % 2) * QBLK, (hh % 2 + 1) * QBLK)
            s = jnp.concatenate([s2[hh // 2][u][:, half] for u in range(ATTN_UNROLL)], axis=0)
            s = jnp.where(sel, s, NEG_BIG)
            m_old = m_s[hh:hh + 1, :]
            m_new = jnp.maximum(m_old, jnp.max(s, axis=0, keepdims=True))
            pe = jnp.exp2(s - m_new)
            al = jnp.exp2(m_old - m_new)
            l_s[hh:hh + 1, :] = al * l_s[hh:hh + 1, :] + jnp.sum(pe, axis=0, keepdims=True)
            m_s[hh:hh + 1, :] = m_new
            pexp.append(pe.astype(BF16))
            alpha.append(al)
        for hh in range(AT_HEADS):
            hs = slice(hh * HEAD_DIM, (hh + 1) * HEAD_DIM)
            acc_s[hs, :] = alpha[hh] * acc_s[hs, :] + _dot(vtb[hs, :], pexp[hh])
        return carry

    lax.fori_loop(0, pl.cdiv(nkb, ATTN_UNROLL), attn_blocks, 0)

    for hh in range(AT_HEADS):
        hs = slice(hh * HEAD_DIM, (hh + 1) * HEAD_DIM)
        oh = acc_s[hs, :] * (1.0 / l_s[hh:hh + 1, :])
        ms = jnp.mean(oh * oh, axis=0, keepdims=True)
        acc_s[hs, :] = oh * lax.rsqrt(ms + NORM_EPS) * og_ref[hs, :]
    o_ref[0] = acc_s[...].T.astype(BF16)


def _dsa(k, vt, ki, qt, qit, wit, at_out_g, interpret):
    B, S, _ = k.shape
    nq = S // QBLK
    topk = min(IDX_TOPK_MAX, S // 4)
    nbits = int(np.log2(S))
    assert 2 ** nbits == S and nq % COUNT_UNROLL == 0 and SCORE_UNROLL == ATTN_UNROLL
    assert COUNT_UNROLL % SCORE_UNROLL == 0
    return pl.pallas_call(
        functools.partial(_dsa_kernel, topk, nbits),
        grid=(B, nq),
        in_specs=[pl.BlockSpec((1, S, AT_WIDTH), lambda b, j: (b, 0, 0)),
                  pl.BlockSpec((1, nq, AT_WIDTH, QBLK), lambda b, j: (b, 0, 0, 0)),
                  pl.BlockSpec((1, S, LANES), lambda b, j: (b, 0, 0)),
                  pl.BlockSpec((1, AT_HEADS, LANES, QBLK), lambda b, j: (b, 0, 0, j)),
                  pl.BlockSpec((1, IDX_HEADS, LANES, QBLK), lambda b, j: (b, 0, 0, j)),
                  pl.BlockSpec((1, 8, QBLK), lambda b, j: (b, 0, j)),
                  pl.BlockSpec((AT_WIDTH, 1), lambda b, j: (0, 0))],
        out_specs=pl.BlockSpec((1, QBLK, AT_WIDTH), lambda b, j: (b, j, 0)),
        out_shape=jax.ShapeDtypeStruct((B, S, AT_WIDTH), BF16),
        scratch_shapes=[pltpu.VMEM((nq, QBLK, QBLK), I32),
                        pltpu.VMEM((AT_WIDTH, QBLK), F32),
                        pltpu.VMEM((AT_HEADS, QBLK), F32),
                        pltpu.VMEM((AT_HEADS, QBLK), F32),
                        pltpu.VMEM((8, QBLK), I32)],
        compiler_params=_cparams(("arbitrary", "arbitrary")),
        name="dsa", interpret=interpret,
    )(k, vt, ki, qt, qit, wit, at_out_g.reshape(AT_WIDTH, 1))


def _first_max(vals, idx, axis, sentinel):
    m = jnp.max(vals, axis=axis, keepdims=True)
    return m, jnp.min(jnp.where(vals == m, idx, sentinel), axis=axis, keepdims=True)


def _route_cols(logits_t, bias_col):
    E, tm = logits_t.shape
    pg = E // N_GROUPS
    scores = _sigmoid(logits_t)
    biased = scores + bias_col
    b3 = biased.reshape(N_GROUPS, pg, tm)
    r3 = lax.broadcasted_iota(I32, (N_GROUPS, pg, tm), 1)
    m1, first = _first_max(b3, r3, 1, pg)
    m2 = jnp.max(jnp.where(r3 == first, -jnp.inf, b3), axis=1, keepdims=True)
    cur = (m1 + m2).reshape(N_GROUPS, tm)
    grow = lax.broadcasted_iota(I32, (N_GROUPS, tm), 0)
    gsel = jnp.zeros((N_GROUPS, tm), F32)
    for _ in range(TOPK_GROUPS):
        _, gi = _first_max(cur, grow, 0, N_GROUPS)
        hit = grow == gi
        gsel = jnp.where(hit, 1.0, gsel)
        cur = jnp.where(hit, -jnp.inf, cur)
    gmask = jnp.broadcast_to(gsel.reshape(N_GROUPS, 1, tm), (N_GROUPS, pg, tm)).reshape(E, tm)
    cur = jnp.where(gmask > 0.0, biased, -jnp.inf)
    row = lax.broadcasted_iota(I32, (E, tm), 0)
    onehot = jnp.zeros((E, tm), F32)
    eids, gws = [], []
    for _ in range(TOP_K):
        _, ei = _first_max(cur, row, 0, E)
        hit = row == ei
        eids.append(ei)
        gws.append(jnp.sum(jnp.where(hit, scores, 0.0), axis=0, keepdims=True))
        onehot = jnp.where(hit, 1.0, onehot)
        cur = jnp.where(hit, -jnp.inf, cur)
    eid = jnp.concatenate(eids, axis=0)
    gw = jnp.concatenate(gws, axis=0)
    gw = gw * (ROUTED_SCALE / jnp.sum(gw, axis=0, keepdims=True))
    return eid, gw, onehot


def _post_kernel(x_ref, yrw_ref, yat_ref, g1_ref, sc_ref, sh_ref, g2_ref, ng_ref, wo_ref, rwh_ref,
                 rwl_ref, rb_ref, s1_ref, s3_ref, s2_ref, base_ref, h2_ref, eid_ref, gw_ref, rank_ref,
                 cnt_ref):
    W = RW_WIDTH
    tm = x_ref.shape[1]
    E = rwh_ref.shape[0]

    @pl.when((pl.program_id(0) == 0) & (pl.program_id(1) == 0))
    def _():
        cnt_ref[...] = jnp.zeros_like(cnt_ref)

    mix = _dot(yrw_ref[0], wo_ref[0:W, :]) + _dot(yat_ref[0], wo_ref[W:, :])
    x1 = x_ref[0] + g1_ref[0] * mix
    ms = jnp.mean(x1 * x1, axis=-1, keepdims=True)
    h2 = x1 * lax.rsqrt(ms + NORM_EPS) * ng_ref[...] * (1.0 + sc_ref[0]) + sh_ref[0]
    hb = h2.astype(BF16)
    h2_ref[0, :, 0, :] = _pack_halves(h2)
    act =(_silu(_dot(hb, s1_ref[...])) * _dot(hb, s3_ref[...])).astype(BF16)
    base_ref[0] = x1 + g2_ref[0] * _dot(act, s2_ref[...])

    h_lo = (h2 - hb.astype(F32)).astype(BF16)
    logits_t = (_dot_nt(rwh_ref[...], hb) + _dot_nt(rwh_ref[...], h_lo)
                + _dot_nt(rwl_ref[...], hb))
    eid, gw, onehot = _route_cols(logits_t, rb_ref[...])
    eid_ref[0] = eid
    gw_ref[0] = gw
    ti = lax.broadcasted_iota(I32, (tm, tm), 0)
    tj = lax.broadcasted_iota(I32, (tm, tm), 1)
    before = _dot(onehot.astype(BF16), (ti < tj).astype(BF16)) + cnt_ref[:, 0:1]
    row = lax.broadcasted_iota(I32, (E, tm), 0)
    ranks = [jnp.sum(jnp.where(row == eid[kk:kk + 1, :], before, 0.0), axis=0, keepdims=True)
             for kk in range(TOP_K)]
    rank_ref[0] = jnp.concatenate(ranks, axis=0).astype(I32)
    cnt_ref[...] = cnt_ref[...] + jnp.sum(onehot, axis=1, keepdims=True)


def _post(x, y_rw, y_at, gate1, scale2, shift2, gate2, norm2_g, w_out, router_w, router_bias,
          sw1, sw3, sw2, interpret):
    B, S, D = x.shape
    tm = min(S, 256)
    E = router_w.shape[1]
    sd = sw1.shape[1]
    full = lambda shape: pl.BlockSpec(shape, lambda b, i: (0,) * len(shape))
    tok = lambda w: pl.BlockSpec((1, tm, w), lambda b, i: (b, i, 0))
    per_b = pl.BlockSpec((1, 1, D), lambda b, i: (b, 0, 0))
    col8 = pl.BlockSpec((1, TOP_K, tm), lambda b, i: (b, 0, i))
    rw_t = router_w.T
    rw_hi = rw_t.astype(BF16)
    rw_lo = (rw_t - rw_hi.astype(F32)).astype(BF16)
    return pl.pallas_call(
        _post_kernel,
        grid=(B, S // tm),
        in_specs=[tok(D), tok(RW_WIDTH), tok(AT_WIDTH), per_b, per_b, per_b, per_b, full((1, D)),
                  full((D, D)), full((E, D)), full((E, D)), full((E, 1)), full((D, sd)), full((D, sd)),
                  full((sd, D))],
        out_specs=[tok(D), pl.BlockSpec((1, tm, 1, D // 2), lambda b, i: (b, i, 0, 0)),
                   col8, col8, col8, full((E, LANES))],
        out_shape=[jax.ShapeDtypeStruct((B, S, D), F32),
                   jax.ShapeDtypeStruct((B, S, 1, D // 2), I32),
                   jax.ShapeDtypeStruct((B, TOP_K, S), I32),
                   jax.ShapeDtypeStruct((B, TOP_K, S), F32),
                   jax.ShapeDtypeStruct((B, TOP_K, S), I32),
                   jax.ShapeDtypeStruct((E, LANES), F32)],
        compiler_params=_cparams(("arbitrary", "arbitrary")),
        name="post", interpret=interpret,
    )(x, y_rw, y_at, gate1, scale2, shift2, gate2, norm2_g.reshape(1, D), w_out.astype(BF16),
      rw_hi, rw_lo, router_bias.reshape(E, 1), sw1.astype(BF16), sw3.astype(BF16), sw2.astype(BF16))


def _expert_kernel(be_ref, nv_ref, nu_ref, xs_ref, w1_ref, w3_ref, w2_ref, o_ref, w13_s, w2_s):
    i = pl.program_id(0)
    blk, _, hw = xs_ref.shape
    F = w1_ref.shape[2]
    used = i < nu_ref[0]

    @pl.when(used & ((i == 0) | (be_ref[i] != be_ref[jnp.maximum(i - 1, 0)])))
    def _():
        w13_s[:, :F] = w1_ref[0].astype(BF16)
        w13_s[:, F:] = w3_ref[0].astype(BF16)
        w2_s[...] = w2_ref[0].astype(BF16)

    @pl.when(used)
    def _():
        live = lax.broadcasted_iota(I32, (blk, 1), 0) < nv_ref[i]
        x_lo, x_hi = _unpack_halves(jnp.where(live, xs_ref[:, 0, :], 0))
        h13 =(_dot(x_lo.astype(BF16), w13_s[:hw, :]) + _dot(x_hi.astype(BF16), w13_s[hw:, :]))
        act = (_silu(h13[:, :F]) * h13[:, F:]).astype(BF16)
        o_ref[:, 0, :] = _pack_halves(_dot(act, w2_s[...]))


def _experts(xs, block_e, block_rows, n_used, w1, w3, w2, interpret):
    P, _, hw = xs.shape
    E, D, F = w1.shape
    nb = P // EXP_BLK
    blk = lambda i, nu: jnp.minimum(i, nu[0] - 1)
    grid_spec = pltpu.PrefetchScalarGridSpec(
        num_scalar_prefetch=3,
        grid=(nb,),
        in_specs=[pl.BlockSpec((EXP_BLK, 1, hw), lambda i, be, nv, nu: (blk(i, nu), 0, 0)),
                  pl.BlockSpec((1, D, F), lambda i, be, nv, nu: (be[blk(i, nu)], 0, 0)),
                  pl.BlockSpec((1, D, F), lambda i, be, nv, nu: (be[blk(i, nu)], 0, 0)),
                  pl.BlockSpec((1, F, D), lambda i, be, nv, nu: (be[blk(i, nu)], 0, 0))],
        out_specs=pl.BlockSpec((EXP_BLK, 1, hw), lambda i, be, nv, nu: (blk(i, nu), 0, 0)),
        scratch_shapes=[pltpu.VMEM((D, 2 * F), BF16), pltpu.VMEM((F, D), BF16)],
    )
    return pl.pallas_call(
        _expert_kernel,
        grid_spec=grid_spec,
        out_shape=jax.ShapeDtypeStruct((P, 1, hw), I32),
        compiler_params=_cparams(("arbitrary",)),
        name="experts", interpret=interpret,
    )(block_e, block_rows, n_used, xs, w1, w3, w2)


def _row_out(tile, t, slots, slot, sem):
    return pltpu.make_async_copy(tile.at[t], slots.at[slot], sem)


def _row_in(slots, slot, tile, t, sem):
    return pltpu.make_async_copy(slots.at[slot], tile.at[pl.ds(t, 1)], sem)


def _dispatch_kernel(dest_ref, h2_ref, xs_ref, sem):
    td = h2_ref.shape[0]

    def issue(t, carry):
        for kk in range(TOP_K):
            _row_out(h2_ref, t, xs_ref, dest_ref[0, 0, kk * td + t], sem).start(priority=kk % 2)
        return carry

    def drain(t, carry):
        for kk in range(TOP_K):
            _row_out(h2_ref, t, xs_ref, dest_ref[0, 0, kk * td + t], sem).wait()
        return carry

    lax.fori_loop(0, td, issue, 0)
    lax.fori_loop(0, td, drain, 0)


def _dispatch(h2, dest_tiles, n_slots, interpret):
    T, _, D = h2.shape
    nt, _, n = dest_tiles.shape
    td = n // TOP_K
    return pl.pallas_call(
        _dispatch_kernel,
        grid=(nt,),
        in_specs=[pl.BlockSpec((1, 1, n), lambda i: (i, 0, 0), memory_space=pltpu.SMEM),
                  pl.BlockSpec((td, 1, D), lambda i: (i, 0, 0))],
        out_specs=pl.BlockSpec(memory_space=pl.ANY),
        out_shape=jax.ShapeDtypeStruct((n_slots, 1, D), h2.dtype),
        scratch_shapes=[pltpu.SemaphoreType.DMA(())],
        compiler_params=_cparams(("arbitrary",)),
        name="dispatch", interpret=interpret,
    )(dest_tiles, h2)


def _combine_kernel(dest_ref, base_ref, g2_ref, gw_ref, ys_ref, o_ref, buf, sem):
    td = base_ref.shape[0]

    def issue(t, carry):
        for kk in range(TOP_K):
            _row_in(ys_ref, dest_ref[0, 0, kk * td + t], buf.at[kk], t, sem).start(priority=kk % 2)
        return carry

    def drain(t, carry):
        for kk in range(TOP_K):
            _row_in(ys_ref, dest_ref[0, 0, kk * td + t], buf.at[kk], t, sem).wait()
        return carry

    lax.fori_loop(0, td, issue, 0)
    lax.fori_loop(0, td, drain, 0)
    gw = gw_ref[...]
    acc_lo, acc_hi = _unpack_halves(buf[0])
    acc_lo, acc_hi = gw[:, 0:1] * acc_lo, gw[:, 0:1] * acc_hi
    for kk in range(1, TOP_K):
        y_lo, y_hi = _unpack_halves(buf[kk])
        acc_lo = acc_lo + gw[:, kk:kk + 1] * y_lo
        acc_hi = acc_hi + gw[:, kk:kk + 1] * y_hi
    o_ref[...] = base_ref[...] + g2_ref[0] * jnp.concatenate([acc_lo, acc_hi], axis=1)


def _combine(base, gate2, gw_tok, ys, dest_tiles, tiles_per_batch, interpret):
    T, D = base.shape
    nt, _, n = dest_tiles.shape
    td = n // TOP_K
    return pl.pallas_call(
        _combine_kernel,
        grid=(nt,),
        in_specs=[pl.BlockSpec((1, 1, n), lambda i: (i, 0, 0), memory_space=pltpu.SMEM),
                  pl.BlockSpec((td, D), lambda i: (i, 0)),
                  pl.BlockSpec((1, 1, D), lambda i: (i // tiles_per_batch, 0, 0)),
                  pl.BlockSpec((td, TOP_K), lambda i: (i, 0)),
                  pl.BlockSpec(memory_space=pl.ANY)],
        out_specs=pl.BlockSpec((td, D), lambda i: (i, 0)),
        out_shape=jax.ShapeDtypeStruct((T, D), F32),
        scratch_shapes=[pltpu.VMEM((TOP_K, td, D // 2), I32), pltpu.SemaphoreType.DMA(())],
        compiler_params=_cparams(("arbitrary",)),
        name="combine", interpret=interpret,
    )(dest_tiles, base, gate2, gw_tok, ys)


def _slots_kernel(eid_ref, rank_ref, pstart_ref, dest_ref):
    td = eid_ref.shape[2]
    E = pstart_ref.shape[0]
    row = lax.broadcasted_iota(I32, (E, td), 0)
    pstart = pstart_ref[...]
    eid = eid_ref[0]
    for kk in range(TOP_K):
        base = jnp.sum(jnp.where(row == eid[kk:kk + 1, :], pstart, 0), axis=0, keepdims=True)
        dest_ref[0, :, kk * td:(kk + 1) * td] = base + rank_ref[0, kk:kk + 1, :]


def _slot_plan(counts, eid_t, rank_t, td, interpret):
    B, _, S = eid_t.shape
    E = counts.shape[0]
    padded = (counts + EXP_BLK - 1) // EXP_BLK * EXP_BLK
    pend = jnp.cumsum(padded)
    pstart = (pend - padded).astype(I32)
    nb = -(-(B * S * TOP_K + E * (EXP_BLK - 1)) // EXP_BLK)
    first_row = jnp.arange(nb, dtype=I32) * EXP_BLK
    block_e = jnp.sum(pend[None, :] <= first_row[:, None], axis=1)
    block_e = jnp.minimum(block_e, E - 1).astype(I32)
    block_rows = jnp.clip(pstart[block_e] + counts[block_e] - first_row, 0, EXP_BLK).astype(I32)
    nt = S // td
    dest_tiles = pl.pallas_call(
        _slots_kernel,
        grid=(B, nt),
        in_specs=[pl.BlockSpec((1, TOP_K, td), lambda b, i: (b, 0, i)),
                  pl.BlockSpec((1, TOP_K, td), lambda b, i: (b, 0, i)),
                  pl.BlockSpec((E, 1), lambda b, i: (0, 0))],
        out_specs=pl.BlockSpec((1, 1, TOP_K * td), lambda b, i: (b * nt + i, 0, 0)),
        out_shape=jax.ShapeDtypeStruct((B * nt, 1, TOP_K * td), I32),
        compiler_params=_cparams(("arbitrary", "arbitrary")),
        name="slots", interpret=interpret,
    )(eid_t, rank_t, pstart.reshape(E, 1))
    n_used = (pend[-1:] // EXP_BLK).astype(I32)
    return block_e, block_rows, n_used, dest_tiles, nb * EXP_BLK


def _forward(x, c, positions, w_ada, b_ada, norm1_g, norm2_g, w_in, rw_mu, rw_w0, rw_w2,
             rw_a0, rw_a2, rw_g2, rw_k_k, rw_k_a, rw_r_k, rw_ln_w, rw_ln_b, q_norm_g,
             k_norm_g, idx_ln_w, idx_ln_b, at_out_g, w_out, router_w, router_bias,
             exp_w1, exp_w3, exp_w2, shared_w1, shared_w3, shared_w2, interpret=False):
    B, S, D = x.shape
    depth = w_ada.shape[0]
    for l in range(depth):
        mod = _mod(c, w_ada[l], b_ada[l], interpret)
        shift1, scale1, gate1, shift2, scale2, gate2 = [
            m.reshape(B, 1, D) for m in jnp.split(mod, 6, axis=-1)]
        tabs = _rope_tables(positions, interpret)
        p_rw, k, ki, qt, vt, qit, wit = _inproj(
            x, scale1, shift1, norm1_g[l], w_in[l], k_norm_g[l], idx_ln_w[l], idx_ln_b[l],
            q_norm_g[l], tabs, interpret)
        y_rw = _rwkv(p_rw, rw_mu[l], rw_w0[l], rw_w2[l], rw_a0[l], rw_a2[l], rw_g2[l], rw_k_k[l],
                     rw_k_a[l], rw_r_k[l], rw_ln_w[l], rw_ln_b[l], interpret)
        y_at = _dsa(k, vt, ki, qt, qit, wit, at_out_g[l], interpret)
        base, h2, eid_t, gw_t, rank_t, cnt = _post(
            x, y_rw, y_at, gate1, scale2, shift2, gate2, norm2_g[l], w_out[l], router_w[l],
            router_bias[l], shared_w1[l], shared_w3[l], shared_w2[l], interpret)
        T = B * S
        td = min(S, ROW_TILE)
        block_e, block_rows, n_used, dest_tiles, n_slots = _slot_plan(
            cnt[:, 0].astype(I32), eid_t, rank_t, td, interpret)
        xs = _dispatch(h2.reshape(T, 1, D // 2), dest_tiles, n_slots, interpret)
        ys = _experts(xs, block_e, block_rows, n_used, exp_w1[l], exp_w3[l], exp_w2[l], interpret)
        gw_tok = gw_t.transpose(0, 2, 1).reshape(T, TOP_K)
        x = _combine(base.reshape(T, D), gate2, gw_tok, ys, dest_tiles, S // td,
                     interpret).reshape(B, S, D)
    return x


def kernel(x, c, positions, w_ada, b_ada, norm1_g, norm2_g, w_in, rw_mu, rw_w0, rw_w2, rw_a0, rw_a2, rw_g2, rw_k_k, rw_k_a, rw_r_k, rw_ln_w, rw_ln_b, q_norm_g, k_norm_g, idx_ln_w, idx_ln_b, at_out_g, w_out, router_w, router_bias, exp_w1, exp_w3, exp_w2, shared_w1, shared_w3, shared_w2):
    return _forward(x, c, positions, w_ada, b_ada, norm1_g, norm2_g, w_in, rw_mu, rw_w0, rw_w2,
                    rw_a0, rw_a2, rw_g2, rw_k_k, rw_k_a, rw_r_k, rw_ln_w, rw_ln_b, q_norm_g,
                    k_norm_g, idx_ln_w, idx_ln_b, at_out_g, w_out, router_w, router_bias,
                    exp_w1, exp_w3, exp_w2, shared_w1, shared_w3, shared_w2)
```

```python
import functools

import jax
import jax.numpy as jnp
import numpy as np
from jax import lax
from jax.experimental import pallas as pl
from jax.experimental.pallas import tpu as pltpu

F32 = jnp.float32
BF16 = jnp.bfloat16
I32 = jnp.int32
HIGHEST = lax.Precision.HIGHEST

LANES = 128
HEAD_DIM = 64
HALF = HEAD_DIM // 2
RW_HEADS = 8
RW_WIDTH = RW_HEADS * HEAD_DIM
AT_HEADS = 8
AT_WIDTH = AT_HEADS * HEAD_DIM
IDX_HEADS = 4
RW_LORA_W, RW_LORA_A, RW_LORA_G = 64, 64, 128
RW_COLS = 3 * RW_WIDTH + RW_LORA_W + RW_LORA_A + RW_LORA_G
KI_OFF = 3 * AT_WIDTH + IDX_HEADS * HEAD_DIM
WI_OFF = KI_OFF + HEAD_DIM
PT_ROWS = 2 * AT_WIDTH + IDX_HEADS * HEAD_DIM + 8
PK_COLS = AT_WIDTH + LANES
ROPE_THETA = 10000.0
NORM_EPS = 1e-6
LN_EPS = 1e-6
RW_GN_EPS = 64e-5
IDX_TOPK_MAX = 256
N_EXPERTS = 256
TOP_K = 8
N_GROUPS = 8
TOPK_GROUPS = 4
ROUTED_SCALE = 2.5
INT_MIN = -2 ** 31
PAIR_HI_MASK = -65536
NEG_BIG = -1e30

RW_CHUNK = 128
RW_BATCH = 4
QBLK = 128
SCORE_UNROLL = 4
ATTN_UNROLL = 4
COUNT_UNROLL = 4
EXP_BLK = 512
ROW_TILE = 256
LOG2_E = 1.4426950408889634
VMEM_LIMIT = 48 * 1024 * 1024


def _cparams(sem):
    return pltpu.CompilerParams(dimension_semantics=sem, vmem_limit_bytes=VMEM_LIMIT)


def _sigmoid(x):
    return 1.0 / (1.0 + jnp.exp(-x))


def _silu(x):
    return x * _sigmoid(x)


def _dot(a, b):
    return jnp.dot(a, b, preferred_element_type=F32)


def _dot_split(a, b):
    hi = a.astype(BF16)
    lo = (a - hi.astype(F32)).astype(BF16)
    return _dot(hi, b) + _dot(lo, b)


def _pack_halves(x):
    w = x.shape[1] // 2
    lo = pltpu.bitcast(x[:, :w].astype(BF16).astype(F32), I32)
    hi = pltpu.bitcast(x[:, w:].astype(BF16).astype(F32), I32)
    return (hi & PAIR_HI_MASK) | lax.shift_right_logical(lo, 16)


def _unpack_halves(p):
    return pltpu.bitcast(p << 16, F32), pltpu.bitcast(p & PAIR_HI_MASK, F32)


def _dot_nt(a, b):
    return lax.dot_general(a, b, (((1,), (1,)), ((), ())), preferred_element_type=F32)


def _mod_kernel(c_ref, w_ref, b_ref, o_ref):
    c = c_ref[...]
    o_ref[...] = jnp.dot(_silu(c), w_ref[...], precision=HIGHEST,
                         preferred_element_type=F32) + b_ref[...]


def _mod(c, w_ada, b_ada, interpret):
    B, D = c.shape
    n = w_ada.shape[1] // D
    return pl.pallas_call(
        _mod_kernel,
        grid=(n,),
        in_specs=[pl.BlockSpec((B, D), lambda i: (0, 0)),
                  pl.BlockSpec((D, D), lambda i: (0, i)),
                  pl.BlockSpec((1, D), lambda i: (0, i))],
        out_specs=pl.BlockSpec((B, D), lambda i: (0, i)),
        out_shape=jax.ShapeDtypeStruct((B, n * D), F32),
        compiler_params=_cparams(("arbitrary",)),
        name="mod", interpret=interpret,
    )(c, w_ada, b_ada.reshape(1, -1))


def _rope_tab_kernel(pc_ref, pr_ref, cr_ref, sr_ref, ct_ref, st_ref):
    log_theta = float(np.log(ROPE_THETA))
    lane = lax.broadcasted_iota(I32, (1, LANES), 1)
    inv_r = jnp.exp((lane % HALF).astype(F32) * (-log_theta / HALF))
    ang = pc_ref[0].astype(F32) * inv_r
    cr_ref[0] = jnp.cos(ang)
    sr_ref[0] = jnp.where((lane % HEAD_DIM) < HALF, -jnp.sin(ang), jnp.sin(ang))
    sub = lax.broadcasted_iota(I32, (HALF, 1), 0)
    inv_c = jnp.exp(sub.astype(F32) * (-log_theta / HALF))
    ang_t = inv_c * pr_ref[0].astype(F32)
    ct_ref[0] = jnp.cos(ang_t)
    st_ref[0] = jnp.sin(ang_t)


def _rope_tables(positions, interpret):
    B, S = positions.shape
    ts = min(S, 512)
    return pl.pallas_call(
        _rope_tab_kernel,
        grid=(B, S // ts),
        in_specs=[pl.BlockSpec((1, ts, 1), lambda b, i: (b, i, 0)),
                  pl.BlockSpec((1, 1, ts), lambda b, i: (b, 0, i))],
        out_specs=[pl.BlockSpec((1, ts, LANES), lambda b, i: (b, i, 0)),
                   pl.BlockSpec((1, ts, LANES), lambda b, i: (b, i, 0)),
                   pl.BlockSpec((1, HALF, ts), lambda b, i: (b, 0, i)),
                   pl.BlockSpec((1, HALF, ts), lambda b, i: (b, 0, i))],
        out_shape=[jax.ShapeDtypeStruct((B, S, LANES), F32),
                   jax.ShapeDtypeStruct((B, S, LANES), F32),
                   jax.ShapeDtypeStruct((B, HALF, S), F32),
                   jax.ShapeDtypeStruct((B, HALF, S), F32)],
        compiler_params=_cparams(("arbitrary", "arbitrary")),
        name="rope_tab", interpret=interpret,
    )(positions.reshape(B, S, 1), positions.reshape(B, 1, S))


def _rope_rows(y, cos, sin_signed):
    lane = lax.broadcasted_iota(I32, (1, LANES), 1)
    partner = jnp.where((lane % HEAD_DIM) < HALF,
                        pltpu.roll(y, LANES - HALF, 1), pltpu.roll(y, HALF, 1))
    return y * cos + partner * sin_signed


def _rope_cols(y, cos_t, sin_t):
    x1, x2 = y[:, :HALF], y[:, HALF:]
    return jnp.concatenate([x1 * cos_t - x2 * sin_t, x2 * cos_t + x1 * sin_t], axis=1)


def _inproj_kernel(x_ref, sc_ref, sh_ref, g_ref, wrw_ref, wk_ref, wt_ref, kg_ref, iw_ref, ib_ref,
                   qg_ref, gsum_ref, cr_ref, sr_ref, ct_ref, st_ref,
                   prw_ref, k_ref, ki_ref, qt_ref, vt_ref, qit_ref, wit_ref):
    tm = x_ref.shape[1]
    x = x_ref[0]
    ms = jnp.mean(x * x, axis=-1, keepdims=True)
    h = x * lax.rsqrt(ms + NORM_EPS) * g_ref[...] * (1.0 + sc_ref[0]) + sh_ref[0]
    hb = h.astype(BF16)
    prw_ref[0] = _dot(hb, wrw_ref[...])
    pk = _dot(hb, wk_ref[...])
    pt = _dot_nt(wt_ref[...], hb)

    cos_r, sin_r = cr_ref[0], sr_ref[0]
    gsum = gsum_ref[...]
    inv_hd = 1.0 / HEAD_DIM
    for p in range(AT_WIDTH // LANES):
        xk = pk[:, p * LANES:(p + 1) * LANES]
        ss = _dot_split(xk * xk, gsum)
        y = xk * lax.rsqrt(ss * inv_hd + NORM_EPS) * kg_ref[...]
        k_ref[0, :, p * LANES:(p + 1) * LANES] = _rope_rows(y, cos_r, sin_r).astype(BF16)
    xi = pk[:, AT_WIDTH:AT_WIDTH + LANES]
    mu = _dot_split(xi, gsum) * inv_hd
    xc = xi - mu
    var = _dot_split(xc * xc, gsum) * inv_hd
    yi = xc * lax.rsqrt(var + LN_EPS) * iw_ref[...] + ib_ref[...]
    ki_ref[0] = _rope_rows(yi, cos_r, sin_r).astype(BF16)

    cos_t, sin_t = ct_ref[0][None], st_ref[0][None]
    xq = pt[0:AT_WIDTH].reshape(AT_HEADS, HEAD_DIM, tm)
    msq = jnp.mean(xq * xq, axis=1, keepdims=True)
    yq = xq * lax.rsqrt(msq + NORM_EPS) * qg_ref[...][None]
    yq = _rope_cols(yq, cos_t, sin_t) * (HEAD_DIM ** -0.5 * LOG2_E)
    zq = jnp.zeros((HEAD_DIM, tm), BF16)
    for hh in range(AT_HEADS):
        parts = [yq[hh].astype(BF16), zq] if hh % 2 == 0 else [zq, yq[hh].astype(BF16)]
        qt_ref[0, hh] = jnp.concatenate(parts, axis=0)
    vt = pt[AT_WIDTH:2 * AT_WIDTH].astype(BF16)
    for cblk in range(tm // QBLK):
        vt_ref[0, cblk] = vt[:, cblk * QBLK:(cblk + 1) * QBLK]
    xqi = pt[2 * AT_WIDTH:2 * AT_WIDTH + IDX_HEADS * HEAD_DIM].reshape(IDX_HEADS, HEAD_DIM, tm)
    yqi = _rope_cols(xqi, cos_t, sin_t)
    for hh in range(IDX_HEADS):
        qit_ref[0, hh] = jnp.concatenate([yqi[hh].astype(BF16), zq], axis=0)
    wit_ref[0] = pt[PT_ROWS - 8:PT_ROWS] * (IDX_HEADS ** -0.5 * HEAD_DIM ** -0.5)


def _inproj(x, scale1, shift1, norm1_g, w_in, k_norm_g, idx_ln_w, idx_ln_b, q_norm_g,
            tabs, interpret):
    B, S, D = x.shape
    tm = min(S, 512)
    cos_r, sin_r, cos_t, sin_t = tabs
    w_at = w_in[:, RW_COLS:]
    w_rw = w_in[:, :RW_COLS].astype(BF16)
    w_k = jnp.concatenate([w_at[:, AT_WIDTH:2 * AT_WIDTH], w_at[:, KI_OFF:KI_OFF + HEAD_DIM],
                           jnp.zeros((D, HEAD_DIM), F32)], axis=1).astype(BF16)
    w_t = jnp.concatenate([w_at[:, 0:AT_WIDTH], w_at[:, 2 * AT_WIDTH:3 * AT_WIDTH],
                           w_at[:, 3 * AT_WIDTH:KI_OFF], w_at[:, WI_OFF:WI_OFF + IDX_HEADS],
                           jnp.zeros((D, 8 - IDX_HEADS), F32)], axis=1).T.astype(BF16)
    kg = jnp.tile(k_norm_g, 2).reshape(1, LANES)
    zpad = jnp.zeros((HEAD_DIM,), F32)
    iw = jnp.concatenate([idx_ln_w, zpad]).reshape(1, LANES)
    ib = jnp.concatenate([idx_ln_b, zpad]).reshape(1, LANES)
    qg = q_norm_g.reshape(HEAD_DIM, 1)
    li = np.arange(LANES)
    gsum = jnp.asarray((li[:, None] // HEAD_DIM == li[None, :] // HEAD_DIM).astype(np.float32)).astype(BF16)

    full = lambda shape: pl.BlockSpec(shape, lambda b, i: (0,) * len(shape))
    return pl.pallas_call(
        _inproj_kernel,
        grid=(B, S // tm),
        in_specs=[pl.BlockSpec((1, tm, D), lambda b, i: (b, i, 0)),
                  pl.BlockSpec((1, 1, D), lambda b, i: (b, 0, 0)),
                  pl.BlockSpec((1, 1, D), lambda b, i: (b, 0, 0)),
                  full((1, D)), full((D, RW_COLS)), full((D, PK_COLS)), full((PT_ROWS, D)),
                  full((1, LANES)), full((1, LANES)), full((1, LANES)), full((HEAD_DIM, 1)),
                  full((LANES, LANES)),
                  pl.BlockSpec((1, tm, LANES), lambda b, i: (b, i, 0)),
                  pl.BlockSpec((1, tm, LANES), lambda b, i: (b, i, 0)),
                  pl.BlockSpec((1, HALF, tm), lambda b, i: (b, 0, i)),
                  pl.BlockSpec((1, HALF, tm), lambda b, i: (b, 0, i))],
        out_specs=[pl.BlockSpec((1, tm, RW_COLS), lambda b, i: (b, i, 0)),
                   pl.BlockSpec((1, tm, AT_WIDTH), lambda b, i: (b, i, 0)),
                   pl.BlockSpec((1, tm, LANES), lambda b, i: (b, i, 0)),
                   pl.BlockSpec((1, AT_HEADS, LANES, tm), lambda b, i: (b, 0, 0, i)),
                   pl.BlockSpec((1, tm // QBLK, AT_WIDTH, QBLK), lambda b, i: (b, i, 0, 0)),
                   pl.BlockSpec((1, IDX_HEADS, LANES, tm), lambda b, i: (b, 0, 0, i)),
                   pl.BlockSpec((1, 8, tm), lambda b, i: (b, 0, i))],
        out_shape=[jax.ShapeDtypeStruct((B, S, RW_COLS), F32),
                   jax.ShapeDtypeStruct((B, S, AT_WIDTH), BF16),
                   jax.ShapeDtypeStruct((B, S, LANES), BF16),
                   jax.ShapeDtypeStruct((B, AT_HEADS, LANES, S), BF16),
                   jax.ShapeDtypeStruct((B, S // QBLK, AT_WIDTH, QBLK), BF16),
                   jax.ShapeDtypeStruct((B, IDX_HEADS, LANES, S), BF16),
                   jax.ShapeDtypeStruct((B, 8, S), F32)],
        compiler_params=_cparams(("arbitrary", "arbitrary")),
        name="inproj", interpret=interpret,
    )(x, scale1, shift1, norm1_g.reshape(1, D), w_rw, w_k, w_t, kg, iw, ib, qg, gsum,
      cos_r, sin_r, cos_t, sin_t)


def _rwkv_kernel(p_ref, mu_ref, w0_ref, w2_ref, a0_ref, a2_ref, g2_ref, kk_ref, ka_ref, rk_ref,
                 lnw_ref, lnb_ref, gsum_ref, y_ref, s_ref, prev_ref, yt_ref):
    C = RW_CHUNK
    W = RW_WIDTH
    nb = p_ref.shape[0]

    @pl.when(pl.program_id(1) == 0)
    def _():
        s_ref[...] = jnp.zeros_like(s_ref)
        prev_ref[...] = jnp.zeros_like(prev_ref)

    row = lax.broadcasted_iota(I32, (C, 1), 0)
    gsum = gsum_ref[...]

    def prepare(bi):
        p = p_ref[bi]
        pprev = jnp.where(row == 0, prev_ref[bi:bi + 1, :], pltpu.roll(p, 1, 0))
        prev_ref[bi:bi + 1, :] = p[C - 1:C]
        ps = p + (pprev - p) * mu_ref[...]
        r, k, v = ps[:, 0:W], ps[:, W:2 * W], ps[:, 2 * W:3 * W]
        o = 3 * W
        wl = ps[:, o:o + RW_LORA_W]
        al = ps[:, o + RW_LORA_W:o + RW_LORA_W + RW_LORA_A]
        gl = ps[:, o + RW_LORA_W + RW_LORA_A:]
        z = w0_ref[...] + _dot(jnp.tanh(wl).astype(BF16), w2_ref[...])
        nz = -z
        softplus = jnp.maximum(nz, 0.0) + jnp.log(1.0 + jnp.exp(-jnp.abs(nz)))
        logw = -jnp.exp(-softplus - 0.5)
        a = _sigmoid(a0_ref[...] + _dot(al.astype(BF16), a2_ref[...]))
        g = _dot(_sigmoid(gl).astype(BF16), g2_ref[...])
        kk = k * kk_ref[...]
        ss = _dot((kk * kk).astype(BF16), gsum)
        kk = kk * (1.0 / jnp.maximum(jnp.sqrt(ss), 1e-12))
        k2 = k * (1.0 + (a - 1.0) * ka_ref[...])
        bb = kk * a
        cw = logw
        sh = 1
        while sh < C:
            cw = cw + jnp.where(row >= sh, pltpu.roll(cw, sh, 0), 0.0)
            sh *= 2
        cw_last = cw[C - 1:C]
        e_neg = jnp.exp(-cw)
        e_end = jnp.exp(cw_last - cw)
        return dict(r=r, v=v, k2=k2, g=g, rw=r * jnp.exp(cw), kkp=kk * jnp.exp(cw - logw),
                    bw=bb * e_neg, kw=k2 * e_neg, bend=bb * e_end, kend=k2 * e_end,
                    wc=jnp.exp(cw_last), vt=v.T.astype(BF16))

    pre = [prepare(bi) for bi in range(nb)]

    ri = lax.broadcasted_iota(I32, (C, C), 0)
    ci = lax.broadcasted_iota(I32, (C, C), 1)
    strict = ri < ci
    incl = ri <= ci
    incl2 = jnp.concatenate([incl, incl], axis=0)
    lane_half = lax.broadcasted_iota(I32, (1, LANES), 1) // HEAD_DIM

    chains = [(bi, h) for bi in range(nb) for h in range(RW_HEADS)]
    pair = lambda h: slice((h // 2) * LANES, (h // 2 + 1) * LANES)
    own = lambda h: lane_half == (h % 2)
    stack = lambda bi, top, bot, h: jnp.concatenate(
        [pre[bi][top][:, pair(h)], pre[bi][bot][:, pair(h)]], axis=0)
    lh = {(bi, q): stack(bi, 'kkp', 'rw', 2 * q).astype(BF16)
          for bi in range(nb) for q in range(RW_HEADS // 2)}
    rh = [jnp.where(own(h), stack(bi, 'bw', 'kw', h), 0.0).astype(BF16) for bi, h in chains]
    aat = [_dot_nt(rh[n], lh[bi, h // 2]) for n, (bi, h) in enumerate(chains)]
    s_old = [s_ref[bi * RW_HEADS + h] for bi, h in chains]
    sl = [_dot_nt(s_old[n].astype(BF16), lh[bi, h // 2]) for n, (bi, h) in enumerate(chains)]
    vt = [pre[bi]['vt'][h * HEAD_DIM:(h + 1) * HEAD_DIM] for bi, h in chains]
    ids = range(len(chains))
    akt = [jnp.where(strict, aat[n][C:, :C], 0.0).astype(BF16) for n in ids]
    m = [jnp.where(strict, aat[n][:C, :C], 0.0).astype(BF16) for n in ids]
    xs = [-(sl[n][:, :C] + _dot(vt[n], akt[n])) for n in ids]
    xs = [xs[n] - _dot_split(xs[n], m[n]) for n in ids]
    lvl = 2
    while lvl < C:
        m = [_dot(m[n], m[n]).astype(BF16) for n in ids]
        xs = [xs[n] + _dot_split(xs[n], m[n]) for n in ids]
        lvl *= 2
    zt = [jnp.concatenate([xs[n].astype(BF16), vt[n]], axis=1) for n in ids]
    for n in ids:
        ymat = jnp.where(incl2, aat[n][:, C:], 0.0).astype(BF16)
        yt_ref[n * HEAD_DIM:(n + 1) * HEAD_DIM, :] = sl[n][:, C:] + _dot(zt[n], ymat)
    for n, (bi, h) in enumerate(chains):
        endz = jnp.where(own(h), stack(bi, 'bend', 'kend', h), 0.0).astype(BF16)
        s_ref[n] = s_old[n] * pre[bi]['wc'][:, pair(h)] + _dot(zt[n], endz)

    lnw = lnw_ref[...].reshape(RW_HEADS, HEAD_DIM, 1)
    lnb = lnb_ref[...].reshape(RW_HEADS, HEAD_DIM, 1)
    for bi in range(nb):
        q = pre[bi]
        yt = yt_ref[bi * W:(bi + 1) * W, :].reshape(RW_HEADS, HEAD_DIM, C)
        mean = jnp.mean(yt, axis=1, keepdims=True)
        yc = yt - mean
        var = jnp.mean(yc * yc, axis=1, keepdims=True)
        yn = yc * lax.rsqrt(var + RW_GN_EPS) * lnw + lnb
        y = yn.reshape(W, C).T
        bonus = _dot((q['r'] * q['k2'] * rk_ref[...]).astype(BF16), gsum) * q['v']
        y_ref[bi] = ((y + bonus) * q['g']).astype(BF16)


def _rwkv(p_rw, rw_mu, rw_w0, rw_w2, rw_a0, rw_a2, rw_g2, rw_k_k, rw_k_a, rw_r_k, rw_ln_w, rw_ln_b,
          interpret):
    B, S, _ = p_rw.shape
    C, W = RW_CHUNK, RW_WIDTH
    li = np.arange(W)
    gsum = jnp.asarray((li[:, None] // HEAD_DIM == li[None, :] // HEAD_DIM).astype(np.float32)).astype(BF16)
    row = lambda a: a.reshape(1, -1)
    full = lambda shape: pl.BlockSpec(shape, lambda b, i: (0,) * len(shape))
    nb = RW_BATCH if B % RW_BATCH == 0 else 1
    return pl.pallas_call(
        _rwkv_kernel,
        grid=(B // nb, S // C),
        in_specs=[pl.BlockSpec((nb, C, RW_COLS), lambda b, i: (b, i, 0)),
                  full((1, RW_COLS)), full((1, W)), full((RW_LORA_W, W)), full((1, W)),
                  full((RW_LORA_A, W)), full((RW_LORA_G, W)), full((1, W)), full((1, W)),
                  full((1, W)), full((W, 1)), full((W, 1)), full((W, W))],
        out_specs=pl.BlockSpec((nb, C, W), lambda b, i: (b, i, 0)),
        out_shape=jax.ShapeDtypeStruct((B, S, W), BF16),
        scratch_shapes=[pltpu.VMEM((nb * RW_HEADS, HEAD_DIM, LANES), F32),
                        pltpu.VMEM((nb, RW_COLS), F32),
                        pltpu.VMEM((nb * W, C), F32)],
        compiler_params=_cparams(("arbitrary", "arbitrary")),
        name="rwkv", interpret=interpret,
    )(p_rw, row(rw_mu), row(rw_w0), rw_w2.astype(BF16), row(rw_a0), rw_a2.astype(BF16),
      rw_g2.astype(BF16), row(rw_k_k), row(rw_k_a), row(rw_r_k), rw_ln_w.reshape(W, 1),
      rw_ln_b.reshape(W, 1), gsum)


def _dsa_kernel(topk, nbits, k_ref, vt_ref, ki_ref, qt_ref, qit_ref, wit_ref, og_ref, o_ref,
                key_s, acc_s, m_s, l_s, thr_s):
    j = pl.program_id(1)
    nkb = j + 1
    lane = lax.broadcasted_iota(I32, (QBLK, QBLK), 1)
    sub = lax.broadcasted_iota(I32, (QBLK, QBLK), 0)
    qpos = j * QBLK + lane
    wit = wit_ref[0]

    def score_blocks(i, carry):
        kbs = [i * SCORE_UNROLL + u for u in range(SCORE_UNROLL)]
        kib = [ki_ref[0, pl.ds(pl.multiple_of(kb * QBLK, QBLK), QBLK), :] for kb in kbs]
        lg = [[_dot(kib[u], qit_ref[0, hh]) for hh in range(IDX_HEADS)] for u in range(SCORE_UNROLL)]
        for u, kb in enumerate(kbs):
            s = wit[0:1, :] * jnp.maximum(lg[u][0], 0.0)
            for hh in range(1, IDX_HEADS):
                s = s + wit[hh:hh + 1, :] * jnp.maximum(lg[u][hh], 0.0)
            s = jnp.where(s == 0.0, 0.0, s)
            bits = pltpu.bitcast(s, I32)
            skey = jnp.where(bits < 0, bits ^ 0x7FFFFFFF, bits)
            key_s[kb] = jnp.where(kb * QBLK + sub <= qpos, skey, INT_MIN)
        return carry

    lax.fori_loop(0, pl.cdiv(nkb, SCORE_UNROLL), score_blocks, 0)

    @pl.when(nkb * QBLK <= topk)
    def _():
        thr_s[0:1, :] = jnp.full((1, QBLK), INT_MIN, I32)
        thr_s[1:2, :] = jnp.zeros((1, QBLK), I32)

    @pl.when(nkb * QBLK > topk)
    def _():
        n_done = pl.cdiv(nkb, SCORE_UNROLL) * SCORE_UNROLL
        n_cnt = pl.cdiv(nkb, COUNT_UNROLL)

        def fill(kb, carry):
            key_s[kb] = jnp.full((QBLK, QBLK), INT_MIN, I32)
            return carry

        lax.fori_loop(n_done, n_cnt * COUNT_UNROLL, fill, 0)

        def count(preds):
            def body(i, accs):
                accs = list(accs)
                for u in range(COUNT_UNROLL):
                    kb = i * COUNT_UNROLL + u
                    ky = key_s[kb]
                    for n, pred in enumerate(preds):
                        hit = pred(ky, kb * QBLK + sub).astype(I32)
                        accs[n] = accs[n] + jnp.sum(hit.reshape(QBLK // 8, 8, QBLK), axis=0)
                return tuple(accs)
            accs = lax.fori_loop(0, n_cnt, body, tuple(jnp.zeros((8, QBLK), I32) for _ in preds))
            return [jnp.sum(a, axis=0, keepdims=True) for a in accs]

        c0, = count([lambda ky, ix: ky >= 0])
        t0 = jnp.where(c0 >= topk, 0, INT_MIN).astype(I32)

        def bit_step(i, t):
            cand = t | jnp.left_shift(jnp.int32(1), 30 - i)
            c, = count([lambda ky, ix: ky >= cand])
            return jnp.where(c >= topk, cand, t)

        thr = lax.fori_loop(0, 31, bit_step, t0)
        n_gt, n_eq = count([lambda ky, ix: ky > thr, lambda ky, ix: ky == thr])
        need = topk - n_gt
        thr_s[0:1, :] = thr
        thr_s[1:2, :] = jnp.full((1, QBLK), 2 ** nbits, I32)

        @pl.when(jnp.max(jnp.abs(n_eq - need)) > 0)
        def _():
            def idx_step(i, mm):
                cand = mm | jnp.left_shift(jnp.int32(1), nbits - 1 - i)
                c, = count([lambda ky, ix: (ky == thr) & (ix < cand)])
                return jnp.where(c < need, cand, mm)

            thr_s[1:2, :] = lax.fori_loop(0, nbits, idx_step, jnp.zeros((1, QBLK), I32))

    thr = thr_s[0:1, :]
    mm = thr_s[1:2, :]
    m_s[...] = jnp.full_like(m_s, NEG_BIG)
    l_s[...] = jnp.zeros_like(l_s)
    acc_s[...] = jnp.zeros_like(acc_s)

    qt2 = [jnp.concatenate([qt_ref[0, 2 * q], qt_ref[0, 2 * q + 1]], axis=1)
           for q in range(AT_HEADS // 2)]

    def attn_blocks(i, carry):
        kbs = [i * ATTN_UNROLL + u for u in range(ATTN_UNROLL)]
        sel = []
        for kb in kbs:
            skey = key_s[kb]
            kidx = kb * QBLK + sub
            sel.append((kidx <= qpos) & ((skey > thr) | ((skey == thr) & (kidx <= mm))))
        sel = jnp.concatenate(sel, axis=0)
        kblk = [k_ref[0, pl.ds(pl.multiple_of(kb * QBLK, QBLK), QBLK), :] for kb in kbs]
        vtb = jnp.concatenate([vt_ref[0, kb] for kb in kbs], axis=1)
        s2 = [[_dot(kblk[u][:, q * LANES:(q + 1) * LANES], qt2[q]) for u in range(ATTN_UNROLL)]
              for q in range(AT_HEADS // 2)]
        pexp, alpha = [], []
        for hh in range(AT_HEADS):
            half = slice((hh % 2) * QBLK, (hh % 2 + 1) * QBLK)
            s = jnp.concatenate([s2[hh // 2][u][:, half] for u in range(ATTN_UNROLL)], axis=0)
            s = jnp.where(sel, s, NEG_BIG)
            m_old = m_s[hh:hh + 1, :]
            m_new = jnp.maximum(m_old, jnp.max(s, axis=0, keepdims=True))
            pe = jnp.exp2(s - m_new)
            al = jnp.exp2(m_old - m_new)
            l_s[hh:hh + 1, :] = al * l_s[hh:hh + 1, :] + jnp.sum(pe, axis=0, keepdims=True)
            m_s[hh:hh + 1, :] = m_new
            pexp.append(pe.astype(BF16))
            alpha.append(al)
        for hh in range(AT_HEADS):
            hs = slice(hh * HEAD_DIM, (hh + 1) * HEAD_DIM)
            acc_s[hs, :] = alpha[hh] * acc_s[hs, :] + _dot(vtb[hs, :], pexp[hh])
        return carry

    lax.fori_loop(0, pl.cdiv(nkb, ATTN_UNROLL), attn_blocks, 0)

    for hh in range(AT_HEADS):
        hs = slice(hh * HEAD_DIM, (hh + 1) * HEAD_DIM)
        oh = acc_s[hs, :] * (1.0 / l_s[hh:hh + 1, :])
        ms = jnp.mean(oh * oh, axis=0, keepdims=True)
        acc_s[hs, :] = oh * lax.rsqrt(ms + NORM_EPS) * og_ref[hs, :]
    o_ref[0] = acc_s[...].T.astype(BF16)


def _dsa(k, vt, ki, qt, qit, wit, at_out_g, interpret):
    B, S, _ = k.shape
    nq = S // QBLK
    topk = min(IDX_TOPK_MAX, S // 4)
    nbits = int(np.log2(S))
    assert 2 ** nbits == S and nq % COUNT_UNROLL == 0 and SCORE_UNROLL == ATTN_UNROLL
    assert COUNT_UNROLL % SCORE_UNROLL == 0
    return pl.pallas_call(
        functools.partial(_dsa_kernel, topk, nbits),
        grid=(B, nq),
        in_specs=[pl.BlockSpec((1, S, AT_WIDTH), lambda b, j: (b, 0, 0)),
                  pl.BlockSpec((1, nq, AT_WIDTH, QBLK), lambda b, j: (b, 0, 0, 0)),
                  pl.BlockSpec((1, S, LANES), lambda b, j: (b, 0, 0)),
                  pl.BlockSpec((1, AT_HEADS, LANES, QBLK), lambda b, j: (b, 0, 0, j)),
                  pl.BlockSpec((1, IDX_HEADS, LANES, QBLK), lambda b, j: (b, 0, 0, j)),
                  pl.BlockSpec((1, 8, QBLK), lambda b, j: (b, 0, j)),
                  pl.BlockSpec((AT_WIDTH, 1), lambda b, j: (0, 0))],
        out_specs=pl.BlockSpec((1, QBLK, AT_WIDTH), lambda b, j: (b, j, 0)),
        out_shape=jax.ShapeDtypeStruct((B, S, AT_WIDTH), BF16),
        scratch_shapes=[pltpu.VMEM((nq, QBLK, QBLK), I32),
                        pltpu.VMEM((AT_WIDTH, QBLK), F32),
                        pltpu.VMEM((AT_HEADS, QBLK), F32),
                        pltpu.VMEM((AT_HEADS, QBLK), F32),
                        pltpu.VMEM((8, QBLK), I32)],
        compiler_params=_cparams(("arbitrary", "arbitrary")),
        name="dsa", interpret=interpret,
    )(k, vt, ki, qt, qit, wit, at_out_g.reshape(AT_WIDTH, 1))


def _first_max(vals, idx, axis, sentinel):
    m = jnp.max(vals, axis=axis, keepdims=True)
    return m, jnp.min(jnp.where(vals == m, idx, sentinel), axis=axis, keepdims=True)


def _route_cols(logits_t, bias_col):
    E, tm = logits_t.shape
    pg = E // N_GROUPS
    scores = _sigmoid(logits_t)
    biased = scores + bias_col
    b3 = biased.reshape(N_GROUPS, pg, tm)
    r3 = lax.broadcasted_iota(I32, (N_GROUPS, pg, tm), 1)
    m1, first = _first_max(b3, r3, 1, pg)
    m2 = jnp.max(jnp.where(r3 == first, -jnp.inf, b3), axis=1, keepdims=True)
    cur = (m1 + m2).reshape(N_GROUPS, tm)
    grow = lax.broadcasted_iota(I32, (N_GROUPS, tm), 0)
    gsel = jnp.zeros((N_GROUPS, tm), F32)
    for _ in range(TOPK_GROUPS):
        _, gi = _first_max(cur, grow, 0, N_GROUPS)
        hit = grow == gi
        gsel = jnp.where(hit, 1.0, gsel)
        cur = jnp.where(hit, -jnp.inf, cur)
    gmask = jnp.broadcast_to(gsel.reshape(N_GROUPS, 1, tm), (N_GROUPS, pg, tm)).reshape(E, tm)
    cur = jnp.where(gmask > 0.0, biased, -jnp.inf)
    row = lax.broadcasted_iota(I32, (E, tm), 0)
    onehot = jnp.zeros((E, tm), F32)
    eids, gws = [], []
    for _ in range(TOP_K):
        _, ei = _first_max(cur, row, 0, E)
        hit = row == ei
        eids.append(ei)
        gws.append(jnp.sum(jnp.where(hit, scores, 0.0), axis=0, keepdims=True))
        onehot = jnp.where(hit, 1.0, onehot)
        cur = jnp.where(hit, -jnp.inf, cur)
    eid = jnp.concatenate(eids, axis=0)
    gw = jnp.concatenate(gws, axis=0)
    gw = gw * (ROUTED_SCALE / jnp.sum(gw, axis=0, keepdims=True))
    return eid, gw, onehot


def _post_kernel(x_ref, yrw_ref, yat_ref, g1_ref, sc_ref, sh_ref, g2_ref, ng_ref, wo_ref, rwh_ref,
                 rwl_ref, rb_ref, s1_ref, s3_ref, s2_ref, base_ref, h2_ref, eid_ref, gw_ref, rank_ref,
                 cnt_ref):
    W = RW_WIDTH
    tm = x_ref.shape[1]
    E = rwh_ref.shape[0]

    @pl.when((pl.program_id(0) == 0) & (pl.program_id(1) == 0))
    def _():
        cnt_ref[...] = jnp.zeros_like(cnt_ref)

    mix = _dot(yrw_ref[0], wo_ref[0:W, :]) + _dot(yat_ref[0], wo_ref[W:, :])
    x1 = x_ref[0] + g1_ref[0] * mix
    ms = jnp.mean(x1 * x1, axis=-1, keepdims=True)
    h2 = x1 * lax.rsqrt(ms + NORM_EPS) * ng_ref[...] * (1.0 + sc_ref[0]) + sh_ref[0]
    hb = h2.astype(BF16)
    h2_ref[0, :, 0, :] = _pack_halves(h2)
    act =(_silu(_dot(hb, s1_ref[...])) * _dot(hb, s3_ref[...])).astype(BF16)
    base_ref[0] = x1 + g2_ref[0] * _dot(act, s2_ref[...])

    h_lo = (h2 - hb.astype(F32)).astype(BF16)
    logits_t = (_dot_nt(rwh_ref[...], hb) + _dot_nt(rwh_ref[...], h_lo)
                + _dot_nt(rwl_ref[...], hb))
    eid, gw, onehot = _route_cols(logits_t, rb_ref[...])
    eid_ref[0] = eid
    gw_ref[0] = gw
    ti = lax.broadcasted_iota(I32, (tm, tm), 0)
    tj = lax.broadcasted_iota(I32, (tm, tm), 1)
    before = _dot(onehot.astype(BF16), (ti < tj).astype(BF16)) + cnt_ref[:, 0:1]
    row = lax.broadcasted_iota(I32, (E, tm), 0)
    ranks = [jnp.sum(jnp.where(row == eid[kk:kk + 1, :], before, 0.0), axis=0, keepdims=True)
             for kk in range(TOP_K)]
    rank_ref[0] = jnp.concatenate(ranks, axis=0).astype(I32)
    cnt_ref[...] = cnt_ref[...] + jnp.sum(onehot, axis=1, keepdims=True)


def _post(x, y_rw, y_at, gate1, scale2, shift2, gate2, norm2_g, w_out, router_w, router_bias,
          sw1, sw3, sw2, interpret):
    B, S, D = x.shape
    tm = min(S, 512)
    E = router_w.shape[1]
    sd = sw1.shape[1]
    full = lambda shape: pl.BlockSpec(shape, lambda b, i: (0,) * len(shape))
    tok = lambda w: pl.BlockSpec((1, tm, w), lambda b, i: (b, i, 0))
    per_b = pl.BlockSpec((1, 1, D), lambda b, i: (b, 0, 0))
    col8 = pl.BlockSpec((1, TOP_K, tm), lambda b, i: (b, 0, i))
    rw_t = router_w.T
    rw_hi = rw_t.astype(BF16)
    rw_lo = (rw_t - rw_hi.astype(F32)).astype(BF16)
    return pl.pallas_call(
        _post_kernel,
        grid=(B, S // tm),
        in_specs=[tok(D), tok(RW_WIDTH), tok(AT_WIDTH), per_b, per_b, per_b, per_b, full((1, D)),
                  full((D, D)), full((E, D)), full((E, D)), full((E, 1)), full((D, sd)), full((D, sd)),
                  full((sd, D))],
        out_specs=[tok(D), pl.BlockSpec((1, tm, 1, D // 2), lambda b, i: (b, i, 0, 0)),
                   col8, col8, col8, full((E, LANES))],
        out_shape=[jax.ShapeDtypeStruct((B, S, D), F32),
                   jax.ShapeDtypeStruct((B, S, 1, D // 2), I32),
                   jax.ShapeDtypeStruct((B, TOP_K, S), I32),
                   jax.ShapeDtypeStruct((B, TOP_K, S), F32),
                   jax.ShapeDtypeStruct((B, TOP_K, S), I32),
                   jax.ShapeDtypeStruct((E, LANES), F32)],
        compiler_params=_cparams(("arbitrary", "arbitrary")),
        name="post", interpret=interpret,
    )(x, y_rw, y_at, gate1, scale2, shift2, gate2, norm2_g.reshape(1, D), w_out.astype(BF16),
      rw_hi, rw_lo, router_bias.reshape(E, 1), sw1.astype(BF16), sw3.astype(BF16), sw2.astype(BF16))


def _expert_kernel(be_ref, nv_ref, nu_ref, xs_ref, w1_ref, w3_ref, w2_ref, o_ref, w13_s, w2_s):
    i = pl.program_id(0)
    blk, _, hw = xs_ref.shape
    F = w1_ref.shape[2]
    used = i < nu_ref[0]

    @pl.when(used & ((i == 0) | (be_ref[i] != be_ref[jnp.maximum(i - 1, 0)])))
    def _():
        w13_s[:, :F] = w1_ref[0].astype(BF16)
        w13_s[:, F:] = w3_ref[0].astype(BF16)
        w2_s[...] = w2_ref[0].astype(BF16)

    @pl.when(used)
    def _():
        live = lax.broadcasted_iota(I32, (blk, 1), 0) < nv_ref[i]
        x_lo, x_hi = _unpack_halves(jnp.where(live, xs_ref[:, 0, :], 0))
        h13 =(_dot(x_lo.astype(BF16), w13_s[:hw, :]) + _dot(x_hi.astype(BF16), w13_s[hw:, :]))
        act = (_silu(h13[:, :F]) * h13[:, F:]).astype(BF16)
        o_ref[:, 0, :] = _pack_halves(_dot(act, w2_s[...]))


def _experts(xs, block_e, block_rows, n_used, w1, w3, w2, interpret):
    P, _, hw = xs.shape
    E, D, F = w1.shape
    nb = P // EXP_BLK
    blk = lambda i, nu: jnp.minimum(i, nu[0] - 1)
    grid_spec = pltpu.PrefetchScalarGridSpec(
        num_scalar_prefetch=3,
        grid=(nb,),
        in_specs=[pl.BlockSpec((EXP_BLK, 1, hw), lambda i, be, nv, nu: (blk(i, nu), 0, 0)),
                  pl.BlockSpec((1, D, F), lambda i, be, nv, nu: (be[blk(i, nu)], 0, 0)),
                  pl.BlockSpec((1, D, F), lambda i, be, nv, nu: (be[blk(i, nu)], 0, 0)),
                  pl.BlockSpec((1, F, D), lambda i, be, nv, nu: (be[blk(i, nu)], 0, 0))],
        out_specs=pl.BlockSpec((EXP_BLK, 1, hw), lambda i, be, nv, nu: (blk(i, nu), 0, 0)),
        scratch_shapes=[pltpu.VMEM((D, 2 * F), BF16), pltpu.VMEM((F, D), BF16)],
    )
    return pl.pallas_call(
        _expert_kernel,
        grid_spec=grid_spec,
        out_shape=jax.ShapeDtypeStruct((P, 1, hw), I32),
        compiler_params=_cparams(("arbitrary",)),
        name="experts", interpret=interpret,
    )(block_e, block_rows, n_used, xs, w1, w3, w2)


def _row_out(tile, t, slots, slot, sem):
    return pltpu.make_async_copy(tile.at[t], slots.at[slot], sem)


def _row_in(slots, slot, tile, t, sem):
    return pltpu.make_async_copy(slots.at[slot], tile.at[pl.ds(t, 1)], sem)


def _dispatch_kernel(dest_ref, h2_ref, xs_ref, sem):
    td = h2_ref.shape[0]

    def issue(t, carry):
        for kk in range(TOP_K):
            _row_out(h2_ref, t, xs_ref, dest_ref[0, 0, kk * td + t], sem).start(priority=kk % 2)
        return carry

    def drain(t, carry):
        for kk in range(TOP_K):
            _row_out(h2_ref, t, xs_ref, dest_ref[0, 0, kk * td + t], sem).wait()
        return carry

    lax.fori_loop(0, td, issue, 0)
    lax.fori_loop(0, td, drain, 0)


def _dispatch(h2, dest_tiles, n_slots, interpret):
    T, _, D = h2.shape
    nt, _, n = dest_tiles.shape
    td = n // TOP_K
    return pl.pallas_call(
        _dispatch_kernel,
        grid=(nt,),
        in_specs=[pl.BlockSpec((1, 1, n), lambda i: (i, 0, 0), memory_space=pltpu.SMEM),
                  pl.BlockSpec((td, 1, D), lambda i: (i, 0, 0))],
        out_specs=pl.BlockSpec(memory_space=pl.ANY),
        out_shape=jax.ShapeDtypeStruct((n_slots, 1, D), h2.dtype),
        scratch_shapes=[pltpu.SemaphoreType.DMA(())],
        compiler_params=_cparams(("arbitrary",)),
        name="dispatch", interpret=interpret,
    )(dest_tiles, h2)


def _combine_kernel(dest_ref, base_ref, g2_ref, gw_ref, ys_ref, o_ref, buf, sem):
    td = base_ref.shape[0]

    def issue(t, carry):
        for kk in range(TOP_K):
            _row_in(ys_ref, dest_ref[0, 0, kk * td + t], buf.at[kk], t, sem).start(priority=kk % 2)
        return carry

    def drain(t, carry):
        for kk in range(TOP_K):
            _row_in(ys_ref, dest_ref[0, 0, kk * td + t], buf.at[kk], t, sem).wait()
        return carry

    lax.fori_loop(0, td, issue, 0)
    lax.fori_loop(0, td, drain, 0)
    gw = gw_ref[...]
    acc_lo, acc_hi = _unpack_halves(buf[0])
    acc_lo, acc_hi = gw[:, 0:1] * acc_lo, gw[:, 0:1] * acc_hi
    for kk in range(1, TOP_K):
        y_lo, y_hi = _unpack_halves(buf[kk])
        acc_lo = acc_lo + gw[:, kk:kk + 1] * y_lo
        acc_hi = acc_hi + gw[:, kk:kk + 1] * y_hi
    o_ref[...] = base_ref[...] + g2_ref[0] * jnp.concatenate([acc_lo, acc_hi], axis=1)


def _combine(base, gate2, gw_tok, ys, dest_tiles, tiles_per_batch, interpret):
    T, D = base.shape
    nt, _, n = dest_tiles.shape
    td = n // TOP_K
    return pl.pallas_call(
        _combine_kernel,
        grid=(nt,),
        in_specs=[pl.BlockSpec((1, 1, n), lambda i: (i, 0, 0), memory_space=pltpu.SMEM),
                  pl.BlockSpec((td, D), lambda i: (i, 0)),
                  pl.BlockSpec((1, 1, D), lambda i: (i // tiles_per_batch, 0, 0)),
                  pl.BlockSpec((td, TOP_K), lambda i: (i, 0)),
                  pl.BlockSpec(memory_space=pl.ANY)],
        out_specs=pl.BlockSpec((td, D), lambda i: (i, 0)),
        out_shape=jax.ShapeDtypeStruct((T, D), F32),
        scratch_shapes=[pltpu.VMEM((TOP_K, td, D // 2), I32), pltpu.SemaphoreType.DMA(())],
        compiler_params=_cparams(("arbitrary",)),
        name="combine", interpret=interpret,
    )(dest_tiles, base, gate2, gw_tok, ys)


def _slots_kernel(eid_ref, rank_ref, pstart_ref, dest_ref):
    td = eid_ref.shape[2]
    E = pstart_ref.shape[0]
    row = lax.broadcasted_iota(I32, (E, td), 0)
    pstart = pstart_ref[...]
    eid = eid_ref[0]
    for kk in range(TOP_K):
        base = jnp.sum(jnp.where(row == eid[kk:kk + 1, :], pstart, 0), axis=0, keepdims=True)
        dest_ref[0, :, kk * td:(kk + 1) * td] = base + rank_ref[0, kk:kk + 1, :]


def _slot_plan(counts, eid_t, rank_t, td, interpret):
    B, _, S = eid_t.shape
    E = counts.shape[0]
    padded = (counts + EXP_BLK - 1) // EXP_BLK * EXP_BLK
    pend = jnp.cumsum(padded)
    pstart = (pend - padded).astype(I32)
    nb = -(-(B * S * TOP_K + E * (EXP_BLK - 1)) // EXP_BLK)
    first_row = jnp.arange(nb, dtype=I32) * EXP_BLK
    block_e = jnp.sum(pend[None, :] <= first_row[:, None], axis=1)
    block_e = jnp.minimum(block_e, E - 1).astype(I32)
    block_rows = jnp.clip(pstart[block_e] + counts[block_e] - first_row, 0, EXP_BLK).astype(I32)
    nt = S // td
    dest_tiles = pl.pallas_call(
        _slots_kernel,
        grid=(B, nt),
        in_specs=[pl.BlockSpec((1, TOP_K, td), lambda b, i: (b, 0, i)),
                  pl.BlockSpec((1, TOP_K, td), lambda b, i: (b, 0, i)),
                  pl.BlockSpec((E, 1), lambda b, i: (0, 0))],
        out_specs=pl.BlockSpec((1, 1, TOP_K * td), lambda b, i: (b * nt + i, 0, 0)),
        out_shape=jax.ShapeDtypeStruct((B * nt, 1, TOP_K * td), I32),
        compiler_params=_cparams(("arbitrary", "arbitrary")),
        name="slots", interpret=interpret,
    )(eid_t, rank_t, pstart.reshape(E, 1))
    n_used = (pend[-1:] // EXP_BLK).astype(I32)
    return block_e, block_rows, n_used, dest_tiles, nb * EXP_BLK


def _forward(x, c, positions, w_ada, b_ada, norm1_g, norm2_g, w_in, rw_mu, rw_w0, rw_w2,
             rw_a0, rw_a2, rw_g2, rw_k_k, rw_k_a, rw_r_k, rw_ln_w, rw_ln_b, q_norm_g,
             k_norm_g, idx_ln_w, idx_ln_b, at_out_g, w_out, router_w, router_bias,
             exp_w1, exp_w3, exp_w2, shared_w1, shared_w3, shared_w2, interpret=False):
    B, S, D = x.shape
    depth = w_ada.shape[0]
    for l in range(depth):
        mod = _mod(c, w_ada[l], b_ada[l], interpret)
        shift1, scale1, gate1, shift2, scale2, gate2 = [
            m.reshape(B, 1, D) for m in jnp.split(mod, 6, axis=-1)]
        tabs = _rope_tables(positions, interpret)
        p_rw, k, ki, qt, vt, qit, wit = _inproj(
            x, scale1, shift1, norm1_g[l], w_in[l], k_norm_g[l], idx_ln_w[l], idx_ln_b[l],
            q_norm_g[l], tabs, interpret)
        y_rw = _rwkv(p_rw, rw_mu[l], rw_w0[l], rw_w2[l], rw_a0[l], rw_a2[l], rw_g2[l], rw_k_k[l],
                     rw_k_a[l], rw_r_k[l], rw_ln_w[l], rw_ln_b[l], interpret)
        y_at = _dsa(k, vt, ki, qt, qit, wit, at_out_g[l], interpret)
        base, h2, eid_t, gw_t, rank_t, cnt = _post(
            x, y_rw, y_at, gate1, scale2, shift2, gate2, norm2_g[l], w_out[l], router_w[l],
            router_bias[l], shared_w1[l], shared_w3[l], shared_w2[l], interpret)
        T = B * S
        td = min(S, ROW_TILE)
        block_e, block_rows, n_used, dest_tiles, n_slots = _slot_plan(
            cnt[:, 0].astype(I32), eid_t, rank_t, td, interpret)
        xs = _dispatch(h2.reshape(T, 1, D // 2), dest_tiles, n_slots, interpret)
        ys = _experts(xs, block_e, block_rows, n_used, exp_w1[l], exp_w3[l], exp_w2[l], interpret)
        gw_tok = gw_t.transpose(0, 2, 1).reshape(T, TOP_K)
        x = _combine(base.reshape(T, D), gate2, gw_tok, ys, dest_tiles, S // td,
                     interpret).reshape(B, S, D)
    return x


def kernel(x, c, positions, w_ada, b_ada, norm1_g, norm2_g, w_in, rw_mu, rw_w0, rw_w2, rw_a0, rw_a2, rw_g2, rw_k_k, rw_k_a, rw_r_k, rw_ln_w, rw_ln_b, q_norm_g, k_norm_g, idx_ln_w, idx_ln_b, at_out_g, w_out, router_w, router_bias, exp_w1, exp_w3, exp_w2, shared_w1, shared_w3, shared_w2):
    return _forward(x, c, positions, w_ada, b_ada, norm1_g, norm2_g, w_in, rw_mu, rw_w0, rw_w2,
                    rw_a0, rw_a2, rw_g2, rw_k_k, rw_k_a, rw_r_k, rw_ln_w, rw_ln_b, q_norm_g,
                    k_norm_g, idx_ln_w, idx_ln_b, at_out_g, w_out, router_w, router_bias,
                    exp_w1, exp_w3, exp_w2, shared_w1, shared_w3, shared_w2)
```

```python
import functools

import jax
import jax.numpy as jnp
import numpy as np
from jax import lax
from jax.experimental import pallas as pl
from jax.experimental.pallas import tpu as pltpu

F32 = jnp.float32
BF16 = jnp.bfloat16
I32 = jnp.int32
HIGHEST = lax.Precision.HIGHEST

LANES = 128
HEAD_DIM = 64
HALF = HEAD_DIM // 2
RW_HEADS = 8
RW_WIDTH = RW_HEADS * HEAD_DIM
AT_HEADS = 8
AT_WIDTH = AT_HEADS * HEAD_DIM
IDX_HEADS = 4
RW_LORA_W, RW_LORA_A, RW_LORA_G = 64, 64, 128
RW_COLS = 3 * RW_WIDTH + RW_LORA_W + RW_LORA_A + RW_LORA_G
KI_OFF = 3 * AT_WIDTH + IDX_HEADS * HEAD_DIM
WI_OFF = KI_OFF + HEAD_DIM
PT_ROWS = 2 * AT_WIDTH + IDX_HEADS * HEAD_DIM + 8
PK_COLS = AT_WIDTH + LANES
ROPE_THETA = 10000.0
NORM_EPS = 1e-6
LN_EPS = 1e-6
RW_GN_EPS = 64e-5
IDX_TOPK_MAX = 256
N_EXPERTS = 256
TOP_K = 8
N_GROUPS = 8
TOPK_GROUPS = 4
ROUTED_SCALE = 2.5
INT_MIN = -2 ** 31
PAIR_HI_MASK = -65536
NEG_BIG = -1e30

RW_CHUNK = 128
RW_BATCH = 4
QBLK = 128
SCORE_UNROLL = 4
ATTN_UNROLL = 4
COUNT_UNROLL = 4
EXP_BLK = 512
ROW_TILE = 256
LOG2_E = 1.4426950408889634
VMEM_LIMIT = 48 * 1024 * 1024


def _cparams(sem):
    return pltpu.CompilerParams(dimension_semantics=sem, vmem_limit_bytes=VMEM_LIMIT)


def _sigmoid(x):
    return 1.0 / (1.0 + jnp.exp(-x))


def _silu(x):
    return x * _sigmoid(x)


def _dot(a, b):
    return jnp.dot(a, b, preferred_element_type=F32)


def _dot_split(a, b):
    hi = a.astype(BF16)
    lo = (a - hi.astype(F32)).astype(BF16)
    return _dot(hi, b) + _dot(lo, b)


def _pack_halves(x):
    w = x.shape[1] // 2
    lo = pltpu.bitcast(x[:, :w].astype(BF16).astype(F32), I32)
    hi = pltpu.bitcast(x[:, w:].astype(BF16).astype(F32), I32)
    return (hi & PAIR_HI_MASK) | lax.shift_right_logical(lo, 16)


def _unpack_halves(p):
    return pltpu.bitcast(p << 16, F32), pltpu.bitcast(p & PAIR_HI_MASK, F32)


def _dot_nt(a, b):
    return lax.dot_general(a, b, (((1,), (1,)), ((), ())), preferred_element_type=F32)


def _mod_kernel(c_ref, w_ref, b_ref, o_ref):
    c = c_ref[...]
    o_ref[...] = jnp.dot(_silu(c), w_ref[...], precision=HIGHEST,
                         preferred_element_type=F32) + b_ref[...]


def _mod(c, w_ada, b_ada, interpret):
    B, D = c.shape
    n = w_ada.shape[1] // D
    return pl.pallas_call(
        _mod_kernel,
        grid=(n,),
        in_specs=[pl.BlockSpec((B, D), lambda i: (0, 0)),
                  pl.BlockSpec((D, D), lambda i: (0, i)),
                  pl.BlockSpec((1, D), lambda i: (0, i))],
        out_specs=pl.BlockSpec((B, D), lambda i: (0, i)),
        out_shape=jax.ShapeDtypeStruct((B, n * D), F32),
        compiler_params=_cparams(("arbitrary",)),
        name="mod", interpret=interpret,
    )(c, w_ada, b_ada.reshape(1, -1))


def _rope_tab_kernel(pc_ref, pr_ref, cr_ref, sr_ref, ct_ref, st_ref):
    log_theta = float(np.log(ROPE_THETA))
    lane = lax.broadcasted_iota(I32, (1, LANES), 1)
    inv_r = jnp.exp((lane % HALF).astype(F32) * (-log_theta / HALF))
    ang = pc_ref[0].astype(F32) * inv_r
    cr_ref[0] = jnp.cos(ang)
    sr_ref[0] = jnp.where((lane % HEAD_DIM) < HALF, -jnp.sin(ang), jnp.sin(ang))
    sub = lax.broadcasted_iota(I32, (HALF, 1), 0)
    inv_c = jnp.exp(sub.astype(F32) * (-log_theta / HALF))
    ang_t = inv_c * pr_ref[0].astype(F32)
    ct_ref[0] = jnp.cos(ang_t)
    st_ref[0] = jnp.sin(ang_t)


def _rope_tables(positions, interpret):
    B, S = positions.shape
    ts = min(S, 512)
    return pl.pallas_call(
        _rope_tab_kernel,
        grid=(B, S // ts),
        in_specs=[pl.BlockSpec((1, ts, 1), lambda b, i: (b, i, 0)),
                  pl.BlockSpec((1, 1, ts), lambda b, i: (b, 0, i))],
        out_specs=[pl.BlockSpec((1, ts, LANES), lambda b, i: (b, i, 0)),
                   pl.BlockSpec((1, ts, LANES), lambda b, i: (b, i, 0)),
                   pl.BlockSpec((1, HALF, ts), lambda b, i: (b, 0, i)),
                   pl.BlockSpec((1, HALF, ts), lambda b, i: (b, 0, i))],
        out_shape=[jax.ShapeDtypeStruct((B, S, LANES), F32),
                   jax.ShapeDtypeStruct((B, S, LANES), F32),
                   jax.ShapeDtypeStruct((B, HALF, S), F32),
                   jax.ShapeDtypeStruct((B, HALF, S), F32)],
        compiler_params=_cparams(("arbitrary", "arbitrary")),
        name="rope_tab", interpret=interpret,
    )(positions.reshape(B, S, 1), positions.reshape(B, 1, S))


def _rope_rows(y, cos, sin_signed):
    lane = lax.broadcasted_iota(I32, (1, LANES), 1)
    partner = jnp.where((lane % HEAD_DIM) < HALF,
                        pltpu.roll(y, LANES - HALF, 1), pltpu.roll(y, HALF, 1))
    return y * cos + partner * sin_signed


def _rope_cols(y, cos_t, sin_t):
    x1, x2 = y[:, :HALF], y[:, HALF:]
    return jnp.concatenate([x1 * cos_t - x2 * sin_t, x2 * cos_t + x1 * sin_t], axis=1)


def _inproj_kernel(x_ref, sc_ref, sh_ref, g_ref, wrw_ref, wk_ref, wt_ref, kg_ref, iw_ref, ib_ref,
                   qg_ref, gsum_ref, cr_ref, sr_ref, ct_ref, st_ref,
                   prw_ref, k_ref, ki_ref, qt_ref, vt_ref, qit_ref, wit_ref):
    tm = x_ref.shape[1]
    x = x_ref[0]
    ms = jnp.mean(x * x, axis=-1, keepdims=True)
    h = x * lax.rsqrt(ms + NORM_EPS) * g_ref[...] * (1.0 + sc_ref[0]) + sh_ref[0]
    hb = h.astype(BF16)
    prw_ref[0] = _dot(hb, wrw_ref[...])
    pk = _dot(hb, wk_ref[...])
    pt = _dot_nt(wt_ref[...], hb)

    cos_r, sin_r = cr_ref[0], sr_ref[0]
    gsum = gsum_ref[...]
    inv_hd = 1.0 / HEAD_DIM
    for p in range(AT_WIDTH // LANES):
        xk = pk[:, p * LANES:(p + 1) * LANES]
        ss = _dot_split(xk * xk, gsum)
        y = xk * lax.rsqrt(ss * inv_hd + NORM_EPS) * kg_ref[...]
        k_ref[0, :, p * LANES:(p + 1) * LANES] = _rope_rows(y, cos_r, sin_r).astype(BF16)
    xi = pk[:, AT_WIDTH:AT_WIDTH + LANES]
    mu = _dot_split(xi, gsum) * inv_hd
    xc = xi - mu
    var = _dot_split(xc * xc, gsum) * inv_hd
    yi = xc * lax.rsqrt(var + LN_EPS) * iw_ref[...] + ib_ref[...]
    ki_ref[0] = _rope_rows(yi, cos_r, sin_r).astype(BF16)

    cos_t, sin_t = ct_ref[0][None], st_ref[0][None]
    xq = pt[0:AT_WIDTH].reshape(AT_HEADS, HEAD_DIM, tm)
    msq = jnp.mean(xq * xq, axis=1, keepdims=True)
    yq = xq * lax.rsqrt(msq + NORM_EPS) * qg_ref[...][None]
    yq = _rope_cols(yq, cos_t, sin_t) * (HEAD_DIM ** -0.5 * LOG2_E)
    zq = jnp.zeros((HEAD_DIM, tm), BF16)
    for hh in range(AT_HEADS):
        parts = [yq[hh].astype(BF16), zq] if hh % 2 == 0 else [zq, yq[hh].astype(BF16)]
        qt_ref[0, hh] = jnp.concatenate(parts, axis=0)
    vt = pt[AT_WIDTH:2 * AT_WIDTH].astype(BF16)
    for cblk in range(tm // QBLK):
        vt_ref[0, cblk] = vt[:, cblk * QBLK:(cblk + 1) * QBLK]
    xqi = pt[2 * AT_WIDTH:2 * AT_WIDTH + IDX_HEADS * HEAD_DIM].reshape(IDX_HEADS, HEAD_DIM, tm)
    yqi = _rope_cols(xqi, cos_t, sin_t)
    for hh in range(IDX_HEADS):
        qit_ref[0, hh] = jnp.concatenate([yqi[hh].astype(BF16), zq], axis=0)
    wit_ref[0] = pt[PT_ROWS - 8:PT_ROWS] * (IDX_HEADS ** -0.5 * HEAD_DIM ** -0.5)


def _inproj(x, scale1, shift1, norm1_g, w_in, k_norm_g, idx_ln_w, idx_ln_b, q_norm_g,
            tabs, interpret):
    B, S, D = x.shape
    tm = min(S, 512)
    cos_r, sin_r, cos_t, sin_t = tabs
    w_at = w_in[:, RW_COLS:]
    w_rw = w_in[:, :RW_COLS].astype(BF16)
    w_k = jnp.concatenate([w_at[:, AT_WIDTH:2 * AT_WIDTH], w_at[:, KI_OFF:KI_OFF + HEAD_DIM],
                           jnp.zeros((D, HEAD_DIM), F32)], axis=1).astype(BF16)
    w_t = jnp.concatenate([w_at[:, 0:AT_WIDTH], w_at[:, 2 * AT_WIDTH:3 * AT_WIDTH],
                           w_at[:, 3 * AT_WIDTH:KI_OFF], w_at[:, WI_OFF:WI_OFF + IDX_HEADS],
                           jnp.zeros((D, 8 - IDX_HEADS), F32)], axis=1).T.astype(BF16)
    kg = jnp.tile(k_norm_g, 2).reshape(1, LANES)
    zpad = jnp.zeros((HEAD_DIM,), F32)
    iw = jnp.concatenate([idx_ln_w, zpad]).reshape(1, LANES)
    ib = jnp.concatenate([idx_ln_b, zpad]).reshape(1, LANES)
    qg = q_norm_g.reshape(HEAD_DIM, 1)
    li = np.arange(LANES)
    gsum = jnp.asarray((li[:, None] // HEAD_DIM == li[None, :] // HEAD_DIM).astype(np.float32)).astype(BF16)

    full = lambda shape: pl.BlockSpec(shape, lambda b, i: (0,) * len(shape))
    return pl.pallas_call(
        _inproj_kernel,
        grid=(B, S // tm),
        in_specs=[pl.BlockSpec((1, tm, D), lambda b, i: (b, i, 0)),
                  pl.BlockSpec((1, 1, D), lambda b, i: (b, 0, 0)),
                  pl.BlockSpec((1, 1, D), lambda b, i: (b, 0, 0)),
                  full((1, D)), full((D, RW_COLS)), full((D, PK_COLS)), full((PT_ROWS, D)),
                  full((1, LANES)), full((1, LANES)), full((1, LANES)), full((HEAD_DIM, 1)),
                  full((LANES, LANES)),
                  pl.BlockSpec((1, tm, LANES), lambda b, i: (b, i, 0)),
                  pl.BlockSpec((1, tm, LANES), lambda b, i: (b, i, 0)),
                  pl.BlockSpec((1, HALF, tm), lambda b, i: (b, 0, i)),
                  pl.BlockSpec((1, HALF, tm), lambda b, i: (b, 0, i))],
        out_specs=[pl.BlockSpec((1, tm, RW_COLS), lambda b, i: (b, i, 0)),
                   pl.BlockSpec((1, tm, AT_WIDTH), lambda b, i: (b, i, 0)),
                   pl.BlockSpec((1, tm, LANES), lambda b, i: (b, i, 0)),
                   pl.BlockSpec((1, AT_HEADS, LANES, tm), lambda b, i: (b, 0, 0, i)),
                   pl.BlockSpec((1, tm // QBLK, AT_WIDTH, QBLK), lambda b, i: (b, i, 0, 0)),
                   pl.BlockSpec((1, IDX_HEADS, LANES, tm), lambda b, i: (b, 0, 0, i)),
                   pl.BlockSpec((1, 8, tm), lambda b, i: (b, 0, i))],
        out_shape=[jax.ShapeDtypeStruct((B, S, RW_COLS), F32),
                   jax.ShapeDtypeStruct((B, S, AT_WIDTH), BF16),
                   jax.ShapeDtypeStruct((B, S, LANES), BF16),
                   jax.ShapeDtypeStruct((B, AT_HEADS, LANES, S), BF16),
                   jax.ShapeDtypeStruct((B, S // QBLK, AT_WIDTH, QBLK), BF16),
                   jax.ShapeDtypeStruct((B, IDX_HEADS, LANES, S), BF16),
                   jax.ShapeDtypeStruct((B, 8, S), F32)],
        compiler_params=_cparams(("arbitrary", "arbitrary")),
        name="inproj", interpret=interpret,
    )(x, scale1, shift1, norm1_g.reshape(1, D), w_rw, w_k, w_t, kg, iw, ib, qg, gsum,
      cos_r, sin_r, cos_t, sin_t)


def _rwkv_kernel(p_ref, mu_ref, w0_ref, w2_ref, a0_ref, a2_ref, g2_ref, kk_ref, ka_ref, rk_ref,
                 lnw_ref, lnb_ref, gsum_ref, y_ref, s_ref, prev_ref, yt_ref):
    C = RW_CHUNK
    W = RW_WIDTH
    nb = p_ref.shape[0]

    @pl.when(pl.program_id(1) == 0)
    def _():
        s_ref[...] = jnp.zeros_like(s_ref)
        prev_ref[...] = jnp.zeros_like(prev_ref)

    row = lax.broadcasted_iota(I32, (C, 1), 0)
    gsum = gsum_ref[...]

    def prepare(bi):
        p = p_ref[bi]
        pprev = jnp.where(row == 0, prev_ref[bi:bi + 1, :], pltpu.roll(p, 1, 0))
        prev_ref[bi:bi + 1, :] = p[C - 1:C]
        ps = p + (pprev - p) * mu_ref[...]
        r, k, v = ps[:, 0:W], ps[:, W:2 * W], ps[:, 2 * W:3 * W]
        o = 3 * W
        wl = ps[:, o:o + RW_LORA_W]
        al = ps[:, o + RW_LORA_W:o + RW_LORA_W + RW_LORA_A]
        gl = ps[:, o + RW_LORA_W + RW_LORA_A:]
        z = w0_ref[...] + _dot(jnp.tanh(wl).astype(BF16), w2_ref[...])
        nz = -z
        softplus = jnp.maximum(nz, 0.0) + jnp.log(1.0 + jnp.exp(-jnp.abs(nz)))
        logw = -jnp.exp(-softplus - 0.5)
        a = _sigmoid(a0_ref[...] + _dot(al.astype(BF16), a2_ref[...]))
        g = _dot(_sigmoid(gl).astype(BF16), g2_ref[...])
        kk = k * kk_ref[...]
        ss = _dot((kk * kk).astype(BF16), gsum)
        kk = kk * (1.0 / jnp.maximum(jnp.sqrt(ss), 1e-12))
        k2 = k * (1.0 + (a - 1.0) * ka_ref[...])
        bb = kk * a
        cw = logw
        sh = 1
        while sh < C:
            cw = cw + jnp.where(row >= sh, pltpu.roll(cw, sh, 0), 0.0)
            sh *= 2
        cw_last = cw[C - 1:C]
        e_neg = jnp.exp(-cw)
        e_end = jnp.exp(cw_last - cw)
        return dict(r=r, v=v, k2=k2, g=g, rw=r * jnp.exp(cw), kkp=kk * jnp.exp(cw - logw),
                    bw=bb * e_neg, kw=k2 * e_neg, bend=bb * e_end, kend=k2 * e_end,
                    wc=jnp.exp(cw_last), vt=v.T.astype(BF16))

    pre = [prepare(bi) for bi in range(nb)]

    ri = lax.broadcasted_iota(I32, (C, C), 0)
    ci = lax.broadcasted_iota(I32, (C, C), 1)
    strict = ri < ci
    incl = ri <= ci
    incl2 = jnp.concatenate([incl, incl], axis=0)
    lane_half = lax.broadcasted_iota(I32, (1, LANES), 1) // HEAD_DIM

    chains = [(bi, h) for bi in range(nb) for h in range(RW_HEADS)]
    pair = lambda h: slice((h // 2) * LANES, (h // 2 + 1) * LANES)
    own = lambda h: lane_half == (h % 2)
    stack = lambda bi, top, bot, h: jnp.concatenate(
        [pre[bi][top][:, pair(h)], pre[bi][bot][:, pair(h)]], axis=0)
    lh = {(bi, q): stack(bi, 'kkp', 'rw', 2 * q).astype(BF16)
          for bi in range(nb) for q in range(RW_HEADS // 2)}
    rh = [jnp.where(own(h), stack(bi, 'bw', 'kw', h), 0.0).astype(BF16) for bi, h in chains]
    aat = [_dot_nt(rh[n], lh[bi, h // 2]) for n, (bi, h) in enumerate(chains)]
    s_old = [s_ref[bi * RW_HEADS + h] for bi, h in chains]
    sl = [_dot_nt(s_old[n].astype(BF16), lh[bi, h // 2]) for n, (bi, h) in enumerate(chains)]
    vt = [pre[bi]['vt'][h * HEAD_DIM:(h + 1) * HEAD_DIM] for bi, h in chains]
    ids = range(len(chains))
    akt = [jnp.where(strict, aat[n][C:, :C], 0.0).astype(BF16) for n in ids]
    m = [jnp.where(strict, aat[n][:C, :C], 0.0).astype(BF16) for n in ids]
    xs = [-(sl[n][:, :C] + _dot(vt[n], akt[n])) for n in ids]
    xs = [xs[n] - _dot_split(xs[n], m[n]) for n in ids]
    lvl = 2
    while lvl < C:
        m = [_dot(m[n], m[n]).astype(BF16) for n in ids]
        xs = [xs[n] + _dot_split(xs[n], m[n]) for n in ids]
        lvl *= 2
    zt = [jnp.concatenate([xs[n].astype(BF16), vt[n]], axis=1) for n in ids]
    for n in ids:
        ymat = jnp.where(incl2, aat[n][:, C:], 0.0).astype(BF16)
        yt_ref[n * HEAD_DIM:(n + 1) * HEAD_DIM, :] = sl[n][:, C:] + _dot(zt[n], ymat)
    for n, (bi, h) in enumerate(chains):
        endz = jnp.where(own(h), stack(bi, 'bend', 'kend', h), 0.0).astype(BF16)
        s_ref[n] = s_old[n] * pre[bi]['wc'][:, pair(h)] + _dot(zt[n], endz)

    lnw = lnw_ref[...].reshape(RW_HEADS, HEAD_DIM, 1)
    lnb = lnb_ref[...].reshape(RW_HEADS, HEAD_DIM, 1)
    for bi in range(nb):
        q = pre[bi]
        yt = yt_ref[bi * W:(bi + 1) * W, :].reshape(RW_HEADS, HEAD_DIM, C)
        mean = jnp.mean(yt, axis=1, keepdims=True)
        yc = yt - mean
        var = jnp.mean(yc * yc, axis=1, keepdims=True)
        yn = yc * lax.rsqrt(var + RW_GN_EPS) * lnw + lnb
        y = yn.reshape(W, C).T
        bonus = _dot((q['r'] * q['k2'] * rk_ref[...]).astype(BF16), gsum) * q['v']
        y_ref[bi] = ((y + bonus) * q['g']).astype(BF16)


def _rwkv(p_rw, rw_mu, rw_w0, rw_w2, rw_a0, rw_a2, rw_g2, rw_k_k, rw_k_a, rw_r_k, rw_ln_w, rw_ln_b,
          interpret):
    B, S, _ = p_rw.shape
    C, W = RW_CHUNK, RW_WIDTH
    li = np.arange(W)
    gsum = jnp.asarray((li[:, None] // HEAD_DIM == li[None, :] // HEAD_DIM).astype(np.float32)).astype(BF16)
    row = lambda a: a.reshape(1, -1)
    full = lambda shape: pl.BlockSpec(shape, lambda b, i: (0,) * len(shape))
    nb = RW_BATCH if B % RW_BATCH == 0 else 1
    return pl.pallas_call(
        _rwkv_kernel,
        grid=(B // nb, S // C),
        in_specs=[pl.BlockSpec((nb, C, RW_COLS), lambda b, i: (b, i, 0)),
                  full((1, RW_COLS)), full((1, W)), full((RW_LORA_W, W)), full((1, W)),
                  full((RW_LORA_A, W)), full((RW_LORA_G, W)), full((1, W)), full((1, W)),
                  full((1, W)), full((W, 1)), full((W, 1)), full((W, W))],
        out_specs=pl.BlockSpec((nb, C, W), lambda b, i: (b, i, 0)),
        out_shape=jax.ShapeDtypeStruct((B, S, W), BF16),
        scratch_shapes=[pltpu.VMEM((nb * RW_HEADS, HEAD_DIM, LANES), F32),
                        pltpu.VMEM((nb, RW_COLS), F32),
                        pltpu.VMEM((nb * W, C), F32)],
        compiler_params=_cparams(("arbitrary", "arbitrary")),
        name="rwkv", interpret=interpret,
    )(p_rw, row(rw_mu), row(rw_w0), rw_w2.astype(BF16), row(rw_a0), rw_a2.astype(BF16),
      rw_g2.astype(BF16), row(rw_k_k), row(rw_k_a), row(rw_r_k), rw_ln_w.reshape(W, 1),
      rw_ln_b.reshape(W, 1), gsum)


def _dsa_kernel(topk, nbits, k_ref, vt_ref, ki_ref, qt_ref, qit_ref, wit_ref, og_ref, o_ref,
                key_s, acc_s, m_s, l_s, thr_s):
    j = pl.program_id(1)
    nkb = j + 1
    lane = lax.broadcasted_iota(I32, (QBLK, QBLK), 1)
    sub = lax.broadcasted_iota(I32, (QBLK, QBLK), 0)
    qpos = j * QBLK + lane
    wit = wit_ref[0]

    def score_blocks(i, carry):
        kbs = [i * SCORE_UNROLL + u for u in range(SCORE_UNROLL)]
        kib = [ki_ref[0, pl.ds(pl.multiple_of(kb * QBLK, QBLK), QBLK), :] for kb in kbs]
        lg = [[_dot(kib[u], qit_ref[0, hh]) for hh in range(IDX_HEADS)] for u in range(SCORE_UNROLL)]
        for u, kb in enumerate(kbs):
            s = wit[0:1, :] * jnp.maximum(lg[u][0], 0.0)
            for hh in range(1, IDX_HEADS):
                s = s + wit[hh:hh + 1, :] * jnp.maximum(lg[u][hh], 0.0)
            s = jnp.where(s == 0.0, 0.0, s)
            bits = pltpu.bitcast(s, I32)
            skey = jnp.where(bits < 0, bits ^ 0x7FFFFFFF, bits)
            key_s[kb] = jnp.where(kb * QBLK + sub <= qpos, skey, INT_MIN)
        return carry

    lax.fori_loop(0, pl.cdiv(nkb, SCORE_UNROLL), score_blocks, 0)

    @pl.when(nkb * QBLK <= topk)
    def _():
        thr_s[0:1, :] = jnp.full((1, QBLK), INT_MIN, I32)
        thr_s[1:2, :] = jnp.zeros((1, QBLK), I32)

    @pl.when(nkb * QBLK > topk)
    def _():
        n_done = pl.cdiv(nkb, SCORE_UNROLL) * SCORE_UNROLL
        n_cnt = pl.cdiv(nkb, COUNT_UNROLL)

        def fill(kb, carry):
            key_s[kb] = jnp.full((QBLK, QBLK), INT_MIN, I32)
            return carry

        lax.fori_loop(n_done, n_cnt * COUNT_UNROLL, fill, 0)

        def count(preds):
            def body(i, accs):
                accs = list(accs)
                for u in range(COUNT_UNROLL):
                    kb = i * COUNT_UNROLL + u
                    ky = key_s[kb]
                    for n, pred in enumerate(preds):
                        hit = pred(ky, kb * QBLK + sub).astype(I32)
                        accs[n] = accs[n] + jnp.sum(hit.reshape(QBLK // 8, 8, QBLK), axis=0)
                return tuple(accs)
            accs = lax.fori_loop(0, n_cnt, body, tuple(jnp.zeros((8, QBLK), I32) for _ in preds))
            return [jnp.sum(a, axis=0, keepdims=True) for a in accs]

        c0, = count([lambda ky, ix: ky >= 0])
        t0 = jnp.where(c0 >= topk, 0, INT_MIN).astype(I32)

        def bit_step(i, t):
            cand = t | jnp.left_shift(jnp.int32(1), 30 - i)
            c, = count([lambda ky, ix: ky >= cand])
            return jnp.where(c >= topk, cand, t)

        thr = lax.fori_loop(0, 31, bit_step, t0)
        n_gt, n_eq = count([lambda ky, ix: ky > thr, lambda ky, ix: ky == thr])
        need = topk - n_gt
        thr_s[0:1, :] = thr
        thr_s[1:2, :] = jnp.full((1, QBLK), 2 ** nbits, I32)

        @pl.when(jnp.max(jnp.abs(n_eq - need)) > 0)
        def _():
            def idx_step(i, mm):
                cand = mm | jnp.left_shift(jnp.int32(1), nbits - 1 - i)
                c, = count([lambda ky, ix: (ky == thr) & (ix < cand)])
                return jnp.where(c < need, cand, mm)

            thr_s[1:2, :] = lax.fori_loop(0, nbits, idx_step, jnp.zeros((1, QBLK), I32))

    thr = thr_s[0:1, :]
    mm = thr_s[1:2, :]
    m_s[...] = jnp.full_like(m_s, NEG_BIG)
    l_s[...] = jnp.zeros_like(l_s)
    acc_s[...] = jnp.zeros_like(acc_s)

    qt2 = [jnp.concatenate([qt_ref[0, 2 * q], qt_ref[0, 2 * q + 1]], axis=1)
           for q in range(AT_HEADS // 2)]

    def attn_blocks(i, carry):
        kbs = [i * ATTN_UNROLL + u for u in range(ATTN_UNROLL)]
        sel = []
        for kb in kbs:
            skey = key_s[kb]
            kidx = kb * QBLK + sub
            sel.append((kidx <= qpos) & ((skey > thr) | ((skey == thr) & (kidx <= mm))))
        sel = jnp.concatenate(sel, axis=0)
        kblk = [k_ref[0, pl.ds(pl.multiple_of(kb * QBLK, QBLK), QBLK), :] for kb in kbs]
        vtb = jnp.concatenate([vt_ref[0, kb] for kb in kbs], axis=1)
        s2 = [[_dot(kblk[u][:, q * LANES:(q + 1) * LANES], qt2[q]) for u in range(ATTN_UNROLL)]
              for q in range(AT_HEADS // 2)]
        pexp, alpha = [], []
        for hh in range(AT_HEADS):
            half = slice((hh % 2) * QBLK, (hh % 2 + 1) * QBLK)
            s = jnp.concatenate([s2[hh // 2][u][:, half] for u in range(ATTN_UNROLL)], axis=0)
            s = jnp.where(sel, s, NEG_BIG)
            m_old = m_s[hh:hh + 1, :]
            m_new = jnp.maximum(m_old, jnp.max(s, axis=0, keepdims=True))
            pe = jnp.exp2(s - m_new)
            al = jnp.exp2(m_old - m_new)
            l_s[hh:hh + 1, :] = al * l_s[hh:hh + 1, :] + jnp.sum(pe, axis=0, keepdims=True)
            m_s[hh:hh + 1, :] = m_new
            pexp.append(pe.astype(BF16))
            alpha.append(al)
        for hh in range(AT_HEADS):
            hs = slice(hh * HEAD_DIM, (hh + 1) * HEAD_DIM)
            acc_s[hs, :] = alpha[hh] * acc_s[hs, :] + _dot(vtb[hs, :], pexp[hh])
        return carry

    lax.fori_loop(0, pl.cdiv(nkb, ATTN_UNROLL), attn_blocks, 0)

    for hh in range(AT_HEADS):
        hs = slice(hh * HEAD_DIM, (hh + 1) * HEAD_DIM)
        oh = acc_s[hs, :] * (1.0 / l_s[hh:hh + 1, :])
        ms = jnp.mean(oh * oh, axis=0, keepdims=True)
        acc_s[hs, :] = oh * lax.rsqrt(ms + NORM_EPS) * og_ref[hs, :]
    o_ref[0] = acc_s[...].T.astype(BF16)


def _dsa(k, vt, ki, qt, qit, wit, at_out_g, interpret):
    B, S, _ = k.shape
    nq = S // QBLK
    topk = min(IDX_TOPK_MAX, S // 4)
    nbits = int(np.log2(S))
    assert 2 ** nbits == S and nq % COUNT_UNROLL == 0 and SCORE_UNROLL == ATTN_UNROLL
    assert COUNT_UNROLL % SCORE_UNROLL == 0
    return pl.pallas_call(
        functools.partial(_dsa_kernel, topk, nbits),
        grid=(B, nq),
        in_specs=[pl.BlockSpec((1, S, AT_WIDTH), lambda b, j: (b, 0, 0)),
                  pl.BlockSpec((1, nq, AT_WIDTH, QBLK), lambda b, j: (b, 0, 0, 0)),
                  pl.BlockSpec((1, S, LANES), lambda b, j: (b, 0, 0)),
                  pl.BlockSpec((1, AT_HEADS, LANES, QBLK), lambda b, j: (b, 0, 0, j)),
                  pl.BlockSpec((1, IDX_HEADS, LANES, QBLK), lambda b, j: (b, 0, 0, j)),
                  pl.BlockSpec((1, 8, QBLK), lambda b, j: (b, 0, j)),
                  pl.BlockSpec((AT_WIDTH, 1), lambda b, j: (0, 0))],
        out_specs=pl.BlockSpec((1, QBLK, AT_WIDTH), lambda b, j: (b, j, 0)),
        out_shape=jax.ShapeDtypeStruct((B, S, AT_WIDTH), BF16),
        scratch_shapes=[pltpu.VMEM((nq, QBLK, QBLK), I32),
                        pltpu.VMEM((AT_WIDTH, QBLK), F32),
                        pltpu.VMEM((AT_HEADS, QBLK), F32),
                        pltpu.VMEM((AT_HEADS, QBLK), F32),
                        pltpu.VMEM((8, QBLK), I32)],
        compiler_params=_cparams(("arbitrary", "arbitrary")),
        name="dsa", interpret=interpret,
    )(k, vt, ki, qt, qit, wit, at_out_g.reshape(AT_WIDTH, 1))


def _first_max(vals, idx, axis, sentinel):
    m = jnp.max(vals, axis=axis, keepdims=True)
    return m, jnp.min(jnp.where(vals == m, idx, sentinel), axis=axis, keepdims=True)


def _route_cols(logits_t, bias_col):
    E, tm = logits_t.shape
    pg = E // N_GROUPS
    scores = _sigmoid(logits_t)
    biased = scores + bias_col
    b3 = biased.reshape(N_GROUPS, pg, tm)
    r3 = lax.broadcasted_iota(I32, (N_GROUPS, pg, tm), 1)
    m1, first = _first_max(b3, r3, 1, pg)
    m2 = jnp.max(jnp.where(r3 == first, -jnp.inf, b3), axis=1, keepdims=True)
    cur = (m1 + m2).reshape(N_GROUPS, tm)
    grow = lax.broadcasted_iota(I32, (N_GROUPS, tm), 0)
    gsel = jnp.zeros((N_GROUPS, tm), F32)
    for _ in range(TOPK_GROUPS):
        _, gi = _first_max(cur, grow, 0, N_GROUPS)
        hit = grow == gi
        gsel = jnp.where(hit, 1.0, gsel)
        cur = jnp.where(hit, -jnp.inf, cur)
    gmask = jnp.broadcast_to(gsel.reshape(N_GROUPS, 1, tm), (N_GROUPS, pg, tm)).reshape(E, tm)
    cur = jnp.where(gmask > 0.0, biased, -jnp.inf)
    row = lax.broadcasted_iota(I32, (E, tm), 0)
    onehot = jnp.zeros((E, tm), F32)
    eids, gws = [], []
    for _ in range(TOP_K):
        _, ei = _first_max(cur, row, 0, E)
        hit = row == ei
        eids.append(ei)
        gws.append(jnp.sum(jnp.where(hit, scores, 0.0), axis=0, keepdims=True))
        onehot = jnp.where(hit, 1.0, onehot)
        cur = jnp.where(hit, -jnp.inf, cur)
    eid = jnp.concatenate(eids, axis=0)
    gw = jnp.concatenate(gws, axis=0)
    gw = gw * (ROUTED_SCALE / jnp.sum(gw, axis=0, keepdims=True))
    return eid, gw, onehot


def _post_kernel(x_ref, yrw_ref, yat_ref, g1_ref, sc_ref, sh_ref, g2_ref, ng_ref, wo_ref, rwh_ref,
                 rwl_ref, rb_ref, s1_ref, s3_ref, s2_ref, base_ref, h2_ref, eid_ref, gw_ref, rank_ref,
                 cnt_ref):
    W = RW_WIDTH
    tm = x_ref.shape[1]
    E = rwh_ref.shape[0]

    @pl.when((pl.program_id(0) == 0) & (pl.program_id(1) == 0))
    def _():
        cnt_ref[...] = jnp.zeros_like(cnt_ref)

    mix = _dot(yrw_ref[0], wo_ref[0:W, :]) + _dot(yat_ref[0], wo_ref[W:, :])
    x1 = x_ref[0] + g1_ref[0] * mix
    ms = jnp.mean(x1 * x1, axis=-1, keepdims=True)
    h2 = x1 * lax.rsqrt(ms + NORM_EPS) * ng_ref[...] * (1.0 + sc_ref[0]) + sh_ref[0]
    hb = h2.astype(BF16)
    h2_ref[0, :, 0, :] = _pack_halves(h2)
    act =(_silu(_dot(hb, s1_ref[...])) * _dot(hb, s3_ref[...])).astype(BF16)
    base_ref[0] = x1 + g2_ref[0] * _dot(act, s2_ref[...])

    h_lo = (h2 - hb.astype(F32)).astype(BF16)
    logits_t = (_dot_nt(rwh_ref[...], hb) + _dot_nt(rwh_ref[...], h_lo)
                + _dot_nt(rwl_ref[...], hb))
    eid, gw, onehot = _route_cols(logits_t, rb_ref[...])
    eid_ref[0] = eid
    gw_ref[0] = gw
    ti = lax.broadcasted_iota(I32, (tm, tm), 0)
    tj = lax.broadcasted_iota(I32, (tm, tm), 1)
    before = _dot(onehot.astype(BF16), (ti < tj).astype(BF16)) + cnt_ref[:, 0:1]
    row = lax.broadcasted_iota(I32, (E, tm), 0)
    ranks = [jnp.sum(jnp.where(row == eid[kk:kk + 1, :], before, 0.0), axis=0, keepdims=True)
             for kk in range(TOP_K)]
    rank_ref[0] = jnp.concatenate(ranks, axis=0).astype(I32)
    cnt_ref[...] = cnt_ref[...] + jnp.sum(onehot, axis=1, keepdims=True)


def _post(x, y_rw, y_at, gate1, scale2, shift2, gate2, norm2_g, w_out, router_w, router_bias,
          sw1, sw3, sw2, interpret):
    B, S, D = x.shape
    tm = min(S, 512)
    E = router_w.shape[1]
    sd = sw1.shape[1]
    full = lambda shape: pl.BlockSpec(shape, lambda b, i: (0,) * len(shape))
    tok = lambda w: pl.BlockSpec((1, tm, w), lambda b, i: (b, i, 0))
    per_b = pl.BlockSpec((1, 1, D), lambda b, i: (b, 0, 0))
    col8 = pl.BlockSpec((1, TOP_K, tm), lambda b, i: (b, 0, i))
    rw_t = router_w.T
    rw_hi = rw_t.astype(BF16)
    rw_lo = (rw_t - rw_hi.astype(F32)).astype(BF16)
    return pl.pallas_call(
        _post_kernel,
        grid=(B, S // tm),
        in_specs=[tok(D), tok(RW_WIDTH), tok(AT_WIDTH), per_b, per_b, per_b, per_b, full((1, D)),
                  full((D, D)), full((E, D)), full((E, D)), full((E, 1)), full((D, sd)), full((D, sd)),
                  full((sd, D))],
        out_specs=[tok(D), pl.BlockSpec((1, tm, 1, D // 2), lambda b, i: (b, i, 0, 0)),
                   col8, col8, col8, full((E, LANES))],
        out_shape=[jax.ShapeDtypeStruct((B, S, D), F32),
                   jax.ShapeDtypeStruct((B, S, 1, D // 2), I32),
                   jax.ShapeDtypeStruct((B, TOP_K, S), I32),
                   jax.ShapeDtypeStruct((B, TOP_K, S), F32),
                   jax.ShapeDtypeStruct((B, TOP_K, S), I32),
                   jax.ShapeDtypeStruct((E, LANES), F32)],
        compiler_params=_cparams(("arbitrary", "arbitrary")),
        name="post", interpret=interpret,
    )(x, y_rw, y_at, gate1, scale2, shift2, gate2, norm2_g.reshape(1, D), w_out.astype(BF16),
      rw_hi, rw_lo, router_bias.reshape(E, 1), sw1.astype(BF16), sw3.astype(BF16), sw2.astype(BF16))


def _expert_kernel(be_ref, nv_ref, nu_ref, xs_ref, w1_ref, w3_ref, w2_ref, o_ref, w13_s, w2_s):
    i = pl.program_id(0)
    blk, _, hw = xs_ref.shape
    F = w1_ref.shape[2]
    used = i < nu_ref[0]

    @pl.when(used & ((i == 0) | (be_ref[i] != be_ref[jnp.maximum(i - 1, 0)])))
    def _():
        w13_s[:, :F] = w1_ref[0].astype(BF16)
        w13_s[:, F:] = w3_ref[0].astype(BF16)
        w2_s[...] = w2_ref[0].astype(BF16)

    @pl.when(used)
    def _():
        live = lax.broadcasted_iota(I32, (blk, 1), 0) < nv_ref[i]
        x_lo, x_hi = _unpack_halves(jnp.where(live, xs_ref[:, 0, :], 0))
        h13 =(_dot(x_lo.astype(BF16), w13_s[:hw, :]) + _dot(x_hi.astype(BF16), w13_s[hw:, :]))
        act = (_silu(h13[:, :F]) * h13[:, F:]).astype(BF16)
        o_ref[:, 0, :] = _pack_halves(_dot(act, w2_s[...]))


def _experts(xs, block_e, block_rows, n_used, w1, w3, w2, interpret):
    P, _, hw = xs.shape
    E, D, F = w1.shape
    nb = P // EXP_BLK
    blk = lambda i, nu: jnp.minimum(i, nu[0] - 1)
    grid_spec = pltpu.PrefetchScalarGridSpec(
        num_scalar_prefetch=3,
        grid=(nb,),
        in_specs=[pl.BlockSpec((EXP_BLK, 1, hw), lambda i, be, nv, nu: (blk(i, nu), 0, 0)),
                  pl.BlockSpec((1, D, F), lambda i, be, nv, nu: (be[blk(i, nu)], 0, 0)),
                  pl.BlockSpec((1, D, F), lambda i, be, nv, nu: (be[blk(i, nu)], 0, 0)),
                  pl.BlockSpec((1, F, D), lambda i, be, nv, nu: (be[blk(i, nu)], 0, 0))],
        out_specs=pl.BlockSpec((EXP_BLK, 1, hw), lambda i, be, nv, nu: (blk(i, nu), 0, 0)),
        scratch_shapes=[pltpu.VMEM((D, 2 * F), BF16), pltpu.VMEM((F, D), BF16)],
    )
    return pl.pallas_call(
        _expert_kernel,
        grid_spec=grid_spec,
        out_shape=jax.ShapeDtypeStruct((P, 1, hw), I32),
        compiler_params=_cparams(("arbitrary",)),
        name="experts", interpret=interpret,
    )(block_e, block_rows, n_used, xs, w1, w3, w2)


def _row_out(tile, t, slots, slot, sem):
    return pltpu.make_async_copy(tile.at[t], slots.at[slot], sem)


def _row_in(slots, slot, tile, t, sem):
    return pltpu.make_async_copy(slots.at[slot], tile.at[pl.ds(t, 1)], sem)


def _dispatch_kernel(dest_ref, h2_ref, xs_ref, sem):
    td = h2_ref.shape[0]

    def issue(t, carry):
        for kk in range(TOP_K):
            _row_out(h2_ref, t, xs_ref, dest_ref[0, 0, kk * td + t], sem).start(priority=kk % 2)
        return carry

    def drain(t, carry):
        for kk in range(TOP_K):
            _row_out(h2_ref, t, xs_ref, dest_ref[0, 0, kk * td + t], sem).wait()
        return carry

    lax.fori_loop(0, td, issue, 0)
    lax.fori_loop(0, td, drain, 0)


def _dispatch(h2, dest_tiles, n_slots, interpret):
    T, _, D = h2.shape
    nt, _, n = dest_tiles.shape
    td = n // TOP_K
    return pl.pallas_call(
        _dispatch_kernel,
        grid=(nt,),
        in_specs=[pl.BlockSpec((1, 1, n), lambda i: (i, 0, 0), memory_space=pltpu.SMEM),
                  pl.BlockSpec((td, 1, D), lambda i: (i, 0, 0))],
        out_specs=pl.BlockSpec(memory_space=pl.ANY),
        out_shape=jax.ShapeDtypeStruct((n_slots, 1, D), h2.dtype),
        scratch_shapes=[pltpu.SemaphoreType.DMA(())],
        compiler_params=_cparams(("arbitrary",)),
        name="dispatch", interpret=interpret,
    )(dest_tiles, h2)


def _combine_kernel(dest_ref, base_ref, g2_ref, gw_ref, ys_ref, o_ref, buf, sem):
    td = base_ref.shape[0]
    hd = td // 2

    def rows(part, wait):
        def body(t, carry):
            for kk in range(TOP_K):
                cp = _row_in(ys_ref, dest_ref[0, 0, kk * td + t], buf.at[kk], t, sem.at[part])
                if wait:
                    cp.wait()
                else:
                    cp.start(priority=kk % 2)
            return carry
        lax.fori_loop(part * hd, (part + 1) * hd, body, 0)

    def reduce(part):
        r = slice(part * hd, (part + 1) * hd)
        gw = gw_ref[r, :]
        acc_lo, acc_hi = _unpack_halves(buf[0, r, :])
        acc_lo, acc_hi = gw[:, 0:1] * acc_lo, gw[:, 0:1] * acc_hi
        for kk in range(1, TOP_K):
            y_lo, y_hi = _unpack_halves(buf[kk, r, :])
            acc_lo = acc_lo + gw[:, kk:kk + 1] * y_lo
            acc_hi = acc_hi + gw[:, kk:kk + 1] * y_hi
        o_ref[r, :] = base_ref[r, :] + g2_ref[0] * jnp.concatenate([acc_lo, acc_hi], axis=1)

    rows(0, wait=False)
    rows(1, wait=False)
    rows(0, wait=True)
    reduce(0)
    rows(1, wait=True)
    reduce(1)


def _combine(base, gate2, gw_tok, ys, dest_tiles, tiles_per_batch, interpret):
    T, D = base.shape
    nt, _, n = dest_tiles.shape
    td = n // TOP_K
    return pl.pallas_call(
        _combine_kernel,
        grid=(nt,),
        in_specs=[pl.BlockSpec((1, 1, n), lambda i: (i, 0, 0), memory_space=pltpu.SMEM),
                  pl.BlockSpec((td, D), lambda i: (i, 0)),
                  pl.BlockSpec((1, 1, D), lambda i: (i // tiles_per_batch, 0, 0)),
                  pl.BlockSpec((td, TOP_K), lambda i: (i, 0)),
                  pl.BlockSpec(memory_space=pl.ANY)],
        out_specs=pl.BlockSpec((td, D), lambda i: (i, 0)),
        out_shape=jax.ShapeDtypeStruct((T, D), F32),
        scratch_shapes=[pltpu.VMEM((TOP_K, td, D // 2), I32), pltpu.SemaphoreType.DMA((2,))],
        compiler_params=_cparams(("arbitrary",)),
        name="combine", interpret=interpret,
    )(dest_tiles, base, gate2, gw_tok, ys)


def _slots_kernel(eid_ref, rank_ref, pstart_ref, dest_ref):
    td = eid_ref.shape[2]
    E = pstart_ref.shape[0]
    row = lax.broadcasted_iota(I32, (E, td), 0)
    pstart = pstart_ref[...]
    eid = eid_ref[0]
    for kk in range(TOP_K):
        base = jnp.sum(jnp.where(row == eid[kk:kk + 1, :], pstart, 0), axis=0, keepdims=True)
        dest_ref[0, :, kk * td:(kk + 1) * td] = base + rank_ref[0, kk:kk + 1, :]


def _slot_plan(counts, eid_t, rank_t, td, interpret):
    B, _, S = eid_t.shape
    E = counts.shape[0]
    padded = (counts + EXP_BLK - 1) // EXP_BLK * EXP_BLK
    pend = jnp.cumsum(padded)
    pstart = (pend - padded).astype(I32)
    nb = -(-(B * S * TOP_K + E * (EXP_BLK - 1)) // EXP_BLK)
    first_row = jnp.arange(nb, dtype=I32) * EXP_BLK
    block_e = jnp.sum(pend[None, :] <= first_row[:, None], axis=1)
    block_e = jnp.minimum(block_e, E - 1).astype(I32)
    block_rows = jnp.clip(pstart[block_e] + counts[block_e] - first_row, 0, EXP_BLK).astype(I32)
    nt = S // td
    dest_tiles = pl.pallas_call(
        _slots_kernel,
        grid=(B, nt),
        in_specs=[pl.BlockSpec((1, TOP_K, td), lambda b, i: (b, 0, i)),
                  pl.BlockSpec((1, TOP_K, td), lambda b, i: (b, 0, i)),
                  pl.BlockSpec((E, 1), lambda b, i: (0, 0))],
        out_specs=pl.BlockSpec((1, 1, TOP_K * td), lambda b, i: (b * nt + i, 0, 0)),
        out_shape=jax.ShapeDtypeStruct((B * nt, 1, TOP_K * td), I32),
        compiler_params=_cparams(("arbitrary", "arbitrary")),
        name="slots", interpret=interpret,
    )(eid_t, rank_t, pstart.reshape(E, 1))
    n_used = (pend[-1:] // EXP_BLK).astype(I32)
    return block_e, block_rows, n_used, dest_tiles, nb * EXP_BLK


def _forward(x, c, positions, w_ada, b_ada, norm1_g, norm2_g, w_in, rw_mu, rw_w0, rw_w2,
             rw_a0, rw_a2, rw_g2, rw_k_k, rw_k_a, rw_r_k, rw_ln_w, rw_ln_b, q_norm_g,
             k_norm_g, idx_ln_w, idx_ln_b, at_out_g, w_out, router_w, router_bias,
             exp_w1, exp_w3, exp_w2, shared_w1, shared_w3, shared_w2, interpret=False):
    B, S, D = x.shape
    depth = w_ada.shape[0]
    for l in range(depth):
        mod = _mod(c, w_ada[l], b_ada[l], interpret)
        shift1, scale1, gate1, shift2, scale2, gate2 = [
            m.reshape(B, 1, D) for m in jnp.split(mod, 6, axis=-1)]
        tabs = _rope_tables(positions, interpret)
        p_rw, k, ki, qt, vt, qit, wit = _inproj(
            x, scale1, shift1, norm1_g[l], w_in[l], k_norm_g[l], idx_ln_w[l], idx_ln_b[l],
            q_norm_g[l], tabs, interpret)
        y_rw = _rwkv(p_rw, rw_mu[l], rw_w0[l], rw_w2[l], rw_a0[l], rw_a2[l], rw_g2[l], rw_k_k[l],
                     rw_k_a[l], rw_r_k[l], rw_ln_w[l], rw_ln_b[l], interpret)
        y_at = _dsa(k, vt, ki, qt, qit, wit, at_out_g[l], interpret)
        base, h2, eid_t, gw_t, rank_t, cnt = _post(
            x, y_rw, y_at, gate1, scale2, shift2, gate2, norm2_g[l], w_out[l], router_w[l],
            router_bias[l], shared_w1[l], shared_w3[l], shared_w2[l], interpret)
        T = B * S
        td = min(S, ROW_TILE)
        block_e, block_rows, n_used, dest_tiles, n_slots = _slot_plan(
            cnt[:, 0].astype(I32), eid_t, rank_t, td, interpret)
        xs = _dispatch(h2.reshape(T, 1, D // 2), dest_tiles, n_slots, interpret)
        ys = _experts(xs, block_e, block_rows, n_used, exp_w1[l], exp_w3[l], exp_w2[l], interpret)
        gw_tok = gw_t.transpose(0, 2, 1).reshape(T, TOP_K)
        x = _combine(base.reshape(T, D), gate2, gw_tok, ys, dest_tiles, S // td,
                     interpret).reshape(B, S, D)
    return x


def kernel(x, c, positions, w_ada, b_ada, norm1_g, norm2_g, w_in, rw_mu, rw_w0, rw_w2, rw_a0, rw_a2, rw_g2, rw_k_k, rw_k_a, rw_r_k, rw_ln_w, rw_ln_b, q_norm_g, k_norm_g, idx_ln_w, idx_ln_b, at_out_g, w_out, router_w, router_bias, exp_w1, exp_w3, exp_w2, shared_w1, shared_w3, shared_w2):
    return _forward(x, c, positions, w_ada, b_ada, norm1_g, norm2_g, w_in, rw_mu, rw_w0, rw_w2,
                    rw_a0, rw_a2, rw_g2, rw_k_k, rw_k_a, rw_r_k, rw_ln_w, rw_ln_b, q_norm_g,
                    k_norm_g, idx_ln_w, idx_ln_b, at_out_g, w_out, router_w, router_bias,
                    exp_w1, exp_w3, exp_w2, shared_w1, shared_w3, shared_w2)
```
